```python
import math
import jax, jax.numpy as jnp
from jax import lax
import numpy as np

D_MODEL = 1024
BATCH = 4
SEQ = 8192
DEPTH = 1

HEAD_DIM = 64
N_HEADS = D_MODEL // HEAD_DIM
H_A = N_HEADS // 2
H_B = N_HEADS - H_A
G_B = 2
R_B = H_B // G_B
BLK_A = 256
TOPK_A = 3
QC_A = 32
CMP_LEN = 32
CMP_STRIDE = 16
CMP_HIDDEN = 256
SEL_BLK = 64
SEL_TOPK = 16
WINDOW = 512
QC_B = 64
N_BUCKETS = 32
MAX_DIST = 128
FFN_HIDDEN = ((8 * D_MODEL // 3 + 255) // 256) * 256
ADA_CHUNKS = 6
PAD_MULT = 256
W_IN_COLS = 3 * H_A * HEAD_DIM + H_B * HEAD_DIM + 6 * G_B * HEAD_DIM + 3 * H_B
NEG = -1e30
BIG = 1e9
EPS = 1e-6

kernel_name = 'hymba_moba_nsa_t5bias_adaln_swiglu'


def rmsnorm(x, g):
    xf = x.astype(jnp.float32)
    y = xf * lax.rsqrt(jnp.mean(xf * xf, axis=-1, keepdims=True) + EPS)
    return (y * g.astype(jnp.float32)).astype(x.dtype)


def t5_bucket(dist):
    max_exact = N_BUCKETS // 2
    d = jnp.maximum(dist, 0)
    df = jnp.maximum(d, 1).astype(jnp.float32)
    large = max_exact + (jnp.log(df / max_exact) / math.log(MAX_DIST / max_exact)
                         * (N_BUCKETS - max_exact)).astype(jnp.int32)
    large = jnp.minimum(large, N_BUCKETS - 1)
    return jnp.where(d < max_exact, d, large)


def masked_softmax(logits, mask):
    z = jnp.where(mask, logits.astype(jnp.float32), NEG)
    p = jax.nn.softmax(z, axis=-1)
    return jnp.where(mask, p, 0.0)


def moba_attention(q, k, v, tab):
    B, H, Sp, D = q.shape
    nb = Sp // BLK_A
    n_sel = max(1, min(TOPK_A, nb - 1))
    scale = D ** -0.5
    kb = k.reshape(B, H, nb, BLK_A, D)
    vb = v.reshape(B, H, nb, BLK_A, D)
    kmean = jnp.mean(kb.astype(jnp.float32), axis=3)
    bi = jnp.arange(B)[:, None, None, None]
    hi = jnp.arange(H)[None, :, None, None]
    blk_ids = jnp.arange(nb)
    offs = jnp.arange(BLK_A)
    n_chunks = Sp // QC_A
    q_chunks = q.reshape(B, H, n_chunks, QC_A, D).transpose(2, 0, 1, 3, 4)

    def chunk(args):
        ci, qc = args
        pos_q = ci * QC_A + jnp.arange(QC_A)
        own = (ci * QC_A) // BLK_A
        s = jnp.einsum('bhqd,bhnd->bhqn', qc.astype(jnp.float32), kmean)
        s = jnp.where(blk_ids < own, s, -jnp.inf)
        top_s, top_i = lax.top_k(s, n_sel)
        sel_ok = jnp.isfinite(top_s)
        kg = kb[bi, hi, top_i]
        vg = vb[bi, hi, top_i]
        key_pos = top_i[..., None] * BLK_A + offs
        l_sel = jnp.einsum('bhqd,bhqnkd->bhqnk', qc, kg).astype(jnp.float32) * scale
        l_sel = l_sel + tab[hi[..., None], t5_bucket(pos_q[:, None, None] - key_pos)]
        l_sel = l_sel.reshape(B, H, QC_A, n_sel * BLK_A)
        m_sel = jnp.broadcast_to(sel_ok[..., None], (B, H, QC_A, n_sel, BLK_A)).reshape(B, H, QC_A, n_sel * BLK_A)
        k_own = lax.dynamic_slice_in_dim(k, own * BLK_A, BLK_A, axis=2)
        v_own = lax.dynamic_slice_in_dim(v, own * BLK_A, BLK_A, axis=2)
        own_pos = own * BLK_A + offs
        l_own = jnp.einsum('bhqd,bhkd->bhqk', qc, k_own).astype(jnp.float32) * scale
        l_own = l_own + tab[:, t5_bucket(pos_q[:, None] - own_pos[None, :])]
        m_own = jnp.broadcast_to(own_pos[None, :] <= pos_q[:, None], l_own.shape)
        p = masked_softmax(jnp.concatenate([l_sel, l_own], axis=-1),
                           jnp.concatenate([m_sel, m_own], axis=-1)).astype(v.dtype)
        p_sel, p_own = p[..., :n_sel * BLK_A], p[..., n_sel * BLK_A:]
        return (jnp.einsum('bhqk,bhqkd->bhqd', p_sel, vg.reshape(B, H, QC_A, n_sel * BLK_A, D))
                + jnp.einsum('bhqk,bhkd->bhqd', p_own, v_own))

    out = lax.map(chunk, (jnp.arange(n_chunks), q_chunks))
    return out.transpose(1, 0, 3, 2, 4).reshape(B, Sp, H * D)


def compress(kv, pe, w1, w2):
    B, G, Sp, D = kv.shape
    n_cmp = (Sp - CMP_LEN) // CMP_STRIDE + 1
    idx = jnp.arange(n_cmp)[:, None] * CMP_STRIDE + jnp.arange(CMP_LEN)[None, :]
    blocks = kv[:, :, idx] + pe
    hid = jax.nn.gelu(blocks.reshape(B, G, n_cmp, CMP_LEN * D) @ w1)
    return hid @ w2


def nsa_attention(q, k_cmp, v_cmp, k_sel, v_sel, k_win, v_win, gates, tab):
    B, G, R, Sp, D = q.shape
    scale = D ** -0.5
    n_cmp = k_cmp.shape[2]
    nb = Sp // SEL_BLK
    n_sel = min(SEL_TOPK, nb)
    cmp_end = jnp.arange(n_cmp) * CMP_STRIDE + CMP_LEN - 1
    cs = jnp.arange(n_cmp)[:, None] * CMP_STRIDE
    ss = jnp.arange(nb)[None, :] * SEL_BLK
    overlap = ((cs < ss + SEL_BLK) & (cs + CMP_LEN > ss)).astype(jnp.float32)
    ksb = k_sel.reshape(B, G, nb, SEL_BLK, D)
    vsb = v_sel.reshape(B, G, nb, SEL_BLK, D)
    k_win_p = jnp.pad(k_win, ((0, 0), (0, 0), (WINDOW, 0), (0, 0)))
    v_win_p = jnp.pad(v_win, ((0, 0), (0, 0), (WINDOW, 0), (0, 0)))
    bi = jnp.arange(B)[:, None, None, None]
    gi = jnp.arange(G)[None, :, None, None]
    hb = jnp.arange(H_B).reshape(1, G, R, 1, 1)
    blk_ids = jnp.arange(nb)
    offs = jnp.arange(SEL_BLK)
    win_offs = jnp.arange(WINDOW + QC_B)
    n_chunks = Sp // QC_B
    q_chunks = q.reshape(B, G, R, n_chunks, QC_B, D).transpose(3, 0, 1, 2, 4, 5)
    g_chunks = gates.reshape(B, G, R, n_chunks, QC_B, 3).transpose(3, 0, 1, 2, 4, 5)

    def chunk(args):
        ci, qc, gc = args
        pos_q = ci * QC_B + jnp.arange(QC_B)
        l_c = jnp.einsum('bgrqd,bgnd->bgrqn', qc, k_cmp).astype(jnp.float32) * scale
        p_c = masked_softmax(l_c, cmp_end[None, :] <= pos_q[:, None])
        o_c = jnp.einsum('bgrqn,bgnd->bgrqd', p_c.astype(v_cmp.dtype), v_cmp)
        imp = jnp.einsum('bgrqn,nj->bgqj', p_c, overlap)
        cur = pos_q // SEL_BLK
        ok = blk_ids[None, :] <= cur[:, None]
        forced = (blk_ids[None, :] == 0) | (blk_ids[None, :] == cur[:, None]) | (blk_ids[None, :] == cur[:, None] - 1)
        score = jnp.where(ok, jnp.where(forced, BIG, imp), -jnp.inf)
        top_s, top_i = lax.top_k(score, n_sel)
        sel_ok = jnp.isfinite(top_s)
        kg = ksb[bi, gi, top_i]
        vg = vsb[bi, gi, top_i]
        n_keys = n_sel * SEL_BLK
        key_pos = (top_i[..., None] * SEL_BLK + offs).reshape(B, G, QC_B, n_keys)
        l_s = jnp.einsum('bgrqd,bgqnkd->bgrqnk', qc, kg).reshape(B, G, R, QC_B, n_keys).astype(jnp.float32) * scale
        l_s = l_s + tab[hb, t5_bucket(pos_q[:, None] - key_pos)[:, :, None]]
        m_s = ((key_pos <= pos_q[:, None]) & jnp.repeat(sel_ok, SEL_BLK, axis=-1))[:, :, None]
        p_s = masked_softmax(l_s, m_s).astype(v_sel.dtype)
        o_s = jnp.einsum('bgrqk,bgqkd->bgrqd', p_s, vg.reshape(B, G, QC_B, n_keys, D))
        kw = lax.dynamic_slice_in_dim(k_win_p, ci * QC_B, WINDOW + QC_B, axis=2)
        vw = lax.dynamic_slice_in_dim(v_win_p, ci * QC_B, WINDOW + QC_B, axis=2)
        w_pos = ci * QC_B - WINDOW + win_offs
        dist = pos_q[:, None] - w_pos[None, :]
        m_w = (dist >= 0) & (dist < WINDOW) & (w_pos[None, :] >= 0)
        l_w = jnp.einsum('bgrqd,bgkd->bgrqk', qc, kw).astype(jnp.float32) * scale
        l_w = l_w + tab[:, t5_bucket(dist)].reshape(G, R, QC_B, WINDOW + QC_B)
        p_w = masked_softmax(l_w, m_w).astype(v_win.dtype)
        o_w = jnp.einsum('bgrqk,bgkd->bgrqd', p_w, vw)
        return gc[..., 0:1] * o_c + gc[..., 1:2] * o_s + gc[..., 2:3] * o_w

    out = lax.map(chunk, (jnp.arange(n_chunks), q_chunks, g_chunks))
    return out.transpose(1, 0, 4, 2, 3, 5).reshape(B, Sp, G * R * D)


def setup_inputs(seed: int = 0) -> dict:
    key = jax.random.key(seed)
    ks = jax.random.split(key, 24)
    L, D, DH = DEPTH, D_MODEL, HEAD_DIM

    def nrm(k, shape, s):
        return jax.random.normal(k, shape, jnp.float32) * s

    def gain(k, shape):
        return 1.0 + 0.05 * jax.random.normal(k, shape, jnp.float32)

    return {
        'x': nrm(ks[0], (BATCH, SEQ, D), 1.0),
        'c': nrm(ks[1], (BATCH, D), 1.0),
        'rel_bias': nrm(ks[2], (N_BUCKETS, N_HEADS), 0.5),
        'w_ada': nrm(ks[3], (L, D, ADA_CHUNKS * D), 0.5 * D ** -0.5),
        'b_ada': nrm(ks[4], (L, ADA_CHUNKS * D), 0.01),
        'g_mix': gain(ks[5], (L, D)),
        'w_in': nrm(ks[6], (L, D, W_IN_COLS), D ** -0.5),
        'q_norm_a': gain(ks[7], (L, DH)),
        'k_norm_a': gain(ks[8], (L, DH)),
        'q_norm_b': gain(ks[9], (L, DH)),
        'k_norm_cmp': gain(ks[10], (L, DH)),
        'k_norm_sel': gain(ks[11], (L, DH)),
        'k_norm_win': gain(ks[12], (L, DH)),
        'cmp_pe_k': nrm(ks[13], (L, CMP_LEN, DH), 0.5),
        'cmp_w1_k': nrm(ks[14], (L, CMP_LEN * DH, CMP_HIDDEN), (CMP_LEN * DH) ** -0.5),
        'cmp_w2_k': nrm(ks[15], (L, CMP_HIDDEN, DH), CMP_HIDDEN ** -0.5),
        'cmp_pe_v': nrm(ks[16], (L, CMP_LEN, DH), 0.5),
        'cmp_w1_v': nrm(ks[17], (L, CMP_LEN * DH, CMP_HIDDEN), (CMP_LEN * DH) ** -0.5),
        'cmp_w2_v': nrm(ks[18], (L, CMP_HIDDEN, DH), CMP_HIDDEN ** -0.5),
        'w_out': nrm(ks[19], (L, D, D), D ** -0.5),
        'g_ffn': gain(ks[20], (L, D)),
        'w_gu': nrm(ks[21], (L, D, 2 * FFN_HIDDEN), D ** -0.5),
        'w_down': nrm(ks[22], (L, FFN_HIDDEN, D), FFN_HIDDEN ** -0.5),
    }


def reference(x, c, rel_bias, w_ada, b_ada, g_mix, w_in, q_norm_a, k_norm_a, q_norm_b,
              k_norm_cmp, k_norm_sel, k_norm_win, cmp_pe_k, cmp_w1_k, cmp_w2_k,
              cmp_pe_v, cmp_w1_v, cmp_w2_v, w_out, g_ffn, w_gu, w_down):
    B, S, D = x.shape
    DH = HEAD_DIM
    Sp = -(-S // PAD_MULT) * PAD_MULT
    tab = rel_bias.T
    tab_a, tab_b = tab[:H_A], tab[H_A:]
    sizes = [H_A * DH] * 3 + [H_B * DH] + [G_B * DH] * 6 + [3 * H_B]
    split_at = [int(v) for v in np.cumsum(sizes)[:-1]]

    def heads_a(t):
        return t.reshape(B, Sp, H_A, DH).transpose(0, 2, 1, 3)

    def kv_heads(t):
        return t.reshape(B, Sp, G_B, DH).transpose(0, 2, 1, 3)

    for l in range(DEPTH):
        mod = jax.nn.silu(c) @ w_ada[l] + b_ada[l]
        sh_m, sc_m, gt_m, sh_f, sc_f, gt_f = jnp.split(mod, ADA_CHUNKS, axis=-1)
        h = rmsnorm(x, g_mix[l]) * (1.0 + sc_m[:, None]) + sh_m[:, None]
        h = jnp.pad(h, ((0, 0), (0, Sp - S), (0, 0)))
        proj = h @ w_in[l]
        qa, ka, va, qb, kc, vc, ksl, vsl, kwn, vwn, gl = jnp.split(proj, split_at, axis=-1)
        qa = rmsnorm(heads_a(qa), q_norm_a[l])
        ka = rmsnorm(heads_a(ka), k_norm_a[l])
        o_a = moba_attention(qa, ka, heads_a(va), tab_a)
        qb = rmsnorm(qb.reshape(B, Sp, G_B, R_B, DH).transpose(0, 2, 3, 1, 4), q_norm_b[l])
        k_cmp = rmsnorm(compress(kv_heads(kc), cmp_pe_k[l], cmp_w1_k[l], cmp_w2_k[l]), k_norm_cmp[l])
        v_cmp = compress(kv_heads(vc), cmp_pe_v[l], cmp_w1_v[l], cmp_w2_v[l])
        k_s = rmsnorm(kv_heads(ksl), k_norm_sel[l])
        k_w = rmsnorm(kv_heads(kwn), k_norm_win[l])
        gates = jax.nn.sigmoid(gl.reshape(B, Sp, G_B, R_B, 3).transpose(0, 2, 3, 1, 4))
        o_b = nsa_attention(qb, k_cmp, v_cmp, k_s, kv_heads(vsl), k_w, kv_heads(vwn), gates, tab_b)
        mix = jnp.concatenate([o_a, o_b], axis=-1)[:, :S]
        x = x + gt_m[:, None] * (mix @ w_out[l])
        h = rmsnorm(x, g_ffn[l]) * (1.0 + sc_f[:, None]) + sh_f[:, None]
        gate, up = jnp.split(h @ w_gu[l], 2, axis=-1)
        x = x + gt_f[:, None] * ((jax.nn.silu(gate) * up) @ w_down[l])
    return x
```

```python
import functools
import math

import jax
import jax.numpy as jnp
import numpy as np
from jax import lax
from jax.experimental import pallas as pl
from jax.experimental.pallas import tpu as pltpu

F32 = jnp.float32
BF16 = jnp.bfloat16

HEAD_DIM = 64
LANES = 128
H_A = 8
H_B = 8
G_B = 2
R_B = H_B // G_B
BLK_A = 256
TOPK_A = 3
CMP_LEN = 32
CMP_STRIDE = 16
CMP_HIDDEN = 256
SEL_BLK = 64
SEL_TOPK = 16
WINDOW = 512
N_BUCKETS = 32
MAX_DIST = 128
ADA_CHUNKS = 6
NEG = -1e30
BIG = 1e9
EPS = 1e-6

_LOG2_BLK_A = BLK_A.bit_length() - 1
_LOG2_SEL_BLK = SEL_BLK.bit_length() - 1

TQ = 256
TM_IN = 512
TM_FFN = 512
VMEM_LIMIT = 56 * 1024 * 1024


def _dot(a, b):
    return jnp.dot(a, b, preferred_element_type=F32)


def _dot_nt(a, b):
    return lax.dot_general(a, b, (((1,), (1,)), ((), ())), preferred_element_type=F32)


def _split(a):
    hi = a.astype(BF16)
    lo = (a - hi.astype(F32)).astype(BF16)
    return hi, lo


def _dot_nt3(a, b):
    ah, al = _split(a)
    bh, bl = _split(b)
    return _dot_nt(ah, bh) + (_dot_nt(al, bh) + _dot_nt(ah, bl))


def _dot2(a, b_bf16):
    ah, al = _split(a)
    return _dot(ah, b_bf16) + _dot(al, b_bf16)


def _params(sem):
    return pltpu.CompilerParams(dimension_semantics=sem, vmem_limit_bytes=VMEM_LIMIT)


def _ada_kernel(c_ref, w_ref, b_ref, o_ref):
    c = c_ref[...]
    sc = c * jax.nn.sigmoid(c)
    w = w_ref[...]
    sh, sl = _split(sc)
    wh, wl = _split(w)
    o_ref[...] = _dot(sh, wh) + (_dot(sl, wh) + _dot(sh, wl)) + b_ref[...]


def _ada(c, w, b):
    bsz, d = c.shape
    n = w.shape[1]
    tn = 512
    return pl.pallas_call(
        _ada_kernel,
        grid=(n // tn,),
        in_specs=[pl.BlockSpec((bsz, d), lambda j: (0, 0)),
                  pl.BlockSpec((d, tn), lambda j: (0, j)),
                  pl.BlockSpec((1, tn), lambda j: (0, j))],
        out_specs=pl.BlockSpec((bsz, tn), lambda j: (0, j)),
        out_shape=jax.ShapeDtypeStruct((bsz, n), F32),
        compiler_params=_params(("arbitrary",)),
        name="ada",
    )(c, w, b.reshape(1, n))


def _rms_rows(xf, g):
    ms = jnp.mean(xf * xf, axis=-1, keepdims=True)
    return xf * lax.rsqrt(ms + EPS) * g


def _head_norm(t, bd, gain):
    ms = _dot2(t * t, bd)
    return t * lax.rsqrt(ms + EPS) * gain


def _inproj_kernel(x_ref, sc_ref, sh_ref, gmix_ref, w_ref, bd_ref, gqa_ref, gka_ref, gqb_ref,
                   gks_ref, gkw_ref,
                   qa_ref, kaug_ref, va_ref, kmean_ref, qb_ref, kc_ref, vc_ref, ksaug_ref,
                   vs_ref, kw_ref, vw_ref, gates_ref):
    tm = x_ref.shape[1]
    ti = pl.program_id(1)
    xf = x_ref[0]
    h = _rms_rows(xf, gmix_ref[...]) * (1.0 + sc_ref[0, 0]) + sh_ref[0, 0]
    hb = h.astype(BF16)

    def proj(c0, c1):
        return _dot(hb, w_ref[:, c0:c1])

    bd = bd_ref[...]
    bd2 = bd_ref[0:LANES, 0:LANES]
    lane = lax.broadcasted_iota(jnp.int32, (tm, LANES), 1)
    row = lax.broadcasted_iota(jnp.int32, (tm, LANES), 0) + ti * tm
    low = lane < HEAD_DIM
    high = lane >= HEAD_DIM

    def head_in_low(pair, odd):
        return pltpu.roll(pair, HEAD_DIM, 1) if odd else pair

    qa = _head_norm(proj(0, 512), bd, gqa_ref[...])
    for hd in range(H_A):
        pair = qa[:, (hd // 2) * LANES:(hd // 2 + 1) * LANES]
        qa_ref[0, hd] = jnp.where(low, head_in_low(pair, hd % 2), 0.0)

    ka = _head_norm(proj(512, 1024), bd, gka_ref[...])
    oh_a = jnp.where(lane - HEAD_DIM == (row >> _LOG2_BLK_A), 1.0, 0.0)
    for hd in range(H_A):
        pair = ka[:, (hd // 2) * LANES:(hd // 2 + 1) * LANES]
        kaug_ref[0, hd] = jnp.where(low, head_in_low(pair, hd % 2), oh_a).astype(BF16)
    for i in range(tm // BLK_A):
        kmean_ref[0, i] = jnp.mean(ka[i * BLK_A:(i + 1) * BLK_A], axis=0, keepdims=True)

    va_ref[0] = proj(1024, 1536).astype(BF16)

    qb = _head_norm(proj(1536, 2048), bd, gqb_ref[...])
    for hd in range(H_B):
        g = hd // R_B
        pair = qb[:, (hd // 2) * LANES:(hd // 2 + 1) * LANES]
        if hd % 2 != g:
            pair = pltpu.roll(pair, HEAD_DIM, 1)
        qb_ref[0, hd] = jnp.where(low if g == 0 else high, pair, 0.0)

    kc_ref[0] = proj(2048, 2176).astype(BF16)
    vc_ref[0] = proj(2176, 2304).astype(BF16)

    ks = _head_norm(proj(2304, 2432), bd2, gks_ref[...])
    oh_s = jnp.where(lane == (row >> _LOG2_SEL_BLK), 1.0, 0.0).astype(BF16)
    for g in range(G_B):
        kg = jnp.where(low if g == 0 else high, ks, 0.0).astype(BF16)
        ksaug_ref[0, g] = jnp.concatenate([kg, oh_s], axis=1)
    vs_ref[0] = proj(2432, 2560).astype(BF16)
    kw_ref[0] = _head_norm(proj(2560, 2688), bd2, gkw_ref[...]).astype(BF16)
    vw_ref[0] = proj(2688, 2816).astype(BF16)
    for g in range(G_B):
        gates_ref[0, g] = jax.nn.sigmoid(proj(2816 + g * LANES, 2816 + (g + 1) * LANES))


def _inproj(x, sc, sh, gmix, w, bd, gqa, gka, gqb, gks, gkw):
    bsz, s, d = x.shape
    tm = TM_IN
    nt = s // tm
    nba = s // BLK_A
    ncols = w.shape[1]
    const2 = lambda b, t: (0, 0)
    tok3 = lambda b, t: (b, t, 0)
    tok4 = lambda b, t: (b, 0, t, 0)
    in_specs = [
        pl.BlockSpec((1, tm, d), tok3),
        pl.BlockSpec((1, 1, 1, d), lambda b, t: (b, 1, 0, 0)),
        pl.BlockSpec((1, 1, 1, d), lambda b, t: (b, 0, 0, 0)),
        pl.BlockSpec((1, d), const2),
        pl.BlockSpec((d, ncols), const2),
        pl.BlockSpec((512, 512), const2),
        pl.BlockSpec((1, 512), const2),
        pl.BlockSpec((1, 512), const2),
        pl.BlockSpec((1, 512), const2),
        pl.BlockSpec((1, LANES), const2),
        pl.BlockSpec((1, LANES), const2),
    ]
    out_shape = [
        jax.ShapeDtypeStruct((bsz, H_A, s, LANES), F32),
        jax.ShapeDtypeStruct((bsz, H_A, s, LANES), BF16),
        jax.ShapeDtypeStruct((bsz, s, 512), BF16),
        jax.ShapeDtypeStruct((bsz, nba, 1, 512), F32),
        jax.ShapeDtypeStruct((bsz, H_B, s, LANES), F32),
        jax.ShapeDtypeStruct((bsz, s, LANES), BF16),
        jax.ShapeDtypeStruct((bsz, s, LANES), BF16),
        jax.ShapeDtypeStruct((bsz, G_B, s, 2 * LANES), BF16),
        jax.ShapeDtypeStruct((bsz, s, LANES), BF16),
        jax.ShapeDtypeStruct((bsz, s, LANES), BF16),
        jax.ShapeDtypeStruct((bsz, s, LANES), BF16),
        jax.ShapeDtypeStruct((bsz, G_B, s, LANES), F32),
    ]
    out_specs = [
        pl.BlockSpec((1, H_A, tm, LANES), tok4),
        pl.BlockSpec((1, H_A, tm, LANES), tok4),
        pl.BlockSpec((1, tm, 512), tok3),
        pl.BlockSpec((1, tm // BLK_A, 1, 512), lambda b, t: (b, t, 0, 0)),
        pl.BlockSpec((1, H_B, tm, LANES), tok4),
        pl.BlockSpec((1, tm, LANES), tok3),
        pl.BlockSpec((1, tm, LANES), tok3),
        pl.BlockSpec((1, G_B, tm, 2 * LANES), tok4),
        pl.BlockSpec((1, tm, LANES), tok3),
        pl.BlockSpec((1, tm, LANES), tok3),
        pl.BlockSpec((1, tm, LANES), tok3),
        pl.BlockSpec((1, G_B, tm, LANES), tok4),
    ]
    return pl.pallas_call(
        _inproj_kernel,
        grid=(bsz, nt),
        in_specs=in_specs,
        out_specs=out_specs,
        out_shape=out_shape,
        compiler_params=_params(("parallel", "parallel")),
        name="inproj",
    )(x, sc, sh, gmix, w, bd, gqa, gka, gqb, gks, gkw)


def _compress_kernel(ck_ref, cv_ref, w1_ref, w2_ref, pe_ref, gk_ref, bd_ref, ok_ref, ov_ref):
    half = CMP_STRIDE * HEAD_DIM
    outs = []
    for kv, c_ref in enumerate((ck_ref, cv_ref)):
        per_group = []
        for g in range(G_B):
            c = c_ref[0, g]
            a = _dot(c, w1_ref[kv, 0:half, :])
            b = _dot(c, w1_ref[kv, half:2 * half, :])
            n = a.shape[0]
            b_next = pltpu.roll(b, n - 1, 0)
            pe_term = _dot(pe_ref[kv], w1_ref[kv])[0:1]
            hid = jax.nn.gelu(a + b_next + pe_term)
            y = _dot(hid.astype(BF16), w2_ref[kv])
            per_group.append(y)
        outs.append(jnp.concatenate(per_group, axis=1))
    k = _head_norm(outs[0], bd_ref[...], gk_ref[...])
    ok_ref[0] = k.astype(BF16)
    ov_ref[0] = outs[1].astype(BF16)


def _compress(ck, cv, w1, w2, pe, gk, bd2):
    bsz, g, n, width = ck.shape
    blk = pl.BlockSpec((1, g, n, width), lambda b: (b, 0, 0, 0))
    full = lambda a: pl.BlockSpec(a.shape, lambda b: (0,) * a.ndim)
    out = pl.BlockSpec((1, n, LANES), lambda b: (b, 0, 0))
    return pl.pallas_call(
        _compress_kernel,
        grid=(bsz,),
        in_specs=[blk, blk, full(w1), full(w2), full(pe), full(gk), full(bd2)],
        out_specs=[out, out],
        out_shape=[jax.ShapeDtypeStruct((bsz, n, LANES), BF16)] * 2,
        compiler_params=_params(("parallel",)),
        name="compress",
    )(ck, cv, w1, w2, pe, gk, bd2)


def _flash_init(s, v):
    m = jnp.max(s, axis=1, keepdims=True)
    p = jnp.exp(s - m)
    return m, jnp.sum(p, axis=1, keepdims=True), _dot(p.astype(BF16), v)


def _flash_step(carry, s, v, bias=None):
    m, l, acc = carry
    smax = jnp.max(s, axis=1, keepdims=True)
    if bias is not None:
        smax = smax + bias
    m_new = jnp.maximum(m, smax)
    alpha = jnp.exp(m - m_new)
    p = jnp.exp(s - (m_new if bias is None else m_new - bias))
    l = alpha * l + jnp.sum(p, axis=1, keepdims=True)
    acc = alpha * acc + _dot(p.astype(BF16), v)
    return m_new, l, acc


def _topk_mask(score, index, k, axis):
    picked = jnp.zeros(score.shape, F32)
    for _ in range(k):
        mx = jnp.max(score, axis=axis, keepdims=True)
        cand = jnp.where(score == mx, index, jnp.int32(1 << 20))
        first = jnp.min(cand, axis=axis, keepdims=True)
        hit = index == first
        picked = jnp.where(hit, 1.0, picked)
        score = jnp.where(hit, -jnp.inf, score)
    return picked


def _moba_kernel(q_ref, k_ref, v_ref, km_ref, tab_ref, cb_ref, o_ref, *, n_sel):
    qi = pl.program_id(2)
    lane = lax.broadcasted_iota(jnp.int32, (TQ, LANES), 1)
    blk = lane - HEAD_DIM
    valid = (blk >= 0) & (blk < qi)
    outs = []
    for hh in range(2):
        q = q_ref[0, hh]
        gate = _dot_nt3(q, km_ref[0, hh])
        picked = _topk_mask(jnp.where(valid, gate, -jnp.inf), lane, n_sel, 1)
        keep = jnp.where(valid, picked, 0.0) + jnp.where(blk == qi, 1.0, 0.0)
        qaug = jnp.where(blk < 0, q, jnp.where(keep > 0.0, 0.0, NEG)).astype(BF16)
        cb = cb_ref[hh][0:1, 0:1]

        def scores(j):
            return _dot_nt(qaug, k_ref[0, hh, pl.ds(pl.multiple_of(j * TQ, TQ), TQ), :])

        def values(j):
            return v_ref[0, pl.ds(pl.multiple_of(j * TQ, TQ), TQ), :]

        carry = _flash_init(scores(qi) + tab_ref[hh, :, TQ:2 * TQ], values(qi))
        jp = jnp.maximum(qi - 1, 0)
        s_prev = scores(jp) + tab_ref[hh, :, 0:TQ] + jnp.where(qi >= 1, 0.0, NEG)
        carry = _flash_step(carry, s_prev, values(jp))

        def far(j, c):
            return _flash_step(c, scores(j), values(j), bias=cb)

        m, l, acc = lax.fori_loop(0, jnp.maximum(qi - 1, 0), far, carry)
        outs.append(acc / l)
    o_ref[0] = jnp.where(lane < HEAD_DIM, outs[0], outs[1]).astype(BF16)


def _moba(qa, kaug, va, km, tab, cb):
    bsz, _, s, _ = qa.shape
    nq = s // TQ
    n_sel = max(1, min(TOPK_A, s // BLK_A - 1))
    return pl.pallas_call(
        functools.partial(_moba_kernel, n_sel=n_sel),
        grid=(bsz, H_A // 2, nq),
        in_specs=[
            pl.BlockSpec((1, 2, TQ, LANES), lambda b, hp, qi: (b, hp, qi, 0)),
            pl.BlockSpec((1, 2, s, LANES), lambda b, hp, qi: (b, hp, 0, 0)),
            pl.BlockSpec((1, s, LANES), lambda b, hp, qi: (b, 0, hp)),
            pl.BlockSpec((1, 2, LANES, LANES), lambda b, hp, qi: (b, hp, 0, 0)),
            pl.BlockSpec((2, TQ, 2 * TQ), lambda b, hp, qi: (hp, 0, 0)),
            pl.BlockSpec((2, 8, LANES), lambda b, hp, qi: (hp, 0, 0)),
        ],
        out_specs=pl.BlockSpec((1, TQ, LANES), lambda b, hp, qi: (b, qi, hp)),
        out_shape=jax.ShapeDtypeStruct((bsz, s, H_A * HEAD_DIM), BF16),
        compiler_params=_params(("parallel", "parallel", "arbitrary")),
        name="moba",
    )(qa, kaug, va, km, tab, cb)


def _nsa_cmp_kernel(q_ref, kc_ref, vc_ref, ovt_ref, oc_ref, selb_ref, *, n_sel, n_cmp):
    qi = pl.program_id(2)
    ncp = kc_ref.shape[1]
    t0 = qi * TQ
    n_idx = lax.broadcasted_iota(jnp.int32, (TQ, ncp), 1)
    t_idx = lax.broadcasted_iota(jnp.int32, (TQ, ncp), 0) + t0
    mask = (n_idx * CMP_STRIDE + (CMP_LEN - 1) <= t_idx) & (n_idx < n_cmp)
    kc = kc_ref[0]
    vc = vc_ref[0]
    psum = jnp.zeros((TQ, ncp), F32)
    for r in range(R_B):
        q = q_ref[0, r].astype(BF16)
        z = jnp.where(mask, _dot_nt(q, kc), NEG)
        e = jnp.exp(z - jnp.max(z, axis=1, keepdims=True))
        p = jnp.where(mask, e / jnp.sum(e, axis=1, keepdims=True), 0.0)
        oc_ref[0, r] = _dot(p.astype(BF16), vc)
        psum = psum + p
    ph, pl_ = _split(psum)
    ovt = ovt_ref[...]
    imp_t = _dot_nt(ovt, ph) + _dot_nt(ovt, pl_)
    blk = lax.broadcasted_iota(jnp.int32, (LANES, TQ), 0)
    cur = (lax.broadcasted_iota(jnp.int32, (LANES, TQ), 1) + t0) >> _LOG2_SEL_BLK
    ok = blk <= cur
    forced = (blk == 0) | (blk == cur) | (blk == cur - 1)
    score = jnp.where(ok, jnp.where(forced, BIG, imp_t), -jnp.inf)
    picked = _topk_mask(score, blk, n_sel, 0)
    selb_t = jnp.where(ok & (picked > 0.0), 0.0, NEG)
    selb_ref[0, 0] = selb_t.T.astype(BF16)


def _nsa_cmp(qb, kcmp, vcmp, ovt, n_cmp):
    bsz, _, s, _ = qb.shape
    nq = s // TQ
    ncp = kcmp.shape[1]
    n_sel = min(SEL_TOPK, s // SEL_BLK)
    return pl.pallas_call(
        functools.partial(_nsa_cmp_kernel, n_sel=n_sel, n_cmp=n_cmp),
        grid=(bsz, G_B, nq),
        in_specs=[
            pl.BlockSpec((1, R_B, TQ, LANES), lambda b, g, qi: (b, g, qi, 0)),
            pl.BlockSpec((1, ncp, LANES), lambda b, g, qi: (b, 0, 0)),
            pl.BlockSpec((1, ncp, LANES), lambda b, g, qi: (b, 0, 0)),
            pl.BlockSpec((LANES, ncp), lambda b, g, qi: (0, 0)),
        ],
        out_specs=[
            pl.BlockSpec((1, R_B, TQ, LANES), lambda b, g, qi: (b, g, qi, 0)),
            pl.BlockSpec((1, 1, TQ, LANES), lambda b, g, qi: (b, g, qi, 0)),
        ],
        out_shape=[jax.ShapeDtypeStruct((bsz, H_B, s, LANES), F32),
                   jax.ShapeDtypeStruct((bsz, G_B, s, LANES), BF16)],
        compiler_params=_params(("parallel", "parallel", "parallel")),
        name="nsa_cmp",
    )(qb, kcmp, vcmp, ovt)


def _nsa_main_kernel(q_ref, selb_ref, ks_ref, vs_ref, kw_ref, vw_ref, oc_ref, gates_ref,
                     tabs_ref, tabw_ref, cb_ref, o_ref):
    g = pl.program_id(1)
    qi = pl.program_id(2)
    selb = selb_ref[0, 0]
    gates = gates_ref[0, 0]
    j1 = jnp.maximum(qi - 1, 0)
    j2 = jnp.maximum(qi - 2, 0)
    off1 = jnp.where(qi >= 1, 0.0, NEG)
    off2 = jnp.where(qi >= 2, 0.0, NEG)

    def rows(j):
        return pl.ds(pl.multiple_of(j * TQ, TQ), TQ)

    def group_half(o):
        return jnp.where(g == 0, o[:, 0:HEAD_DIM], o[:, HEAD_DIM:LANES])

    outs = []
    for r in range(R_B):
        q = q_ref[0, r].astype(BF16)
        qaug = jnp.concatenate([q, selb], axis=1)
        cb = cb_ref[r][0:1, 0:1]

        def s_sel(j):
            return _dot_nt(qaug, ks_ref[0, 0, rows(j), :])

        carry = _flash_init(s_sel(qi) + tabs_ref[r, :, TQ:2 * TQ], vs_ref[0, rows(qi), :])
        carry = _flash_step(carry, s_sel(j1) + tabs_ref[r, :, 0:TQ] + off1, vs_ref[0, rows(j1), :])

        def far(j, c):
            return _flash_step(c, s_sel(j), vs_ref[0, rows(j), :], bias=cb)

        m, l, acc = lax.fori_loop(0, j1, far, carry)
        o_s = group_half(acc / l)

        def s_win(j):
            return _dot_nt(q, kw_ref[0, rows(j), :])

        def b_win(k):
            return tabw_ref[r, :, k * TQ:(k + 1) * TQ]

        carry = _flash_init(s_win(qi) + b_win(2), vw_ref[0, rows(qi), :])
        carry = _flash_step(carry, s_win(j1) + b_win(1) + off1, vw_ref[0, rows(j1), :])
        m, l, acc = _flash_step(carry, s_win(j2) + b_win(0) + off2, vw_ref[0, rows(j2), :])
        o_w = group_half(acc / l)

        o_c = group_half(oc_ref[0, r])
        outs.append(gates[:, 3 * r:3 * r + 1] * o_c + gates[:, 3 * r + 1:3 * r + 2] * o_s
                    + gates[:, 3 * r + 2:3 * r + 3] * o_w)
    o_ref[0] = jnp.concatenate(outs, axis=1).astype(BF16)


def _nsa_main(qb, selb, ksaug, vs, kw, vw, oc, gates, tabs, tabw, cb):
    bsz, _, s, _ = qb.shape
    nq = s // TQ
    return pl.pallas_call(
        _nsa_main_kernel,
        grid=(bsz, G_B, nq),
        in_specs=[
            pl.BlockSpec((1, R_B, TQ, LANES), lambda b, g, qi: (b, g, qi, 0)),
            pl.BlockSpec((1, 1, TQ, LANES), lambda b, g, qi: (b, g, qi, 0)),
            pl.BlockSpec((1, 1, s, 2 * LANES), lambda b, g, qi: (b, g, 0, 0)),
            pl.BlockSpec((1, s, LANES), lambda b, g, qi: (b, 0, 0)),
            pl.BlockSpec((1, s, LANES), lambda b, g, qi: (b, 0, 0)),
            pl.BlockSpec((1, s, LANES), lambda b, g, qi: (b, 0, 0)),
            pl.BlockSpec((1, R_B, TQ, LANES), lambda b, g, qi: (b, g, qi, 0)),
            pl.BlockSpec((1, 1, TQ, LANES), lambda b, g, qi: (b, g, qi, 0)),
            pl.BlockSpec((R_B, TQ, 2 * TQ), lambda b, g, qi: (g, 0, 0)),
            pl.BlockSpec((R_B, TQ, 3 * TQ), lambda b, g, qi: (g, 0, 0)),
            pl.BlockSpec((R_B, 8, LANES), lambda b, g, qi: (g, 0, 0)),
        ],
        out_specs=pl.BlockSpec((1, TQ, R_B * HEAD_DIM), lambda b, g, qi: (b, qi, g)),
        out_shape=jax.ShapeDtypeStruct((bsz, s, H_B * HEAD_DIM), BF16),
        compiler_params=_params(("parallel", "parallel", "arbitrary")),
        name="nsa_main",
    )(qb, selb, ksaug, vs, kw, vw, oc, gates, tabs, tabw, cb)


def _out_ffn_kernel(x_ref, oa_ref, ob_ref, wo_ref, gtm_ref, gffn_ref, scf_ref, shf_ref, gtf_ref,
                    wg_ref, wu_ref, wd_ref, o_ref, x1_ref, h_ref, acc_ref):
    f = pl.program_id(2)
    half = oa_ref.shape[2]

    @pl.when(f == 0)
    def _():
        mix = _dot(oa_ref[0], wo_ref[0:half, :]) + _dot(ob_ref[0], wo_ref[half:2 * half, :])
        x1 = x_ref[0] + gtm_ref[0, 0] * mix
        x1_ref[...] = x1
        h = _rms_rows(x1, gffn_ref[...]) * (1.0 + scf_ref[0, 0]) + shf_ref[0, 0]
        h_ref[...] = h.astype(BF16)

    h = h_ref[...]
    gate = _dot(h, wg_ref[...])
    up = _dot(h, wu_ref[...])
    act = (gate * jax.nn.sigmoid(gate) * up).astype(BF16)
    part = _dot(act, wd_ref[...])

    @pl.when(f == 0)
    def _():
        acc_ref[...] = part

    @pl.when(f > 0)
    def _():
        acc_ref[...] += part

    @pl.when(f == pl.num_programs(2) - 1)
    def _():
        o_ref[0] = x1_ref[...] + gtf_ref[0, 0] * acc_ref[...]


def _out_ffn(x, oa, ob, wo, mod4, gffn, wgu, wd):
    bsz, s, d = x.shape
    fh = wd.shape[0]
    tm = TM_FFN
    tf = fh // 2 if (fh // 2) % LANES == 0 else fh
    nf = fh // tf
    tok = lambda b, t, f: (b, t, 0)
    modspec = lambda k: pl.BlockSpec((1, 1, 1, d), lambda b, t, f: (b, k, 0, 0))
    return pl.pallas_call(
        _out_ffn_kernel,
        grid=(bsz, s // tm, nf),
        in_specs=[
            pl.BlockSpec((1, tm, d), tok),
            pl.BlockSpec((1, tm, oa.shape[2]), tok),
            pl.BlockSpec((1, tm, ob.shape[2]), tok),
            pl.BlockSpec((d, d), lambda b, t, f: (0, 0)),
            modspec(2),
            pl.BlockSpec((1, d), lambda b, t, f: (0, 0)),
            modspec(4),
            modspec(3),
            modspec(5),
            pl.BlockSpec((d, tf), lambda b, t, f: (0, f)),
            pl.BlockSpec((d, tf), lambda b, t, f: (0, f + nf)),
            pl.BlockSpec((tf, d), lambda b, t, f: (f, 0)),
        ],
        out_specs=pl.BlockSpec((1, tm, d), tok),
        out_shape=jax.ShapeDtypeStruct((bsz, s, d), F32),
        scratch_shapes=[pltpu.VMEM((tm, d), F32), pltpu.VMEM((tm, d), BF16), pltpu.VMEM((tm, d), F32)],
        compiler_params=_params(("parallel", "parallel", "arbitrary")),
        name="out_ffn",
    )(x, oa, ob, wo, mod4, gffn, mod4, mod4, mod4, wgu, wgu, wd)


def _t5_bucket_np(d):
    max_exact = N_BUCKETS // 2
    d = np.maximum(d, 0)
    df = np.maximum(d, 1).astype(np.float64)
    large = max_exact + (np.log(df / max_exact) / math.log(MAX_DIST / max_exact)
                         * (N_BUCKETS - max_exact)).astype(np.int64)
    large = np.minimum(large, N_BUCKETS - 1)
    return np.where(d < max_exact, d, large).astype(np.int32)


def _bias_tables(rel_bias):
    tab = rel_bias.T.astype(F32)
    i = np.arange(TQ)[:, None]
    d_near = i + TQ - np.arange(2 * TQ)[None, :]
    near = jnp.where(jnp.asarray(d_near >= 0)[None], tab[:, _t5_bucket_np(d_near)], NEG)
    d_win = i + 2 * TQ - np.arange(3 * TQ)[None, :]
    ok_win = (d_win >= 0) & (d_win < WINDOW)
    win = jnp.where(jnp.asarray(ok_win)[None], tab[H_A:, _t5_bucket_np(d_win)], NEG)
    far = jnp.broadcast_to(tab[:, N_BUCKETS - 1][:, None, None], (tab.shape[0], 8, LANES))
    return near, win, far


def _overlap_t(n_cmp_pad, n_cmp):
    cs = np.arange(n_cmp_pad)[None, :] * CMP_STRIDE
    ss = np.arange(LANES)[:, None] * SEL_BLK
    ov = (cs < ss + SEL_BLK) & (cs + CMP_LEN > ss) & (np.arange(n_cmp_pad)[None, :] < n_cmp)
    return jnp.asarray(ov.astype(np.float32), BF16)


def _block_diag(n):
    m = (np.arange(n)[:, None] // HEAD_DIM == np.arange(n)[None, :] // HEAD_DIM)
    return jnp.asarray(m.astype(np.float32) / HEAD_DIM, BF16)


def kernel(x, c, rel_bias, w_ada, b_ada, g_mix, w_in, q_norm_a, k_norm_a, q_norm_b, k_norm_cmp,
           k_norm_sel, k_norm_win, cmp_pe_k, cmp_w1_k, cmp_w2_k, cmp_pe_v, cmp_w1_v, cmp_w2_v,
           w_out, g_ffn, w_gu, w_down):
    bsz, s, d = x.shape
    depth = w_ada.shape[0]
    assert s % TM_IN == 0 and s % TM_FFN == 0 and s % TQ == 0
    assert s // BLK_A <= HEAD_DIM and s // SEL_BLK <= LANES
    assert WINDOW == 2 * TQ and BLK_A == TQ and MAX_DIST <= TQ
    n_chunks = s // CMP_STRIDE
    n_cmp = (s - CMP_LEN) // CMP_STRIDE + 1
    scale = HEAD_DIM ** -0.5

    near, win, far = _bias_tables(rel_bias)
    ovt = _overlap_t(n_chunks, n_cmp)
    bd = _block_diag(512)
    bd2 = _block_diag(LANES)
    tile = lambda g, n: jnp.tile(g.astype(F32), n).reshape(1, -1)

    for l in range(depth):
        mod = _ada(c, w_ada[l], b_ada[l])
        mod4 = mod.reshape(bsz, ADA_CHUNKS, 1, d)

        wl = w_in[l]
        n_main = wl.shape[1] - 3 * H_B
        gl = wl[:, n_main:].reshape(d, G_B, 3 * R_B)
        gl = jnp.pad(gl, ((0, 0), (0, 0), (0, LANES - 3 * R_B))).reshape(d, G_B * LANES)
        w_in_p = jnp.concatenate([wl[:, :n_main], gl], axis=1).astype(BF16)

        (qa, kaug, va, kmean, qb, kc, vc, ksaug, vs, kw, vw, gates) = _inproj(
            x, mod4, mod4, g_mix[l].reshape(1, d), w_in_p, bd,
            tile(q_norm_a[l], H_A) * scale, tile(k_norm_a[l], H_A), tile(q_norm_b[l], H_B) * scale,
            tile(k_norm_sel[l], G_B), tile(k_norm_win[l], G_B))

        nba = s // BLK_A
        km = kmean.reshape(bsz, nba, H_A, HEAD_DIM).transpose(0, 2, 1, 3)
        km = jnp.pad(km, ((0, 0), (0, 0), (HEAD_DIM, LANES - HEAD_DIM - nba), (0, LANES - HEAD_DIM)))
        o_a = _moba(qa, kaug, va, km, near[:H_A], far[:H_A])

        chunks = lambda t: t.reshape(bsz, s, G_B, HEAD_DIM).transpose(0, 2, 1, 3).reshape(
            bsz, G_B, n_chunks, CMP_STRIDE * HEAD_DIM)
        w1 = jnp.stack([cmp_w1_k[l], cmp_w1_v[l]]).astype(BF16)
        w2 = jnp.stack([cmp_w2_k[l], cmp_w2_v[l]]).astype(BF16)
        pe = jnp.stack([cmp_pe_k[l], cmp_pe_v[l]]).reshape(2, 1, CMP_LEN * HEAD_DIM)
        pe = jnp.broadcast_to(pe, (2, 8, CMP_LEN * HEAD_DIM)).astype(BF16)
        kcmp, vcmp = _compress(chunks(kc), chunks(vc), w1, w2, pe, tile(k_norm_cmp[l], G_B), bd2)

        oc, selb = _nsa_cmp(qb, kcmp, vcmp, ovt, n_cmp)
        o_b = _nsa_main(qb, selb, ksaug, vs, kw, vw, oc, gates, near[H_A:], win, far[H_A:])

        x = _out_ffn(x, o_a, o_b, w_out[l].astype(BF16), mod4, g_ffn[l].reshape(1, d),
                     w_gu[l].astype(BF16), w_down[l].astype(BF16))
    return x
```

```python
import functools
import math

import jax
import jax.numpy as jnp
import numpy as np
from jax import lax
from jax.experimental import pallas as pl
from jax.experimental.pallas import tpu as pltpu

F32 = jnp.float32
BF16 = jnp.bfloat16

HEAD_DIM = 64
LANES = 128
H_A = 8
H_B = 8
G_B = 2
R_B = H_B // G_B
BLK_A = 256
TOPK_A = 3
CMP_LEN = 32
CMP_STRIDE = 16
CMP_HIDDEN = 256
SEL_BLK = 64
SEL_TOPK = 16
WINDOW = 512
N_BUCKETS = 32
MAX_DIST = 128
ADA_CHUNKS = 6
NEG = -1e30
BIG = 1e9
EPS = 1e-6

_LOG2_BLK_A = BLK_A.bit_length() - 1
_LOG2_SEL_BLK = SEL_BLK.bit_length() - 1

TQ = 256
TM_IN = 512
TM_FFN = 512
VMEM_LIMIT = 56 * 1024 * 1024


def _dot(a, b):
    return jnp.dot(a, b, preferred_element_type=F32)


def _dot_nt(a, b):
    return lax.dot_general(a, b, (((1,), (1,)), ((), ())), preferred_element_type=F32)


def _split(a):
    hi = a.astype(BF16)
    lo = (a - hi.astype(F32)).astype(BF16)
    return hi, lo


def _dot_nt3(a, b):
    ah, al = _split(a)
    bh, bl = _split(b)
    return _dot_nt(ah, bh) + (_dot_nt(al, bh) + _dot_nt(ah, bl))


def _dot2(a, b_bf16):
    ah, al = _split(a)
    return _dot(ah, b_bf16) + _dot(al, b_bf16)


def _params(sem):
    return pltpu.CompilerParams(dimension_semantics=sem, vmem_limit_bytes=VMEM_LIMIT)


def _ada_kernel(c_ref, w_ref, b_ref, o_ref):
    c = c_ref[...]
    sc = c * jax.nn.sigmoid(c)
    w = w_ref[...]
    sh, sl = _split(sc)
    wh, wl = _split(w)
    o_ref[...] = _dot(sh, wh) + (_dot(sl, wh) + _dot(sh, wl)) + b_ref[...]


def _ada(c, w, b):
    bsz, d = c.shape
    n = w.shape[1]
    tn = 512
    return pl.pallas_call(
        _ada_kernel,
        grid=(n // tn,),
        in_specs=[pl.BlockSpec((bsz, d), lambda j: (0, 0)),
                  pl.BlockSpec((d, tn), lambda j: (0, j)),
                  pl.BlockSpec((1, tn), lambda j: (0, j))],
        out_specs=pl.BlockSpec((bsz, tn), lambda j: (0, j)),
        out_shape=jax.ShapeDtypeStruct((bsz, n), F32),
        compiler_params=_params(("arbitrary",)),
        name="ada",
    )(c, w, b.reshape(1, n))


def _rms_rows(xf, g):
    ms = jnp.mean(xf * xf, axis=-1, keepdims=True)
    return xf * lax.rsqrt(ms + EPS) * g


def _head_norm(t, bd, gain):
    ms = _dot2(t * t, bd)
    return t * lax.rsqrt(ms + EPS) * gain


def _inproj_kernel(x_ref, sc_ref, sh_ref, gmix_ref, w_ref, bd_ref, gqa_ref, gka_ref, gqb_ref,
                   gks_ref, gkw_ref,
                   qa_ref, kaug_ref, va_ref, kmean_ref, qb_ref, kc_ref, vc_ref, ksaug_ref,
                   vs_ref, kw_ref, vw_ref, gates_ref):
    tm = x_ref.shape[1]
    ti = pl.program_id(1)
    xf = x_ref[0]
    h = _rms_rows(xf, gmix_ref[...]) * (1.0 + sc_ref[0, 0]) + sh_ref[0, 0]
    hb = h.astype(BF16)

    def proj(c0, c1):
        return _dot(hb, w_ref[:, c0:c1])

    bd = bd_ref[...]
    bd2 = bd_ref[0:LANES, 0:LANES]
    lane = lax.broadcasted_iota(jnp.int32, (tm, LANES), 1)
    row = lax.broadcasted_iota(jnp.int32, (tm, LANES), 0) + ti * tm
    low = lane < HEAD_DIM
    high = lane >= HEAD_DIM

    def head_in_low(pair, odd):
        return pltpu.roll(pair, HEAD_DIM, 1) if odd else pair

    qa = _head_norm(proj(0, 512), bd, gqa_ref[...])
    for hd in range(H_A):
        pair = qa[:, (hd // 2) * LANES:(hd // 2 + 1) * LANES]
        qa_ref[0, hd] = jnp.where(low, head_in_low(pair, hd % 2), 0.0)

    ka = _head_norm(proj(512, 1024), bd, gka_ref[...])
    oh_a = jnp.where(lane - HEAD_DIM == (row >> _LOG2_BLK_A), 1.0, 0.0)
    for hd in range(H_A):
        pair = ka[:, (hd // 2) * LANES:(hd // 2 + 1) * LANES]
        kaug_ref[0, hd] = jnp.where(low, head_in_low(pair, hd % 2), oh_a).astype(BF16)
    for i in range(tm // BLK_A):
        kmean_ref[0, i] = jnp.mean(ka[i * BLK_A:(i + 1) * BLK_A], axis=0, keepdims=True)

    va_ref[0] = proj(1024, 1536).astype(BF16)

    qb = _head_norm(proj(1536, 2048), bd, gqb_ref[...])
    for hd in range(H_B):
        g = hd // R_B
        pair = qb[:, (hd // 2) * LANES:(hd // 2 + 1) * LANES]
        if hd % 2 != g:
            pair = pltpu.roll(pair, HEAD_DIM, 1)
        qb_ref[0, hd] = jnp.where(low if g == 0 else high, pair, 0.0)

    kc_ref[0] = proj(2048, 2176).astype(BF16)
    vc_ref[0] = proj(2176, 2304).astype(BF16)

    ks = _head_norm(proj(2304, 2432), bd2, gks_ref[...])
    oh_s = jnp.where(lane == (row >> _LOG2_SEL_BLK), 1.0, 0.0).astype(BF16)
    for g in range(G_B):
        kg = jnp.where(low if g == 0 else high, ks, 0.0).astype(BF16)
        ksaug_ref[0, g] = jnp.concatenate([kg, oh_s], axis=1)
    vs_ref[0] = proj(2432, 2560).astype(BF16)
    kw_ref[0] = _head_norm(proj(2560, 2688), bd2, gkw_ref[...]).astype(BF16)
    vw_ref[0] = proj(2688, 2816).astype(BF16)
    for g in range(G_B):
        gates_ref[0, g] = jax.nn.sigmoid(proj(2816 + g * LANES, 2816 + (g + 1) * LANES))


def _inproj(x, sc, sh, gmix, w, bd, gqa, gka, gqb, gks, gkw):
    bsz, s, d = x.shape
    tm = TM_IN
    nt = s // tm
    nba = s // BLK_A
    ncols = w.shape[1]
    const2 = lambda b, t: (0, 0)
    tok3 = lambda b, t: (b, t, 0)
    tok4 = lambda b, t: (b, 0, t, 0)
    in_specs = [
        pl.BlockSpec((1, tm, d), tok3),
        pl.BlockSpec((1, 1, 1, d), lambda b, t: (b, 1, 0, 0)),
        pl.BlockSpec((1, 1, 1, d), lambda b, t: (b, 0, 0, 0)),
        pl.BlockSpec((1, d), const2),
        pl.BlockSpec((d, ncols), const2),
        pl.BlockSpec((512, 512), const2),
        pl.BlockSpec((1, 512), const2),
        pl.BlockSpec((1, 512), const2),
        pl.BlockSpec((1, 512), const2),
        pl.BlockSpec((1, LANES), const2),
        pl.BlockSpec((1, LANES), const2),
    ]
    out_shape = [
        jax.ShapeDtypeStruct((bsz, H_A, s, LANES), F32),
        jax.ShapeDtypeStruct((bsz, H_A, s, LANES), BF16),
        jax.ShapeDtypeStruct((bsz, s, 512), BF16),
        jax.ShapeDtypeStruct((bsz, nba, 1, 512), F32),
        jax.ShapeDtypeStruct((bsz, H_B, s, LANES), F32),
        jax.ShapeDtypeStruct((bsz, s, LANES), BF16),
        jax.ShapeDtypeStruct((bsz, s, LANES), BF16),
        jax.ShapeDtypeStruct((bsz, G_B, s, 2 * LANES), BF16),
        jax.ShapeDtypeStruct((bsz, s, LANES), BF16),
        jax.ShapeDtypeStruct((bsz, s, LANES), BF16),
        jax.ShapeDtypeStruct((bsz, s, LANES), BF16),
        jax.ShapeDtypeStruct((bsz, G_B, s, LANES), F32),
    ]
    out_specs = [
        pl.BlockSpec((1, H_A, tm, LANES), tok4),
        pl.BlockSpec((1, H_A, tm, LANES), tok4),
        pl.BlockSpec((1, tm, 512), tok3),
        pl.BlockSpec((1, tm // BLK_A, 1, 512), lambda b, t: (b, t, 0, 0)),
        pl.BlockSpec((1, H_B, tm, LANES), tok4),
        pl.BlockSpec((1, tm, LANES), tok3),
        pl.BlockSpec((1, tm, LANES), tok3),
        pl.BlockSpec((1, G_B, tm, 2 * LANES), tok4),
        pl.BlockSpec((1, tm, LANES), tok3),
        pl.BlockSpec((1, tm, LANES), tok3),
        pl.BlockSpec((1, tm, LANES), tok3),
        pl.BlockSpec((1, G_B, tm, LANES), tok4),
    ]
    return pl.pallas_call(
        _inproj_kernel,
        grid=(bsz, nt),
        in_specs=in_specs,
        out_specs=out_specs,
        out_shape=out_shape,
        compiler_params=_params(("parallel", "parallel")),
        name="inproj",
    )(x, sc, sh, gmix, w, bd, gqa, gka, gqb, gks, gkw)


def _compress_kernel(ck_ref, cv_ref, w1_ref, w2_ref, pe_ref, gk_ref, bd_ref, ok_ref, ov_ref):
    half = CMP_STRIDE * HEAD_DIM
    outs = []
    for kv, c_ref in enumerate((ck_ref, cv_ref)):
        per_group = []
        for g in range(G_B):
            c = c_ref[0, g]
            a = _dot(c, w1_ref[kv, 0:half, :])
            b = _dot(c, w1_ref[kv, half:2 * half, :])
            n = a.shape[0]
            b_next = pltpu.roll(b, n - 1, 0)
            pe_term = _dot(pe_ref[kv], w1_ref[kv])[0:1]
            hid = jax.nn.gelu(a + b_next + pe_term)
            y = _dot(hid.astype(BF16), w2_ref[kv])
            per_group.append(y)
        outs.append(jnp.concatenate(per_group, axis=1))
    k = _head_norm(outs[0], bd_ref[...], gk_ref[...])
    ok_ref[0] = k.astype(BF16)
    ov_ref[0] = outs[1].astype(BF16)


def _compress(ck, cv, w1, w2, pe, gk, bd2):
    bsz, g, n, width = ck.shape
    blk = pl.BlockSpec((1, g, n, width), lambda b: (b, 0, 0, 0))
    full = lambda a: pl.BlockSpec(a.shape, lambda b: (0,) * a.ndim)
    out = pl.BlockSpec((1, n, LANES), lambda b: (b, 0, 0))
    return pl.pallas_call(
        _compress_kernel,
        grid=(bsz,),
        in_specs=[blk, blk, full(w1), full(w2), full(pe), full(gk), full(bd2)],
        out_specs=[out, out],
        out_shape=[jax.ShapeDtypeStruct((bsz, n, LANES), BF16)] * 2,
        compiler_params=_params(("parallel",)),
        name="compress",
    )(ck, cv, w1, w2, pe, gk, bd2)


def _flash_init(s, v):
    m = jnp.max(s, axis=1, keepdims=True)
    p = jnp.exp(s - m)
    return m, jnp.sum(p, axis=1, keepdims=True), _dot(p.astype(BF16), v)


def _flash_step(carry, s, v, bias=None):
    m, l, acc = carry
    smax = jnp.max(s, axis=1, keepdims=True)
    if bias is not None:
        smax = smax + bias
    m_new = jnp.maximum(m, smax)
    alpha = jnp.exp(m - m_new)
    p = jnp.exp(s - (m_new if bias is None else m_new - bias))
    l = alpha * l + jnp.sum(p, axis=1, keepdims=True)
    acc = alpha * acc + _dot(p.astype(BF16), v)
    return m_new, l, acc


def _topk_mask(score, index, k, axis):
    picked = jnp.zeros(score.shape, F32)
    for _ in range(k):
        mx = jnp.max(score, axis=axis, keepdims=True)
        cand = jnp.where(score == mx, index, jnp.int32(1 << 20))
        first = jnp.min(cand, axis=axis, keepdims=True)
        hit = index == first
        picked = jnp.where(hit, 1.0, picked)
        score = jnp.where(hit, -jnp.inf, score)
    return picked


def _moba_kernel(q_ref, k_ref, v_ref, km_ref, tab_ref, cb_ref, o_ref, *, n_sel):
    qi = pl.program_id(2)
    lane = lax.broadcasted_iota(jnp.int32, (TQ, LANES), 1)
    blk = lane - HEAD_DIM
    valid = (blk >= 0) & (blk < qi)
    outs = []
    for hh in range(2):
        q = q_ref[0, hh]
        gate = _dot_nt3(q, km_ref[0, hh])
        picked = _topk_mask(jnp.where(valid, gate, -jnp.inf), lane, n_sel, 1)
        keep = jnp.where(valid, picked, 0.0) + jnp.where(blk == qi, 1.0, 0.0)
        qaug = jnp.where(blk < 0, q, jnp.where(keep > 0.0, 0.0, NEG)).astype(BF16)
        cb = cb_ref[hh][0:1, 0:1]

        def scores(j):
            return _dot_nt(qaug, k_ref[0, hh, pl.ds(pl.multiple_of(j * TQ, TQ), TQ), :])

        def values(j):
            return v_ref[0, pl.ds(pl.multiple_of(j * TQ, TQ), TQ), :]

        carry = _flash_init(scores(qi) + tab_ref[hh, :, TQ:2 * TQ], values(qi))
        jp = jnp.maximum(qi - 1, 0)
        s_prev = scores(jp) + tab_ref[hh, :, 0:TQ] + jnp.where(qi >= 1, 0.0, NEG)
        carry = _flash_step(carry, s_prev, values(jp))

        def far(j, c):
            return _flash_step(c, scores(j), values(j), bias=cb)

        m, l, acc = lax.fori_loop(0, jnp.maximum(qi - 1, 0), far, carry)
        outs.append(acc / l)
    o_ref[0] = jnp.where(lane < HEAD_DIM, outs[0], outs[1]).astype(BF16)


def _moba(qa, kaug, va, km, tab, cb):
    bsz, _, s, _ = qa.shape
    nq = s // TQ
    n_sel = max(1, min(TOPK_A, s // BLK_A - 1))
    return pl.pallas_call(
        functools.partial(_moba_kernel, n_sel=n_sel),
        grid=(bsz, H_A // 2, nq),
        in_specs=[
            pl.BlockSpec((1, 2, TQ, LANES), lambda b, hp, qi: (b, hp, qi, 0)),
            pl.BlockSpec((1, 2, s, LANES), lambda b, hp, qi: (b, hp, 0, 0)),
            pl.BlockSpec((1, s, LANES), lambda b, hp, qi: (b, 0, hp)),
            pl.BlockSpec((1, 2, LANES, LANES), lambda b, hp, qi: (b, hp, 0, 0)),
            pl.BlockSpec((2, TQ, 2 * TQ), lambda b, hp, qi: (hp, 0, 0)),
            pl.BlockSpec((2, 8, LANES), lambda b, hp, qi: (hp, 0, 0)),
        ],
        out_specs=pl.BlockSpec((1, TQ, LANES), lambda b, hp, qi: (b, qi, hp)),
        out_shape=jax.ShapeDtypeStruct((bsz, s, H_A * HEAD_DIM), BF16),
        compiler_params=_params(("parallel", "parallel", "arbitrary")),
        name="moba",
    )(qa, kaug, va, km, tab, cb)


def _nsa_cmp_kernel(q_ref, kc_ref, vc_ref, ovt_ref, oc_ref, selb_ref, *, n_sel, n_cmp):
    qi = pl.program_id(2)
    ncp = kc_ref.shape[1]
    t0 = qi * TQ
    n_idx = lax.broadcasted_iota(jnp.int32, (TQ, ncp), 1)
    t_idx = lax.broadcasted_iota(jnp.int32, (TQ, ncp), 0) + t0
    mask = (n_idx * CMP_STRIDE + (CMP_LEN - 1) <= t_idx) & (n_idx < n_cmp)
    kc = kc_ref[0]
    vc = vc_ref[0]
    psum = jnp.zeros((TQ, ncp), F32)
    for r in range(R_B):
        q = q_ref[0, r].astype(BF16)
        z = jnp.where(mask, _dot_nt(q, kc), NEG)
        e = jnp.exp(z - jnp.max(z, axis=1, keepdims=True))
        p = jnp.where(mask, e / jnp.sum(e, axis=1, keepdims=True), 0.0)
        oc_ref[0, r] = _dot(p.astype(BF16), vc)
        psum = psum + p
    ph, pl_ = _split(psum)
    ovt = ovt_ref[...]
    imp_t = _dot_nt(ovt, ph) + _dot_nt(ovt, pl_)
    blk = lax.broadcasted_iota(jnp.int32, (LANES, TQ), 0)
    cur = (lax.broadcasted_iota(jnp.int32, (LANES, TQ), 1) + t0) >> _LOG2_SEL_BLK
    ok = blk <= cur
    forced = (blk == 0) | (blk == cur) | (blk == cur - 1)
    score = jnp.where(ok, jnp.where(forced, BIG, imp_t), -jnp.inf)
    picked = _topk_mask(score, blk, n_sel, 0)
    selb_t = jnp.where(ok & (picked > 0.0), 0.0, NEG)
    selb_ref[0, 0] = selb_t.T.astype(BF16)


def _nsa_cmp(qb, kcmp, vcmp, ovt, n_cmp):
    bsz, _, s, _ = qb.shape
    nq = s // TQ
    ncp = kcmp.shape[1]
    n_sel = min(SEL_TOPK, s // SEL_BLK)
    return pl.pallas_call(
        functools.partial(_nsa_cmp_kernel, n_sel=n_sel, n_cmp=n_cmp),
        grid=(bsz, G_B, nq),
        in_specs=[
            pl.BlockSpec((1, R_B, TQ, LANES), lambda b, g, qi: (b, g, qi, 0)),
            pl.BlockSpec((1, ncp, LANES), lambda b, g, qi: (b, 0, 0)),
            pl.BlockSpec((1, ncp, LANES), lambda b, g, qi: (b, 0, 0)),
            pl.BlockSpec((LANES, ncp), lambda b, g, qi: (0, 0)),
        ],
        out_specs=[
            pl.BlockSpec((1, R_B, TQ, LANES), lambda b, g, qi: (b, g, qi, 0)),
            pl.BlockSpec((1, 1, TQ, LANES), lambda b, g, qi: (b, g, qi, 0)),
        ],
        out_shape=[jax.ShapeDtypeStruct((bsz, H_B, s, LANES), F32),
                   jax.ShapeDtypeStruct((bsz, G_B, s, LANES), BF16)],
        compiler_params=_params(("parallel", "parallel", "parallel")),
        name="nsa_cmp",
    )(qb, kcmp, vcmp, ovt)


def _nsa_main_kernel(q_ref, selb_ref, ks_ref, vs_ref, kw_ref, vw_ref, oc_ref, gates_ref,
                     tabs_ref, tabw_ref, cb_ref, o_ref):
    g = pl.program_id(1)
    qi = pl.program_id(2)
    selb = selb_ref[0, 0]
    gates = gates_ref[0, 0]
    j1 = jnp.maximum(qi - 1, 0)
    j2 = jnp.maximum(qi - 2, 0)
    off1 = jnp.where(qi >= 1, 0.0, NEG)
    off2 = jnp.where(qi >= 2, 0.0, NEG)

    def rows(j):
        return pl.ds(pl.multiple_of(j * TQ, TQ), TQ)

    def group_half(o):
        return jnp.where(g == 0, o[:, 0:HEAD_DIM], o[:, HEAD_DIM:LANES])

    outs = []
    for r in range(R_B):
        q = q_ref[0, r].astype(BF16)
        qaug = jnp.concatenate([q, selb], axis=1)
        cb = cb_ref[r][0:1, 0:1]

        def s_sel(j):
            return _dot_nt(qaug, ks_ref[0, 0, rows(j), :])

        carry = _flash_init(s_sel(qi) + tabs_ref[r, :, TQ:2 * TQ], vs_ref[0, rows(qi), :])
        carry = _flash_step(carry, s_sel(j1) + tabs_ref[r, :, 0:TQ] + off1, vs_ref[0, rows(j1), :])

        def far(j, c):
            return _flash_step(c, s_sel(j), vs_ref[0, rows(j), :], bias=cb)

        m, l, acc = lax.fori_loop(0, j1, far, carry)
        o_s = group_half(acc / l)

        def s_win(j):
            return _dot_nt(q, kw_ref[0, rows(j), :])

        def b_win(k):
            return tabw_ref[r, :, k * TQ:(k + 1) * TQ]

        carry = _flash_init(s_win(qi) + b_win(2), vw_ref[0, rows(qi), :])
        carry = _flash_step(carry, s_win(j1) + b_win(1) + off1, vw_ref[0, rows(j1), :])
        m, l, acc = _flash_step(carry, s_win(j2) + b_win(0) + off2, vw_ref[0, rows(j2), :])
        o_w = group_half(acc / l)

        o_c = group_half(oc_ref[0, r])
        outs.append(gates[:, 3 * r:3 * r + 1] * o_c + gates[:, 3 * r + 1:3 * r + 2] * o_s
                    + gates[:, 3 * r + 2:3 * r + 3] * o_w)
    o_ref[0] = jnp.concatenate(outs, axis=1).astype(BF16)


def _nsa_main(qb, selb, ksaug, vs, kw, vw, oc, gates, tabs, tabw, cb):
    bsz, _, s, _ = qb.shape
    nq = s // TQ
    return pl.pallas_call(
        _nsa_main_kernel,
        grid=(bsz, G_B, nq),
        in_specs=[
            pl.BlockSpec((1, R_B, TQ, LANES), lambda b, g, qi: (b, g, qi, 0)),
            pl.BlockSpec((1, 1, TQ, LANES), lambda b, g, qi: (b, g, qi, 0)),
            pl.BlockSpec((1, 1, s, 2 * LANES), lambda b, g, qi: (b, g, 0, 0)),
            pl.BlockSpec((1, s, LANES), lambda b, g, qi: (b, 0, 0)),
            pl.BlockSpec((1, s, LANES), lambda b, g, qi: (b, 0, 0)),
            pl.BlockSpec((1, s, LANES), lambda b, g, qi: (b, 0, 0)),
            pl.BlockSpec((1, R_B, TQ, LANES), lambda b, g, qi: (b, g, qi, 0)),
            pl.BlockSpec((1, 1, TQ, LANES), lambda b, g, qi: (b, g, qi, 0)),
            pl.BlockSpec((R_B, TQ, 2 * TQ), lambda b, g, qi: (g, 0, 0)),
            pl.BlockSpec((R_B, TQ, 3 * TQ), lambda b, g, qi: (g, 0, 0)),
            pl.BlockSpec((R_B, 8, LANES), lambda b, g, qi: (g, 0, 0)),
        ],
        out_specs=pl.BlockSpec((1, TQ, R_B * HEAD_DIM), lambda b, g, qi: (b, qi, g)),
        out_shape=jax.ShapeDtypeStruct((bsz, s, H_B * HEAD_DIM), BF16),
        compiler_params=_params(("parallel", "parallel", "arbitrary")),
        name="nsa_main",
    )(qb, selb, ksaug, vs, kw, vw, oc, gates, tabs, tabw, cb)


def _out_ffn_kernel(x_ref, oa_ref, ob_ref, wo_ref, gtm_ref, gffn_ref, scf_ref, shf_ref, gtf_ref,
                    wg_ref, wu_ref, wd_ref, o_ref, x1_ref, h_ref, acc_ref):
    f = pl.program_id(2)
    half = oa_ref.shape[2]

    @pl.when(f == 0)
    def _():
        mix = _dot(oa_ref[0], wo_ref[0:half, :]) + _dot(ob_ref[0], wo_ref[half:2 * half, :])
        x1 = x_ref[0] + gtm_ref[0, 0] * mix
        x1_ref[...] = x1
        h = _rms_rows(x1, gffn_ref[...]) * (1.0 + scf_ref[0, 0]) + shf_ref[0, 0]
        h_ref[...] = h.astype(BF16)

    h = h_ref[...]
    gate = _dot(h, wg_ref[...])
    up = _dot(h, wu_ref[...])
    act = (gate * jax.nn.sigmoid(gate) * up).astype(BF16)
    part = _dot(act, wd_ref[...])

    @pl.when(f == 0)
    def _():
        acc_ref[...] = part

    @pl.when(f > 0)
    def _():
        acc_ref[...] += part

    @pl.when(f == pl.num_programs(2) - 1)
    def _():
        o_ref[0] = x1_ref[...] + gtf_ref[0, 0] * acc_ref[...]


def _out_ffn(x, oa, ob, wo, mod4, gffn, wgu, wd):
    bsz, s, d = x.shape
    fh = wd.shape[0]
    tm = TM_FFN
    tf = fh // 2 if (fh // 2) % LANES == 0 else fh
    nf = fh // tf
    tok = lambda b, t, f: (b, t, 0)
    modspec = lambda k: pl.BlockSpec((1, 1, 1, d), lambda b, t, f: (b, k, 0, 0))
    return pl.pallas_call(
        _out_ffn_kernel,
        grid=(bsz, s // tm, nf),
        in_specs=[
            pl.BlockSpec((1, tm, d), tok),
            pl.BlockSpec((1, tm, oa.shape[2]), tok),
            pl.BlockSpec((1, tm, ob.shape[2]), tok),
            pl.BlockSpec((d, d), lambda b, t, f: (0, 0)),
            modspec(2),
            pl.BlockSpec((1, d), lambda b, t, f: (0, 0)),
            modspec(4),
            modspec(3),
            modspec(5),
            pl.BlockSpec((d, tf), lambda b, t, f: (0, f)),
            pl.BlockSpec((d, tf), lambda b, t, f: (0, f + nf)),
            pl.BlockSpec((tf, d), lambda b, t, f: (f, 0)),
        ],
        out_specs=pl.BlockSpec((1, tm, d), tok),
        out_shape=jax.ShapeDtypeStruct((bsz, s, d), F32),
        scratch_shapes=[pltpu.VMEM((tm, d), F32), pltpu.VMEM((tm, d), BF16), pltpu.VMEM((tm, d), F32)],
        compiler_params=_params(("parallel", "parallel", "arbitrary")),
        name="out_ffn",
    )(x, oa, ob, wo, mod4, gffn, mod4, mod4, mod4, wgu, wgu, wd)


def _t5_bucket_np(d):
    max_exact = N_BUCKETS // 2
    d = np.maximum(d, 0)
    df = np.maximum(d, 1).astype(np.float64)
    large = max_exact + (np.log(df / max_exact) / math.log(MAX_DIST / max_exact)
                         * (N_BUCKETS - max_exact)).astype(np.int64)
    large = np.minimum(large, N_BUCKETS - 1)
    return np.where(d < max_exact, d, large).astype(np.int32)


def _bias_expand_kernel(tab_ref, bucket_ref, o_ref):
    hd = pl.program_id(0)
    bucket = bucket_ref[...]
    acc = jnp.full(bucket.shape, NEG, F32)
    for b in range(N_BUCKETS):
        acc = jnp.where(bucket == b, tab_ref[hd, b], acc)
    o_ref[0] = acc


def _bias_expand(tab, bucket):
    nh = tab.shape[0]
    return pl.pallas_call(
        _bias_expand_kernel,
        grid=(nh,),
        in_specs=[pl.BlockSpec(memory_space=pltpu.SMEM),
                  pl.BlockSpec(bucket.shape, lambda h: (0, 0))],
        out_specs=pl.BlockSpec((1,) + bucket.shape, lambda h: (h, 0, 0)),
        out_shape=jax.ShapeDtypeStruct((nh,) + bucket.shape, F32),
        compiler_params=_params(("parallel",)),
        name="bias_expand",
    )(tab, jnp.asarray(bucket, jnp.int32))


def _bias_tables(rel_bias):
    tab = rel_bias.T.astype(F32)
    i = np.arange(TQ)[:, None]
    d_near = i + TQ - np.arange(2 * TQ)[None, :]
    near = _bias_expand(tab, np.where(d_near >= 0, _t5_bucket_np(d_near), -1))
    d_win = i + 2 * TQ - np.arange(3 * TQ)[None, :]
    ok_win = (d_win >= 0) & (d_win < WINDOW)
    win = _bias_expand(tab[H_A:], np.where(ok_win, _t5_bucket_np(d_win), -1))
    far = jnp.broadcast_to(tab[:, N_BUCKETS - 1][:, None, None], (tab.shape[0], 8, LANES))
    return near, win, far


def _overlap_t(n_cmp_pad, n_cmp):
    cs = np.arange(n_cmp_pad)[None, :] * CMP_STRIDE
    ss = np.arange(LANES)[:, None] * SEL_BLK
    ov = (cs < ss + SEL_BLK) & (cs + CMP_LEN > ss) & (np.arange(n_cmp_pad)[None, :] < n_cmp)
    return jnp.asarray(ov.astype(np.float32), BF16)


def _block_diag(n):
    m = (np.arange(n)[:, None] // HEAD_DIM == np.arange(n)[None, :] // HEAD_DIM)
    return jnp.asarray(m.astype(np.float32) / HEAD_DIM, BF16)


def kernel(x, c, rel_bias, w_ada, b_ada, g_mix, w_in, q_norm_a, k_norm_a, q_norm_b, k_norm_cmp,
           k_norm_sel, k_norm_win, cmp_pe_k, cmp_w1_k, cmp_w2_k, cmp_pe_v, cmp_w1_v, cmp_w2_v,
           w_out, g_ffn, w_gu, w_down):
    bsz, s, d = x.shape
    depth = w_ada.shape[0]
    assert s % TM_IN == 0 and s % TM_FFN == 0 and s % TQ == 0
    assert s // BLK_A <= HEAD_DIM and s // SEL_BLK <= LANES
    assert WINDOW == 2 * TQ and BLK_A == TQ and MAX_DIST <= TQ
    n_chunks = s // CMP_STRIDE
    n_cmp = (s - CMP_LEN) // CMP_STRIDE + 1
    scale = HEAD_DIM ** -0.5

    near, win, far = _bias_tables(rel_bias)
    ovt = _overlap_t(n_chunks, n_cmp)
    bd = _block_diag(512)
    bd2 = _block_diag(LANES)
    tile = lambda g, n: jnp.tile(g.astype(F32), n).reshape(1, -1)

    for l in range(depth):
        mod = _ada(c, w_ada[l], b_ada[l])
        mod4 = mod.reshape(bsz, ADA_CHUNKS, 1, d)

        wl = w_in[l]
        n_main = wl.shape[1] - 3 * H_B
        gl = wl[:, n_main:].reshape(d, G_B, 3 * R_B)
        gl = jnp.pad(gl, ((0, 0), (0, 0), (0, LANES - 3 * R_B))).reshape(d, G_B * LANES)
        w_in_p = jnp.concatenate([wl[:, :n_main], gl], axis=1).astype(BF16)

        (qa, kaug, va, kmean, qb, kc, vc, ksaug, vs, kw, vw, gates) = _inproj(
            x, mod4, mod4, g_mix[l].reshape(1, d), w_in_p, bd,
            tile(q_norm_a[l], H_A) * scale, tile(k_norm_a[l], H_A), tile(q_norm_b[l], H_B) * scale,
            tile(k_norm_sel[l], G_B), tile(k_norm_win[l], G_B))

        nba = s // BLK_A
        km = kmean.reshape(bsz, nba, H_A, HEAD_DIM).transpose(0, 2, 1, 3)
        km = jnp.pad(km, ((0, 0), (0, 0), (HEAD_DIM, LANES - HEAD_DIM - nba), (0, LANES - HEAD_DIM)))
        o_a = _moba(qa, kaug, va, km, near[:H_A], far[:H_A])

        chunks = lambda t: t.reshape(bsz, s, G_B, HEAD_DIM).transpose(0, 2, 1, 3).reshape(
            bsz, G_B, n_chunks, CMP_STRIDE * HEAD_DIM)
        w1 = jnp.stack([cmp_w1_k[l], cmp_w1_v[l]]).astype(BF16)
        w2 = jnp.stack([cmp_w2_k[l], cmp_w2_v[l]]).astype(BF16)
        pe = jnp.stack([cmp_pe_k[l], cmp_pe_v[l]]).reshape(2, 1, CMP_LEN * HEAD_DIM)
        pe = jnp.broadcast_to(pe, (2, 8, CMP_LEN * HEAD_DIM)).astype(BF16)
        kcmp, vcmp = _compress(chunks(kc), chunks(vc), w1, w2, pe, tile(k_norm_cmp[l], G_B), bd2)

        oc, selb = _nsa_cmp(qb, kcmp, vcmp, ovt, n_cmp)
        o_b = _nsa_main(qb, selb, ksaug, vs, kw, vw, oc, gates, near[H_A:], win, far[H_A:])

        x = _out_ffn(x, o_a, o_b, w_out[l].astype(BF16), mod4, g_ffn[l].reshape(1, d),
                     w_gu[l].astype(BF16), w_down[l].astype(BF16))
    return x
```

```python
import functools
import math

import jax
import jax.numpy as jnp
import numpy as np
from jax import lax
from jax.experimental import pallas as pl
from jax.experimental.pallas import tpu as pltpu

F32 = jnp.float32
BF16 = jnp.bfloat16

HEAD_DIM = 64
LANES = 128
BF16_ROWS = 16
H_A = 8
H_B = 8
G_B = 2
R_B = H_B // G_B
BLK_A = 256
TOPK_A = 3
CMP_LEN = 32
CMP_STRIDE = 16
CMP_HIDDEN = 256
SEL_BLK = 64
SEL_TOPK = 16
WINDOW = 512
N_BUCKETS = 32
MAX_DIST = 128
ADA_CHUNKS = 6
NEG = -1e30
BIG = 1e9
EPS = 1e-6

_LOG2_BLK_A = BLK_A.bit_length() - 1
_LOG2_SEL_BLK = SEL_BLK.bit_length() - 1

TQ = 256
TM_IN = 512
TM_FFN = 512
GATE_ROWS = 16
ACC_ROWS = HEAD_DIM + BF16_ROWS
VMEM_LIMIT = 56 * 1024 * 1024


def _dot(a, b):
    return jnp.dot(a, b, preferred_element_type=F32)


def _dot_nt(a, b):
    return lax.dot_general(a, b, (((1,), (1,)), ((), ())), preferred_element_type=F32)


def _split(a):
    hi = a.astype(BF16)
    lo = (a - hi.astype(F32)).astype(BF16)
    return hi, lo


def _dot3(a, b):
    ah, al = _split(a)
    bh, bl = _split(b)
    return _dot(ah, bh) + (_dot(al, bh) + _dot(ah, bl))


def _params(sem):
    return pltpu.CompilerParams(dimension_semantics=sem, vmem_limit_bytes=VMEM_LIMIT)


def _ada_kernel(c_ref, w_ref, b_ref, o_ref):
    c = c_ref[...]
    o_ref[...] = _dot3(c * jax.nn.sigmoid(c), w_ref[...]) + b_ref[...]


def _ada(c, w, b):
    bsz, d = c.shape
    n = w.shape[1]
    tn = 512
    return pl.pallas_call(
        _ada_kernel,
        grid=(n // tn,),
        in_specs=[pl.BlockSpec((bsz, d), lambda j: (0, 0)),
                  pl.BlockSpec((d, tn), lambda j: (0, j)),
                  pl.BlockSpec((1, tn), lambda j: (0, j))],
        out_specs=pl.BlockSpec((bsz, tn), lambda j: (0, j)),
        out_shape=jax.ShapeDtypeStruct((bsz, n), F32),
        compiler_params=_params(("arbitrary",)),
        name="ada",
    )(c, w, b.reshape(1, n))


def _rms_rows(xf, g):
    ms = jnp.mean(xf * xf, axis=-1, keepdims=True)
    return xf * lax.rsqrt(ms + EPS) * g


def _head_norm(t, bd, gain):
    hi, lo = _split(t * t)
    ms = _dot(hi, bd) + _dot(lo, bd)
    return t * lax.rsqrt(ms + EPS) * gain


def _head_norm_t(t, bd, gain):
    hi, lo = _split(t * t)
    ms = _dot(bd, hi) + _dot(bd, lo)
    return t * lax.rsqrt(ms + EPS) * gain


def _inproj_kernel(x_ref, sc_ref, sh_ref, gmix_ref, wr_ref, wt_ref, bd_ref, gqa_ref, gka_ref, gqb_ref,
                   gks_ref, gkw_ref,
                   qa_ref, kaug_ref, va_ref, kmean_ref, qb_ref, kc_ref, vc_ref, ksaug_ref,
                   vs_ref, kw_ref, vw_ref, gates_ref):
    tm = x_ref.shape[1]
    ti = pl.program_id(1)
    xf = x_ref[0]
    h = _rms_rows(xf, gmix_ref[...]) * (1.0 + sc_ref[0, 0]) + sh_ref[0, 0]
    hb = h.astype(BF16)

    def proj(c0, c1):
        return _dot(hb, wr_ref[:, c0:c1])

    def proj_t(r0, r1):
        return _dot_nt(wt_ref[r0:r1, :], hb)

    bd = bd_ref[...]
    bd2 = bd_ref[0:LANES, 0:LANES]
    lane = lax.broadcasted_iota(jnp.int32, (tm, LANES), 1)
    row = lax.broadcasted_iota(jnp.int32, (tm, LANES), 0) + ti * tm
    low = lane < HEAD_DIM

    def k_in_low(pair, odd):
        return pltpu.roll(pair, HEAD_DIM, 1) if odd else pair

    qa = _head_norm_t(proj_t(0, 512), bd, gqa_ref[...])
    for hd in range(H_A):
        qa_ref[0, hd] = qa[hd * HEAD_DIM:(hd + 1) * HEAD_DIM]
    def put_tiles(ref, vt):
        for i in range(tm // TQ):
            ref[0, i] = vt[:, i * TQ:(i + 1) * TQ].astype(BF16)

    put_tiles(va_ref, proj_t(512, 1024))
    qb = _head_norm_t(proj_t(1024, 1536), bd, gqb_ref[...])
    for hd in range(H_B):
        qb_ref[0, hd] = qb[hd * HEAD_DIM:(hd + 1) * HEAD_DIM]
    put_tiles(vs_ref, proj_t(1536, 1664))
    put_tiles(vw_ref, proj_t(1664, 1792))
    gl = jax.nn.sigmoid(proj_t(1792, 1792 + G_B * GATE_ROWS))
    for g in range(G_B):
        gates_ref[0, g] = gl[g * GATE_ROWS:(g + 1) * GATE_ROWS]

    ka = _head_norm(proj(0, 512), bd, gka_ref[...])
    oh_a = jnp.where(lane - HEAD_DIM == (row >> _LOG2_BLK_A), 1.0, 0.0)
    for hd in range(H_A):
        pair = ka[:, (hd // 2) * LANES:(hd // 2 + 1) * LANES]
        kaug_ref[0, hd] = jnp.where(low, k_in_low(pair, hd % 2), oh_a).astype(BF16)
    for i in range(tm // BLK_A):
        kmean_ref[0, i] = jnp.mean(ka[i * BLK_A:(i + 1) * BLK_A], axis=0, keepdims=True)

    kc_ref[0] = proj(512, 640).astype(BF16)
    vc_ref[0] = proj(640, 768).astype(BF16)

    ks = _head_norm(proj(768, 896), bd2, gks_ref[...])
    kw = _head_norm(proj(896, 1024), bd2, gkw_ref[...])
    oh_s = jnp.where(lane == (row >> _LOG2_SEL_BLK), 1.0, 0.0).astype(BF16)
    for g in range(G_B):
        ksaug_ref[0, g] = jnp.concatenate(
            [jnp.where(low, k_in_low(ks, g), 0.0).astype(BF16), oh_s], axis=1)
        kw_ref[0, g] = jnp.where(low, k_in_low(kw, g), 0.0).astype(BF16)


def _inproj(x, sc, sh, gmix, wr, wt, bd, gqa, gka, gqb, gks, gkw):
    bsz, s, d = x.shape
    tm = TM_IN
    nt = s // tm
    nba = s // BLK_A
    const2 = lambda b, t: (0, 0)
    tok3 = lambda b, t: (b, t, 0)
    tok4 = lambda b, t: (b, 0, t, 0)
    tile4 = lambda b, t: (b, t, 0, 0)
    tr4 = lambda b, t: (b, 0, 0, t)
    in_specs = [
        pl.BlockSpec((1, tm, d), tok3),
        pl.BlockSpec((1, 1, 1, d), lambda b, t: (b, 1, 0, 0)),
        pl.BlockSpec((1, 1, 1, d), lambda b, t: (b, 0, 0, 0)),
        pl.BlockSpec((1, d), const2),
        pl.BlockSpec(wr.shape, const2),
        pl.BlockSpec(wt.shape, const2),
        pl.BlockSpec((512, 512), const2),
        pl.BlockSpec((512, tm), const2),
        pl.BlockSpec((1, 512), const2),
        pl.BlockSpec((512, tm), const2),
        pl.BlockSpec((1, LANES), const2),
        pl.BlockSpec((1, LANES), const2),
    ]
    out_shape = [
        jax.ShapeDtypeStruct((bsz, H_A, HEAD_DIM, s), F32),
        jax.ShapeDtypeStruct((bsz, H_A, s, LANES), BF16),
        jax.ShapeDtypeStruct((bsz, s // TQ, 512, TQ), BF16),
        jax.ShapeDtypeStruct((bsz, nba, 1, 512), F32),
        jax.ShapeDtypeStruct((bsz, H_B, HEAD_DIM, s), F32),
        jax.ShapeDtypeStruct((bsz, s, LANES), BF16),
        jax.ShapeDtypeStruct((bsz, s, LANES), BF16),
        jax.ShapeDtypeStruct((bsz, G_B, s, 2 * LANES), BF16),
        jax.ShapeDtypeStruct((bsz, s // TQ, LANES, TQ), BF16),
        jax.ShapeDtypeStruct((bsz, G_B, s, LANES), BF16),
        jax.ShapeDtypeStruct((bsz, s // TQ, LANES, TQ), BF16),
        jax.ShapeDtypeStruct((bsz, G_B, GATE_ROWS, s), F32),
    ]
    out_specs = [
        pl.BlockSpec((1, H_A, HEAD_DIM, tm), tr4),
        pl.BlockSpec((1, H_A, tm, LANES), tok4),
        pl.BlockSpec((1, tm // TQ, 512, TQ), tile4),
        pl.BlockSpec((1, tm // BLK_A, 1, 512), lambda b, t: (b, t, 0, 0)),
        pl.BlockSpec((1, H_B, HEAD_DIM, tm), tr4),
        pl.BlockSpec((1, tm, LANES), tok3),
        pl.BlockSpec((1, tm, LANES), tok3),
        pl.BlockSpec((1, G_B, tm, 2 * LANES), tok4),
        pl.BlockSpec((1, tm // TQ, LANES, TQ), tile4),
        pl.BlockSpec((1, G_B, tm, LANES), tok4),
        pl.BlockSpec((1, tm // TQ, LANES, TQ), tile4),
        pl.BlockSpec((1, G_B, GATE_ROWS, tm), tr4),
    ]
    return pl.pallas_call(
        _inproj_kernel,
        grid=(bsz, nt),
        in_specs=in_specs,
        out_specs=out_specs,
        out_shape=out_shape,
        compiler_params=_params(("parallel", "parallel")),
        name="inproj",
    )(x, sc, sh, gmix, wr, wt, bd, gqa, gka, gqb, gks, gkw)


def _compress_kernel(ck_ref, cv_ref, w1_ref, w2t_ref, pe_ref, gk_ref, ok_ref, ov_ref):
    half = CMP_STRIDE * HEAD_DIM
    for kv, c_ref in enumerate((ck_ref, cv_ref)):
        for g in range(G_B):
            c = c_ref[0, g]
            a = _dot(c, w1_ref[kv, 0:half, :])
            b = _dot(c, w1_ref[kv, half:2 * half, :])
            n = a.shape[0]
            b_next = pltpu.roll(b, n - 1, 0)
            pe_term = _dot(pe_ref[kv], w1_ref[kv])[0:1]
            hid = jax.nn.gelu(a + b_next + pe_term).astype(BF16)
            if kv == 0:
                y = _dot_nt(hid, w2t_ref[kv])
                ms = jnp.mean(y * y, axis=1, keepdims=True)
                y = y * lax.rsqrt(ms + EPS) * gk_ref[...]
                ok_ref[0, g] = jnp.concatenate([y, jnp.zeros_like(y)], axis=1).astype(BF16)
            else:
                ov_ref[0, g] = _dot_nt(w2t_ref[kv], hid).astype(BF16)


def _compress(ck, cv, w1, w2t, pe, gk):
    bsz, g, n, width = ck.shape
    blk = pl.BlockSpec((1, g, n, width), lambda b: (b, 0, 0, 0))
    full = lambda a: pl.BlockSpec(a.shape, lambda b: (0,) * a.ndim)
    return pl.pallas_call(
        _compress_kernel,
        grid=(bsz,),
        in_specs=[blk, blk, full(w1), full(w2t), full(pe), full(gk)],
        out_specs=[pl.BlockSpec((1, g, n, LANES), lambda b: (b, 0, 0, 0)),
                   pl.BlockSpec((1, g, HEAD_DIM, n), lambda b: (b, 0, 0, 0))],
        out_shape=[jax.ShapeDtypeStruct((bsz, g, n, LANES), BF16),
                   jax.ShapeDtypeStruct((bsz, g, HEAD_DIM, n), BF16)],
        compiler_params=_params(("parallel",)),
        name="compress",
    )(ck, cv, w1, w2t, pe, gk)


def _with_ones(vt):
    return jnp.concatenate([vt, jnp.ones((BF16_ROWS, vt.shape[1]), BF16)], axis=0)


def _flash_init(s, vt):
    m = jnp.max(s, axis=0, keepdims=True)
    return m, _dot(_with_ones(vt), jnp.exp(s - m).astype(BF16))


def _flash_step(carry, s, vt, bias=None):
    m, acc = carry
    smax = jnp.max(s, axis=0, keepdims=True)
    if bias is not None:
        smax = smax + bias
    m_new = jnp.maximum(m, smax)
    alpha = jnp.exp(m - m_new)
    p = jnp.exp(s - (m_new if bias is None else m_new - bias))
    return m_new, alpha * acc + _dot(_with_ones(vt), p.astype(BF16))


def _flash_out(carry):
    _, acc = carry
    return acc[0:HEAD_DIM] / acc[HEAD_DIM:HEAD_DIM + 1]


def _topk_rows(score, index, k):
    picked = jnp.zeros(score.shape, F32)
    for _ in range(k):
        mx = jnp.max(score, axis=0, keepdims=True)
        cand = jnp.where(score == mx, index, jnp.int32(1 << 20))
        first = jnp.min(cand, axis=0, keepdims=True)
        hit = index == first
        picked = jnp.where(hit, 1.0, picked)
        score = jnp.where(hit, -jnp.inf, score)
    return picked


def _tile_rows(j):
    return pl.ds(pl.multiple_of(j * TQ, TQ), TQ)


def _moba_gate_kernel(q_ref, km_ref, o_ref, *, n_sel):
    tg = q_ref.shape[3]
    t0 = pl.program_id(2) * tg
    q = q_ref[0, 0]
    q128 = jnp.concatenate([q, jnp.zeros_like(q)], axis=0)
    gate = _dot3(km_ref[0, 0], q128)
    blk = lax.broadcasted_iota(jnp.int32, gate.shape, 0)
    own = (lax.broadcasted_iota(jnp.int32, gate.shape, 1) + t0) >> _LOG2_BLK_A
    valid = blk < own
    picked = _topk_rows(jnp.where(valid, gate, -jnp.inf), blk, n_sel)
    keep = jnp.where(valid, picked, 0.0) + jnp.where(blk == own, 1.0, 0.0)
    selb = jnp.where(keep > 0.0, 0.0, NEG)
    o_ref[0, 0] = jnp.concatenate([q, selb], axis=0).astype(BF16)


def _moba_gate(qa, km):
    bsz, nh, _, s = qa.shape
    tg = min(s, 1024)
    n_sel = max(1, min(TOPK_A, s // BLK_A - 1))
    return pl.pallas_call(
        functools.partial(_moba_gate_kernel, n_sel=n_sel),
        grid=(bsz, nh, s // tg),
        in_specs=[pl.BlockSpec((1, 1, HEAD_DIM, tg), lambda b, h, t: (b, h, 0, t)),
                  pl.BlockSpec((1, 1, HEAD_DIM, LANES), lambda b, h, t: (b, h, 0, 0))],
        out_specs=pl.BlockSpec((1, 1, LANES, tg), lambda b, h, t: (b, h, 0, t)),
        out_shape=jax.ShapeDtypeStruct((bsz, nh, LANES, s), BF16),
        compiler_params=_params(("parallel", "parallel", "parallel")),
        name="moba_gate",
    )(qa, km)


def _moba_kernel(q_ref, k_ref, vt_ref, tab_ref, cb_ref, o_ref):
    qi = pl.program_id(2)
    jp = jnp.maximum(qi - 1, 0)
    off_p = jnp.where(qi >= 1, 0.0, NEG)
    heads = range(2)
    qaug = [q_ref[0, hh] for hh in heads]
    cbs = [cb_ref[hh][0:1, 0:1] for hh in heads]

    def scores(hh, j):
        return _dot(k_ref[0, hh, _tile_rows(j), :], qaug[hh])

    def values(hh, j):
        return vt_ref[0, j, hh * HEAD_DIM:(hh + 1) * HEAD_DIM, :]

    s_own = [scores(hh, qi) for hh in heads]
    s_prev = [scores(hh, jp) for hh in heads]
    s_far = tuple(scores(hh, 0) for hh in heads)
    carries = []
    for hh in heads:
        c = _flash_init(s_own[hh] + tab_ref[hh, TQ:2 * TQ, :], values(hh, qi))
        carries.append(_flash_step(c, s_prev[hh] + tab_ref[hh, 0:TQ, :] + off_p, values(hh, jp)))

    def far(j, state):
        cs, s_cur = state
        jn = jnp.minimum(j + 1, jp)
        s_next = tuple(scores(hh, jn) for hh in heads)
        cs = tuple(_flash_step(cs[hh], s_cur[hh], values(hh, j), bias=cbs[hh]) for hh in heads)
        return cs, s_next

    carries, _ = lax.fori_loop(0, jp, far, (tuple(carries), s_far))
    out_t = jnp.concatenate([_flash_out(c) for c in carries], axis=0)
    o_ref[0] = out_t.T.astype(BF16)


def _moba(qaug, kaug, vat, tab, cb):
    bsz, _, _, s = qaug.shape
    nq = s // TQ
    return pl.pallas_call(
        _moba_kernel,
        grid=(bsz, H_A // 2, nq),
        in_specs=[
            pl.BlockSpec((1, 2, LANES, TQ), lambda b, hp, qi: (b, hp, 0, qi)),
            pl.BlockSpec((1, 2, s, LANES), lambda b, hp, qi: (b, hp, 0, 0)),
            pl.BlockSpec((1, nq, LANES, TQ), lambda b, hp, qi: (b, 0, hp, 0)),
            pl.BlockSpec((2, 2 * TQ, TQ), lambda b, hp, qi: (hp, 0, 0)),
            pl.BlockSpec((2, 8, LANES), lambda b, hp, qi: (hp, 0, 0)),
        ],
        out_specs=pl.BlockSpec((1, TQ, LANES), lambda b, hp, qi: (b, qi, hp)),
        out_shape=jax.ShapeDtypeStruct((bsz, s, H_A * HEAD_DIM), BF16),
        compiler_params=_params(("parallel", "parallel", "arbitrary")),
        name="moba",
    )(qaug, kaug, vat, tab, cb)


def _nsa_cmp_kernel(q_ref, kc_ref, vct_ref, ovt_ref, oc_ref, selb_ref, *, n_sel, n_cmp):
    qi = pl.program_id(2)
    ncp = kc_ref.shape[2]
    t0 = qi * TQ
    n_idx = lax.broadcasted_iota(jnp.int32, (ncp, TQ), 0)
    t_idx = lax.broadcasted_iota(jnp.int32, (ncp, TQ), 1) + t0
    mask = (n_idx * CMP_STRIDE + (CMP_LEN - 1) <= t_idx) & (n_idx < n_cmp)
    kc = kc_ref[0, 0]
    vct = vct_ref[0, 0]
    zeros = jnp.zeros((HEAD_DIM, TQ), BF16)
    psum = jnp.zeros((ncp, TQ), F32)
    for r in range(R_B):
        q = jnp.concatenate([q_ref[0, r].astype(BF16), zeros], axis=0)
        z = jnp.where(mask, _dot(kc, q), NEG)
        e = jnp.exp(z - jnp.max(z, axis=0, keepdims=True))
        p = jnp.where(mask, e / jnp.sum(e, axis=0, keepdims=True), 0.0)
        oc_ref[0, r] = _dot(vct, p.astype(BF16))
        psum = psum + p
    ph, pl_ = _split(psum)
    ovt = ovt_ref[...]
    imp_t = _dot(ovt, ph) + _dot(ovt, pl_)
    blk = lax.broadcasted_iota(jnp.int32, (LANES, TQ), 0)
    cur = (lax.broadcasted_iota(jnp.int32, (LANES, TQ), 1) + t0) >> _LOG2_SEL_BLK
    ok = blk <= cur
    forced = (blk == 0) | (blk == cur) | (blk == cur - 1)
    score = jnp.where(ok, jnp.where(forced, BIG, imp_t), -jnp.inf)
    picked = _topk_rows(score, blk, n_sel)
    selb_ref[0, 0] = jnp.where(ok & (picked > 0.0), 0.0, NEG).astype(BF16)


def _nsa_cmp(qb, kcmp, vcmpt, ovt, n_cmp):
    bsz, _, _, s = qb.shape
    nq = s // TQ
    ncp = kcmp.shape[2]
    n_sel = min(SEL_TOPK, s // SEL_BLK)
    return pl.pallas_call(
        functools.partial(_nsa_cmp_kernel, n_sel=n_sel, n_cmp=n_cmp),
        grid=(bsz, G_B, nq),
        in_specs=[
            pl.BlockSpec((1, R_B, HEAD_DIM, TQ), lambda b, g, qi: (b, g, 0, qi)),
            pl.BlockSpec((1, 1, ncp, LANES), lambda b, g, qi: (b, g, 0, 0)),
            pl.BlockSpec((1, 1, HEAD_DIM, ncp), lambda b, g, qi: (b, g, 0, 0)),
            pl.BlockSpec((LANES, ncp), lambda b, g, qi: (0, 0)),
        ],
        out_specs=[
            pl.BlockSpec((1, R_B, HEAD_DIM, TQ), lambda b, g, qi: (b, g, 0, qi)),
            pl.BlockSpec((1, 1, LANES, TQ), lambda b, g, qi: (b, g, 0, qi)),
        ],
        out_shape=[jax.ShapeDtypeStruct((bsz, H_B, HEAD_DIM, s), F32),
                   jax.ShapeDtypeStruct((bsz, G_B, LANES, s), BF16)],
        compiler_params=_params(("parallel", "parallel", "parallel")),
        name="nsa_cmp",
    )(qb, kcmp, vcmpt, ovt)


def _nsa_main_kernel(q_ref, selb_ref, ks_ref, vst_ref, kw_ref, vwt_ref, oc_ref, gates_ref,
                     tabs_ref, tabw_ref, cb_ref, o_ref):
    qi = pl.program_id(2)
    selb = selb_ref[0, 0]
    gates = gates_ref[0, 0]
    zeros = jnp.zeros((HEAD_DIM, TQ), BF16)
    j1 = jnp.maximum(qi - 1, 0)
    j2 = jnp.maximum(qi - 2, 0)
    off1 = jnp.where(qi >= 1, 0.0, NEG)
    off2 = jnp.where(qi >= 2, 0.0, NEG)

    heads = range(R_B)
    qw = [jnp.concatenate([q_ref[0, r].astype(BF16), zeros], axis=0) for r in heads]
    qs = [jnp.concatenate([qw[r], selb], axis=0) for r in heads]
    cbs = [cb_ref[r][0:1, 0:1] for r in heads]

    def s_sel(r, j):
        return _dot(ks_ref[0, 0, _tile_rows(j), :], qs[r])

    def s_win(r, j):
        return _dot(kw_ref[0, 0, _tile_rows(j), :], qw[r])

    sw = [[s_win(r, j) for r in heads] for j in (qi, j1, j2)]
    ss = [[s_sel(r, j) for r in heads] for j in (qi, j1)]
    s_far = tuple(s_sel(r, 0) for r in heads)

    win = []
    for r in heads:
        c = _flash_init(sw[0][r] + tabw_ref[r, 2 * TQ:3 * TQ, :], vwt_ref[0, qi])
        c = _flash_step(c, sw[1][r] + tabw_ref[r, TQ:2 * TQ, :] + off1, vwt_ref[0, j1])
        win.append(_flash_step(c, sw[2][r] + tabw_ref[r, 0:TQ, :] + off2, vwt_ref[0, j2]))
    sel = []
    for r in heads:
        c = _flash_init(ss[0][r] + tabs_ref[r, TQ:2 * TQ, :], vst_ref[0, qi])
        sel.append(_flash_step(c, ss[1][r] + tabs_ref[r, 0:TQ, :] + off1, vst_ref[0, j1]))

    def far(j, state):
        cs, s_cur = state
        jn = jnp.minimum(j + 1, j1)
        s_next = tuple(s_sel(r, jn) for r in heads)
        vt = vst_ref[0, j]
        cs = tuple(_flash_step(cs[r], s_cur[r], vt, bias=cbs[r]) for r in heads)
        return cs, s_next

    sel, _ = lax.fori_loop(0, j1, far, (tuple(sel), s_far))

    outs = [gates[3 * r:3 * r + 1] * oc_ref[0, r]
            + gates[3 * r + 1:3 * r + 2] * _flash_out(sel[r])
            + gates[3 * r + 2:3 * r + 3] * _flash_out(win[r]) for r in heads]
    o_ref[0] = jnp.concatenate(outs, axis=0).T.astype(BF16)


def _nsa_main(qb, selb, ksaug, vst, kw, vwt, oc, gates, tabs, tabw, cb):
    bsz, _, _, s = qb.shape
    nq = s // TQ
    return pl.pallas_call(
        _nsa_main_kernel,
        grid=(bsz, G_B, nq),
        in_specs=[
            pl.BlockSpec((1, R_B, HEAD_DIM, TQ), lambda b, g, qi: (b, g, 0, qi)),
            pl.BlockSpec((1, 1, LANES, TQ), lambda b, g, qi: (b, g, 0, qi)),
            pl.BlockSpec((1, 1, s, 2 * LANES), lambda b, g, qi: (b, g, 0, 0)),
            pl.BlockSpec((1, nq, HEAD_DIM, TQ), lambda b, g, qi: (b, 0, g, 0)),
            pl.BlockSpec((1, 1, s, LANES), lambda b, g, qi: (b, g, 0, 0)),
            pl.BlockSpec((1, nq, HEAD_DIM, TQ), lambda b, g, qi: (b, 0, g, 0)),
            pl.BlockSpec((1, R_B, HEAD_DIM, TQ), lambda b, g, qi: (b, g, 0, qi)),
            pl.BlockSpec((1, 1, GATE_ROWS, TQ), lambda b, g, qi: (b, g, 0, qi)),
            pl.BlockSpec((R_B, 2 * TQ, TQ), lambda b, g, qi: (g, 0, 0)),
            pl.BlockSpec((R_B, 3 * TQ, TQ), lambda b, g, qi: (g, 0, 0)),
            pl.BlockSpec((R_B, 8, LANES), lambda b, g, qi: (g, 0, 0)),
        ],
        out_specs=pl.BlockSpec((1, TQ, R_B * HEAD_DIM), lambda b, g, qi: (b, qi, g)),
        out_shape=jax.ShapeDtypeStruct((bsz, s, H_B * HEAD_DIM), BF16),
        compiler_params=_params(("parallel", "parallel", "arbitrary")),
        name="nsa_main",
    )(qb, selb, ksaug, vst, kw, vwt, oc, gates, tabs, tabw, cb)


def _out_ffn_kernel(x_ref, oa_ref, ob_ref, wo_ref, gtm_ref, gffn_ref, scf_ref, shf_ref, gtf_ref,
                    wg_ref, wu_ref, wd_ref, o_ref, x1_ref, h_ref, acc_ref):
    f = pl.program_id(2)
    half = oa_ref.shape[2]

    @pl.when(f == 0)
    def _():
        mix = _dot(oa_ref[0], wo_ref[0:half, :]) + _dot(ob_ref[0], wo_ref[half:2 * half, :])
        x1 = x_ref[0] + gtm_ref[0, 0] * mix
        x1_ref[...] = x1
        h = _rms_rows(x1, gffn_ref[...]) * (1.0 + scf_ref[0, 0]) + shf_ref[0, 0]
        h_ref[...] = h.astype(BF16)

    h = h_ref[...]
    gate = _dot(h, wg_ref[...])
    up = _dot(h, wu_ref[...])
    act = (gate * jax.nn.sigmoid(gate) * up).astype(BF16)
    part = _dot(act, wd_ref[...])

    @pl.when(f == 0)
    def _():
        acc_ref[...] = part

    @pl.when(f > 0)
    def _():
        acc_ref[...] += part

    @pl.when(f == pl.num_programs(2) - 1)
    def _():
        o_ref[0] = x1_ref[...] + gtf_ref[0, 0] * acc_ref[...]


def _out_ffn(x, oa, ob, wo, mod4, gffn, wgu, wd):
    bsz, s, d = x.shape
    fh = wd.shape[0]
    tm = TM_FFN
    tf = fh // 2 if (fh // 2) % LANES == 0 else fh
    nf = fh // tf
    tok = lambda b, t, f: (b, t, 0)
    modspec = lambda k: pl.BlockSpec((1, 1, 1, d), lambda b, t, f: (b, k, 0, 0))
    return pl.pallas_call(
        _out_ffn_kernel,
        grid=(bsz, s // tm, nf),
        in_specs=[
            pl.BlockSpec((1, tm, d), tok),
            pl.BlockSpec((1, tm, oa.shape[2]), tok),
            pl.BlockSpec((1, tm, ob.shape[2]), tok),
            pl.BlockSpec((d, d), lambda b, t, f: (0, 0)),
            modspec(2),
            pl.BlockSpec((1, d), lambda b, t, f: (0, 0)),
            modspec(4),
            modspec(3),
            modspec(5),
            pl.BlockSpec((d, tf), lambda b, t, f: (0, f)),
            pl.BlockSpec((d, tf), lambda b, t, f: (0, f + nf)),
            pl.BlockSpec((tf, d), lambda b, t, f: (f, 0)),
        ],
        out_specs=pl.BlockSpec((1, tm, d), tok),
        out_shape=jax.ShapeDtypeStruct((bsz, s, d), F32),
        scratch_shapes=[pltpu.VMEM((tm, d), F32), pltpu.VMEM((tm, d), BF16), pltpu.VMEM((tm, d), F32)],
        compiler_params=_params(("parallel", "parallel", "arbitrary")),
        name="out_ffn",
    )(x, oa, ob, wo, mod4, gffn, mod4, mod4, mod4, wgu, wgu, wd)


def _t5_bucket_np(d):
    max_exact = N_BUCKETS // 2
    d = np.maximum(d, 0)
    df = np.maximum(d, 1).astype(np.float64)
    large = max_exact + (np.log(df / max_exact) / math.log(MAX_DIST / max_exact)
                         * (N_BUCKETS - max_exact)).astype(np.int64)
    large = np.minimum(large, N_BUCKETS - 1)
    return np.where(d < max_exact, d, large).astype(np.int32)


def _bias_expand_kernel(tab_ref, bucket_ref, o_ref):
    hd = pl.program_id(0)
    bucket = bucket_ref[...]
    acc = jnp.full(bucket.shape, NEG, F32)
    for b in range(N_BUCKETS):
        acc = jnp.where(bucket == b, tab_ref[hd, b], acc)
    o_ref[0] = acc


def _bias_expand(tab, bucket):
    nh = tab.shape[0]
    return pl.pallas_call(
        _bias_expand_kernel,
        grid=(nh,),
        in_specs=[pl.BlockSpec(memory_space=pltpu.SMEM),
                  pl.BlockSpec(bucket.shape, lambda h: (0, 0))],
        out_specs=pl.BlockSpec((1,) + bucket.shape, lambda h: (h, 0, 0)),
        out_shape=jax.ShapeDtypeStruct((nh,) + bucket.shape, F32),
        compiler_params=_params(("parallel",)),
        name="bias_expand",
    )(tab, jnp.asarray(bucket, jnp.int32))


def _bias_tables(rel_bias):
    tab = rel_bias.T.astype(F32)
    i = np.arange(TQ)[None, :]
    d_near = i + TQ - np.arange(2 * TQ)[:, None]
    near = _bias_expand(tab, np.where(d_near >= 0, _t5_bucket_np(d_near), -1))
    d_win = i + 2 * TQ - np.arange(3 * TQ)[:, None]
    ok_win = (d_win >= 0) & (d_win < WINDOW)
    win = _bias_expand(tab[H_A:], np.where(ok_win, _t5_bucket_np(d_win), -1))
    far = jnp.broadcast_to(tab[:, N_BUCKETS - 1][:, None, None], (tab.shape[0], 8, LANES))
    return near, win, far


def _overlap_t(n_cmp_pad, n_cmp):
    cs = np.arange(n_cmp_pad)[None, :] * CMP_STRIDE
    ss = np.arange(LANES)[:, None] * SEL_BLK
    ov = (cs < ss + SEL_BLK) & (cs + CMP_LEN > ss) & (np.arange(n_cmp_pad)[None, :] < n_cmp)
    return jnp.asarray(ov.astype(np.float32), BF16)


def _block_diag(n):
    m = (np.arange(n)[:, None] // HEAD_DIM == np.arange(n)[None, :] // HEAD_DIM)
    return jnp.asarray(m.astype(np.float32) / HEAD_DIM, BF16)


def kernel(x, c, rel_bias, w_ada, b_ada, g_mix, w_in, q_norm_a, k_norm_a, q_norm_b, k_norm_cmp,
           k_norm_sel, k_norm_win, cmp_pe_k, cmp_w1_k, cmp_w2_k, cmp_pe_v, cmp_w1_v, cmp_w2_v,
           w_out, g_ffn, w_gu, w_down):
    bsz, s, d = x.shape
    depth = w_ada.shape[0]
    assert s % TM_IN == 0 and s % TM_FFN == 0 and s % TQ == 0
    assert s // BLK_A <= HEAD_DIM and s // SEL_BLK <= LANES
    assert WINDOW == 2 * TQ and BLK_A == TQ and MAX_DIST <= TQ
    n_chunks = s // CMP_STRIDE
    n_cmp = (s - CMP_LEN) // CMP_STRIDE + 1
    scale = HEAD_DIM ** -0.5
    hd = HEAD_DIM

    near, win, far = _bias_tables(rel_bias)
    ovt = _overlap_t(n_chunks, n_cmp)
    bd = _block_diag(512)
    tile = lambda g, n: jnp.tile(g.astype(F32), n).reshape(1, -1)
    tile_t = lambda g, n: jnp.broadcast_to(jnp.tile(g.astype(F32), n)[:, None], (n * hd, TM_IN))

    for l in range(depth):
        mod = _ada(c, w_ada[l], b_ada[l])
        mod4 = mod.reshape(bsz, ADA_CHUNKS, 1, d)

        wl = w_in[l]
        cols = np.cumsum([0, H_A * hd, H_A * hd, H_A * hd, H_B * hd] + [G_B * hd] * 6)
        qa_c, ka_c, va_c, qb_c, kc_c, vc_c, ks_c, vs_c, kw_c, vw_c = [
            wl[:, int(a):int(b)] for a, b in zip(cols[:-1], cols[1:])]
        gl = wl[:, int(cols[-1]):].reshape(d, G_B, 3 * R_B)
        gl = jnp.pad(gl, ((0, 0), (0, 0), (0, GATE_ROWS - 3 * R_B))).reshape(d, G_B * GATE_ROWS)
        w_rows = jnp.concatenate([ka_c, kc_c, vc_c, ks_c, kw_c], axis=1).astype(BF16)
        w_t = jnp.concatenate([qa_c, va_c, qb_c, vs_c, vw_c, gl], axis=1).T.astype(BF16)

        (qa, kaug, vat, kmean, qb, kc, vc, ksaug, vst, kw, vwt, gates) = _inproj(
            x, mod4, mod4, g_mix[l].reshape(1, d), w_rows, w_t, bd,
            tile_t(q_norm_a[l], H_A) * scale, tile(k_norm_a[l], H_A), tile_t(q_norm_b[l], H_B) * scale,
            tile(k_norm_sel[l], G_B), tile(k_norm_win[l], G_B))

        nba = s // BLK_A
        km = kmean.reshape(bsz, nba, H_A, hd).transpose(0, 2, 1, 3)
        km = jnp.pad(km, ((0, 0), (0, 0), (0, hd - nba), (0, LANES - hd)))
        o_a = _moba(_moba_gate(qa, km), kaug, vat, near[:H_A], far[:H_A])

        chunks = lambda t: t.reshape(bsz, s, G_B, hd).transpose(0, 2, 1, 3).reshape(
            bsz, G_B, n_chunks, CMP_STRIDE * hd)
        w1 = jnp.stack([cmp_w1_k[l], cmp_w1_v[l]]).astype(BF16)
        w2t = jnp.stack([cmp_w2_k[l].T, cmp_w2_v[l].T]).astype(BF16)
        pe = jnp.stack([cmp_pe_k[l], cmp_pe_v[l]]).reshape(2, 1, CMP_LEN * hd)
        pe = jnp.broadcast_to(pe, (2, 8, CMP_LEN * hd)).astype(BF16)
        kcmp, vcmpt = _compress(chunks(kc), chunks(vc), w1, w2t, pe,
                                k_norm_cmp[l].astype(F32).reshape(1, hd))

        oc, selb = _nsa_cmp(qb, kcmp, vcmpt, ovt, n_cmp)
        o_b = _nsa_main(qb, selb, ksaug, vst, kw, vwt, oc, gates, near[H_A:], win, far[H_A:])

        x = _out_ffn(x, o_a, o_b, w_out[l].astype(BF16), mod4, g_ffn[l].reshape(1, d),
                     w_gu[l].astype(BF16), w_down[l].astype(BF16))
    return x
```

```python
import functools
import math

import jax
import jax.numpy as jnp
import numpy as np
from jax import lax
from jax.experimental import pallas as pl
from jax.experimental.pallas import tpu as pltpu

F32 = jnp.float32
BF16 = jnp.bfloat16

HEAD_DIM = 64
LANES = 128
BF16_ROWS = 16
H_A = 8
H_B = 8
G_B = 2
R_B = H_B // G_B
BLK_A = 256
TOPK_A = 3
CMP_LEN = 32
CMP_STRIDE = 16
CMP_HIDDEN = 256
SEL_BLK = 64
SEL_TOPK = 16
WINDOW = 512
N_BUCKETS = 32
MAX_DIST = 128
ADA_CHUNKS = 6
NEG = -1e30
BIG = 1e9
EPS = 1e-6
LOG2_E = math.log2(math.e)

_LOG2_BLK_A = BLK_A.bit_length() - 1
_LOG2_SEL_BLK = SEL_BLK.bit_length() - 1

TQ = 256
TM_IN = 512
TM_FFN = 512
GATE_ROWS = 16
ACC_ROWS = HEAD_DIM + BF16_ROWS
VMEM_LIMIT = 56 * 1024 * 1024


def _dot(a, b):
    return jnp.dot(a, b, preferred_element_type=F32)


def _dot_nt(a, b):
    return lax.dot_general(a, b, (((1,), (1,)), ((), ())), preferred_element_type=F32)


def _split(a):
    hi = a.astype(BF16)
    lo = (a - hi.astype(F32)).astype(BF16)
    return hi, lo


def _dot3(a, b):
    ah, al = _split(a)
    bh, bl = _split(b)
    return _dot(ah, bh) + (_dot(al, bh) + _dot(ah, bl))


def _params(sem):
    return pltpu.CompilerParams(dimension_semantics=sem, vmem_limit_bytes=VMEM_LIMIT)


def _ada_kernel(c_ref, w_ref, b_ref, o_ref):
    c = c_ref[...]
    o_ref[...] = _dot3(c * jax.nn.sigmoid(c), w_ref[...]) + b_ref[...]


def _ada(c, w, b):
    bsz, d = c.shape
    n = w.shape[1]
    tn = 512
    return pl.pallas_call(
        _ada_kernel,
        grid=(n // tn,),
        in_specs=[pl.BlockSpec((bsz, d), lambda j: (0, 0)),
                  pl.BlockSpec((d, tn), lambda j: (0, j)),
                  pl.BlockSpec((1, tn), lambda j: (0, j))],
        out_specs=pl.BlockSpec((bsz, tn), lambda j: (0, j)),
        out_shape=jax.ShapeDtypeStruct((bsz, n), F32),
        compiler_params=_params(("arbitrary",)),
        name="ada",
    )(c, w, b.reshape(1, n))


def _rms_rows(xf, g):
    ms = jnp.mean(xf * xf, axis=-1, keepdims=True)
    return xf * lax.rsqrt(ms + EPS) * g


def _head_norm(t, bd, gain):
    hi, lo = _split(t * t)
    ms = _dot(hi, bd) + _dot(lo, bd)
    return t * lax.rsqrt(ms + EPS) * gain


def _head_norm_t(t, bd, gain):
    hi, lo = _split(t * t)
    ms = _dot(bd, hi) + _dot(bd, lo)
    return t * lax.rsqrt(ms + EPS) * gain


def _inproj_kernel(x_ref, sc_ref, sh_ref, gmix_ref, wr_ref, wt_ref, bd_ref, gqa_ref, gka_ref, gqb_ref,
                   gks_ref, gkw_ref,
                   qa_ref, kaug_ref, va_ref, kmean_ref, qb_ref, kc_ref, vc_ref, ksaug_ref,
                   vs_ref, kw_ref, vw_ref, gates_ref):
    tm = x_ref.shape[1]
    ti = pl.program_id(1)
    xf = x_ref[0]
    h = _rms_rows(xf, gmix_ref[...]) * (1.0 + sc_ref[0, 0]) + sh_ref[0, 0]
    hb = h.astype(BF16)

    def proj(c0, c1):
        return _dot(hb, wr_ref[:, c0:c1])

    def proj_t(r0, r1):
        return _dot_nt(wt_ref[r0:r1, :], hb)

    bd = bd_ref[...]
    bd2 = bd_ref[0:LANES, 0:LANES]
    lane = lax.broadcasted_iota(jnp.int32, (tm, LANES), 1)
    row = lax.broadcasted_iota(jnp.int32, (tm, LANES), 0) + ti * tm
    low = lane < HEAD_DIM

    def k_in_low(pair, odd):
        return pltpu.roll(pair, HEAD_DIM, 1) if odd else pair

    qa = _head_norm_t(proj_t(0, 512), bd, gqa_ref[...])
    for hd in range(H_A):
        qa_ref[0, hd] = qa[hd * HEAD_DIM:(hd + 1) * HEAD_DIM]
    def put_tiles(ref, vt):
        for i in range(tm // TQ):
            ref[0, i] = vt[:, i * TQ:(i + 1) * TQ].astype(BF16)

    put_tiles(va_ref, proj_t(512, 1024))
    qb = _head_norm_t(proj_t(1024, 1536), bd, gqb_ref[...])
    for hd in range(H_B):
        qb_ref[0, hd] = qb[hd * HEAD_DIM:(hd + 1) * HEAD_DIM]
    put_tiles(vs_ref, proj_t(1536, 1664))
    put_tiles(vw_ref, proj_t(1664, 1792))
    gl = jax.nn.sigmoid(proj_t(1792, 1792 + G_B * GATE_ROWS))
    for g in range(G_B):
        gates_ref[0, g] = gl[g * GATE_ROWS:(g + 1) * GATE_ROWS]

    ka = _head_norm(proj(0, 512), bd, gka_ref[...])
    oh_a = jnp.where(lane - HEAD_DIM == (row >> _LOG2_BLK_A), 1.0, 0.0)
    for hd in range(H_A):
        pair = ka[:, (hd // 2) * LANES:(hd // 2 + 1) * LANES]
        kaug_ref[0, hd] = jnp.where(low, k_in_low(pair, hd % 2), oh_a).astype(BF16)
    for i in range(tm // BLK_A):
        kmean_ref[0, i] = jnp.mean(ka[i * BLK_A:(i + 1) * BLK_A], axis=0, keepdims=True)

    kc_ref[0] = proj(512, 640).astype(BF16)
    vc_ref[0] = proj(640, 768).astype(BF16)

    ks = _head_norm(proj(768, 896), bd2, gks_ref[...])
    kw = _head_norm(proj(896, 1024), bd2, gkw_ref[...])
    oh_s = jnp.where(lane == (row >> _LOG2_SEL_BLK), 1.0, 0.0).astype(BF16)
    for g in range(G_B):
        ksaug_ref[0, g] = jnp.concatenate(
            [jnp.where(low, k_in_low(ks, g), 0.0).astype(BF16), oh_s], axis=1)
        kw_ref[0, g] = jnp.where(low, k_in_low(kw, g), 0.0).astype(BF16)


def _inproj(x, sc, sh, gmix, wr, wt, bd, gqa, gka, gqb, gks, gkw):
    bsz, s, d = x.shape
    tm = TM_IN
    nt = s // tm
    nba = s // BLK_A
    const2 = lambda b, t: (0, 0)
    tok3 = lambda b, t: (b, t, 0)
    tok4 = lambda b, t: (b, 0, t, 0)
    tile4 = lambda b, t: (b, t, 0, 0)
    tr4 = lambda b, t: (b, 0, 0, t)
    in_specs = [
        pl.BlockSpec((1, tm, d), tok3),
        pl.BlockSpec((1, 1, 1, d), lambda b, t: (b, 1, 0, 0)),
        pl.BlockSpec((1, 1, 1, d), lambda b, t: (b, 0, 0, 0)),
        pl.BlockSpec((1, d), const2),
        pl.BlockSpec(wr.shape, const2),
        pl.BlockSpec(wt.shape, const2),
        pl.BlockSpec((512, 512), const2),
        pl.BlockSpec((512, tm), const2),
        pl.BlockSpec((1, 512), const2),
        pl.BlockSpec((512, tm), const2),
        pl.BlockSpec((1, LANES), const2),
        pl.BlockSpec((1, LANES), const2),
    ]
    out_shape = [
        jax.ShapeDtypeStruct((bsz, H_A, HEAD_DIM, s), F32),
        jax.ShapeDtypeStruct((bsz, H_A, s, LANES), BF16),
        jax.ShapeDtypeStruct((bsz, s // TQ, 512, TQ), BF16),
        jax.ShapeDtypeStruct((bsz, nba, 1, 512), F32),
        jax.ShapeDtypeStruct((bsz, H_B, HEAD_DIM, s), F32),
        jax.ShapeDtypeStruct((bsz, s, LANES), BF16),
        jax.ShapeDtypeStruct((bsz, s, LANES), BF16),
        jax.ShapeDtypeStruct((bsz, G_B, s, 2 * LANES), BF16),
        jax.ShapeDtypeStruct((bsz, s // TQ, LANES, TQ), BF16),
        jax.ShapeDtypeStruct((bsz, G_B, s, LANES), BF16),
        jax.ShapeDtypeStruct((bsz, s // TQ, LANES, TQ), BF16),
        jax.ShapeDtypeStruct((bsz, G_B, GATE_ROWS, s), F32),
    ]
    out_specs = [
        pl.BlockSpec((1, H_A, HEAD_DIM, tm), tr4),
        pl.BlockSpec((1, H_A, tm, LANES), tok4),
        pl.BlockSpec((1, tm // TQ, 512, TQ), tile4),
        pl.BlockSpec((1, tm // BLK_A, 1, 512), lambda b, t: (b, t, 0, 0)),
        pl.BlockSpec((1, H_B, HEAD_DIM, tm), tr4),
        pl.BlockSpec((1, tm, LANES), tok3),
        pl.BlockSpec((1, tm, LANES), tok3),
        pl.BlockSpec((1, G_B, tm, 2 * LANES), tok4),
        pl.BlockSpec((1, tm // TQ, LANES, TQ), tile4),
        pl.BlockSpec((1, G_B, tm, LANES), tok4),
        pl.BlockSpec((1, tm // TQ, LANES, TQ), tile4),
        pl.BlockSpec((1, G_B, GATE_ROWS, tm), tr4),
    ]
    return pl.pallas_call(
        _inproj_kernel,
        grid=(bsz, nt),
        in_specs=in_specs,
        out_specs=out_specs,
        out_shape=out_shape,
        compiler_params=_params(("parallel", "parallel")),
        name="inproj",
    )(x, sc, sh, gmix, wr, wt, bd, gqa, gka, gqb, gks, gkw)


def _compress_kernel(ck_ref, cv_ref, w1_ref, w2t_ref, pe_ref, gk_ref, ok_ref, ov_ref):
    half = CMP_STRIDE * HEAD_DIM
    for kv, c_ref in enumerate((ck_ref, cv_ref)):
        for g in range(G_B):
            c = c_ref[0, g]
            a = _dot(c, w1_ref[kv, 0:half, :])
            b = _dot(c, w1_ref[kv, half:2 * half, :])
            n = a.shape[0]
            b_next = pltpu.roll(b, n - 1, 0)
            pe_term = _dot(pe_ref[kv], w1_ref[kv])[0:1]
            hid = jax.nn.gelu(a + b_next + pe_term).astype(BF16)
            if kv == 0:
                y = _dot_nt(hid, w2t_ref[kv])
                ms = jnp.mean(y * y, axis=1, keepdims=True)
                y = y * lax.rsqrt(ms + EPS) * gk_ref[...]
                ok_ref[0, g] = jnp.concatenate([y, jnp.zeros_like(y)], axis=1).astype(BF16)
            else:
                ov_ref[0, g] = _dot_nt(w2t_ref[kv], hid).astype(BF16)


def _compress(ck, cv, w1, w2t, pe, gk):
    bsz, g, n, width = ck.shape
    blk = pl.BlockSpec((1, g, n, width), lambda b: (b, 0, 0, 0))
    full = lambda a: pl.BlockSpec(a.shape, lambda b: (0,) * a.ndim)
    return pl.pallas_call(
        _compress_kernel,
        grid=(bsz,),
        in_specs=[blk, blk, full(w1), full(w2t), full(pe), full(gk)],
        out_specs=[pl.BlockSpec((1, g, n, LANES), lambda b: (b, 0, 0, 0)),
                   pl.BlockSpec((1, g, HEAD_DIM, n), lambda b: (b, 0, 0, 0))],
        out_shape=[jax.ShapeDtypeStruct((bsz, g, n, LANES), BF16),
                   jax.ShapeDtypeStruct((bsz, g, HEAD_DIM, n), BF16)],
        compiler_params=_params(("parallel",)),
        name="compress",
    )(ck, cv, w1, w2t, pe, gk)


def _with_ones(vt):
    return jnp.concatenate([vt, jnp.ones((BF16_ROWS, vt.shape[1]), BF16)], axis=0)


def _col_max(s):
    while s.shape[0] > 8:
        half = s.shape[0] // 2
        s = jnp.maximum(s[0:half], s[half:2 * half])
    return jnp.max(s, axis=0, keepdims=True)


def _flash_update(carry, tiles):
    tops = []
    for s, _, bias, top in tiles:
        top = _col_max(s) if top is None else top
        tops.append(top if bias is None else top + bias)
    m_new = functools.reduce(jnp.maximum, tops if carry is None else tops + [carry[0]])
    pv = None
    for s, vt, bias, _ in tiles:
        p = jnp.exp2(s - (m_new if bias is None else m_new - bias)).astype(BF16)
        part = _dot(_with_ones(vt), p)
        pv = part if pv is None else pv + part
    if carry is None:
        return m_new, pv
    return m_new, jnp.exp2(carry[0] - m_new) * carry[1] + pv


def _flash_out(acc):
    return acc[0:HEAD_DIM] / acc[HEAD_DIM:HEAD_DIM + 1]


def _topk_rows(score, index, k):
    picked = jnp.zeros(score.shape, F32)
    for _ in range(k):
        mx = jnp.max(score, axis=0, keepdims=True)
        cand = jnp.where(score == mx, index, jnp.int32(1 << 20))
        first = jnp.min(cand, axis=0, keepdims=True)
        hit = index == first
        picked = jnp.where(hit, 1.0, picked)
        score = jnp.where(hit, -jnp.inf, score)
    return picked


def _tile_rows(j):
    return pl.ds(pl.multiple_of(j * TQ, TQ), TQ)


def _moba_gate_kernel(q_ref, km_ref, o_ref, *, n_sel):
    tg = q_ref.shape[3]
    t0 = pl.program_id(2) * tg
    q = q_ref[0, 0]
    q128 = jnp.concatenate([q, jnp.zeros_like(q)], axis=0)
    gate = _dot3(km_ref[0, 0], q128)
    blk = lax.broadcasted_iota(jnp.int32, gate.shape, 0)
    own = (lax.broadcasted_iota(jnp.int32, gate.shape, 1) + t0) >> _LOG2_BLK_A
    valid = blk < own
    picked = _topk_rows(jnp.where(valid, gate, -jnp.inf), blk, n_sel)
    keep = jnp.where(valid, picked, 0.0) + jnp.where(blk == own, 1.0, 0.0)
    selb = jnp.where(keep > 0.0, 0.0, NEG)
    o_ref[0, 0] = jnp.concatenate([q, selb], axis=0).astype(BF16)


def _moba_gate(qa, km):
    bsz, nh, _, s = qa.shape
    tg = min(s, 1024)
    n_sel = max(1, min(TOPK_A, s // BLK_A - 1))
    return pl.pallas_call(
        functools.partial(_moba_gate_kernel, n_sel=n_sel),
        grid=(bsz, nh, s // tg),
        in_specs=[pl.BlockSpec((1, 1, HEAD_DIM, tg), lambda b, h, t: (b, h, 0, t)),
                  pl.BlockSpec((1, 1, HEAD_DIM, LANES), lambda b, h, t: (b, h, 0, 0))],
        out_specs=pl.BlockSpec((1, 1, LANES, tg), lambda b, h, t: (b, h, 0, t)),
        out_shape=jax.ShapeDtypeStruct((bsz, nh, LANES, s), BF16),
        compiler_params=_params(("parallel", "parallel", "parallel")),
        name="moba_gate",
    )(qa, km)


def _far_tiles(qi):
    n_far = jnp.maximum(qi - 1, 0)
    left = jnp.maximum(n_far - 1, 0)
    off_left = jnp.where((n_far & 1) == 1, 0.0, NEG)
    return n_far >> 1, left, off_left


def _pair_rows(i):
    return pl.ds(pl.multiple_of(i * (2 * TQ), 2 * TQ), 2 * TQ)


class _FarLoop:
    def __init__(self, n_pairs, heads, qk_pair, values, cbs, sa_ref, sb_ref, m_ref, acc_ref):
        self.n_pairs, self.heads, self.qk_pair, self.values, self.cbs = n_pairs, heads, qk_pair, values, cbs
        self.sa_ref, self.sb_ref, self.m_ref, self.acc_ref = sa_ref, sb_ref, m_ref, acc_ref
        self.last = jnp.maximum(n_pairs - 1, 0)

    def fetch(self, buf_ref, i):
        tops = []
        for h in self.heads:
            s = self.qk_pair(h, i)
            buf_ref[h] = s
            tops.append((_col_max(s[0:TQ]), _col_max(s[TQ:2 * TQ])))
        return tuple(tops)

    def consume(self, buf_ref, tops, i, off):
        for h in self.heads:
            bias = self.cbs[h] if off is None else self.cbs[h] + off
            m, acc = _flash_update((self.m_ref[h], self.acc_ref[h]), [
                (buf_ref[h, 0:TQ, :], self.values(h, 2 * i), bias, tops[h][0]),
                (buf_ref[h, TQ:2 * TQ, :], self.values(h, 2 * i + 1), bias, tops[h][1])])
            self.m_ref[h] = m
            self.acc_ref[h] = acc

    def first(self):
        return self.fetch(self.sa_ref, 0)

    def run(self, tops_first):
        def two_pairs(q, tops_a):
            ia = 2 * q
            ib = jnp.minimum(ia + 1, self.last)
            off_b = jnp.where(ia + 1 < self.n_pairs, 0.0, NEG)
            tops_b = self.fetch(self.sb_ref, ib)
            self.consume(self.sa_ref, tops_a, ia, None)
            tops_next = self.fetch(self.sa_ref, jnp.minimum(ia + 2, self.last))
            self.consume(self.sb_ref, tops_b, ib, off_b)
            return tops_next

        lax.fori_loop(0, (self.n_pairs + 1) >> 1, two_pairs, tops_first)


def _moba_kernel(q_ref, k_ref, vt_ref, tab_ref, cb_ref, o_ref, sa_ref, sb_ref, m_ref, acc_ref):
    qi = pl.program_id(2)
    jp = jnp.maximum(qi - 1, 0)
    off_p = jnp.where(qi >= 1, 0.0, NEG)
    n_pairs, jl, off_l = _far_tiles(qi)
    heads = range(2)
    qaug = [q_ref[0, hh] for hh in heads]
    cbs = [cb_ref[hh][0:1, 0:1] for hh in heads]

    def scores(hh, j):
        return _dot(k_ref[0, hh, _tile_rows(j), :], qaug[hh])

    def scores2(hh, i):
        return _dot(k_ref[0, hh, _pair_rows(i), :], qaug[hh])

    def values(hh, j):
        return vt_ref[0, j, hh * HEAD_DIM:(hh + 1) * HEAD_DIM, :]

    far = _FarLoop(n_pairs, heads, scores2, values, cbs, sa_ref, sb_ref, m_ref, acc_ref)

    s_own = [scores(hh, qi) for hh in heads]
    s_prev = [scores(hh, jp) for hh in heads]
    s_left = [scores(hh, jl) for hh in heads]
    tops_first = far.first()
    for hh in heads:
        m, acc = _flash_update(None, [
            (s_own[hh] + tab_ref[hh, TQ:2 * TQ, :], values(hh, qi), None, None),
            (s_prev[hh] + tab_ref[hh, 0:TQ, :] + off_p, values(hh, jp), None, None),
            (s_left[hh], values(hh, jl), cbs[hh] + off_l, None)])
        m_ref[hh] = m
        acc_ref[hh] = acc
    far.run(tops_first)
    out_t = jnp.concatenate([_flash_out(acc_ref[hh]) for hh in heads], axis=0)
    o_ref[0] = out_t.T.astype(BF16)


def _flash_scratch(n_heads):
    return [pltpu.VMEM((n_heads, 2 * TQ, TQ), F32),
            pltpu.VMEM((n_heads, 2 * TQ, TQ), F32),
            pltpu.VMEM((n_heads, 1, TQ), F32),
            pltpu.VMEM((n_heads, ACC_ROWS, TQ), F32)]


def _moba(qaug, kaug, vat, tab, cb):
    bsz, _, _, s = qaug.shape
    nq = s // TQ
    return pl.pallas_call(
        _moba_kernel,
        grid=(bsz, H_A // 2, nq),
        in_specs=[
            pl.BlockSpec((1, 2, LANES, TQ), lambda b, hp, qi: (b, hp, 0, qi)),
            pl.BlockSpec((1, 2, s, LANES), lambda b, hp, qi: (b, hp, 0, 0)),
            pl.BlockSpec((1, nq, LANES, TQ), lambda b, hp, qi: (b, 0, hp, 0)),
            pl.BlockSpec((2, 2 * TQ, TQ), lambda b, hp, qi: (hp, 0, 0)),
            pl.BlockSpec((2, 8, LANES), lambda b, hp, qi: (hp, 0, 0)),
        ],
        out_specs=pl.BlockSpec((1, TQ, LANES), lambda b, hp, qi: (b, qi, hp)),
        out_shape=jax.ShapeDtypeStruct((bsz, s, H_A * HEAD_DIM), BF16),
        scratch_shapes=_flash_scratch(2),
        compiler_params=_params(("parallel", "parallel", "arbitrary")),
        name="moba",
    )(qaug, kaug, vat, tab, cb)


def _nsa_cmp_kernel(q_ref, kc_ref, vct_ref, ovt_ref, oc_ref, selb_ref, *, n_sel, n_cmp):
    qi = pl.program_id(2)
    ncp = kc_ref.shape[2]
    t0 = qi * TQ
    n_idx = lax.broadcasted_iota(jnp.int32, (ncp, TQ), 0)
    t_idx = lax.broadcasted_iota(jnp.int32, (ncp, TQ), 1) + t0
    mask = (n_idx * CMP_STRIDE + (CMP_LEN - 1) <= t_idx) & (n_idx < n_cmp)
    kc = kc_ref[0, 0]
    vct = vct_ref[0, 0]
    zeros = jnp.zeros((HEAD_DIM, TQ), BF16)
    psum = jnp.zeros((ncp, TQ), F32)
    for r in range(R_B):
        q = jnp.concatenate([q_ref[0, r].astype(BF16), zeros], axis=0)
        z = jnp.where(mask, _dot(kc, q), NEG)
        e = jnp.exp2(z - jnp.max(z, axis=0, keepdims=True))
        p = jnp.where(mask, e / jnp.sum(e, axis=0, keepdims=True), 0.0)
        oc_ref[0, r] = _dot(vct, p.astype(BF16))
        psum = psum + p
    ph, pl_ = _split(psum)
    ovt = ovt_ref[...]
    imp_t = _dot(ovt, ph) + _dot(ovt, pl_)
    blk = lax.broadcasted_iota(jnp.int32, (LANES, TQ), 0)
    cur = (lax.broadcasted_iota(jnp.int32, (LANES, TQ), 1) + t0) >> _LOG2_SEL_BLK
    ok = blk <= cur
    forced = (blk == 0) | (blk == cur) | (blk == cur - 1)
    score = jnp.where(ok, jnp.where(forced, BIG, imp_t), -jnp.inf)
    picked = _topk_rows(score, blk, n_sel)
    selb_ref[0, 0] = jnp.where(ok & (picked > 0.0), 0.0, NEG).astype(BF16)


def _nsa_cmp(qb, kcmp, vcmpt, ovt, n_cmp):
    bsz, _, _, s = qb.shape
    nq = s // TQ
    ncp = kcmp.shape[2]
    n_sel = min(SEL_TOPK, s // SEL_BLK)
    return pl.pallas_call(
        functools.partial(_nsa_cmp_kernel, n_sel=n_sel, n_cmp=n_cmp),
        grid=(bsz, G_B, nq),
        in_specs=[
            pl.BlockSpec((1, R_B, HEAD_DIM, TQ), lambda b, g, qi: (b, g, 0, qi)),
            pl.BlockSpec((1, 1, ncp, LANES), lambda b, g, qi: (b, g, 0, 0)),
            pl.BlockSpec((1, 1, HEAD_DIM, ncp), lambda b, g, qi: (b, g, 0, 0)),
            pl.BlockSpec((LANES, ncp), lambda b, g, qi: (0, 0)),
        ],
        out_specs=[
            pl.BlockSpec((1, R_B, HEAD_DIM, TQ), lambda b, g, qi: (b, g, 0, qi)),
            pl.BlockSpec((1, 1, LANES, TQ), lambda b, g, qi: (b, g, 0, qi)),
        ],
        out_shape=[jax.ShapeDtypeStruct((bsz, H_B, HEAD_DIM, s), F32),
                   jax.ShapeDtypeStruct((bsz, G_B, LANES, s), BF16)],
        compiler_params=_params(("parallel", "parallel", "parallel")),
        name="nsa_cmp",
    )(qb, kcmp, vcmpt, ovt)


def _nsa_main_kernel(q_ref, selb_ref, ks_ref, vst_ref, kw_ref, vwt_ref, oc_ref, gates_ref,
                     tabs_ref, tabw_ref, cb_ref, o_ref, sa_ref, sb_ref, m_ref, acc_ref):
    qi = pl.program_id(2)
    selb = selb_ref[0, 0]
    gates = gates_ref[0, 0]
    zeros = jnp.zeros((HEAD_DIM, TQ), BF16)
    j1 = jnp.maximum(qi - 1, 0)
    j2 = jnp.maximum(qi - 2, 0)
    off1 = jnp.where(qi >= 1, 0.0, NEG)
    off2 = jnp.where(qi >= 2, 0.0, NEG)
    n_pairs, jl, off_l = _far_tiles(qi)

    heads = range(R_B)
    qw = [jnp.concatenate([q_ref[0, r].astype(BF16), zeros], axis=0) for r in heads]
    qs = [jnp.concatenate([qw[r], selb], axis=0) for r in heads]
    cbs = [cb_ref[r][0:1, 0:1] for r in heads]

    def s_sel(r, j):
        return _dot(ks_ref[0, 0, _tile_rows(j), :], qs[r])

    def s_win(r, j):
        return _dot(kw_ref[0, 0, _tile_rows(j), :], qw[r])

    def s_sel2(r, i):
        return _dot(ks_ref[0, 0, _pair_rows(i), :], qs[r])

    far = _FarLoop(n_pairs, heads, s_sel2, lambda r, j: vst_ref[0, j], cbs, sa_ref, sb_ref, m_ref, acc_ref)

    sw = [[s_win(r, j) for r in heads] for j in (qi, j1, j2)]
    ss = [[s_sel(r, j) for r in heads] for j in (qi, j1, jl)]
    tops_first = far.first()

    win = []
    for r in heads:
        _, acc = _flash_update(None, [
            (sw[0][r] + tabw_ref[r, 2 * TQ:3 * TQ, :], vwt_ref[0, qi], None, None),
            (sw[1][r] + tabw_ref[r, TQ:2 * TQ, :] + off1, vwt_ref[0, j1], None, None),
            (sw[2][r] + tabw_ref[r, 0:TQ, :] + off2, vwt_ref[0, j2], None, None)])
        win.append(_flash_out(acc))
    for r in heads:
        m, acc = _flash_update(None, [
            (ss[0][r] + tabs_ref[r, TQ:2 * TQ, :], vst_ref[0, qi], None, None),
            (ss[1][r] + tabs_ref[r, 0:TQ, :] + off1, vst_ref[0, j1], None, None),
            (ss[2][r], vst_ref[0, jl], cbs[r] + off_l, None)])
        m_ref[r] = m
        acc_ref[r] = acc
    far.run(tops_first)

    outs = [gates[3 * r:3 * r + 1] * oc_ref[0, r]
            + gates[3 * r + 1:3 * r + 2] * _flash_out(acc_ref[r])
            + gates[3 * r + 2:3 * r + 3] * win[r] for r in heads]
    o_ref[0] = jnp.concatenate(outs, axis=0).T.astype(BF16)


def _nsa_main(qb, selb, ksaug, vst, kw, vwt, oc, gates, tabs, tabw, cb):
    bsz, _, _, s = qb.shape
    nq = s // TQ
    return pl.pallas_call(
        _nsa_main_kernel,
        grid=(bsz, G_B, nq),
        in_specs=[
            pl.BlockSpec((1, R_B, HEAD_DIM, TQ), lambda b, g, qi: (b, g, 0, qi)),
            pl.BlockSpec((1, 1, LANES, TQ), lambda b, g, qi: (b, g, 0, qi)),
            pl.BlockSpec((1, 1, s, 2 * LANES), lambda b, g, qi: (b, g, 0, 0)),
            pl.BlockSpec((1, nq, HEAD_DIM, TQ), lambda b, g, qi: (b, 0, g, 0)),
            pl.BlockSpec((1, 1, s, LANES), lambda b, g, qi: (b, g, 0, 0)),
            pl.BlockSpec((1, nq, HEAD_DIM, TQ), lambda b, g, qi: (b, 0, g, 0)),
            pl.BlockSpec((1, R_B, HEAD_DIM, TQ), lambda b, g, qi: (b, g, 0, qi)),
            pl.BlockSpec((1, 1, GATE_ROWS, TQ), lambda b, g, qi: (b, g, 0, qi)),
            pl.BlockSpec((R_B, 2 * TQ, TQ), lambda b, g, qi: (g, 0, 0)),
            pl.BlockSpec((R_B, 3 * TQ, TQ), lambda b, g, qi: (g, 0, 0)),
            pl.BlockSpec((R_B, 8, LANES), lambda b, g, qi: (g, 0, 0)),
        ],
        out_specs=pl.BlockSpec((1, TQ, R_B * HEAD_DIM), lambda b, g, qi: (b, qi, g)),
        out_shape=jax.ShapeDtypeStruct((bsz, s, H_B * HEAD_DIM), BF16),
        scratch_shapes=_flash_scratch(R_B),
        compiler_params=_params(("parallel", "parallel", "arbitrary")),
        name="nsa_main",
    )(qb, selb, ksaug, vst, kw, vwt, oc, gates, tabs, tabw, cb)


def _out_ffn_kernel(x_ref, oa_ref, ob_ref, wo_ref, gtm_ref, gffn_ref, scf_ref, shf_ref, gtf_ref,
                    wg_ref, wu_ref, wd_ref, o_ref, x1_ref, h_ref, acc_ref):
    f = pl.program_id(2)
    half = oa_ref.shape[2]

    @pl.when(f == 0)
    def _():
        mix = _dot(oa_ref[0], wo_ref[0:half, :]) + _dot(ob_ref[0], wo_ref[half:2 * half, :])
        x1 = x_ref[0] + gtm_ref[0, 0] * mix
        x1_ref[...] = x1
        h = _rms_rows(x1, gffn_ref[...]) * (1.0 + scf_ref[0, 0]) + shf_ref[0, 0]
        h_ref[...] = h.astype(BF16)

    h = h_ref[...]
    gate = _dot(h, wg_ref[...])
    up = _dot(h, wu_ref[...])
    act = (gate * jax.nn.sigmoid(gate) * up).astype(BF16)
    part = _dot(act, wd_ref[...])

    @pl.when(f == 0)
    def _():
        acc_ref[...] = part

    @pl.when(f > 0)
    def _():
        acc_ref[...] += part

    @pl.when(f == pl.num_programs(2) - 1)
    def _():
        o_ref[0] = x1_ref[...] + gtf_ref[0, 0] * acc_ref[...]


def _out_ffn(x, oa, ob, wo, mod4, gffn, wgu, wd):
    bsz, s, d = x.shape
    fh = wd.shape[0]
    tm = TM_FFN
    tf = fh // 2 if (fh // 2) % LANES == 0 else fh
    nf = fh // tf
    tok = lambda b, t, f: (b, t, 0)
    modspec = lambda k: pl.BlockSpec((1, 1, 1, d), lambda b, t, f: (b, k, 0, 0))
    return pl.pallas_call(
        _out_ffn_kernel,
        grid=(bsz, s // tm, nf),
        in_specs=[
            pl.BlockSpec((1, tm, d), tok),
            pl.BlockSpec((1, tm, oa.shape[2]), tok),
            pl.BlockSpec((1, tm, ob.shape[2]), tok),
            pl.BlockSpec((d, d), lambda b, t, f: (0, 0)),
            modspec(2),
            pl.BlockSpec((1, d), lambda b, t, f: (0, 0)),
            modspec(4),
            modspec(3),
            modspec(5),
            pl.BlockSpec((d, tf), lambda b, t, f: (0, f)),
            pl.BlockSpec((d, tf), lambda b, t, f: (0, f + nf)),
            pl.BlockSpec((tf, d), lambda b, t, f: (f, 0)),
        ],
        out_specs=pl.BlockSpec((1, tm, d), tok),
        out_shape=jax.ShapeDtypeStruct((bsz, s, d), F32),
        scratch_shapes=[pltpu.VMEM((tm, d), F32), pltpu.VMEM((tm, d), BF16), pltpu.VMEM((tm, d), F32)],
        compiler_params=_params(("parallel", "parallel", "arbitrary")),
        name="out_ffn",
    )(x, oa, ob, wo, mod4, gffn, mod4, mod4, mod4, wgu, wgu, wd)


def _t5_bucket_np(d):
    max_exact = N_BUCKETS // 2
    d = np.maximum(d, 0)
    df = np.maximum(d, 1).astype(np.float64)
    large = max_exact + (np.log(df / max_exact) / math.log(MAX_DIST / max_exact)
                         * (N_BUCKETS - max_exact)).astype(np.int64)
    large = np.minimum(large, N_BUCKETS - 1)
    return np.where(d < max_exact, d, large).astype(np.int32)


def _bias_expand_kernel(tab_ref, bucket_ref, o_ref):
    hd = pl.program_id(0)
    bucket = bucket_ref[...]
    acc = jnp.full(bucket.shape, NEG, F32)
    for b in range(N_BUCKETS):
        acc = jnp.where(bucket == b, tab_ref[hd, b], acc)
    o_ref[0] = acc


def _bias_expand(tab, bucket):
    nh = tab.shape[0]
    return pl.pallas_call(
        _bias_expand_kernel,
        grid=(nh,),
        in_specs=[pl.BlockSpec(memory_space=pltpu.SMEM),
                  pl.BlockSpec(bucket.shape, lambda h: (0, 0))],
        out_specs=pl.BlockSpec((1,) + bucket.shape, lambda h: (h, 0, 0)),
        out_shape=jax.ShapeDtypeStruct((nh,) + bucket.shape, F32),
        compiler_params=_params(("parallel",)),
        name="bias_expand",
    )(tab, jnp.asarray(bucket, jnp.int32))


def _bias_tables(rel_bias):
    tab = rel_bias.T.astype(F32) * LOG2_E
    i = np.arange(TQ)[None, :]
    d_near = i + TQ - np.arange(2 * TQ)[:, None]
    near = _bias_expand(tab, np.where(d_near >= 0, _t5_bucket_np(d_near), -1))
    d_win = i + 2 * TQ - np.arange(3 * TQ)[:, None]
    ok_win = (d_win >= 0) & (d_win < WINDOW)
    win = _bias_expand(tab[H_A:], np.where(ok_win, _t5_bucket_np(d_win), -1))
    far = jnp.broadcast_to(tab[:, N_BUCKETS - 1][:, None, None], (tab.shape[0], 8, LANES))
    return near, win, far


def _overlap_t(n_cmp_pad, n_cmp):
    cs = np.arange(n_cmp_pad)[None, :] * CMP_STRIDE
    ss = np.arange(LANES)[:, None] * SEL_BLK
    ov = (cs < ss + SEL_BLK) & (cs + CMP_LEN > ss) & (np.arange(n_cmp_pad)[None, :] < n_cmp)
    return jnp.asarray(ov.astype(np.float32), BF16)


def _block_diag(n):
    m = (np.arange(n)[:, None] // HEAD_DIM == np.arange(n)[None, :] // HEAD_DIM)
    return jnp.asarray(m.astype(np.float32) / HEAD_DIM, BF16)


def kernel(x, c, rel_bias, w_ada, b_ada, g_mix, w_in, q_norm_a, k_norm_a, q_norm_b, k_norm_cmp,
           k_norm_sel, k_norm_win, cmp_pe_k, cmp_w1_k, cmp_w2_k, cmp_pe_v, cmp_w1_v, cmp_w2_v,
           w_out, g_ffn, w_gu, w_down):
    bsz, s, d = x.shape
    depth = w_ada.shape[0]
    assert s % TM_IN == 0 and s % TM_FFN == 0 and s % (2 * TQ) == 0
    assert s // BLK_A <= HEAD_DIM and s // SEL_BLK <= LANES
    assert WINDOW == 2 * TQ and BLK_A == TQ and MAX_DIST <= TQ
    n_chunks = s // CMP_STRIDE
    n_cmp = (s - CMP_LEN) // CMP_STRIDE + 1
    scale = HEAD_DIM ** -0.5 * LOG2_E
    hd = HEAD_DIM

    near, win, far = _bias_tables(rel_bias)
    ovt = _overlap_t(n_chunks, n_cmp)
    bd = _block_diag(512)
    tile = lambda g, n: jnp.tile(g.astype(F32), n).reshape(1, -1)
    tile_t = lambda g, n: jnp.broadcast_to(jnp.tile(g.astype(F32), n)[:, None], (n * hd, TM_IN))

    for l in range(depth):
        mod = _ada(c, w_ada[l], b_ada[l])
        mod4 = mod.reshape(bsz, ADA_CHUNKS, 1, d)

        wl = w_in[l]
        cols = np.cumsum([0, H_A * hd, H_A * hd, H_A * hd, H_B * hd] + [G_B * hd] * 6)
        qa_c, ka_c, va_c, qb_c, kc_c, vc_c, ks_c, vs_c, kw_c, vw_c = [
            wl[:, int(a):int(b)] for a, b in zip(cols[:-1], cols[1:])]
        gl = wl[:, int(cols[-1]):].reshape(d, G_B, 3 * R_B)
        gl = jnp.pad(gl, ((0, 0), (0, 0), (0, GATE_ROWS - 3 * R_B))).reshape(d, G_B * GATE_ROWS)
        w_rows = jnp.concatenate([ka_c, kc_c, vc_c, ks_c, kw_c], axis=1).astype(BF16)
        w_t = jnp.concatenate([qa_c, va_c, qb_c, vs_c, vw_c, gl], axis=1).T.astype(BF16)

        (qa, kaug, vat, kmean, qb, kc, vc, ksaug, vst, kw, vwt, gates) = _inproj(
            x, mod4, mod4, g_mix[l].reshape(1, d), w_rows, w_t, bd,
            tile_t(q_norm_a[l], H_A) * scale, tile(k_norm_a[l], H_A), tile_t(q_norm_b[l], H_B) * scale,
            tile(k_norm_sel[l], G_B), tile(k_norm_win[l], G_B))

        nba = s // BLK_A
        km = kmean.reshape(bsz, nba, H_A, hd).transpose(0, 2, 1, 3)
        km = jnp.pad(km, ((0, 0), (0, 0), (0, hd - nba), (0, LANES - hd)))
        o_a = _moba(_moba_gate(qa, km), kaug, vat, near[:H_A], far[:H_A])

        chunks = lambda t: t.reshape(bsz, s, G_B, hd).transpose(0, 2, 1, 3).reshape(
            bsz, G_B, n_chunks, CMP_STRIDE * hd)
        w1 = jnp.stack([cmp_w1_k[l], cmp_w1_v[l]]).astype(BF16)
        w2t = jnp.stack([cmp_w2_k[l].T, cmp_w2_v[l].T]).astype(BF16)
        pe = jnp.stack([cmp_pe_k[l], cmp_pe_v[l]]).reshape(2, 1, CMP_LEN * hd)
        pe = jnp.broadcast_to(pe, (2, 8, CMP_LEN * hd)).astype(BF16)
        kcmp, vcmpt = _compress(chunks(kc), chunks(vc), w1, w2t, pe,
                                k_norm_cmp[l].astype(F32).reshape(1, hd))

        oc, selb = _nsa_cmp(qb, kcmp, vcmpt, ovt, n_cmp)
        o_b = _nsa_main(qb, selb, ksaug, vst, kw, vwt, oc, gates, near[H_A:], win, far[H_A:])

        x = _out_ffn(x, o_a, o_b, w_out[l].astype(BF16), mod4, g_ffn[l].reshape(1, d),
                     w_gu[l].astype(BF16), w_down[l].astype(BF16))
    return x
```

```python
import functools
import math

import jax
import jax.numpy as jnp
import numpy as np
from jax import lax
from jax.experimental import pallas as pl
from jax.experimental.pallas import tpu as pltpu

F32 = jnp.float32
BF16 = jnp.bfloat16

HEAD_DIM = 64
LANES = 128
BF16_ROWS = 16
H_A = 8
H_B = 8
G_B = 2
R_B = H_B // G_B
BLK_A = 256
TOPK_A = 3
CMP_LEN = 32
CMP_STRIDE = 16
CMP_HIDDEN = 256
SEL_BLK = 64
SEL_TOPK = 16
WINDOW = 512
N_BUCKETS = 32
MAX_DIST = 128
ADA_CHUNKS = 6
NEG = -1e30
BIG = 1e9
EPS = 1e-6
LOG2_E = math.log2(math.e)

_LOG2_BLK_A = BLK_A.bit_length() - 1
_LOG2_SEL_BLK = SEL_BLK.bit_length() - 1

TQ = 256
TM_IN = 512
TM_FFN = 512
GATE_ROWS = 16
ACC_ROWS = HEAD_DIM + BF16_ROWS
VMEM_LIMIT = 56 * 1024 * 1024


def _dot(a, b):
    return jnp.dot(a, b, preferred_element_type=F32)


def _dot_nt(a, b):
    return lax.dot_general(a, b, (((1,), (1,)), ((), ())), preferred_element_type=F32)


def _split(a):
    hi = a.astype(BF16)
    lo = (a - hi.astype(F32)).astype(BF16)
    return hi, lo


def _dot3(a, b):
    ah, al = _split(a)
    bh, bl = _split(b)
    return _dot(ah, bh) + (_dot(al, bh) + _dot(ah, bl))


def _params(sem):
    return pltpu.CompilerParams(dimension_semantics=sem, vmem_limit_bytes=VMEM_LIMIT)


def _ada_kernel(c_ref, w_ref, b_ref, o_ref):
    c = c_ref[...]
    o_ref[...] = _dot3(c * jax.nn.sigmoid(c), w_ref[...]) + b_ref[...]


def _ada(c, w, b):
    bsz, d = c.shape
    n = w.shape[1]
    tn = 512
    return pl.pallas_call(
        _ada_kernel,
        grid=(n // tn,),
        in_specs=[pl.BlockSpec((bsz, d), lambda j: (0, 0)),
                  pl.BlockSpec((d, tn), lambda j: (0, j)),
                  pl.BlockSpec((1, tn), lambda j: (0, j))],
        out_specs=pl.BlockSpec((bsz, tn), lambda j: (0, j)),
        out_shape=jax.ShapeDtypeStruct((bsz, n), F32),
        compiler_params=_params(("arbitrary",)),
        name="ada",
    )(c, w, b.reshape(1, n))


def _rms_rows(xf, g):
    ms = jnp.mean(xf * xf, axis=-1, keepdims=True)
    return xf * lax.rsqrt(ms + EPS) * g


def _head_norm(t, bd, gain):
    hi, lo = _split(t * t)
    ms = _dot(hi, bd) + _dot(lo, bd)
    return t * lax.rsqrt(ms + EPS) * gain


def _head_norm_t(t, bd, gain):
    hi, lo = _split(t * t)
    ms = _dot(bd, hi) + _dot(bd, lo)
    return t * lax.rsqrt(ms + EPS) * gain


def _inproj_kernel(x_ref, sc_ref, sh_ref, gmix_ref, wr_ref, wt_ref, bd_ref, gqa_ref, gka_ref, gqb_ref,
                   gks_ref, gkw_ref,
                   qa_ref, kaug_ref, va_ref, kmean_ref, qb_ref, kc_ref, vc_ref, ksaug_ref,
                   vs_ref, kw_ref, vw_ref, gates_ref):
    tm = x_ref.shape[1]
    ti = pl.program_id(1)
    xf = x_ref[0]
    h = _rms_rows(xf, gmix_ref[...]) * (1.0 + sc_ref[0, 0]) + sh_ref[0, 0]
    hb = h.astype(BF16)

    def proj(c0, c1):
        return _dot(hb, wr_ref[:, c0:c1])

    def proj_t(r0, r1):
        return _dot_nt(wt_ref[r0:r1, :], hb)

    bd = bd_ref[...]
    bd2 = bd_ref[0:LANES, 0:LANES]
    lane = lax.broadcasted_iota(jnp.int32, (tm, LANES), 1)
    row = lax.broadcasted_iota(jnp.int32, (tm, LANES), 0) + ti * tm
    low = lane < HEAD_DIM

    def k_in_low(pair, odd):
        return pltpu.roll(pair, HEAD_DIM, 1) if odd else pair

    qa = _head_norm_t(proj_t(0, 512), bd, gqa_ref[...])
    for hd in range(H_A):
        qa_ref[0, hd] = qa[hd * HEAD_DIM:(hd + 1) * HEAD_DIM]
    def put_tiles(ref, vt):
        for i in range(tm // TQ):
            ref[0, i] = vt[:, i * TQ:(i + 1) * TQ].astype(BF16)

    put_tiles(va_ref, proj_t(512, 1024))
    qb = _head_norm_t(proj_t(1024, 1536), bd, gqb_ref[...])
    for hd in range(H_B):
        qb_ref[0, hd] = qb[hd * HEAD_DIM:(hd + 1) * HEAD_DIM]
    put_tiles(vs_ref, proj_t(1536, 1664))
    put_tiles(vw_ref, proj_t(1664, 1792))
    gl = jax.nn.sigmoid(proj_t(1792, 1792 + G_B * GATE_ROWS))
    for g in range(G_B):
        gates_ref[0, g] = gl[g * GATE_ROWS:(g + 1) * GATE_ROWS]

    ka = _head_norm(proj(0, 512), bd, gka_ref[...])
    oh_a = jnp.where(lane - HEAD_DIM == (row >> _LOG2_BLK_A), 1.0, 0.0)
    for hd in range(H_A):
        pair = ka[:, (hd // 2) * LANES:(hd // 2 + 1) * LANES]
        kaug_ref[0, hd] = jnp.where(low, k_in_low(pair, hd % 2), oh_a).astype(BF16)
    for i in range(tm // BLK_A):
        kmean_ref[0, i] = jnp.mean(ka[i * BLK_A:(i + 1) * BLK_A], axis=0, keepdims=True)

    kc_ref[0] = proj(512, 640).astype(BF16)
    vc_ref[0] = proj(640, 768).astype(BF16)

    ks = _head_norm(proj(768, 896), bd2, gks_ref[...])
    kw = _head_norm(proj(896, 1024), bd2, gkw_ref[...])
    oh_s = jnp.where(lane == (row >> _LOG2_SEL_BLK), 1.0, 0.0).astype(BF16)
    for g in range(G_B):
        ksaug_ref[0, g] = jnp.concatenate(
            [jnp.where(low, k_in_low(ks, g), 0.0).astype(BF16), oh_s], axis=1)
        kw_ref[0, g] = jnp.where(low, k_in_low(kw, g), 0.0).astype(BF16)


def _inproj(x, sc, sh, gmix, wr, wt, bd, gqa, gka, gqb, gks, gkw):
    bsz, s, d = x.shape
    tm = TM_IN
    nt = s // tm
    nba = s // BLK_A
    const2 = lambda b, t: (0, 0)
    tok3 = lambda b, t: (b, t, 0)
    tok4 = lambda b, t: (b, 0, t, 0)
    tile4 = lambda b, t: (b, t, 0, 0)
    tr4 = lambda b, t: (b, 0, 0, t)
    in_specs = [
        pl.BlockSpec((1, tm, d), tok3),
        pl.BlockSpec((1, 1, 1, d), lambda b, t: (b, 1, 0, 0)),
        pl.BlockSpec((1, 1, 1, d), lambda b, t: (b, 0, 0, 0)),
        pl.BlockSpec((1, d), const2),
        pl.BlockSpec(wr.shape, const2),
        pl.BlockSpec(wt.shape, const2),
        pl.BlockSpec((512, 512), const2),
        pl.BlockSpec((512, tm), const2),
        pl.BlockSpec((1, 512), const2),
        pl.BlockSpec((512, tm), const2),
        pl.BlockSpec((1, LANES), const2),
        pl.BlockSpec((1, LANES), const2),
    ]
    out_shape = [
        jax.ShapeDtypeStruct((bsz, H_A, HEAD_DIM, s), F32),
        jax.ShapeDtypeStruct((bsz, H_A, s, LANES), BF16),
        jax.ShapeDtypeStruct((bsz, s // TQ, 512, TQ), BF16),
        jax.ShapeDtypeStruct((bsz, nba, 1, 512), F32),
        jax.ShapeDtypeStruct((bsz, H_B, HEAD_DIM, s), F32),
        jax.ShapeDtypeStruct((bsz, s, LANES), BF16),
        jax.ShapeDtypeStruct((bsz, s, LANES), BF16),
        jax.ShapeDtypeStruct((bsz, G_B, s, 2 * LANES), BF16),
        jax.ShapeDtypeStruct((bsz, s // TQ, LANES, TQ), BF16),
        jax.ShapeDtypeStruct((bsz, G_B, s, LANES), BF16),
        jax.ShapeDtypeStruct((bsz, s // TQ, LANES, TQ), BF16),
        jax.ShapeDtypeStruct((bsz, G_B, GATE_ROWS, s), F32),
    ]
    out_specs = [
        pl.BlockSpec((1, H_A, HEAD_DIM, tm), tr4),
        pl.BlockSpec((1, H_A, tm, LANES), tok4),
        pl.BlockSpec((1, tm // TQ, 512, TQ), tile4),
        pl.BlockSpec((1, tm // BLK_A, 1, 512), lambda b, t: (b, t, 0, 0)),
        pl.BlockSpec((1, H_B, HEAD_DIM, tm), tr4),
        pl.BlockSpec((1, tm, LANES), tok3),
        pl.BlockSpec((1, tm, LANES), tok3),
        pl.BlockSpec((1, G_B, tm, 2 * LANES), tok4),
        pl.BlockSpec((1, tm // TQ, LANES, TQ), tile4),
        pl.BlockSpec((1, G_B, tm, LANES), tok4),
        pl.BlockSpec((1, tm // TQ, LANES, TQ), tile4),
        pl.BlockSpec((1, G_B, GATE_ROWS, tm), tr4),
    ]
    return pl.pallas_call(
        _inproj_kernel,
        grid=(bsz, nt),
        in_specs=in_specs,
        out_specs=out_specs,
        out_shape=out_shape,
        compiler_params=_params(("parallel", "parallel")),
        name="inproj",
    )(x, sc, sh, gmix, wr, wt, bd, gqa, gka, gqb, gks, gkw)


def _compress_kernel(ck_ref, cv_ref, w1_ref, w2t_ref, pe_ref, gk_ref, ok_ref, ov_ref):
    half = CMP_STRIDE * HEAD_DIM
    for kv, c_ref in enumerate((ck_ref, cv_ref)):
        for g in range(G_B):
            c = c_ref[0, g]
            a = _dot(c, w1_ref[kv, 0:half, :])
            b = _dot(c, w1_ref[kv, half:2 * half, :])
            n = a.shape[0]
            b_next = pltpu.roll(b, n - 1, 0)
            pe_term = _dot(pe_ref[kv], w1_ref[kv])[0:1]
            hid = jax.nn.gelu(a + b_next + pe_term).astype(BF16)
            if kv == 0:
                y = _dot_nt(hid, w2t_ref[kv])
                ms = jnp.mean(y * y, axis=1, keepdims=True)
                y = y * lax.rsqrt(ms + EPS) * gk_ref[...]
                ok_ref[0, g] = jnp.concatenate([y, jnp.zeros_like(y)], axis=1).astype(BF16)
            else:
                ov_ref[0, g] = _dot_nt(w2t_ref[kv], hid).astype(BF16)


def _compress(ck, cv, w1, w2t, pe, gk):
    bsz, g, n, width = ck.shape
    blk = pl.BlockSpec((1, g, n, width), lambda b: (b, 0, 0, 0))
    full = lambda a: pl.BlockSpec(a.shape, lambda b: (0,) * a.ndim)
    return pl.pallas_call(
        _compress_kernel,
        grid=(bsz,),
        in_specs=[blk, blk, full(w1), full(w2t), full(pe), full(gk)],
        out_specs=[pl.BlockSpec((1, g, n, LANES), lambda b: (b, 0, 0, 0)),
                   pl.BlockSpec((1, g, HEAD_DIM, n), lambda b: (b, 0, 0, 0))],
        out_shape=[jax.ShapeDtypeStruct((bsz, g, n, LANES), BF16),
                   jax.ShapeDtypeStruct((bsz, g, HEAD_DIM, n), BF16)],
        compiler_params=_params(("parallel",)),
        name="compress",
    )(ck, cv, w1, w2t, pe, gk)


def _with_ones(vt):
    return jnp.concatenate([vt, jnp.ones((BF16_ROWS, vt.shape[1]), BF16)], axis=0)


def _col_max(s):
    while s.shape[0] > 8:
        half = s.shape[0] // 2
        s = jnp.maximum(s[0:half], s[half:2 * half])
    return jnp.max(s, axis=0, keepdims=True)


def _flash_update(carries, chains):
    m_news = []
    for carry, tiles in zip(carries, chains):
        tops = []
        for s, _, bias, top in tiles:
            top = _col_max(s) if top is None else top
            tops.append(top if bias is None else top + bias)
        m_news.append(functools.reduce(jnp.maximum, tops if carry is None else tops + [carry[0]]))
    pvs = [None] * len(chains)
    for t in range(max(len(tiles) for tiles in chains)):
        for c, tiles in enumerate(chains):
            if t < len(tiles):
                s, vt, bias, _ = tiles[t]
                p = jnp.exp2(s - (m_news[c] if bias is None else m_news[c] - bias)).astype(BF16)
                part = _dot(_with_ones(vt), p)
                pvs[c] = part if pvs[c] is None else pvs[c] + part
    outs = []
    for carry, m_new, pv in zip(carries, m_news, pvs):
        outs.append((m_new, pv if carry is None else jnp.exp2(carry[0] - m_new) * carry[1] + pv))
    return outs


def _flash_out(acc):
    return acc[0:HEAD_DIM] / acc[HEAD_DIM:HEAD_DIM + 1]


def _topk_rows(scores, index, k):
    scores = list(scores)
    picked = [jnp.zeros(sc.shape, F32) for sc in scores]
    for _ in range(k):
        mx = [jnp.max(sc, axis=0, keepdims=True) for sc in scores]
        cand = [jnp.where(sc == m, index, jnp.int32(1 << 20)) for sc, m in zip(scores, mx)]
        first = [jnp.min(c, axis=0, keepdims=True) for c in cand]
        hit = [index == f for f in first]
        picked = [jnp.where(h, 1.0, p) for h, p in zip(hit, picked)]
        scores = [jnp.where(h, -jnp.inf, sc) for h, sc in zip(hit, scores)]
    return picked


def _tile_rows(j):
    return pl.ds(pl.multiple_of(j * TQ, TQ), TQ)


def _moba_gate_kernel(q_ref, km_ref, o_ref, *, n_sel):
    tg = q_ref.shape[3]
    t0 = pl.program_id(2) * tg
    q = q_ref[0, 0]
    q128 = jnp.concatenate([q, jnp.zeros_like(q)], axis=0)
    gate = _dot3(km_ref[0, 0], q128)
    blk = lax.broadcasted_iota(jnp.int32, gate.shape, 0)
    own = (lax.broadcasted_iota(jnp.int32, gate.shape, 1) + t0) >> _LOG2_BLK_A
    valid = blk < own
    picked, = _topk_rows([jnp.where(valid, gate, -jnp.inf)], blk, n_sel)
    keep = jnp.where(valid, picked, 0.0) + jnp.where(blk == own, 1.0, 0.0)
    selb = jnp.where(keep > 0.0, 0.0, NEG)
    o_ref[0, 0] = jnp.concatenate([q, selb], axis=0).astype(BF16)


def _moba_gate(qa, km):
    bsz, nh, _, s = qa.shape
    tg = min(s, 1024)
    n_sel = max(1, min(TOPK_A, s // BLK_A - 1))
    return pl.pallas_call(
        functools.partial(_moba_gate_kernel, n_sel=n_sel),
        grid=(bsz, nh, s // tg),
        in_specs=[pl.BlockSpec((1, 1, HEAD_DIM, tg), lambda b, h, t: (b, h, 0, t)),
                  pl.BlockSpec((1, 1, HEAD_DIM, LANES), lambda b, h, t: (b, h, 0, 0))],
        out_specs=pl.BlockSpec((1, 1, LANES, tg), lambda b, h, t: (b, h, 0, t)),
        out_shape=jax.ShapeDtypeStruct((bsz, nh, LANES, s), BF16),
        compiler_params=_params(("parallel", "parallel", "parallel")),
        name="moba_gate",
    )(qa, km)


def _far_tiles(qi):
    n_far = jnp.maximum(qi - 1, 0)
    left = jnp.maximum(n_far - 1, 0)
    off_left = jnp.where((n_far & 1) == 1, 0.0, NEG)
    return n_far >> 1, left, off_left


def _pair_rows(i):
    return pl.ds(pl.multiple_of(i * (2 * TQ), 2 * TQ), 2 * TQ)


class _FarLoop:
    def __init__(self, n_pairs, heads, qk_pair, values, cbs, sa_ref, sb_ref, m_ref, acc_ref):
        self.n_pairs, self.heads, self.qk_pair, self.values, self.cbs = n_pairs, heads, qk_pair, values, cbs
        self.sa_ref, self.sb_ref, self.m_ref, self.acc_ref = sa_ref, sb_ref, m_ref, acc_ref
        self.last = jnp.maximum(n_pairs - 1, 0)

    def fetch(self, buf_ref, i):
        tops = []
        for h in self.heads:
            s = self.qk_pair(h, i)
            buf_ref[h] = s
            tops.append((_col_max(s[0:TQ]), _col_max(s[TQ:2 * TQ])))
        return tuple(tops)

    def consume(self, buf_ref, tops, i, off):
        carries, chains = [], []
        for h in self.heads:
            bias = self.cbs[h] if off is None else self.cbs[h] + off
            carries.append((self.m_ref[h], self.acc_ref[h]))
            chains.append([(buf_ref[h, 0:TQ, :], self.values(h, 2 * i), bias, tops[h][0]),
                           (buf_ref[h, TQ:2 * TQ, :], self.values(h, 2 * i + 1), bias, tops[h][1])])
        for h, (m, acc) in zip(self.heads, _flash_update(carries, chains)):
            self.m_ref[h] = m
            self.acc_ref[h] = acc

    def first(self):
        return self.fetch(self.sa_ref, 0)

    def run(self, tops_first):
        def two_pairs(q, tops_a):
            ia = 2 * q
            ib = jnp.minimum(ia + 1, self.last)
            off_b = jnp.where(ia + 1 < self.n_pairs, 0.0, NEG)
            tops_b = self.fetch(self.sb_ref, ib)
            self.consume(self.sa_ref, tops_a, ia, None)
            tops_next = self.fetch(self.sa_ref, jnp.minimum(ia + 2, self.last))
            self.consume(self.sb_ref, tops_b, ib, off_b)
            return tops_next

        lax.fori_loop(0, (self.n_pairs + 1) >> 1, two_pairs, tops_first)


def _moba_kernel(q_ref, k_ref, vt_ref, tab_ref, cb_ref, o_ref, sa_ref, sb_ref, m_ref, acc_ref):
    qi = pl.program_id(2)
    jp = jnp.maximum(qi - 1, 0)
    off_p = jnp.where(qi >= 1, 0.0, NEG)
    n_pairs, jl, off_l = _far_tiles(qi)
    heads = range(2)
    qaug = [q_ref[0, hh] for hh in heads]
    cbs = [cb_ref[hh][0:1, 0:1] for hh in heads]

    def scores(hh, j):
        return _dot(k_ref[0, hh, _tile_rows(j), :], qaug[hh])

    def scores2(hh, i):
        return _dot(k_ref[0, hh, _pair_rows(i), :], qaug[hh])

    def values(hh, j):
        return vt_ref[0, j, hh * HEAD_DIM:(hh + 1) * HEAD_DIM, :]

    far = _FarLoop(n_pairs, heads, scores2, values, cbs, sa_ref, sb_ref, m_ref, acc_ref)

    s_own = [scores(hh, qi) for hh in heads]
    s_prev = [scores(hh, jp) for hh in heads]
    s_left = [scores(hh, jl) for hh in heads]
    tops_first = far.first()
    chains = [[(s_own[hh] + tab_ref[hh, TQ:2 * TQ, :], values(hh, qi), None, None),
               (s_prev[hh] + tab_ref[hh, 0:TQ, :] + off_p, values(hh, jp), None, None),
               (s_left[hh], values(hh, jl), cbs[hh] + off_l, None)] for hh in heads]
    for hh, (m, acc) in zip(heads, _flash_update([None] * len(chains), chains)):
        m_ref[hh] = m
        acc_ref[hh] = acc
    far.run(tops_first)
    out_t = jnp.concatenate([_flash_out(acc_ref[hh]) for hh in heads], axis=0)
    o_ref[0] = out_t.T.astype(BF16)


def _flash_scratch(n_heads):
    return [pltpu.VMEM((n_heads, 2 * TQ, TQ), F32),
            pltpu.VMEM((n_heads, 2 * TQ, TQ), F32),
            pltpu.VMEM((n_heads, 1, TQ), F32),
            pltpu.VMEM((n_heads, ACC_ROWS, TQ), F32)]


def _moba(qaug, kaug, vat, tab, cb):
    bsz, _, _, s = qaug.shape
    nq = s // TQ
    return pl.pallas_call(
        _moba_kernel,
        grid=(bsz, H_A // 2, nq),
        in_specs=[
            pl.BlockSpec((1, 2, LANES, TQ), lambda b, hp, qi: (b, hp, 0, qi)),
            pl.BlockSpec((1, 2, s, LANES), lambda b, hp, qi: (b, hp, 0, 0)),
            pl.BlockSpec((1, nq, LANES, TQ), lambda b, hp, qi: (b, 0, hp, 0)),
            pl.BlockSpec((2, 2 * TQ, TQ), lambda b, hp, qi: (hp, 0, 0)),
            pl.BlockSpec((2, 8, LANES), lambda b, hp, qi: (hp, 0, 0)),
        ],
        out_specs=pl.BlockSpec((1, TQ, LANES), lambda b, hp, qi: (b, qi, hp)),
        out_shape=jax.ShapeDtypeStruct((bsz, s, H_A * HEAD_DIM), BF16),
        scratch_shapes=_flash_scratch(2),
        compiler_params=_params(("parallel", "parallel", "arbitrary")),
        name="moba",
    )(qaug, kaug, vat, tab, cb)


def _nsa_cmp_kernel(q_ref, kc_ref, vct_ref, ovt_ref, oc_ref, selb_ref, *, n_sel, n_cmp, n_parts):
    qi = pl.program_id(1)
    ncp = kc_ref.shape[2]
    t0 = qi * TQ
    zeros = jnp.zeros((HEAD_DIM, TQ), BF16)
    qs = [jnp.concatenate([q_ref[0, hd].astype(BF16), zeros], axis=0) for hd in range(H_B)]

    def body(nk, nb):
        n_idx = lax.broadcasted_iota(jnp.int32, (nk, TQ), 0)
        t_idx = lax.broadcasted_iota(jnp.int32, (nk, TQ), 1) + t0
        mask = (n_idx * CMP_STRIDE + (CMP_LEN - 1) <= t_idx) & (n_idx < n_cmp)
        any_key = t_idx[0:1] >= CMP_LEN - 1
        blk = lax.broadcasted_iota(jnp.int32, (nb, TQ), 0)
        cur = (lax.broadcasted_iota(jnp.int32, (nb, TQ), 1) + t0) >> _LOG2_SEL_BLK
        ok = blk <= cur
        forced = (blk == 0) | (blk == cur) | (blk == cur - 1)
        ovt = ovt_ref[0:nb, 0:nk]
        scores = []
        for g in range(G_B):
            kc = kc_ref[0, g, 0:nk, :]
            vct = _with_ones(vct_ref[0, g, :, 0:nk])
            psum = jnp.zeros((nk, TQ), F32)
            for hd in range(g * R_B, (g + 1) * R_B):
                z = jnp.where(mask, _dot(kc, qs[hd]), NEG)
                e = jnp.exp2(z - _col_max(z))
                acc = _dot(vct, e.astype(BF16))
                rinv = jnp.where(any_key, 1.0 / acc[HEAD_DIM:HEAD_DIM + 1], 0.0)
                oc_ref[0, hd] = acc[0:HEAD_DIM] * rinv
                psum = psum + e * rinv
            ph, pl_ = _split(psum)
            imp_t = _dot(ovt, ph) + _dot(ovt, pl_)
            scores.append(jnp.where(ok, jnp.where(forced, BIG, imp_t), -jnp.inf))
        for g, picked in enumerate(_topk_rows(scores, blk, n_sel)):
            selb_ref[0, g, 0:nb, :] = jnp.where(ok & (picked > 0.0), 0.0, NEG).astype(BF16)
            if nb < LANES:
                selb_ref[0, g, nb:LANES, :] = jnp.full((LANES - nb, TQ), NEG, BF16)

    part = ncp // n_parts
    need = jnp.minimum(((qi + 1) * (TQ // CMP_STRIDE) + part - 1) // part, n_parts)
    for v in range(1, n_parts + 1):
        pl.when(need == v)(functools.partial(body, v * part, min(LANES, v * part * CMP_STRIDE // SEL_BLK)))


def _cmp_parts(ncp):
    return 4 if ncp % (4 * LANES) == 0 else 1


def _nsa_cmp(qb, kcmp, vcmpt, ovt, n_cmp):
    bsz, _, _, s = qb.shape
    nq = s // TQ
    ncp = kcmp.shape[2]
    n_sel = min(SEL_TOPK, s // SEL_BLK)
    n_parts = _cmp_parts(ncp)
    return pl.pallas_call(
        functools.partial(_nsa_cmp_kernel, n_sel=n_sel, n_cmp=n_cmp, n_parts=n_parts),
        grid=(bsz, nq),
        in_specs=[
            pl.BlockSpec((1, H_B, HEAD_DIM, TQ), lambda b, qi: (b, 0, 0, qi)),
            pl.BlockSpec((1, G_B, ncp, LANES), lambda b, qi: (b, 0, 0, 0)),
            pl.BlockSpec((1, G_B, HEAD_DIM, ncp), lambda b, qi: (b, 0, 0, 0)),
            pl.BlockSpec((LANES, ncp), lambda b, qi: (0, 0)),
        ],
        out_specs=[
            pl.BlockSpec((1, H_B, HEAD_DIM, TQ), lambda b, qi: (b, 0, 0, qi)),
            pl.BlockSpec((1, G_B, LANES, TQ), lambda b, qi: (b, 0, 0, qi)),
        ],
        out_shape=[jax.ShapeDtypeStruct((bsz, H_B, HEAD_DIM, s), F32),
                   jax.ShapeDtypeStruct((bsz, G_B, LANES, s), BF16)],
        compiler_params=_params(("parallel", "parallel")),
        name="nsa_cmp",
    )(qb, kcmp, vcmpt, ovt)


def _nsa_main_kernel(q_ref, selb_ref, ks_ref, vst_ref, kw_ref, vwt_ref, oc_ref, gates_ref,
                     tabs_ref, tabw_ref, cb_ref, o_ref, sa_ref, sb_ref, m_ref, acc_ref):
    qi = pl.program_id(2)
    selb = selb_ref[0, 0]
    gates = gates_ref[0, 0]
    zeros = jnp.zeros((HEAD_DIM, TQ), BF16)
    j1 = jnp.maximum(qi - 1, 0)
    j2 = jnp.maximum(qi - 2, 0)
    off1 = jnp.where(qi >= 1, 0.0, NEG)
    off2 = jnp.where(qi >= 2, 0.0, NEG)
    n_pairs, jl, off_l = _far_tiles(qi)

    heads = range(R_B)
    qw = [jnp.concatenate([q_ref[0, r].astype(BF16), zeros], axis=0) for r in heads]
    qs = [jnp.concatenate([qw[r], selb], axis=0) for r in heads]
    cbs = [cb_ref[r][0:1, 0:1] for r in heads]

    def s_sel(r, j):
        return _dot(ks_ref[0, 0, _tile_rows(j), :], qs[r])

    def s_win(r, j):
        return _dot(kw_ref[0, 0, _tile_rows(j), :], qw[r])

    def s_sel2(r, i):
        return _dot(ks_ref[0, 0, _pair_rows(i), :], qs[r])

    far = _FarLoop(n_pairs, heads, s_sel2, lambda r, j: vst_ref[0, j], cbs, sa_ref, sb_ref, m_ref, acc_ref)

    sw = [[s_win(r, j) for r in heads] for j in (qi, j1, j2)]
    ss = [[s_sel(r, j) for r in heads] for j in (qi, j1, jl)]
    tops_first = far.first()

    chains = [[(sw[0][r] + tabw_ref[r, 2 * TQ:3 * TQ, :], vwt_ref[0, qi], None, None),
               (sw[1][r] + tabw_ref[r, TQ:2 * TQ, :] + off1, vwt_ref[0, j1], None, None),
               (sw[2][r] + tabw_ref[r, 0:TQ, :] + off2, vwt_ref[0, j2], None, None)] for r in heads]
    chains += [[(ss[0][r] + tabs_ref[r, TQ:2 * TQ, :], vst_ref[0, qi], None, None),
                (ss[1][r] + tabs_ref[r, 0:TQ, :] + off1, vst_ref[0, j1], None, None),
                (ss[2][r], vst_ref[0, jl], cbs[r] + off_l, None)] for r in heads]
    done = _flash_update([None] * len(chains), chains)
    win = [_flash_out(acc) for _, acc in done[:R_B]]
    for r, (m, acc) in zip(heads, done[R_B:]):
        m_ref[r] = m
        acc_ref[r] = acc
    far.run(tops_first)

    outs = [gates[3 * r:3 * r + 1] * oc_ref[0, r]
            + gates[3 * r + 1:3 * r + 2] * _flash_out(acc_ref[r])
            + gates[3 * r + 2:3 * r + 3] * win[r] for r in heads]
    o_ref[0] = jnp.concatenate(outs, axis=0).T.astype(BF16)


def _nsa_main(qb, selb, ksaug, vst, kw, vwt, oc, gates, tabs, tabw, cb):
    bsz, _, _, s = qb.shape
    nq = s // TQ
    return pl.pallas_call(
        _nsa_main_kernel,
        grid=(bsz, G_B, nq),
        in_specs=[
            pl.BlockSpec((1, R_B, HEAD_DIM, TQ), lambda b, g, qi: (b, g, 0, qi)),
            pl.BlockSpec((1, 1, LANES, TQ), lambda b, g, qi: (b, g, 0, qi)),
            pl.BlockSpec((1, 1, s, 2 * LANES), lambda b, g, qi: (b, g, 0, 0)),
            pl.BlockSpec((1, nq, HEAD_DIM, TQ), lambda b, g, qi: (b, 0, g, 0)),
            pl.BlockSpec((1, 1, s, LANES), lambda b, g, qi: (b, g, 0, 0)),
            pl.BlockSpec((1, nq, HEAD_DIM, TQ), lambda b, g, qi: (b, 0, g, 0)),
            pl.BlockSpec((1, R_B, HEAD_DIM, TQ), lambda b, g, qi: (b, g, 0, qi)),
            pl.BlockSpec((1, 1, GATE_ROWS, TQ), lambda b, g, qi: (b, g, 0, qi)),
            pl.BlockSpec((R_B, 2 * TQ, TQ), lambda b, g, qi: (g, 0, 0)),
            pl.BlockSpec((R_B, 3 * TQ, TQ), lambda b, g, qi: (g, 0, 0)),
            pl.BlockSpec((R_B, 8, LANES), lambda b, g, qi: (g, 0, 0)),
        ],
        out_specs=pl.BlockSpec((1, TQ, R_B * HEAD_DIM), lambda b, g, qi: (b, qi, g)),
        out_shape=jax.ShapeDtypeStruct((bsz, s, H_B * HEAD_DIM), BF16),
        scratch_shapes=_flash_scratch(R_B),
        compiler_params=_params(("parallel", "parallel", "arbitrary")),
        name="nsa_main",
    )(qb, selb, ksaug, vst, kw, vwt, oc, gates, tabs, tabw, cb)


def _out_ffn_kernel(x_ref, oa_ref, ob_ref, wo_ref, gtm_ref, gffn_ref, scf_ref, shf_ref, gtf_ref,
                    wg_ref, wu_ref, wd_ref, o_ref, x1_ref, h_ref, acc_ref):
    f = pl.program_id(2)
    half = oa_ref.shape[2]

    @pl.when(f == 0)
    def _():
        mix = _dot(oa_ref[0], wo_ref[0:half, :]) + _dot(ob_ref[0], wo_ref[half:2 * half, :])
        x1 = x_ref[0] + gtm_ref[0, 0] * mix
        x1_ref[...] = x1
        h = _rms_rows(x1, gffn_ref[...]) * (1.0 + scf_ref[0, 0]) + shf_ref[0, 0]
        h_ref[...] = h.astype(BF16)

    h = h_ref[...]
    gate = _dot(h, wg_ref[...])
    up = _dot(h, wu_ref[...])
    act = (gate * jax.nn.sigmoid(gate) * up).astype(BF16)
    part = _dot(act, wd_ref[...])

    @pl.when(f == 0)
    def _():
        acc_ref[...] = part

    @pl.when(f > 0)
    def _():
        acc_ref[...] += part

    @pl.when(f == pl.num_programs(2) - 1)
    def _():
        o_ref[0] = x1_ref[...] + gtf_ref[0, 0] * acc_ref[...]


def _out_ffn(x, oa, ob, wo, mod4, gffn, wgu, wd):
    bsz, s, d = x.shape
    fh = wd.shape[0]
    tm = TM_FFN
    tf = fh // 2 if (fh // 2) % LANES == 0 else fh
    nf = fh // tf
    tok = lambda b, t, f: (b, t, 0)
    modspec = lambda k: pl.BlockSpec((1, 1, 1, d), lambda b, t, f: (b, k, 0, 0))
    return pl.pallas_call(
        _out_ffn_kernel,
        grid=(bsz, s // tm, nf),
        in_specs=[
            pl.BlockSpec((1, tm, d), tok),
            pl.BlockSpec((1, tm, oa.shape[2]), tok),
            pl.BlockSpec((1, tm, ob.shape[2]), tok),
            pl.BlockSpec((d, d), lambda b, t, f: (0, 0)),
            modspec(2),
            pl.BlockSpec((1, d), lambda b, t, f: (0, 0)),
            modspec(4),
            modspec(3),
            modspec(5),
            pl.BlockSpec((d, tf), lambda b, t, f: (0, f)),
            pl.BlockSpec((d, tf), lambda b, t, f: (0, f + nf)),
            pl.BlockSpec((tf, d), lambda b, t, f: (f, 0)),
        ],
        out_specs=pl.BlockSpec((1, tm, d), tok),
        out_shape=jax.ShapeDtypeStruct((bsz, s, d), F32),
        scratch_shapes=[pltpu.VMEM((tm, d), F32), pltpu.VMEM((tm, d), BF16), pltpu.VMEM((tm, d), F32)],
        compiler_params=_params(("parallel", "parallel", "arbitrary")),
        name="out_ffn",
    )(x, oa, ob, wo, mod4, gffn, mod4, mod4, mod4, wgu, wgu, wd)


def _t5_bucket_np(d):
    max_exact = N_BUCKETS // 2
    d = np.maximum(d, 0)
    df = np.maximum(d, 1).astype(np.float64)
    large = max_exact + (np.log(df / max_exact) / math.log(MAX_DIST / max_exact)
                         * (N_BUCKETS - max_exact)).astype(np.int64)
    large = np.minimum(large, N_BUCKETS - 1)
    return np.where(d < max_exact, d, large).astype(np.int32)


def _bias_expand_kernel(tab_ref, bucket_ref, o_ref):
    hd = pl.program_id(0)
    bucket = bucket_ref[...]
    acc = jnp.full(bucket.shape, NEG, F32)
    for b in range(N_BUCKETS):
        acc = jnp.where(bucket == b, tab_ref[hd, b], acc)
    o_ref[0] = acc


def _bias_expand(tab, bucket):
    nh = tab.shape[0]
    return pl.pallas_call(
        _bias_expand_kernel,
        grid=(nh,),
        in_specs=[pl.BlockSpec(memory_space=pltpu.SMEM),
                  pl.BlockSpec(bucket.shape, lambda h: (0, 0))],
        out_specs=pl.BlockSpec((1,) + bucket.shape, lambda h: (h, 0, 0)),
        out_shape=jax.ShapeDtypeStruct((nh,) + bucket.shape, F32),
        compiler_params=_params(("parallel",)),
        name="bias_expand",
    )(tab, jnp.asarray(bucket, jnp.int32))


def _bias_tables(rel_bias):
    tab = rel_bias.T.astype(F32) * LOG2_E
    i = np.arange(TQ)[None, :]
    d_near = i + TQ - np.arange(2 * TQ)[:, None]
    near = _bias_expand(tab, np.where(d_near >= 0, _t5_bucket_np(d_near), -1))
    d_win = i + 2 * TQ - np.arange(3 * TQ)[:, None]
    ok_win = (d_win >= 0) & (d_win < WINDOW)
    win = _bias_expand(tab[H_A:], np.where(ok_win, _t5_bucket_np(d_win), -1))
    far = jnp.broadcast_to(tab[:, N_BUCKETS - 1][:, None, None], (tab.shape[0], 8, LANES))
    return near, win, far


def _overlap_t(n_cmp_pad, n_cmp):
    cs = np.arange(n_cmp_pad)[None, :] * CMP_STRIDE
    ss = np.arange(LANES)[:, None] * SEL_BLK
    ov = (cs < ss + SEL_BLK) & (cs + CMP_LEN > ss) & (np.arange(n_cmp_pad)[None, :] < n_cmp)
    return jnp.asarray(ov.astype(np.float32), BF16)


def _block_diag(n):
    m = (np.arange(n)[:, None] // HEAD_DIM == np.arange(n)[None, :] // HEAD_DIM)
    return jnp.asarray(m.astype(np.float32) / HEAD_DIM, BF16)


def kernel(x, c, rel_bias, w_ada, b_ada, g_mix, w_in, q_norm_a, k_norm_a, q_norm_b, k_norm_cmp,
           k_norm_sel, k_norm_win, cmp_pe_k, cmp_w1_k, cmp_w2_k, cmp_pe_v, cmp_w1_v, cmp_w2_v,
           w_out, g_ffn, w_gu, w_down):
    bsz, s, d = x.shape
    depth = w_ada.shape[0]
    assert s % TM_IN == 0 and s % TM_FFN == 0 and s % (2 * TQ) == 0
    assert s // BLK_A <= HEAD_DIM and s // SEL_BLK <= LANES
    assert WINDOW == 2 * TQ and BLK_A == TQ and MAX_DIST <= TQ
    n_chunks = s // CMP_STRIDE
    n_cmp = (s - CMP_LEN) // CMP_STRIDE + 1
    scale = HEAD_DIM ** -0.5 * LOG2_E
    hd = HEAD_DIM

    near, win, far = _bias_tables(rel_bias)
    ovt = _overlap_t(n_chunks, n_cmp)
    bd = _block_diag(512)
    tile = lambda g, n: jnp.tile(g.astype(F32), n).reshape(1, -1)
    tile_t = lambda g, n: jnp.broadcast_to(jnp.tile(g.astype(F32), n)[:, None], (n * hd, TM_IN))

    for l in range(depth):
        mod = _ada(c, w_ada[l], b_ada[l])
        mod4 = mod.reshape(bsz, ADA_CHUNKS, 1, d)

        wl = w_in[l]
        cols = np.cumsum([0, H_A * hd, H_A * hd, H_A * hd, H_B * hd] + [G_B * hd] * 6)
        qa_c, ka_c, va_c, qb_c, kc_c, vc_c, ks_c, vs_c, kw_c, vw_c = [
            wl[:, int(a):int(b)] for a, b in zip(cols[:-1], cols[1:])]
        gl = wl[:, int(cols[-1]):].reshape(d, G_B, 3 * R_B)
        gl = jnp.pad(gl, ((0, 0), (0, 0), (0, GATE_ROWS - 3 * R_B))).reshape(d, G_B * GATE_ROWS)
        w_rows = jnp.concatenate([ka_c, kc_c, vc_c, ks_c, kw_c], axis=1).astype(BF16)
        w_t = jnp.concatenate([qa_c, va_c, qb_c, vs_c, vw_c, gl], axis=1).T.astype(BF16)

        (qa, kaug, vat, kmean, qb, kc, vc, ksaug, vst, kw, vwt, gates) = _inproj(
            x, mod4, mod4, g_mix[l].reshape(1, d), w_rows, w_t, bd,
            tile_t(q_norm_a[l], H_A) * scale, tile(k_norm_a[l], H_A), tile_t(q_norm_b[l], H_B) * scale,
            tile(k_norm_sel[l], G_B), tile(k_norm_win[l], G_B))

        nba = s // BLK_A
        km = kmean.reshape(bsz, nba, H_A, hd).transpose(0, 2, 1, 3)
        km = jnp.pad(km, ((0, 0), (0, 0), (0, hd - nba), (0, LANES - hd)))
        o_a = _moba(_moba_gate(qa, km), kaug, vat, near[:H_A], far[:H_A])

        chunks = lambda t: t.reshape(bsz, s, G_B, hd).transpose(0, 2, 1, 3).reshape(
            bsz, G_B, n_chunks, CMP_STRIDE * hd)
        w1 = jnp.stack([cmp_w1_k[l], cmp_w1_v[l]]).astype(BF16)
        w2t = jnp.stack([cmp_w2_k[l].T, cmp_w2_v[l].T]).astype(BF16)
        pe = jnp.stack([cmp_pe_k[l], cmp_pe_v[l]]).reshape(2, 1, CMP_LEN * hd)
        pe = jnp.broadcast_to(pe, (2, 8, CMP_LEN * hd)).astype(BF16)
        kcmp, vcmpt = _compress(chunks(kc), chunks(vc), w1, w2t, pe,
                                k_norm_cmp[l].astype(F32).reshape(1, hd))

        oc, selb = _nsa_cmp(qb, kcmp, vcmpt, ovt, n_cmp)
        o_b = _nsa_main(qb, selb, ksaug, vst, kw, vwt, oc, gates, near[H_A:], win, far[H_A:])

        x = _out_ffn(x, o_a, o_b, w_out[l].astype(BF16), mod4, g_ffn[l].reshape(1, d),
                     w_gu[l].astype(BF16), w_down[l].astype(BF16))
    return x
```

```python
import functools
import math

import jax
import jax.numpy as jnp
import numpy as np
from jax import lax
from jax.experimental import pallas as pl
from jax.experimental.pallas import tpu as pltpu

F32 = jnp.float32
BF16 = jnp.bfloat16

HEAD_DIM = 64
LANES = 128
BF16_ROWS = 16
H_A = 8
H_B = 8
G_B = 2
R_B = H_B // G_B
BLK_A = 256
TOPK_A = 3
CMP_LEN = 32
CMP_STRIDE = 16
CMP_HIDDEN = 256
SEL_BLK = 64
SEL_TOPK = 16
WINDOW = 512
N_BUCKETS = 32
MAX_DIST = 128
ADA_CHUNKS = 6
NEG = -1e30
BIG = 1e9
EPS = 1e-6
LOG2_E = math.log2(math.e)

_LOG2_BLK_A = BLK_A.bit_length() - 1
_LOG2_SEL_BLK = SEL_BLK.bit_length() - 1

TQ = 256
TM_IN = 512
TM_FFN = 512
MOBA_HEADS = 4
GATE_ROWS = 16
ACC_ROWS = HEAD_DIM + BF16_ROWS
VMEM_LIMIT = 56 * 1024 * 1024


def _dot(a, b):
    return jnp.dot(a, b, preferred_element_type=F32)


def _dot_nt(a, b):
    return lax.dot_general(a, b, (((1,), (1,)), ((), ())), preferred_element_type=F32)


def _split(a):
    hi = a.astype(BF16)
    lo = (a - hi.astype(F32)).astype(BF16)
    return hi, lo


def _dot3(a, b):
    ah, al = _split(a)
    bh, bl = _split(b)
    return _dot(ah, bh) + (_dot(al, bh) + _dot(ah, bl))


def _params(sem):
    return pltpu.CompilerParams(dimension_semantics=sem, vmem_limit_bytes=VMEM_LIMIT)


def _ada_kernel(c_ref, w_ref, b_ref, o_ref):
    c = c_ref[...]
    o_ref[...] = _dot3(c * jax.nn.sigmoid(c), w_ref[...]) + b_ref[...]


def _ada(c, w, b):
    bsz, d = c.shape
    n = w.shape[1]
    tn = 512
    return pl.pallas_call(
        _ada_kernel,
        grid=(n // tn,),
        in_specs=[pl.BlockSpec((bsz, d), lambda j: (0, 0)),
                  pl.BlockSpec((d, tn), lambda j: (0, j)),
                  pl.BlockSpec((1, tn), lambda j: (0, j))],
        out_specs=pl.BlockSpec((bsz, tn), lambda j: (0, j)),
        out_shape=jax.ShapeDtypeStruct((bsz, n), F32),
        compiler_params=_params(("arbitrary",)),
        name="ada",
    )(c, w, b.reshape(1, n))


def _rms_rows(xf, g):
    ms = jnp.mean(xf * xf, axis=-1, keepdims=True)
    return xf * lax.rsqrt(ms + EPS) * g


def _head_norm(t, bd, gain):
    hi, lo = _split(t * t)
    ms = _dot(hi, bd) + _dot(lo, bd)
    return t * lax.rsqrt(ms + EPS) * gain


def _head_norm_t(t, gain):
    heads = []
    for hd in range(t.shape[0] // HEAD_DIM):
        th = t[hd * HEAD_DIM:(hd + 1) * HEAD_DIM]
        ms = jnp.mean(th * th, axis=0, keepdims=True)
        heads.append(th * lax.rsqrt(ms + EPS) * gain[hd * HEAD_DIM:(hd + 1) * HEAD_DIM])
    return heads


def _inproj_kernel(x_ref, sc_ref, sh_ref, gmix_ref, wr_ref, wt_ref, bd_ref, gqa_ref, gka_ref, gqb_ref,
                   gks_ref, gkw_ref,
                   qa_ref, kaug_ref, va_ref, kmean_ref, qb_ref, kc_ref, vc_ref, ksaug_ref,
                   vs_ref, kw_ref, vw_ref, gates_ref):
    tm = x_ref.shape[1]
    ti = pl.program_id(1)
    xf = x_ref[0]
    h = _rms_rows(xf, gmix_ref[...]) * (1.0 + sc_ref[0, 0]) + sh_ref[0, 0]
    hb = h.astype(BF16)

    def proj(c0, c1):
        return _dot(hb, wr_ref[:, c0:c1])

    def proj_t(r0, r1):
        return _dot_nt(wt_ref[r0:r1, :], hb)

    bd = bd_ref[...]
    bd2 = bd_ref[0:LANES, 0:LANES]
    lane = lax.broadcasted_iota(jnp.int32, (tm, LANES), 1)
    row = lax.broadcasted_iota(jnp.int32, (tm, LANES), 0) + ti * tm
    low = lane < HEAD_DIM

    def k_in_low(pair, odd):
        return pltpu.roll(pair, HEAD_DIM, 1) if odd else pair

    for hd, qh in enumerate(_head_norm_t(proj_t(0, 512), gqa_ref[...])):
        qa_ref[0, hd] = qh

    def put_tiles(ref, vt):
        for i in range(tm // TQ):
            ref[0, i] = vt[:, i * TQ:(i + 1) * TQ].astype(BF16)

    put_tiles(va_ref, proj_t(512, 1024))
    for hd, qh in enumerate(_head_norm_t(proj_t(1024, 1536), gqb_ref[...])):
        qb_ref[0, hd] = qh
    put_tiles(vs_ref, proj_t(1536, 1664))
    put_tiles(vw_ref, proj_t(1664, 1792))
    gl = jax.nn.sigmoid(proj_t(1792, 1792 + G_B * GATE_ROWS))
    for g in range(G_B):
        gates_ref[0, g] = gl[g * GATE_ROWS:(g + 1) * GATE_ROWS]

    ka = _head_norm(proj(0, 512), bd, gka_ref[...])
    oh_a = jnp.where(lane - HEAD_DIM == (row >> _LOG2_BLK_A), 1.0, 0.0)
    for hd in range(H_A):
        pair = ka[:, (hd // 2) * LANES:(hd // 2 + 1) * LANES]
        kaug_ref[0, hd] = jnp.where(low, k_in_low(pair, hd % 2), oh_a).astype(BF16)
    for i in range(tm // BLK_A):
        kmean_ref[0, i] = jnp.mean(ka[i * BLK_A:(i + 1) * BLK_A], axis=0, keepdims=True)

    kc_ref[0] = proj(512, 640).astype(BF16)
    vc_ref[0] = proj(640, 768).astype(BF16)

    ks = _head_norm(proj(768, 896), bd2, gks_ref[...])
    kw = _head_norm(proj(896, 1024), bd2, gkw_ref[...])
    oh_s = jnp.where(lane == (row >> _LOG2_SEL_BLK), 1.0, 0.0).astype(BF16)
    for g in range(G_B):
        ksaug_ref[0, g] = jnp.concatenate(
            [jnp.where(low, k_in_low(ks, g), 0.0).astype(BF16), oh_s], axis=1)
        kw_ref[0, g] = jnp.where(low, k_in_low(kw, g), 0.0).astype(BF16)


def _inproj(x, sc, sh, gmix, wr, wt, bd, gqa, gka, gqb, gks, gkw):
    bsz, s, d = x.shape
    tm = TM_IN
    nt = s // tm
    nba = s // BLK_A
    const2 = lambda b, t: (0, 0)
    tok3 = lambda b, t: (b, t, 0)
    tok4 = lambda b, t: (b, 0, t, 0)
    tile4 = lambda b, t: (b, t, 0, 0)
    tr4 = lambda b, t: (b, 0, 0, t)
    in_specs = [
        pl.BlockSpec((1, tm, d), tok3),
        pl.BlockSpec((1, 1, 1, d), lambda b, t: (b, 1, 0, 0)),
        pl.BlockSpec((1, 1, 1, d), lambda b, t: (b, 0, 0, 0)),
        pl.BlockSpec((1, d), const2),
        pl.BlockSpec(wr.shape, const2),
        pl.BlockSpec(wt.shape, const2),
        pl.BlockSpec((512, 512), const2),
        pl.BlockSpec((512, tm), const2),
        pl.BlockSpec((1, 512), const2),
        pl.BlockSpec((512, tm), const2),
        pl.BlockSpec((1, LANES), const2),
        pl.BlockSpec((1, LANES), const2),
    ]
    out_shape = [
        jax.ShapeDtypeStruct((bsz, H_A, HEAD_DIM, s), F32),
        jax.ShapeDtypeStruct((bsz, H_A, s, LANES), BF16),
        jax.ShapeDtypeStruct((bsz, s // TQ, 512, TQ), BF16),
        jax.ShapeDtypeStruct((bsz, nba, 1, 512), F32),
        jax.ShapeDtypeStruct((bsz, H_B, HEAD_DIM, s), F32),
        jax.ShapeDtypeStruct((bsz, s, LANES), BF16),
        jax.ShapeDtypeStruct((bsz, s, LANES), BF16),
        jax.ShapeDtypeStruct((bsz, G_B, s, 2 * LANES), BF16),
        jax.ShapeDtypeStruct((bsz, s // TQ, LANES, TQ), BF16),
        jax.ShapeDtypeStruct((bsz, G_B, s, LANES), BF16),
        jax.ShapeDtypeStruct((bsz, s // TQ, LANES, TQ), BF16),
        jax.ShapeDtypeStruct((bsz, G_B, GATE_ROWS, s), F32),
    ]
    out_specs = [
        pl.BlockSpec((1, H_A, HEAD_DIM, tm), tr4),
        pl.BlockSpec((1, H_A, tm, LANES), tok4),
        pl.BlockSpec((1, tm // TQ, 512, TQ), tile4),
        pl.BlockSpec((1, tm // BLK_A, 1, 512), lambda b, t: (b, t, 0, 0)),
        pl.BlockSpec((1, H_B, HEAD_DIM, tm), tr4),
        pl.BlockSpec((1, tm, LANES), tok3),
        pl.BlockSpec((1, tm, LANES), tok3),
        pl.BlockSpec((1, G_B, tm, 2 * LANES), tok4),
        pl.BlockSpec((1, tm // TQ, LANES, TQ), tile4),
        pl.BlockSpec((1, G_B, tm, LANES), tok4),
        pl.BlockSpec((1, tm // TQ, LANES, TQ), tile4),
        pl.BlockSpec((1, G_B, GATE_ROWS, tm), tr4),
    ]
    return pl.pallas_call(
        _inproj_kernel,
        grid=(bsz, nt),
        in_specs=in_specs,
        out_specs=out_specs,
        out_shape=out_shape,
        compiler_params=_params(("parallel", "parallel")),
        name="inproj",
    )(x, sc, sh, gmix, wr, wt, bd, gqa, gka, gqb, gks, gkw)


def _compress_kernel(ck_ref, cv_ref, w1_ref, w2t_ref, pe_ref, gk_ref, ok_ref, ov_ref):
    half = CMP_STRIDE * HEAD_DIM
    for kv, c_ref in enumerate((ck_ref, cv_ref)):
        for g in range(G_B):
            c = c_ref[0, g]
            a = _dot(c, w1_ref[kv, 0:half, :])
            b = _dot(c, w1_ref[kv, half:2 * half, :])
            n = a.shape[0]
            b_next = pltpu.roll(b, n - 1, 0)
            pe_term = _dot(pe_ref[kv], w1_ref[kv])[0:1]
            hid = jax.nn.gelu(a + b_next + pe_term).astype(BF16)
            if kv == 0:
                y = _dot_nt(hid, w2t_ref[kv])
                ms = jnp.mean(y * y, axis=1, keepdims=True)
                y = y * lax.rsqrt(ms + EPS) * gk_ref[...]
                ok_ref[0, g] = jnp.concatenate([y, jnp.zeros_like(y)], axis=1).astype(BF16)
            else:
                ov_ref[0, g] = _dot_nt(w2t_ref[kv], hid).astype(BF16)


def _compress(ck, cv, w1, w2t, pe, gk):
    bsz, g, n, width = ck.shape
    blk = pl.BlockSpec((1, g, n, width), lambda b: (b, 0, 0, 0))
    full = lambda a: pl.BlockSpec(a.shape, lambda b: (0,) * a.ndim)
    return pl.pallas_call(
        _compress_kernel,
        grid=(bsz,),
        in_specs=[blk, blk, full(w1), full(w2t), full(pe), full(gk)],
        out_specs=[pl.BlockSpec((1, g, n, LANES), lambda b: (b, 0, 0, 0)),
                   pl.BlockSpec((1, g, HEAD_DIM, n), lambda b: (b, 0, 0, 0))],
        out_shape=[jax.ShapeDtypeStruct((bsz, g, n, LANES), BF16),
                   jax.ShapeDtypeStruct((bsz, g, HEAD_DIM, n), BF16)],
        compiler_params=_params(("parallel",)),
        name="compress",
    )(ck, cv, w1, w2t, pe, gk)


def _with_ones(vt):
    return jnp.concatenate([vt, jnp.ones((BF16_ROWS, vt.shape[1]), BF16)], axis=0)


def _col_max(s):
    while s.shape[0] > 8:
        half = s.shape[0] // 2
        s = jnp.maximum(s[0:half], s[half:2 * half])
    return jnp.max(s, axis=0, keepdims=True)


def _flash_update(carries, chains):
    m_news = []
    for carry, tiles in zip(carries, chains):
        tops = []
        for s, _, bias, top in tiles:
            top = _col_max(s) if top is None else top
            tops.append(top if bias is None else top + bias)
        m_news.append(functools.reduce(jnp.maximum, tops if carry is None else tops + [carry[0]]))
    pvs = [None] * len(chains)
    for t in range(max(len(tiles) for tiles in chains)):
        for c, tiles in enumerate(chains):
            if t < len(tiles):
                s, vt, bias, _ = tiles[t]
                p = jnp.exp2(s - (m_news[c] if bias is None else m_news[c] - bias)).astype(BF16)
                part = _dot(_with_ones(vt), p)
                pvs[c] = part if pvs[c] is None else pvs[c] + part
    outs = []
    for carry, m_new, pv in zip(carries, m_news, pvs):
        outs.append((m_new, pv if carry is None else jnp.exp2(carry[0] - m_new) * carry[1] + pv))
    return outs


def _flash_out(acc):
    return acc[0:HEAD_DIM] / acc[HEAD_DIM:HEAD_DIM + 1]


def _topk_rows(scores, index, k):
    scores = list(scores)
    picked = [jnp.zeros(sc.shape, F32) for sc in scores]
    for _ in range(k):
        mx = [jnp.max(sc, axis=0, keepdims=True) for sc in scores]
        cand = [jnp.where(sc == m, index, jnp.int32(1 << 20)) for sc, m in zip(scores, mx)]
        first = [jnp.min(c, axis=0, keepdims=True) for c in cand]
        hit = [index == f for f in first]
        picked = [jnp.where(h, 1.0, p) for h, p in zip(hit, picked)]
        scores = [jnp.where(h, -jnp.inf, sc) for h, sc in zip(hit, scores)]
    return picked


def _tile_rows(j):
    return pl.ds(pl.multiple_of(j * TQ, TQ), TQ)


def _moba_gate_kernel(q_ref, km_ref, o_ref, *, n_sel):
    nh, tg = q_ref.shape[1], q_ref.shape[3]
    t0 = pl.program_id(2) * tg
    shape = (km_ref.shape[2], tg)
    blk = lax.broadcasted_iota(jnp.int32, shape, 0)
    own = (lax.broadcasted_iota(jnp.int32, shape, 1) + t0) >> _LOG2_BLK_A
    valid = blk < own
    qs, gates = [], []
    for hh in range(nh):
        q = q_ref[0, hh]
        gate = _dot3(km_ref[0, hh], jnp.concatenate([q, jnp.zeros_like(q)], axis=0))
        qs.append(q)
        gates.append(jnp.where(valid, gate, -jnp.inf))
    for hh, picked in enumerate(_topk_rows(gates, blk, n_sel)):
        keep = jnp.where(valid, picked, 0.0) + jnp.where(blk == own, 1.0, 0.0)
        selb = jnp.where(keep > 0.0, 0.0, NEG)
        o_ref[0, hh] = jnp.concatenate([qs[hh], selb], axis=0).astype(BF16)


def _moba_gate(qa, km):
    bsz, nh, _, s = qa.shape
    tg = min(s, 2048)
    hpb = 2
    n_sel = max(1, min(TOPK_A, s // BLK_A - 1))
    return pl.pallas_call(
        functools.partial(_moba_gate_kernel, n_sel=n_sel),
        grid=(bsz, nh // hpb, s // tg),
        in_specs=[pl.BlockSpec((1, hpb, HEAD_DIM, tg), lambda b, h, t: (b, h, 0, t)),
                  pl.BlockSpec((1, hpb, HEAD_DIM, LANES), lambda b, h, t: (b, h, 0, 0))],
        out_specs=pl.BlockSpec((1, hpb, LANES, tg), lambda b, h, t: (b, h, 0, t)),
        out_shape=jax.ShapeDtypeStruct((bsz, nh, LANES, s), BF16),
        compiler_params=_params(("parallel", "parallel", "parallel")),
        name="moba_gate",
    )(qa, km)


def _far_tiles(qi):
    n_far = jnp.maximum(qi - 1, 0)
    left = jnp.maximum(n_far - 1, 0)
    off_left = jnp.where((n_far & 1) == 1, 0.0, NEG)
    return n_far >> 1, left, off_left


def _pair_rows(i):
    return pl.ds(pl.multiple_of(i * (2 * TQ), 2 * TQ), 2 * TQ)


class _FarLoop:
    def __init__(self, n_pairs, heads, qk_pair, values, cbs, sa_ref, sb_ref, m_ref, acc_ref):
        self.n_pairs, self.heads, self.qk_pair, self.values, self.cbs = n_pairs, heads, qk_pair, values, cbs
        self.sa_ref, self.sb_ref, self.m_ref, self.acc_ref = sa_ref, sb_ref, m_ref, acc_ref
        self.last = jnp.maximum(n_pairs - 1, 0)

    def fetch(self, buf_ref, i):
        tops = []
        for h in self.heads:
            s = self.qk_pair(h, i)
            buf_ref[h] = s
            tops.append((_col_max(s[0:TQ]), _col_max(s[TQ:2 * TQ])))
        return tuple(tops)

    def consume(self, buf_ref, tops, i, off):
        carries, chains = [], []
        for h in self.heads:
            bias = self.cbs[h] if off is None else self.cbs[h] + off
            carries.append((self.m_ref[h], self.acc_ref[h]))
            chains.append([(buf_ref[h, 0:TQ, :], self.values(h, 2 * i), bias, tops[h][0]),
                           (buf_ref[h, TQ:2 * TQ, :], self.values(h, 2 * i + 1), bias, tops[h][1])])
        for h, (m, acc) in zip(self.heads, _flash_update(carries, chains)):
            self.m_ref[h] = m
            self.acc_ref[h] = acc

    def first(self):
        return self.fetch(self.sa_ref, 0)

    def run(self, tops_first):
        def two_pairs(q, tops_a):
            ia = 2 * q
            ib = jnp.minimum(ia + 1, self.last)
            off_b = jnp.where(ia + 1 < self.n_pairs, 0.0, NEG)
            tops_b = self.fetch(self.sb_ref, ib)
            self.consume(self.sa_ref, tops_a, ia, None)
            tops_next = self.fetch(self.sa_ref, jnp.minimum(ia + 2, self.last))
            self.consume(self.sb_ref, tops_b, ib, off_b)
            return tops_next

        lax.fori_loop(0, (self.n_pairs + 1) >> 1, two_pairs, tops_first)


def _moba_kernel(q_ref, k_ref, vt_ref, tab_ref, cb_ref, o_ref, sa_ref, sb_ref, m_ref, acc_ref):
    qi = pl.program_id(2)
    jp = jnp.maximum(qi - 1, 0)
    off_p = jnp.where(qi >= 1, 0.0, NEG)
    n_pairs, jl, off_l = _far_tiles(qi)
    heads = range(MOBA_HEADS)
    qaug = [q_ref[0, hh] for hh in heads]
    cbs = [cb_ref[hh][0:1, 0:1] for hh in heads]

    def scores(hh, j):
        return _dot(k_ref[0, hh, _tile_rows(j), :], qaug[hh])

    def scores2(hh, i):
        return _dot(k_ref[0, hh, _pair_rows(i), :], qaug[hh])

    def values(hh, j):
        return vt_ref[0, j, hh * HEAD_DIM:(hh + 1) * HEAD_DIM, :]

    far = _FarLoop(n_pairs, heads, scores2, values, cbs, sa_ref, sb_ref, m_ref, acc_ref)

    s_own = [scores(hh, qi) for hh in heads]
    s_prev = [scores(hh, jp) for hh in heads]
    s_left = [scores(hh, jl) for hh in heads]
    tops_first = far.first()
    chains = [[(s_own[hh] + tab_ref[hh, TQ:2 * TQ, :], values(hh, qi), None, None),
               (s_prev[hh] + tab_ref[hh, 0:TQ, :] + off_p, values(hh, jp), None, None),
               (s_left[hh], values(hh, jl), cbs[hh] + off_l, None)] for hh in heads]
    for hh, (m, acc) in zip(heads, _flash_update([None] * len(chains), chains)):
        m_ref[hh] = m
        acc_ref[hh] = acc
    far.run(tops_first)
    out_t = jnp.concatenate([_flash_out(acc_ref[hh]) for hh in heads], axis=0)
    o_ref[0] = out_t.T.astype(BF16)


def _flash_scratch(n_heads):
    return [pltpu.VMEM((n_heads, 2 * TQ, TQ), F32),
            pltpu.VMEM((n_heads, 2 * TQ, TQ), F32),
            pltpu.VMEM((n_heads, 1, TQ), F32),
            pltpu.VMEM((n_heads, ACC_ROWS, TQ), F32)]


def _moba(qaug, kaug, vat, tab, cb):
    bsz, _, _, s = qaug.shape
    nq = s // TQ
    nh = MOBA_HEADS
    return pl.pallas_call(
        _moba_kernel,
        grid=(bsz, H_A // nh, nq),
        in_specs=[
            pl.BlockSpec((1, nh, LANES, TQ), lambda b, hp, qi: (b, hp, 0, qi)),
            pl.BlockSpec((1, nh, s, LANES), lambda b, hp, qi: (b, hp, 0, 0)),
            pl.BlockSpec((1, nq, nh * HEAD_DIM, TQ), lambda b, hp, qi: (b, 0, hp, 0)),
            pl.BlockSpec((nh, 2 * TQ, TQ), lambda b, hp, qi: (hp, 0, 0)),
            pl.BlockSpec((nh, 8, LANES), lambda b, hp, qi: (hp, 0, 0)),
        ],
        out_specs=pl.BlockSpec((1, TQ, nh * HEAD_DIM), lambda b, hp, qi: (b, qi, hp)),
        out_shape=jax.ShapeDtypeStruct((bsz, s, H_A * HEAD_DIM), BF16),
        scratch_shapes=_flash_scratch(nh),
        compiler_params=_params(("parallel", "parallel", "arbitrary")),
        name="moba",
    )(qaug, kaug, vat, tab, cb)


def _nsa_cmp_kernel(q_ref, kc_ref, vct_ref, ovt_ref, oc_ref, selb_ref, *, n_sel, n_cmp, n_parts):
    qi = pl.program_id(1)
    ncp = kc_ref.shape[2]
    t0 = qi * TQ
    zeros = jnp.zeros((HEAD_DIM, TQ), BF16)
    qs = [jnp.concatenate([q_ref[0, hd].astype(BF16), zeros], axis=0) for hd in range(H_B)]

    def body(nk, nb):
        n_idx = lax.broadcasted_iota(jnp.int32, (nk, TQ), 0)
        t_idx = lax.broadcasted_iota(jnp.int32, (nk, TQ), 1) + t0
        mask = (n_idx * CMP_STRIDE + (CMP_LEN - 1) <= t_idx) & (n_idx < n_cmp)
        any_key = t_idx[0:1] >= CMP_LEN - 1
        blk = lax.broadcasted_iota(jnp.int32, (nb, TQ), 0)
        cur = (lax.broadcasted_iota(jnp.int32, (nb, TQ), 1) + t0) >> _LOG2_SEL_BLK
        ok = blk <= cur
        forced = (blk == 0) | (blk == cur) | (blk == cur - 1)
        ovt = ovt_ref[0:nb, 0:nk]
        scores = []
        for g in range(G_B):
            kc = kc_ref[0, g, 0:nk, :]
            vct = _with_ones(vct_ref[0, g, :, 0:nk])
            psum = jnp.zeros((nk, TQ), F32)
            for hd in range(g * R_B, (g + 1) * R_B):
                z = jnp.where(mask, _dot(kc, qs[hd]), NEG)
                e = jnp.exp2(z - _col_max(z))
                acc = _dot(vct, e.astype(BF16))
                rinv = jnp.where(any_key, 1.0 / acc[HEAD_DIM:HEAD_DIM + 1], 0.0)
                oc_ref[0, hd] = acc[0:HEAD_DIM] * rinv
                psum = psum + e * rinv
            ph, pl_ = _split(psum)
            imp_t = _dot(ovt, ph) + _dot(ovt, pl_)
            scores.append(jnp.where(ok, jnp.where(forced, BIG, imp_t), -jnp.inf))
        for g, picked in enumerate(_topk_rows(scores, blk, n_sel)):
            selb_ref[0, g, 0:nb, :] = jnp.where(ok & (picked > 0.0), 0.0, NEG).astype(BF16)
            if nb < LANES:
                selb_ref[0, g, nb:LANES, :] = jnp.full((LANES - nb, TQ), NEG, BF16)

    part = ncp // n_parts
    need = jnp.minimum(((qi + 1) * (TQ // CMP_STRIDE) + part - 1) // part, n_parts)
    for v in range(1, n_parts + 1):
        pl.when(need == v)(functools.partial(body, v * part, min(LANES, v * part * CMP_STRIDE // SEL_BLK)))


def _cmp_parts(ncp):
    return 4 if ncp % (4 * LANES) == 0 else 1


def _nsa_cmp(qb, kcmp, vcmpt, ovt, n_cmp):
    bsz, _, _, s = qb.shape
    nq = s // TQ
    ncp = kcmp.shape[2]
    n_sel = min(SEL_TOPK, s // SEL_BLK)
    n_parts = _cmp_parts(ncp)
    return pl.pallas_call(
        functools.partial(_nsa_cmp_kernel, n_sel=n_sel, n_cmp=n_cmp, n_parts=n_parts),
        grid=(bsz, nq),
        in_specs=[
            pl.BlockSpec((1, H_B, HEAD_DIM, TQ), lambda b, qi: (b, 0, 0, qi)),
            pl.BlockSpec((1, G_B, ncp, LANES), lambda b, qi: (b, 0, 0, 0)),
            pl.BlockSpec((1, G_B, HEAD_DIM, ncp), lambda b, qi: (b, 0, 0, 0)),
            pl.BlockSpec((LANES, ncp), lambda b, qi: (0, 0)),
        ],
        out_specs=[
            pl.BlockSpec((1, H_B, HEAD_DIM, TQ), lambda b, qi: (b, 0, 0, qi)),
            pl.BlockSpec((1, G_B, LANES, TQ), lambda b, qi: (b, 0, 0, qi)),
        ],
        out_shape=[jax.ShapeDtypeStruct((bsz, H_B, HEAD_DIM, s), F32),
                   jax.ShapeDtypeStruct((bsz, G_B, LANES, s), BF16)],
        compiler_params=_params(("parallel", "parallel")),
        name="nsa_cmp",
    )(qb, kcmp, vcmpt, ovt)


def _nsa_main_kernel(q_ref, selb_ref, ks_ref, vst_ref, kw_ref, vwt_ref, oc_ref, gates_ref,
                     tabs_ref, tabw_ref, cb_ref, o_ref, sa_ref, sb_ref, m_ref, acc_ref):
    qi = pl.program_id(2)
    selb = selb_ref[0, 0]
    gates = gates_ref[0, 0]
    zeros = jnp.zeros((HEAD_DIM, TQ), BF16)
    j1 = jnp.maximum(qi - 1, 0)
    j2 = jnp.maximum(qi - 2, 0)
    off1 = jnp.where(qi >= 1, 0.0, NEG)
    off2 = jnp.where(qi >= 2, 0.0, NEG)
    n_pairs, jl, off_l = _far_tiles(qi)

    heads = range(R_B)
    qw = [jnp.concatenate([q_ref[0, r].astype(BF16), zeros], axis=0) for r in heads]
    qs = [jnp.concatenate([qw[r], selb], axis=0) for r in heads]
    cbs = [cb_ref[r][0:1, 0:1] for r in heads]

    def s_sel(r, j):
        return _dot(ks_ref[0, 0, _tile_rows(j), :], qs[r])

    def s_win(r, j):
        return _dot(kw_ref[0, 0, _tile_rows(j), :], qw[r])

    def s_sel2(r, i):
        return _dot(ks_ref[0, 0, _pair_rows(i), :], qs[r])

    far = _FarLoop(n_pairs, heads, s_sel2, lambda r, j: vst_ref[0, j], cbs, sa_ref, sb_ref, m_ref, acc_ref)

    sw = [[s_win(r, j) for r in heads] for j in (qi, j1, j2)]
    ss = [[s_sel(r, j) for r in heads] for j in (qi, j1, jl)]
    tops_first = far.first()

    chains = [[(sw[0][r] + tabw_ref[r, 2 * TQ:3 * TQ, :], vwt_ref[0, qi], None, None),
               (sw[1][r] + tabw_ref[r, TQ:2 * TQ, :] + off1, vwt_ref[0, j1], None, None),
               (sw[2][r] + tabw_ref[r, 0:TQ, :] + off2, vwt_ref[0, j2], None, None)] for r in heads]
    chains += [[(ss[0][r] + tabs_ref[r, TQ:2 * TQ, :], vst_ref[0, qi], None, None),
                (ss[1][r] + tabs_ref[r, 0:TQ, :] + off1, vst_ref[0, j1], None, None),
                (ss[2][r], vst_ref[0, jl], cbs[r] + off_l, None)] for r in heads]
    done = _flash_update([None] * len(chains), chains)
    win = [_flash_out(acc) for _, acc in done[:R_B]]
    for r, (m, acc) in zip(heads, done[R_B:]):
        m_ref[r] = m
        acc_ref[r] = acc
    far.run(tops_first)

    outs = [gates[3 * r:3 * r + 1] * oc_ref[0, r]
            + gates[3 * r + 1:3 * r + 2] * _flash_out(acc_ref[r])
            + gates[3 * r + 2:3 * r + 3] * win[r] for r in heads]
    o_ref[0] = jnp.concatenate(outs, axis=0).T.astype(BF16)


def _nsa_main(qb, selb, ksaug, vst, kw, vwt, oc, gates, tabs, tabw, cb):
    bsz, _, _, s = qb.shape
    nq = s // TQ
    return pl.pallas_call(
        _nsa_main_kernel,
        grid=(bsz, G_B, nq),
        in_specs=[
            pl.BlockSpec((1, R_B, HEAD_DIM, TQ), lambda b, g, qi: (b, g, 0, qi)),
            pl.BlockSpec((1, 1, LANES, TQ), lambda b, g, qi: (b, g, 0, qi)),
            pl.BlockSpec((1, 1, s, 2 * LANES), lambda b, g, qi: (b, g, 0, 0)),
            pl.BlockSpec((1, nq, HEAD_DIM, TQ), lambda b, g, qi: (b, 0, g, 0)),
            pl.BlockSpec((1, 1, s, LANES), lambda b, g, qi: (b, g, 0, 0)),
            pl.BlockSpec((1, nq, HEAD_DIM, TQ), lambda b, g, qi: (b, 0, g, 0)),
            pl.BlockSpec((1, R_B, HEAD_DIM, TQ), lambda b, g, qi: (b, g, 0, qi)),
            pl.BlockSpec((1, 1, GATE_ROWS, TQ), lambda b, g, qi: (b, g, 0, qi)),
            pl.BlockSpec((R_B, 2 * TQ, TQ), lambda b, g, qi: (g, 0, 0)),
            pl.BlockSpec((R_B, 3 * TQ, TQ), lambda b, g, qi: (g, 0, 0)),
            pl.BlockSpec((R_B, 8, LANES), lambda b, g, qi: (g, 0, 0)),
        ],
        out_specs=pl.BlockSpec((1, TQ, R_B * HEAD_DIM), lambda b, g, qi: (b, qi, g)),
        out_shape=jax.ShapeDtypeStruct((bsz, s, H_B * HEAD_DIM), BF16),
        scratch_shapes=_flash_scratch(R_B),
        compiler_params=_params(("parallel", "parallel", "arbitrary")),
        name="nsa_main",
    )(qb, selb, ksaug, vst, kw, vwt, oc, gates, tabs, tabw, cb)


def _out_ffn_kernel(x_ref, oa_ref, ob_ref, wo_ref, gtm_ref, gffn_ref, scf_ref, shf_ref, gtf_ref,
                    wg_ref, wu_ref, wd_ref, o_ref, x1_ref, h_ref, acc_ref):
    f = pl.program_id(2)
    half = oa_ref.shape[2]

    @pl.when(f == 0)
    def _():
        mix = _dot(oa_ref[0], wo_ref[0:half, :]) + _dot(ob_ref[0], wo_ref[half:2 * half, :])
        x1 = x_ref[0] + gtm_ref[0, 0] * mix
        x1_ref[...] = x1
        h = _rms_rows(x1, gffn_ref[...]) * (1.0 + scf_ref[0, 0]) + shf_ref[0, 0]
        h_ref[...] = h.astype(BF16)

    h = h_ref[...]
    gate = _dot(h, wg_ref[...])
    up = _dot(h, wu_ref[...])
    act = (gate * jax.nn.sigmoid(gate) * up).astype(BF16)
    part = _dot(act, wd_ref[...])

    @pl.when(f == 0)
    def _():
        acc_ref[...] = part

    @pl.when(f > 0)
    def _():
        acc_ref[...] += part

    @pl.when(f == pl.num_programs(2) - 1)
    def _():
        o_ref[0] = x1_ref[...] + gtf_ref[0, 0] * acc_ref[...]


def _out_ffn(x, oa, ob, wo, mod4, gffn, wgu, wd):
    bsz, s, d = x.shape
    fh = wd.shape[0]
    tm = TM_FFN
    tf = fh // 2 if (fh // 2) % LANES == 0 else fh
    nf = fh // tf
    tok = lambda b, t, f: (b, t, 0)
    modspec = lambda k: pl.BlockSpec((1, 1, 1, d), lambda b, t, f: (b, k, 0, 0))
    return pl.pallas_call(
        _out_ffn_kernel,
        grid=(bsz, s // tm, nf),
        in_specs=[
            pl.BlockSpec((1, tm, d), tok),
            pl.BlockSpec((1, tm, oa.shape[2]), tok),
            pl.BlockSpec((1, tm, ob.shape[2]), tok),
            pl.BlockSpec((d, d), lambda b, t, f: (0, 0)),
            modspec(2),
            pl.BlockSpec((1, d), lambda b, t, f: (0, 0)),
            modspec(4),
            modspec(3),
            modspec(5),
            pl.BlockSpec((d, tf), lambda b, t, f: (0, f)),
            pl.BlockSpec((d, tf), lambda b, t, f: (0, f + nf)),
            pl.BlockSpec((tf, d), lambda b, t, f: (f, 0)),
        ],
        out_specs=pl.BlockSpec((1, tm, d), tok),
        out_shape=jax.ShapeDtypeStruct((bsz, s, d), F32),
        scratch_shapes=[pltpu.VMEM((tm, d), F32), pltpu.VMEM((tm, d), BF16), pltpu.VMEM((tm, d), F32)],
        compiler_params=_params(("parallel", "parallel", "arbitrary")),
        name="out_ffn",
    )(x, oa, ob, wo, mod4, gffn, mod4, mod4, mod4, wgu, wgu, wd)


def _t5_bucket_np(d):
    max_exact = N_BUCKETS // 2
    d = np.maximum(d, 0)
    df = np.maximum(d, 1).astype(np.float64)
    large = max_exact + (np.log(df / max_exact) / math.log(MAX_DIST / max_exact)
                         * (N_BUCKETS - max_exact)).astype(np.int64)
    large = np.minimum(large, N_BUCKETS - 1)
    return np.where(d < max_exact, d, large).astype(np.int32)


def _bias_expand_kernel(tab_ref, bucket_ref, o_ref):
    hd = pl.program_id(0)
    bucket = bucket_ref[...]
    acc = jnp.full(bucket.shape, NEG, F32)
    for b in range(N_BUCKETS):
        acc = jnp.where(bucket == b, tab_ref[hd, b], acc)
    o_ref[0] = acc


def _bias_expand(tab, bucket):
    nh = tab.shape[0]
    return pl.pallas_call(
        _bias_expand_kernel,
        grid=(nh,),
        in_specs=[pl.BlockSpec(memory_space=pltpu.SMEM),
                  pl.BlockSpec(bucket.shape, lambda h: (0, 0))],
        out_specs=pl.BlockSpec((1,) + bucket.shape, lambda h: (h, 0, 0)),
        out_shape=jax.ShapeDtypeStruct((nh,) + bucket.shape, F32),
        compiler_params=_params(("parallel",)),
        name="bias_expand",
    )(tab, jnp.asarray(bucket, jnp.int32))


def _bias_tables(rel_bias):
    tab = rel_bias.T.astype(F32) * LOG2_E
    i = np.arange(TQ)[None, :]
    d_near = i + TQ - np.arange(2 * TQ)[:, None]
    near = _bias_expand(tab, np.where(d_near >= 0, _t5_bucket_np(d_near), -1))
    d_win = i + 2 * TQ - np.arange(3 * TQ)[:, None]
    ok_win = (d_win >= 0) & (d_win < WINDOW)
    win = _bias_expand(tab[H_A:], np.where(ok_win, _t5_bucket_np(d_win), -1))
    far = jnp.broadcast_to(tab[:, N_BUCKETS - 1][:, None, None], (tab.shape[0], 8, LANES))
    return near, win, far


def _overlap_t(n_cmp_pad, n_cmp):
    cs = np.arange(n_cmp_pad)[None, :] * CMP_STRIDE
    ss = np.arange(LANES)[:, None] * SEL_BLK
    ov = (cs < ss + SEL_BLK) & (cs + CMP_LEN > ss) & (np.arange(n_cmp_pad)[None, :] < n_cmp)
    return jnp.asarray(ov.astype(np.float32), BF16)


def _block_diag(n):
    m = (np.arange(n)[:, None] // HEAD_DIM == np.arange(n)[None, :] // HEAD_DIM)
    return jnp.asarray(m.astype(np.float32) / HEAD_DIM, BF16)


def kernel(x, c, rel_bias, w_ada, b_ada, g_mix, w_in, q_norm_a, k_norm_a, q_norm_b, k_norm_cmp,
           k_norm_sel, k_norm_win, cmp_pe_k, cmp_w1_k, cmp_w2_k, cmp_pe_v, cmp_w1_v, cmp_w2_v,
           w_out, g_ffn, w_gu, w_down):
    bsz, s, d = x.shape
    depth = w_ada.shape[0]
    assert s % TM_IN == 0 and s % TM_FFN == 0 and s % (2 * TQ) == 0
    assert s // BLK_A <= HEAD_DIM and s // SEL_BLK <= LANES
    assert WINDOW == 2 * TQ and BLK_A == TQ and MAX_DIST <= TQ
    n_chunks = s // CMP_STRIDE
    n_cmp = (s - CMP_LEN) // CMP_STRIDE + 1
    scale = HEAD_DIM ** -0.5 * LOG2_E
    hd = HEAD_DIM

    near, win, far = _bias_tables(rel_bias)
    ovt = _overlap_t(n_chunks, n_cmp)
    bd = _block_diag(512)
    tile = lambda g, n: jnp.tile(g.astype(F32), n).reshape(1, -1)
    tile_t = lambda g, n: jnp.broadcast_to(jnp.tile(g.astype(F32), n)[:, None], (n * hd, TM_IN))

    for l in range(depth):
        mod = _ada(c, w_ada[l], b_ada[l])
        mod4 = mod.reshape(bsz, ADA_CHUNKS, 1, d)

        wl = w_in[l]
        cols = np.cumsum([0, H_A * hd, H_A * hd, H_A * hd, H_B * hd] + [G_B * hd] * 6)
        qa_c, ka_c, va_c, qb_c, kc_c, vc_c, ks_c, vs_c, kw_c, vw_c = [
            wl[:, int(a):int(b)] for a, b in zip(cols[:-1], cols[1:])]
        gl = wl[:, int(cols[-1]):].reshape(d, G_B, 3 * R_B)
        gl = jnp.pad(gl, ((0, 0), (0, 0), (0, GATE_ROWS - 3 * R_B))).reshape(d, G_B * GATE_ROWS)
        w_rows = jnp.concatenate([ka_c, kc_c, vc_c, ks_c, kw_c], axis=1).astype(BF16)
        w_t = jnp.concatenate([qa_c, va_c, qb_c, vs_c, vw_c, gl], axis=1).T.astype(BF16)

        (qa, kaug, vat, kmean, qb, kc, vc, ksaug, vst, kw, vwt, gates) = _inproj(
            x, mod4, mod4, g_mix[l].reshape(1, d), w_rows, w_t, bd,
            tile_t(q_norm_a[l], H_A) * scale, tile(k_norm_a[l], H_A), tile_t(q_norm_b[l], H_B) * scale,
            tile(k_norm_sel[l], G_B), tile(k_norm_win[l], G_B))

        nba = s // BLK_A
        km = kmean.reshape(bsz, nba, H_A, hd).transpose(0, 2, 1, 3)
        km = jnp.pad(km, ((0, 0), (0, 0), (0, hd - nba), (0, LANES - hd)))
        o_a = _moba(_moba_gate(qa, km), kaug, vat, near[:H_A], far[:H_A])

        chunks = lambda t: t.reshape(bsz, s, G_B, hd).transpose(0, 2, 1, 3).reshape(
            bsz, G_B, n_chunks, CMP_STRIDE * hd)
        w1 = jnp.stack([cmp_w1_k[l], cmp_w1_v[l]]).astype(BF16)
        w2t = jnp.stack([cmp_w2_k[l].T, cmp_w2_v[l].T]).astype(BF16)
        pe = jnp.stack([cmp_pe_k[l], cmp_pe_v[l]]).reshape(2, 1, CMP_LEN * hd)
        pe = jnp.broadcast_to(pe, (2, 8, CMP_LEN * hd)).astype(BF16)
        kcmp, vcmpt = _compress(chunks(kc), chunks(vc), w1, w2t, pe,
                                k_norm_cmp[l].astype(F32).reshape(1, hd))

        oc, selb = _nsa_cmp(qb, kcmp, vcmpt, ovt, n_cmp)
        o_b = _nsa_main(qb, selb, ksaug, vst, kw, vwt, oc, gates, near[H_A:], win, far[H_A:])

        x = _out_ffn(x, o_a, o_b, w_out[l].astype(BF16), mod4, g_ffn[l].reshape(1, d),
                     w_gu[l].astype(BF16), w_down[l].astype(BF16))
    return x
```

```python
import functools
import math

import jax
import jax.numpy as jnp
import numpy as np
from jax import lax
from jax.experimental import pallas as pl
from jax.experimental.pallas import tpu as pltpu

F32 = jnp.float32
BF16 = jnp.bfloat16

HEAD_DIM = 64
LANES = 128
BF16_ROWS = 16
H_A = 8
H_B = 8
G_B = 2
R_B = H_B // G_B
BLK_A = 256
TOPK_A = 3
CMP_LEN = 32
CMP_STRIDE = 16
CMP_HIDDEN = 256
SEL_BLK = 64
SEL_TOPK = 16
WINDOW = 512
N_BUCKETS = 32
MAX_DIST = 128
ADA_CHUNKS = 6
NEG = -1e30
BIG = 1e9
EPS = 1e-6
LOG2_E = math.log2(math.e)

_LOG2_BLK_A = BLK_A.bit_length() - 1
_LOG2_SEL_BLK = SEL_BLK.bit_length() - 1

TQ = 256
TM_IN = 512
TM_FFN = 512
MOBA_HEADS = 4
GATE_ROWS = 16
ACC_ROWS = HEAD_DIM + BF16_ROWS
VMEM_LIMIT = 56 * 1024 * 1024


def _dot(a, b):
    return jnp.dot(a, b, preferred_element_type=F32)


def _dot_nt(a, b):
    return lax.dot_general(a, b, (((1,), (1,)), ((), ())), preferred_element_type=F32)


def _split(a):
    hi = a.astype(BF16)
    lo = (a - hi.astype(F32)).astype(BF16)
    return hi, lo


def _dot3(a, b):
    ah, al = _split(a)
    bh, bl = _split(b)
    return _dot(ah, bh) + (_dot(al, bh) + _dot(ah, bl))


def _params(sem):
    return pltpu.CompilerParams(dimension_semantics=sem, vmem_limit_bytes=VMEM_LIMIT)


def _ada_kernel(c_ref, w_ref, b_ref, o_ref):
    c = c_ref[...]
    o_ref[...] = _dot3(c * jax.nn.sigmoid(c), w_ref[...]) + b_ref[...]


def _ada(c, w, b):
    bsz, d = c.shape
    n = w.shape[1]
    tn = 512
    return pl.pallas_call(
        _ada_kernel,
        grid=(n // tn,),
        in_specs=[pl.BlockSpec((bsz, d), lambda j: (0, 0)),
                  pl.BlockSpec((d, tn), lambda j: (0, j)),
                  pl.BlockSpec((1, tn), lambda j: (0, j))],
        out_specs=pl.BlockSpec((bsz, tn), lambda j: (0, j)),
        out_shape=jax.ShapeDtypeStruct((bsz, n), F32),
        compiler_params=_params(("arbitrary",)),
        name="ada",
    )(c, w, b.reshape(1, n))


def _rms_rows(xf, g):
    ms = jnp.mean(xf * xf, axis=-1, keepdims=True)
    return xf * lax.rsqrt(ms + EPS) * g


def _head_norm(t, bd, gain):
    hi, lo = _split(t * t)
    ms = _dot(hi, bd) + _dot(lo, bd)
    return t * lax.rsqrt(ms + EPS) * gain


def _head_norm_t(t, gain):
    heads = []
    for hd in range(t.shape[0] // HEAD_DIM):
        th = t[hd * HEAD_DIM:(hd + 1) * HEAD_DIM]
        ms = jnp.mean(th * th, axis=0, keepdims=True)
        heads.append(th * lax.rsqrt(ms + EPS) * gain[hd * HEAD_DIM:(hd + 1) * HEAD_DIM])
    return heads


def _inproj_kernel(x_ref, sc_ref, sh_ref, gmix_ref, wr_ref, wt_ref, bd_ref, gqa_ref, gka_ref, gqb_ref,
                   gks_ref, gkw_ref,
                   qa_ref, kaug_ref, va_ref, kmean_ref, qb_ref, kc_ref, vc_ref, ksaug_ref,
                   vs_ref, kw_ref, vw_ref, gates_ref):
    tm = x_ref.shape[1]
    ti = pl.program_id(1)
    xf = x_ref[0]
    h = _rms_rows(xf, gmix_ref[...]) * (1.0 + sc_ref[0, 0]) + sh_ref[0, 0]
    hb = h.astype(BF16)

    def proj(c0, c1):
        return _dot(hb, wr_ref[:, c0:c1])

    def proj_t(r0, r1):
        return _dot_nt(wt_ref[r0:r1, :], hb)

    bd = bd_ref[...]
    bd2 = bd_ref[0:LANES, 0:LANES]
    lane = lax.broadcasted_iota(jnp.int32, (tm, LANES), 1)
    row = lax.broadcasted_iota(jnp.int32, (tm, LANES), 0) + ti * tm
    low = lane < HEAD_DIM

    def k_in_low(pair, odd):
        return pltpu.roll(pair, HEAD_DIM, 1) if odd else pair

    for hd, qh in enumerate(_head_norm_t(proj_t(0, 512), gqa_ref[...])):
        qa_ref[0, hd] = qh

    def put_tiles(ref, vt):
        for i in range(tm // TQ):
            ref[0, i] = vt[:, i * TQ:(i + 1) * TQ].astype(BF16)

    put_tiles(va_ref, proj_t(512, 1024))
    for hd, qh in enumerate(_head_norm_t(proj_t(1024, 1536), gqb_ref[...])):
        qb_ref[0, hd] = qh
    put_tiles(vs_ref, proj_t(1536, 1664))
    put_tiles(vw_ref, proj_t(1664, 1792))
    gl = jax.nn.sigmoid(proj_t(1792, 1792 + G_B * GATE_ROWS))
    for g in range(G_B):
        gates_ref[0, g] = gl[g * GATE_ROWS:(g + 1) * GATE_ROWS]

    ka = _head_norm(proj(0, 512), bd, gka_ref[...])
    oh_a = jnp.where(lane - HEAD_DIM == (row >> _LOG2_BLK_A), 1.0, 0.0)
    for hd in range(H_A):
        pair = ka[:, (hd // 2) * LANES:(hd // 2 + 1) * LANES]
        kaug_ref[0, hd] = jnp.where(low, k_in_low(pair, hd % 2), oh_a).astype(BF16)
    for i in range(tm // BLK_A):
        kmean_ref[0, i] = jnp.mean(ka[i * BLK_A:(i + 1) * BLK_A], axis=0, keepdims=True)

    kc_ref[0] = proj(512, 640).astype(BF16)
    vc_ref[0] = proj(640, 768).astype(BF16)

    ks = _head_norm(proj(768, 896), bd2, gks_ref[...])
    kw = _head_norm(proj(896, 1024), bd2, gkw_ref[...])
    oh_s = jnp.where(lane == (row >> _LOG2_SEL_BLK), 1.0, 0.0).astype(BF16)
    for g in range(G_B):
        ksaug_ref[0, g] = jnp.concatenate(
            [jnp.where(low, k_in_low(ks, g), 0.0).astype(BF16), oh_s], axis=1)
        kw_ref[0, g] = jnp.where(low, k_in_low(kw, g), 0.0).astype(BF16)


def _inproj(x, sc, sh, gmix, wr, wt, bd, gqa, gka, gqb, gks, gkw):
    bsz, s, d = x.shape
    tm = TM_IN
    nt = s // tm
    nba = s // BLK_A
    const2 = lambda b, t: (0, 0)
    tok3 = lambda b, t: (b, t, 0)
    tok4 = lambda b, t: (b, 0, t, 0)
    tile4 = lambda b, t: (b, t, 0, 0)
    tr4 = lambda b, t: (b, 0, 0, t)
    in_specs = [
        pl.BlockSpec((1, tm, d), tok3),
        pl.BlockSpec((1, 1, 1, d), lambda b, t: (b, 1, 0, 0)),
        pl.BlockSpec((1, 1, 1, d), lambda b, t: (b, 0, 0, 0)),
        pl.BlockSpec((1, d), const2),
        pl.BlockSpec(wr.shape, const2),
        pl.BlockSpec(wt.shape, const2),
        pl.BlockSpec((512, 512), const2),
        pl.BlockSpec((512, tm), const2),
        pl.BlockSpec((1, 512), const2),
        pl.BlockSpec((512, tm), const2),
        pl.BlockSpec((1, LANES), const2),
        pl.BlockSpec((1, LANES), const2),
    ]
    out_shape = [
        jax.ShapeDtypeStruct((bsz, H_A, HEAD_DIM, s), F32),
        jax.ShapeDtypeStruct((bsz, H_A, s, LANES), BF16),
        jax.ShapeDtypeStruct((bsz, s // TQ, 512, TQ), BF16),
        jax.ShapeDtypeStruct((bsz, nba, 1, 512), F32),
        jax.ShapeDtypeStruct((bsz, H_B, HEAD_DIM, s), F32),
        jax.ShapeDtypeStruct((bsz, s, LANES), BF16),
        jax.ShapeDtypeStruct((bsz, s, LANES), BF16),
        jax.ShapeDtypeStruct((bsz, G_B, s, 2 * LANES), BF16),
        jax.ShapeDtypeStruct((bsz, s // TQ, LANES, TQ), BF16),
        jax.ShapeDtypeStruct((bsz, G_B, s, LANES), BF16),
        jax.ShapeDtypeStruct((bsz, s // TQ, LANES, TQ), BF16),
        jax.ShapeDtypeStruct((bsz, G_B, GATE_ROWS, s), F32),
    ]
    out_specs = [
        pl.BlockSpec((1, H_A, HEAD_DIM, tm), tr4),
        pl.BlockSpec((1, H_A, tm, LANES), tok4),
        pl.BlockSpec((1, tm // TQ, 512, TQ), tile4),
        pl.BlockSpec((1, tm // BLK_A, 1, 512), lambda b, t: (b, t, 0, 0)),
        pl.BlockSpec((1, H_B, HEAD_DIM, tm), tr4),
        pl.BlockSpec((1, tm, LANES), tok3),
        pl.BlockSpec((1, tm, LANES), tok3),
        pl.BlockSpec((1, G_B, tm, 2 * LANES), tok4),
        pl.BlockSpec((1, tm // TQ, LANES, TQ), tile4),
        pl.BlockSpec((1, G_B, tm, LANES), tok4),
        pl.BlockSpec((1, tm // TQ, LANES, TQ), tile4),
        pl.BlockSpec((1, G_B, GATE_ROWS, tm), tr4),
    ]
    return pl.pallas_call(
        _inproj_kernel,
        grid=(bsz, nt),
        in_specs=in_specs,
        out_specs=out_specs,
        out_shape=out_shape,
        compiler_params=_params(("parallel", "parallel")),
        name="inproj",
    )(x, sc, sh, gmix, wr, wt, bd, gqa, gka, gqb, gks, gkw)


def _compress_kernel(ck_ref, cv_ref, w1_ref, w2t_ref, pe_ref, gk_ref, ok_ref, ov_ref):
    half = CMP_STRIDE * HEAD_DIM
    for kv, c_ref in enumerate((ck_ref, cv_ref)):
        for g in range(G_B):
            c = c_ref[0, g]
            a = _dot(c, w1_ref[kv, 0:half, :])
            b = _dot(c, w1_ref[kv, half:2 * half, :])
            n = a.shape[0]
            b_next = pltpu.roll(b, n - 1, 0)
            pe_term = _dot(pe_ref[kv], w1_ref[kv])[0:1]
            hid = jax.nn.gelu(a + b_next + pe_term).astype(BF16)
            if kv == 0:
                y = _dot_nt(hid, w2t_ref[kv])
                ms = jnp.mean(y * y, axis=1, keepdims=True)
                y = y * lax.rsqrt(ms + EPS) * gk_ref[...]
                ok_ref[0, g] = jnp.concatenate([y, jnp.zeros_like(y)], axis=1).astype(BF16)
            else:
                ov_ref[0, g] = _dot_nt(w2t_ref[kv], hid).astype(BF16)


def _compress(ck, cv, w1, w2t, pe, gk):
    bsz, g, n, width = ck.shape
    blk = pl.BlockSpec((1, g, n, width), lambda b: (b, 0, 0, 0))
    full = lambda a: pl.BlockSpec(a.shape, lambda b: (0,) * a.ndim)
    return pl.pallas_call(
        _compress_kernel,
        grid=(bsz,),
        in_specs=[blk, blk, full(w1), full(w2t), full(pe), full(gk)],
        out_specs=[pl.BlockSpec((1, g, n, LANES), lambda b: (b, 0, 0, 0)),
                   pl.BlockSpec((1, g, HEAD_DIM, n), lambda b: (b, 0, 0, 0))],
        out_shape=[jax.ShapeDtypeStruct((bsz, g, n, LANES), BF16),
                   jax.ShapeDtypeStruct((bsz, g, HEAD_DIM, n), BF16)],
        compiler_params=_params(("parallel",)),
        name="compress",
    )(ck, cv, w1, w2t, pe, gk)


def _with_ones(vt):
    return jnp.concatenate([vt, jnp.ones((BF16_ROWS, vt.shape[1]), BF16)], axis=0)


def _col_max(s):
    while s.shape[0] > 8:
        half = s.shape[0] // 2
        s = jnp.maximum(s[0:half], s[half:2 * half])
    return jnp.max(s, axis=0, keepdims=True)


def _flash_update(carries, chains):
    m_news = []
    for carry, tiles in zip(carries, chains):
        tops = []
        for s, _, bias, top in tiles:
            top = _col_max(s) if top is None else top
            tops.append(top if bias is None else top + bias)
        m_news.append(functools.reduce(jnp.maximum, tops if carry is None else tops + [carry[0]]))
    pvs = [None] * len(chains)
    for t in range(max(len(tiles) for tiles in chains)):
        for c, tiles in enumerate(chains):
            if t < len(tiles):
                s, vt, bias, _ = tiles[t]
                p = jnp.exp2(s - (m_news[c] if bias is None else m_news[c] - bias)).astype(BF16)
                part = _dot(_with_ones(vt), p)
                pvs[c] = part if pvs[c] is None else pvs[c] + part
    outs = []
    for carry, m_new, pv in zip(carries, m_news, pvs):
        outs.append((m_new, pv if carry is None else jnp.exp2(carry[0] - m_new) * carry[1] + pv))
    return outs


def _flash_out(acc):
    return acc[0:HEAD_DIM] / acc[HEAD_DIM:HEAD_DIM + 1]


def _topk_rows(scores, index, k):
    scores = list(scores)
    picked = [jnp.zeros(sc.shape, F32) for sc in scores]
    for _ in range(k):
        mx = [jnp.max(sc, axis=0, keepdims=True) for sc in scores]
        cand = [jnp.where(sc == m, index, jnp.int32(1 << 20)) for sc, m in zip(scores, mx)]
        first = [jnp.min(c, axis=0, keepdims=True) for c in cand]
        hit = [index == f for f in first]
        picked = [jnp.where(h, 1.0, p) for h, p in zip(hit, picked)]
        scores = [jnp.where(h, -jnp.inf, sc) for h, sc in zip(hit, scores)]
    return picked


def _tile_rows(j):
    return pl.ds(pl.multiple_of(j * TQ, TQ), TQ)


def _moba_gate_kernel(q_ref, km_ref, o_ref, *, n_sel):
    nh, tg = q_ref.shape[1], q_ref.shape[3]
    t0 = pl.program_id(2) * tg
    shape = (km_ref.shape[2], tg)
    blk = lax.broadcasted_iota(jnp.int32, shape, 0)
    own = (lax.broadcasted_iota(jnp.int32, shape, 1) + t0) >> _LOG2_BLK_A
    valid = blk < own
    qs, gates = [], []
    for hh in range(nh):
        q = q_ref[0, hh]
        gate = _dot3(km_ref[0, hh], jnp.concatenate([q, jnp.zeros_like(q)], axis=0))
        qs.append(q)
        gates.append(jnp.where(valid, gate, -jnp.inf))
    for hh, picked in enumerate(_topk_rows(gates, blk, n_sel)):
        keep = jnp.where(valid, picked, 0.0) + jnp.where(blk == own, 1.0, 0.0)
        selb = jnp.where(keep > 0.0, 0.0, NEG)
        o_ref[0, hh] = jnp.concatenate([qs[hh], selb], axis=0).astype(BF16)


def _moba_gate(qa, km):
    bsz, nh, _, s = qa.shape
    tg = min(s, 2048)
    hpb = 2
    n_sel = max(1, min(TOPK_A, s // BLK_A - 1))
    return pl.pallas_call(
        functools.partial(_moba_gate_kernel, n_sel=n_sel),
        grid=(bsz, nh // hpb, s // tg),
        in_specs=[pl.BlockSpec((1, hpb, HEAD_DIM, tg), lambda b, h, t: (b, h, 0, t)),
                  pl.BlockSpec((1, hpb, HEAD_DIM, LANES), lambda b, h, t: (b, h, 0, 0))],
        out_specs=pl.BlockSpec((1, hpb, LANES, tg), lambda b, h, t: (b, h, 0, t)),
        out_shape=jax.ShapeDtypeStruct((bsz, nh, LANES, s), BF16),
        compiler_params=_params(("parallel", "parallel", "parallel")),
        name="moba_gate",
    )(qa, km)


def _far_tiles(qi):
    n_far = jnp.maximum(qi - 1, 0)
    left = jnp.maximum(n_far - 1, 0)
    off_left = jnp.where((n_far & 1) == 1, 0.0, NEG)
    return n_far >> 1, left, off_left


class _FarLoop:
    def __init__(self, n_pairs, heads, qk_tile, values, cbs, sa_ref, sb_ref, m_ref, acc_ref):
        self.n_pairs, self.heads, self.qk_tile, self.values, self.cbs = n_pairs, heads, qk_tile, values, cbs
        self.sa_ref, self.sb_ref, self.m_ref, self.acc_ref = sa_ref, sb_ref, m_ref, acc_ref
        self.last = jnp.maximum(n_pairs - 1, 0)

    def fetch(self, buf_ref, h, i):
        s_lo = self.qk_tile(h, 2 * i)
        s_hi = self.qk_tile(h, 2 * i + 1)
        buf_ref[h, 0:TQ, :] = s_lo
        buf_ref[h, TQ:2 * TQ, :] = s_hi
        return _col_max(s_lo), _col_max(s_hi)

    def consume(self, buf_ref, h, top, i, off):
        bias = self.cbs[h] if off is None else self.cbs[h] + off
        (m, acc), = _flash_update([(self.m_ref[h], self.acc_ref[h])], [[
            (buf_ref[h, 0:TQ, :], self.values(h, 2 * i), bias, top[0]),
            (buf_ref[h, TQ:2 * TQ, :], self.values(h, 2 * i + 1), bias, top[1])]])
        self.m_ref[h] = m
        self.acc_ref[h] = acc

    def first(self):
        return tuple(self.fetch(self.sa_ref, h, 0) for h in self.heads)

    def run(self, tops_first):
        def two_pairs(q, tops_a):
            ia = 2 * q
            ib = jnp.minimum(ia + 1, self.last)
            ic = jnp.minimum(ia + 2, self.last)
            off_b = jnp.where(ia + 1 < self.n_pairs, 0.0, NEG)
            tops_b, tops_next = [], []
            for h in self.heads:
                tops_b.append(self.fetch(self.sb_ref, h, ib))
                self.consume(self.sa_ref, h, tops_a[h], ia, None)
            for h in self.heads:
                tops_next.append(self.fetch(self.sa_ref, h, ic))
                self.consume(self.sb_ref, h, tops_b[h], ib, off_b)
            return tuple(tops_next)

        lax.fori_loop(0, (self.n_pairs + 1) >> 1, two_pairs, tops_first)


def _moba_kernel(q_ref, k_ref, vt_ref, tab_ref, cb_ref, o_ref, sa_ref, sb_ref, m_ref, acc_ref):
    qi = pl.program_id(2)
    jp = jnp.maximum(qi - 1, 0)
    off_p = jnp.where(qi >= 1, 0.0, NEG)
    n_pairs, jl, off_l = _far_tiles(qi)
    heads = range(MOBA_HEADS)
    qaug = [q_ref[0, hh] for hh in heads]
    cbs = [cb_ref[hh][0:1, 0:1] for hh in heads]

    def scores(hh, j):
        return _dot(k_ref[0, hh, _tile_rows(j), :], qaug[hh])

    def values(hh, j):
        return vt_ref[0, j, hh * HEAD_DIM:(hh + 1) * HEAD_DIM, :]

    far = _FarLoop(n_pairs, heads, scores, values, cbs, sa_ref, sb_ref, m_ref, acc_ref)

    s_own = [scores(hh, qi) for hh in heads]
    s_prev = [scores(hh, jp) for hh in heads]
    s_left = [scores(hh, jl) for hh in heads]
    tops_first = far.first()
    chains = [[(s_own[hh] + tab_ref[hh, TQ:2 * TQ, :], values(hh, qi), None, None),
               (s_prev[hh] + tab_ref[hh, 0:TQ, :] + off_p, values(hh, jp), None, None),
               (s_left[hh], values(hh, jl), cbs[hh] + off_l, None)] for hh in heads]
    for hh, (m, acc) in zip(heads, _flash_update([None] * len(chains), chains)):
        m_ref[hh] = m
        acc_ref[hh] = acc
    far.run(tops_first)
    out_t = jnp.concatenate([_flash_out(acc_ref[hh]) for hh in heads], axis=0)
    o_ref[0] = out_t.T.astype(BF16)


def _flash_scratch(n_heads):
    return [pltpu.VMEM((n_heads, 2 * TQ, TQ), F32),
            pltpu.VMEM((n_heads, 2 * TQ, TQ), F32),
            pltpu.VMEM((n_heads, 1, TQ), F32),
            pltpu.VMEM((n_heads, ACC_ROWS, TQ), F32)]


def _moba(qaug, kaug, vat, tab, cb):
    bsz, _, _, s = qaug.shape
    nq = s // TQ
    nh = MOBA_HEADS
    return pl.pallas_call(
        _moba_kernel,
        grid=(bsz, H_A // nh, nq),
        in_specs=[
            pl.BlockSpec((1, nh, LANES, TQ), lambda b, hp, qi: (b, hp, 0, qi)),
            pl.BlockSpec((1, nh, s, LANES), lambda b, hp, qi: (b, hp, 0, 0)),
            pl.BlockSpec((1, nq, nh * HEAD_DIM, TQ), lambda b, hp, qi: (b, 0, hp, 0)),
            pl.BlockSpec((nh, 2 * TQ, TQ), lambda b, hp, qi: (hp, 0, 0)),
            pl.BlockSpec((nh, 8, LANES), lambda b, hp, qi: (hp, 0, 0)),
        ],
        out_specs=pl.BlockSpec((1, TQ, nh * HEAD_DIM), lambda b, hp, qi: (b, qi, hp)),
        out_shape=jax.ShapeDtypeStruct((bsz, s, H_A * HEAD_DIM), BF16),
        scratch_shapes=_flash_scratch(nh),
        compiler_params=_params(("parallel", "parallel", "arbitrary")),
        name="moba",
    )(qaug, kaug, vat, tab, cb)


def _nsa_cmp_kernel(q_ref, kc_ref, vct_ref, ovt_ref, oc_ref, selb_ref, *, n_sel, n_cmp, n_parts):
    qi = pl.program_id(1)
    ncp = kc_ref.shape[2]
    t0 = qi * TQ
    zeros = jnp.zeros((HEAD_DIM, TQ), BF16)
    qs = [jnp.concatenate([q_ref[0, hd].astype(BF16), zeros], axis=0) for hd in range(H_B)]

    def body(nk, nb):
        n_idx = lax.broadcasted_iota(jnp.int32, (nk, TQ), 0)
        t_idx = lax.broadcasted_iota(jnp.int32, (nk, TQ), 1) + t0
        mask = (n_idx * CMP_STRIDE + (CMP_LEN - 1) <= t_idx) & (n_idx < n_cmp)
        any_key = t_idx[0:1] >= CMP_LEN - 1
        blk = lax.broadcasted_iota(jnp.int32, (nb, TQ), 0)
        cur = (lax.broadcasted_iota(jnp.int32, (nb, TQ), 1) + t0) >> _LOG2_SEL_BLK
        ok = blk <= cur
        forced = (blk == 0) | (blk == cur) | (blk == cur - 1)
        ovt = ovt_ref[0:nb, 0:nk]
        scores = []
        for g in range(G_B):
            kc = kc_ref[0, g, 0:nk, :]
            vct = _with_ones(vct_ref[0, g, :, 0:nk])
            psum = jnp.zeros((nk, TQ), F32)
            for hd in range(g * R_B, (g + 1) * R_B):
                z = jnp.where(mask, _dot(kc, qs[hd]), NEG)
                e = jnp.exp2(z - _col_max(z))
                acc = _dot(vct, e.astype(BF16))
                rinv = jnp.where(any_key, 1.0 / acc[HEAD_DIM:HEAD_DIM + 1], 0.0)
                oc_ref[0, hd] = acc[0:HEAD_DIM] * rinv
                psum = psum + e * rinv
            ph, pl_ = _split(psum)
            imp_t = _dot(ovt, ph) + _dot(ovt, pl_)
            scores.append(jnp.where(ok, jnp.where(forced, BIG, imp_t), -jnp.inf))
        for g, picked in enumerate(_topk_rows(scores, blk, n_sel)):
            selb_ref[0, g, 0:nb, :] = jnp.where(ok & (picked > 0.0), 0.0, NEG).astype(BF16)
            if nb < LANES:
                selb_ref[0, g, nb:LANES, :] = jnp.full((LANES - nb, TQ), NEG, BF16)

    part = ncp // n_parts
    need = jnp.minimum(((qi + 1) * (TQ // CMP_STRIDE) + part - 1) // part, n_parts)
    for v in range(1, n_parts + 1):
        pl.when(need == v)(functools.partial(body, v * part, min(LANES, v * part * CMP_STRIDE // SEL_BLK)))


def _cmp_parts(ncp):
    return 4 if ncp % (4 * LANES) == 0 else 1


def _nsa_cmp(qb, kcmp, vcmpt, ovt, n_cmp):
    bsz, _, _, s = qb.shape
    nq = s // TQ
    ncp = kcmp.shape[2]
    n_sel = min(SEL_TOPK, s // SEL_BLK)
    n_parts = _cmp_parts(ncp)
    return pl.pallas_call(
        functools.partial(_nsa_cmp_kernel, n_sel=n_sel, n_cmp=n_cmp, n_parts=n_parts),
        grid=(bsz, nq),
        in_specs=[
            pl.BlockSpec((1, H_B, HEAD_DIM, TQ), lambda b, qi: (b, 0, 0, qi)),
            pl.BlockSpec((1, G_B, ncp, LANES), lambda b, qi: (b, 0, 0, 0)),
            pl.BlockSpec((1, G_B, HEAD_DIM, ncp), lambda b, qi: (b, 0, 0, 0)),
            pl.BlockSpec((LANES, ncp), lambda b, qi: (0, 0)),
        ],
        out_specs=[
            pl.BlockSpec((1, H_B, HEAD_DIM, TQ), lambda b, qi: (b, 0, 0, qi)),
            pl.BlockSpec((1, G_B, LANES, TQ), lambda b, qi: (b, 0, 0, qi)),
        ],
        out_shape=[jax.ShapeDtypeStruct((bsz, H_B, HEAD_DIM, s), F32),
                   jax.ShapeDtypeStruct((bsz, G_B, LANES, s), BF16)],
        compiler_params=_params(("parallel", "parallel")),
        name="nsa_cmp",
    )(qb, kcmp, vcmpt, ovt)


def _nsa_main_kernel(q_ref, selb_ref, ks_ref, vst_ref, kw_ref, vwt_ref, oc_ref, gates_ref,
                     tabs_ref, tabw_ref, cb_ref, o_ref, sa_ref, sb_ref, m_ref, acc_ref):
    qi = pl.program_id(2)
    selb = selb_ref[0, 0]
    gates = gates_ref[0, 0]
    zeros = jnp.zeros((HEAD_DIM, TQ), BF16)
    j1 = jnp.maximum(qi - 1, 0)
    j2 = jnp.maximum(qi - 2, 0)
    off1 = jnp.where(qi >= 1, 0.0, NEG)
    off2 = jnp.where(qi >= 2, 0.0, NEG)
    n_pairs, jl, off_l = _far_tiles(qi)

    heads = range(R_B)
    qw = [jnp.concatenate([q_ref[0, r].astype(BF16), zeros], axis=0) for r in heads]
    qs = [jnp.concatenate([qw[r], selb], axis=0) for r in heads]
    cbs = [cb_ref[r][0:1, 0:1] for r in heads]

    def s_sel(r, j):
        return _dot(ks_ref[0, 0, _tile_rows(j), :], qs[r])

    def s_win(r, j):
        return _dot(kw_ref[0, 0, _tile_rows(j), :], qw[r])

    far = _FarLoop(n_pairs, heads, s_sel, lambda r, j: vst_ref[0, j], cbs, sa_ref, sb_ref, m_ref, acc_ref)

    sw = [[s_win(r, j) for r in heads] for j in (qi, j1, j2)]
    ss = [[s_sel(r, j) for r in heads] for j in (qi, j1, jl)]
    tops_first = far.first()

    chains = [[(sw[0][r] + tabw_ref[r, 2 * TQ:3 * TQ, :], vwt_ref[0, qi], None, None),
               (sw[1][r] + tabw_ref[r, TQ:2 * TQ, :] + off1, vwt_ref[0, j1], None, None),
               (sw[2][r] + tabw_ref[r, 0:TQ, :] + off2, vwt_ref[0, j2], None, None)] for r in heads]
    chains += [[(ss[0][r] + tabs_ref[r, TQ:2 * TQ, :], vst_ref[0, qi], None, None),
                (ss[1][r] + tabs_ref[r, 0:TQ, :] + off1, vst_ref[0, j1], None, None),
                (ss[2][r], vst_ref[0, jl], cbs[r] + off_l, None)] for r in heads]
    done = _flash_update([None] * len(chains), chains)
    win = [_flash_out(acc) for _, acc in done[:R_B]]
    for r, (m, acc) in zip(heads, done[R_B:]):
        m_ref[r] = m
        acc_ref[r] = acc
    far.run(tops_first)

    outs = [gates[3 * r:3 * r + 1] * oc_ref[0, r]
            + gates[3 * r + 1:3 * r + 2] * _flash_out(acc_ref[r])
            + gates[3 * r + 2:3 * r + 3] * win[r] for r in heads]
    o_ref[0] = jnp.concatenate(outs, axis=0).T.astype(BF16)


def _nsa_main(qb, selb, ksaug, vst, kw, vwt, oc, gates, tabs, tabw, cb):
    bsz, _, _, s = qb.shape
    nq = s // TQ
    return pl.pallas_call(
        _nsa_main_kernel,
        grid=(bsz, G_B, nq),
        in_specs=[
            pl.BlockSpec((1, R_B, HEAD_DIM, TQ), lambda b, g, qi: (b, g, 0, qi)),
            pl.BlockSpec((1, 1, LANES, TQ), lambda b, g, qi: (b, g, 0, qi)),
            pl.BlockSpec((1, 1, s, 2 * LANES), lambda b, g, qi: (b, g, 0, 0)),
            pl.BlockSpec((1, nq, HEAD_DIM, TQ), lambda b, g, qi: (b, 0, g, 0)),
            pl.BlockSpec((1, 1, s, LANES), lambda b, g, qi: (b, g, 0, 0)),
            pl.BlockSpec((1, nq, HEAD_DIM, TQ), lambda b, g, qi: (b, 0, g, 0)),
            pl.BlockSpec((1, R_B, HEAD_DIM, TQ), lambda b, g, qi: (b, g, 0, qi)),
            pl.BlockSpec((1, 1, GATE_ROWS, TQ), lambda b, g, qi: (b, g, 0, qi)),
            pl.BlockSpec((R_B, 2 * TQ, TQ), lambda b, g, qi: (g, 0, 0)),
            pl.BlockSpec((R_B, 3 * TQ, TQ), lambda b, g, qi: (g, 0, 0)),
            pl.BlockSpec((R_B, 8, LANES), lambda b, g, qi: (g, 0, 0)),
        ],
        out_specs=pl.BlockSpec((1, TQ, R_B * HEAD_DIM), lambda b, g, qi: (b, qi, g)),
        out_shape=jax.ShapeDtypeStruct((bsz, s, H_B * HEAD_DIM), BF16),
        scratch_shapes=_flash_scratch(R_B),
        compiler_params=_params(("parallel", "parallel", "arbitrary")),
        name="nsa_main",
    )(qb, selb, ksaug, vst, kw, vwt, oc, gates, tabs, tabw, cb)


def _out_ffn_kernel(x_ref, oa_ref, ob_ref, wo_ref, gtm_ref, gffn_ref, scf_ref, shf_ref, gtf_ref,
                    wg_ref, wu_ref, wd_ref, o_ref, x1_ref, h_ref, acc_ref):
    f = pl.program_id(2)
    half = oa_ref.shape[2]

    @pl.when(f == 0)
    def _():
        mix = _dot(oa_ref[0], wo_ref[0:half, :]) + _dot(ob_ref[0], wo_ref[half:2 * half, :])
        x1 = x_ref[0] + gtm_ref[0, 0] * mix
        x1_ref[...] = x1
        h = _rms_rows(x1, gffn_ref[...]) * (1.0 + scf_ref[0, 0]) + shf_ref[0, 0]
        h_ref[...] = h.astype(BF16)

    h = h_ref[...]
    gate = _dot(h, wg_ref[...])
    up = _dot(h, wu_ref[...])
    act = (gate * jax.nn.sigmoid(gate) * up).astype(BF16)
    part = _dot(act, wd_ref[...])

    @pl.when(f == 0)
    def _():
        acc_ref[...] = part

    @pl.when(f > 0)
    def _():
        acc_ref[...] += part

    @pl.when(f == pl.num_programs(2) - 1)
    def _():
        o_ref[0] = x1_ref[...] + gtf_ref[0, 0] * acc_ref[...]


def _out_ffn(x, oa, ob, wo, mod4, gffn, wgu, wd):
    bsz, s, d = x.shape
    fh = wd.shape[0]
    tm = TM_FFN
    tf = fh // 2 if (fh // 2) % LANES == 0 else fh
    nf = fh // tf
    tok = lambda b, t, f: (b, t, 0)
    modspec = lambda k: pl.BlockSpec((1, 1, 1, d), lambda b, t, f: (b, k, 0, 0))
    return pl.pallas_call(
        _out_ffn_kernel,
        grid=(bsz, s // tm, nf),
        in_specs=[
            pl.BlockSpec((1, tm, d), tok),
            pl.BlockSpec((1, tm, oa.shape[2]), tok),
            pl.BlockSpec((1, tm, ob.shape[2]), tok),
            pl.BlockSpec((d, d), lambda b, t, f: (0, 0)),
            modspec(2),
            pl.BlockSpec((1, d), lambda b, t, f: (0, 0)),
            modspec(4),
            modspec(3),
            modspec(5),
            pl.BlockSpec((d, tf), lambda b, t, f: (0, f)),
            pl.BlockSpec((d, tf), lambda b, t, f: (0, f + nf)),
            pl.BlockSpec((tf, d), lambda b, t, f: (f, 0)),
        ],
        out_specs=pl.BlockSpec((1, tm, d), tok),
        out_shape=jax.ShapeDtypeStruct((bsz, s, d), F32),
        scratch_shapes=[pltpu.VMEM((tm, d), F32), pltpu.VMEM((tm, d), BF16), pltpu.VMEM((tm, d), F32)],
        compiler_params=_params(("parallel", "parallel", "arbitrary")),
        name="out_ffn",
    )(x, oa, ob, wo, mod4, gffn, mod4, mod4, mod4, wgu, wgu, wd)


def _t5_bucket_np(d):
    max_exact = N_BUCKETS // 2
    d = np.maximum(d, 0)
    df = np.maximum(d, 1).astype(np.float64)
    large = max_exact + (np.log(df / max_exact) / math.log(MAX_DIST / max_exact)
                         * (N_BUCKETS - max_exact)).astype(np.int64)
    large = np.minimum(large, N_BUCKETS - 1)
    return np.where(d < max_exact, d, large).astype(np.int32)


def _bias_expand_kernel(tab_ref, bucket_ref, o_ref):
    hd = pl.program_id(0)
    bucket = bucket_ref[...]
    acc = jnp.full(bucket.shape, NEG, F32)
    for b in range(N_BUCKETS):
        acc = jnp.where(bucket == b, tab_ref[hd, b], acc)
    o_ref[0] = acc


def _bias_expand(tab, bucket):
    nh = tab.shape[0]
    return pl.pallas_call(
        _bias_expand_kernel,
        grid=(nh,),
        in_specs=[pl.BlockSpec(memory_space=pltpu.SMEM),
                  pl.BlockSpec(bucket.shape, lambda h: (0, 0))],
        out_specs=pl.BlockSpec((1,) + bucket.shape, lambda h: (h, 0, 0)),
        out_shape=jax.ShapeDtypeStruct((nh,) + bucket.shape, F32),
        compiler_params=_params(("parallel",)),
        name="bias_expand",
    )(tab, jnp.asarray(bucket, jnp.int32))


def _bias_tables(rel_bias):
    tab = rel_bias.T.astype(F32) * LOG2_E
    i = np.arange(TQ)[None, :]
    d_near = i + TQ - np.arange(2 * TQ)[:, None]
    near = _bias_expand(tab, np.where(d_near >= 0, _t5_bucket_np(d_near), -1))
    d_win = i + 2 * TQ - np.arange(3 * TQ)[:, None]
    ok_win = (d_win >= 0) & (d_win < WINDOW)
    win = _bias_expand(tab[H_A:], np.where(ok_win, _t5_bucket_np(d_win), -1))
    far = jnp.broadcast_to(tab[:, N_BUCKETS - 1][:, None, None], (tab.shape[0], 8, LANES))
    return near, win, far


def _overlap_t(n_cmp_pad, n_cmp):
    cs = np.arange(n_cmp_pad)[None, :] * CMP_STRIDE
    ss = np.arange(LANES)[:, None] * SEL_BLK
    ov = (cs < ss + SEL_BLK) & (cs + CMP_LEN > ss) & (np.arange(n_cmp_pad)[None, :] < n_cmp)
    return jnp.asarray(ov.astype(np.float32), BF16)


def _block_diag(n):
    m = (np.arange(n)[:, None] // HEAD_DIM == np.arange(n)[None, :] // HEAD_DIM)
    return jnp.asarray(m.astype(np.float32) / HEAD_DIM, BF16)


def kernel(x, c, rel_bias, w_ada, b_ada, g_mix, w_in, q_norm_a, k_norm_a, q_norm_b, k_norm_cmp,
           k_norm_sel, k_norm_win, cmp_pe_k, cmp_w1_k, cmp_w2_k, cmp_pe_v, cmp_w1_v, cmp_w2_v,
           w_out, g_ffn, w_gu, w_down):
    bsz, s, d = x.shape
    depth = w_ada.shape[0]
    assert s % TM_IN == 0 and s % TM_FFN == 0 and s % (2 * TQ) == 0
    assert s // BLK_A <= HEAD_DIM and s // SEL_BLK <= LANES
    assert WINDOW == 2 * TQ and BLK_A == TQ and MAX_DIST <= TQ
    n_chunks = s // CMP_STRIDE
    n_cmp = (s - CMP_LEN) // CMP_STRIDE + 1
    scale = HEAD_DIM ** -0.5 * LOG2_E
    hd = HEAD_DIM

    near, win, far = _bias_tables(rel_bias)
    ovt = _overlap_t(n_chunks, n_cmp)
    bd = _block_diag(512)
    tile = lambda g, n: jnp.tile(g.astype(F32), n).reshape(1, -1)
    tile_t = lambda g, n: jnp.broadcast_to(jnp.tile(g.astype(F32), n)[:, None], (n * hd, TM_IN))

    for l in range(depth):
        mod = _ada(c, w_ada[l], b_ada[l])
        mod4 = mod.reshape(bsz, ADA_CHUNKS, 1, d)

        wl = w_in[l]
        cols = np.cumsum([0, H_A * hd, H_A * hd, H_A * hd, H_B * hd] + [G_B * hd] * 6)
        qa_c, ka_c, va_c, qb_c, kc_c, vc_c, ks_c, vs_c, kw_c, vw_c = [
            wl[:, int(a):int(b)] for a, b in zip(cols[:-1], cols[1:])]
        gl = wl[:, int(cols[-1]):].reshape(d, G_B, 3 * R_B)
        gl = jnp.pad(gl, ((0, 0), (0, 0), (0, GATE_ROWS - 3 * R_B))).reshape(d, G_B * GATE_ROWS)
        w_rows = jnp.concatenate([ka_c, kc_c, vc_c, ks_c, kw_c], axis=1).astype(BF16)
        w_t = jnp.concatenate([qa_c, va_c, qb_c, vs_c, vw_c, gl], axis=1).T.astype(BF16)

        (qa, kaug, vat, kmean, qb, kc, vc, ksaug, vst, kw, vwt, gates) = _inproj(
            x, mod4, mod4, g_mix[l].reshape(1, d), w_rows, w_t, bd,
            tile_t(q_norm_a[l], H_A) * scale, tile(k_norm_a[l], H_A), tile_t(q_norm_b[l], H_B) * scale,
            tile(k_norm_sel[l], G_B), tile(k_norm_win[l], G_B))

        nba = s // BLK_A
        km = kmean.reshape(bsz, nba, H_A, hd).transpose(0, 2, 1, 3)
        km = jnp.pad(km, ((0, 0), (0, 0), (0, hd - nba), (0, LANES - hd)))
        o_a = _moba(_moba_gate(qa, km), kaug, vat, near[:H_A], far[:H_A])

        chunks = lambda t: t.reshape(bsz, s, G_B, hd).transpose(0, 2, 1, 3).reshape(
            bsz, G_B, n_chunks, CMP_STRIDE * hd)
        w1 = jnp.stack([cmp_w1_k[l], cmp_w1_v[l]]).astype(BF16)
        w2t = jnp.stack([cmp_w2_k[l].T, cmp_w2_v[l].T]).astype(BF16)
        pe = jnp.stack([cmp_pe_k[l], cmp_pe_v[l]]).reshape(2, 1, CMP_LEN * hd)
        pe = jnp.broadcast_to(pe, (2, 8, CMP_LEN * hd)).astype(BF16)
        kcmp, vcmpt = _compress(chunks(kc), chunks(vc), w1, w2t, pe,
                                k_norm_cmp[l].astype(F32).reshape(1, hd))

        oc, selb = _nsa_cmp(qb, kcmp, vcmpt, ovt, n_cmp)
        o_b = _nsa_main(qb, selb, ksaug, vst, kw, vwt, oc, gates, near[H_A:], win, far[H_A:])

        x = _out_ffn(x, o_a, o_b, w_out[l].astype(BF16), mod4, g_ffn[l].reshape(1, d),
                     w_gu[l].astype(BF16), w_down[l].astype(BF16))
    return x
```

```python
import functools
import math

import jax
import jax.numpy as jnp
import numpy as np
from jax import lax
from jax.experimental import pallas as pl
from jax.experimental.pallas import tpu as pltpu

F32 = jnp.float32
BF16 = jnp.bfloat16

HEAD_DIM = 64
LANES = 128
BF16_ROWS = 16
H_A = 8
H_B = 8
G_B = 2
R_B = H_B // G_B
BLK_A = 256
TOPK_A = 3
CMP_LEN = 32
CMP_STRIDE = 16
CMP_HIDDEN = 256
SEL_BLK = 64
SEL_TOPK = 16
WINDOW = 512
N_BUCKETS = 32
MAX_DIST = 128
ADA_CHUNKS = 6
NEG = -1e30
BIG = 1e9
EPS = 1e-6
LOG2_E = math.log2(math.e)

_LOG2_BLK_A = BLK_A.bit_length() - 1
_LOG2_SEL_BLK = SEL_BLK.bit_length() - 1

TQ = 256
TM_IN = 512
TM_FFN = 512
MOBA_HEADS = 4
GATE_ROWS = 16
ACC_ROWS = HEAD_DIM + BF16_ROWS
VMEM_LIMIT = 56 * 1024 * 1024


def _dot(a, b):
    return jnp.dot(a, b, preferred_element_type=F32)


def _dot_nt(a, b):
    return lax.dot_general(a, b, (((1,), (1,)), ((), ())), preferred_element_type=F32)


def _split(a):
    hi = a.astype(BF16)
    lo = (a - hi.astype(F32)).astype(BF16)
    return hi, lo


def _dot3(a, b):
    ah, al = _split(a)
    bh, bl = _split(b)
    return _dot(ah, bh) + (_dot(al, bh) + _dot(ah, bl))


def _params(sem):
    return pltpu.CompilerParams(dimension_semantics=sem, vmem_limit_bytes=VMEM_LIMIT)


def _ada_kernel(c_ref, w_ref, b_ref, o_ref):
    c = c_ref[...]
    o_ref[...] = _dot3(c * jax.nn.sigmoid(c), w_ref[...]) + b_ref[...]


def _ada(c, w, b):
    bsz, d = c.shape
    n = w.shape[1]
    tn = 512
    return pl.pallas_call(
        _ada_kernel,
        grid=(n // tn,),
        in_specs=[pl.BlockSpec((bsz, d), lambda j: (0, 0)),
                  pl.BlockSpec((d, tn), lambda j: (0, j)),
                  pl.BlockSpec((1, tn), lambda j: (0, j))],
        out_specs=pl.BlockSpec((bsz, tn), lambda j: (0, j)),
        out_shape=jax.ShapeDtypeStruct((bsz, n), F32),
        compiler_params=_params(("arbitrary",)),
        name="ada",
    )(c, w, b.reshape(1, n))


def _rms_rows(xf, g):
    ms = jnp.mean(xf * xf, axis=-1, keepdims=True)
    return xf * lax.rsqrt(ms + EPS) * g


def _head_norm(t, bd, gain):
    hi, lo = _split(t * t)
    ms = _dot(hi, bd) + _dot(lo, bd)
    return t * lax.rsqrt(ms + EPS) * gain


def _head_norm_t(t, gain):
    heads = []
    for hd in range(t.shape[0] // HEAD_DIM):
        th = t[hd * HEAD_DIM:(hd + 1) * HEAD_DIM]
        ms = jnp.mean(th * th, axis=0, keepdims=True)
        heads.append(th * lax.rsqrt(ms + EPS) * gain[hd * HEAD_DIM:(hd + 1) * HEAD_DIM])
    return heads


def _inproj_kernel(x_ref, sc_ref, sh_ref, gmix_ref, wr_ref, wt_ref, bd_ref, gqa_ref, gka_ref, gqb_ref,
                   gks_ref, gkw_ref,
                   qa_ref, kaug_ref, va_ref, kmean_ref, qb_ref, kc_ref, vc_ref, ksaug_ref,
                   vs_ref, kw_ref, vw_ref, gates_ref):
    tm = x_ref.shape[1]
    ti = pl.program_id(1)
    xf = x_ref[0]
    h = _rms_rows(xf, gmix_ref[...]) * (1.0 + sc_ref[0, 0]) + sh_ref[0, 0]
    hb = h.astype(BF16)

    def proj(c0, c1):
        return _dot(hb, wr_ref[:, c0:c1])

    def proj_t(r0, r1):
        return _dot_nt(wt_ref[r0:r1, :], hb)

    bd = bd_ref[...]
    bd2 = bd_ref[0:LANES, 0:LANES]
    lane = lax.broadcasted_iota(jnp.int32, (tm, LANES), 1)
    row = lax.broadcasted_iota(jnp.int32, (tm, LANES), 0) + ti * tm
    low = lane < HEAD_DIM

    def k_in_low(pair, odd):
        return pltpu.roll(pair, HEAD_DIM, 1) if odd else pair

    for hd, qh in enumerate(_head_norm_t(proj_t(0, 512), gqa_ref[...])):
        qa_ref[0, hd] = qh

    def put_tiles(ref, vt):
        for i in range(tm // TQ):
            ref[0, i] = vt[:, i * TQ:(i + 1) * TQ].astype(BF16)

    put_tiles(va_ref, proj_t(512, 1024))
    for hd, qh in enumerate(_head_norm_t(proj_t(1024, 1536), gqb_ref[...])):
        qb_ref[0, hd] = qh
    put_tiles(vs_ref, proj_t(1536, 1664))
    put_tiles(vw_ref, proj_t(1664, 1792))
    gl = jax.nn.sigmoid(proj_t(1792, 1792 + G_B * GATE_ROWS))
    for g in range(G_B):
        gates_ref[0, g] = gl[g * GATE_ROWS:(g + 1) * GATE_ROWS]

    ka = _head_norm(proj(0, 512), bd, gka_ref[...])
    oh_a = jnp.where(lane - HEAD_DIM == (row >> _LOG2_BLK_A), 1.0, 0.0)
    for hd in range(H_A):
        pair = ka[:, (hd // 2) * LANES:(hd // 2 + 1) * LANES]
        kaug_ref[0, hd] = jnp.where(low, k_in_low(pair, hd % 2), oh_a).astype(BF16)
    for i in range(tm // BLK_A):
        kmean_ref[0, i] = jnp.mean(ka[i * BLK_A:(i + 1) * BLK_A], axis=0, keepdims=True)

    kc_ref[0] = proj(512, 640).astype(BF16)
    vc_ref[0] = proj(640, 768).astype(BF16)

    ks = _head_norm(proj(768, 896), bd2, gks_ref[...])
    kw = _head_norm(proj(896, 1024), bd2, gkw_ref[...])
    oh_s = jnp.where(lane == (row >> _LOG2_SEL_BLK), 1.0, 0.0).astype(BF16)
    for g in range(G_B):
        ksaug_ref[0, g] = jnp.concatenate(
            [jnp.where(low, k_in_low(ks, g), 0.0).astype(BF16), oh_s], axis=1)
        kw_ref[0, g] = jnp.where(low, k_in_low(kw, g), 0.0).astype(BF16)


def _inproj(x, sc, sh, gmix, wr, wt, bd, gqa, gka, gqb, gks, gkw):
    bsz, s, d = x.shape
    tm = TM_IN
    nt = s // tm
    nba = s // BLK_A
    const2 = lambda b, t: (0, 0)
    tok3 = lambda b, t: (b, t, 0)
    tok4 = lambda b, t: (b, 0, t, 0)
    tile4 = lambda b, t: (b, t, 0, 0)
    tr4 = lambda b, t: (b, 0, 0, t)
    in_specs = [
        pl.BlockSpec((1, tm, d), tok3),
        pl.BlockSpec((1, 1, 1, d), lambda b, t: (b, 1, 0, 0)),
        pl.BlockSpec((1, 1, 1, d), lambda b, t: (b, 0, 0, 0)),
        pl.BlockSpec((1, d), const2),
        pl.BlockSpec(wr.shape, const2),
        pl.BlockSpec(wt.shape, const2),
        pl.BlockSpec((512, 512), const2),
        pl.BlockSpec((512, tm), const2),
        pl.BlockSpec((1, 512), const2),
        pl.BlockSpec((512, tm), const2),
        pl.BlockSpec((1, LANES), const2),
        pl.BlockSpec((1, LANES), const2),
    ]
    out_shape = [
        jax.ShapeDtypeStruct((bsz, H_A, HEAD_DIM, s), F32),
        jax.ShapeDtypeStruct((bsz, H_A, s, LANES), BF16),
        jax.ShapeDtypeStruct((bsz, s // TQ, 512, TQ), BF16),
        jax.ShapeDtypeStruct((bsz, nba, 1, 512), F32),
        jax.ShapeDtypeStruct((bsz, H_B, HEAD_DIM, s), F32),
        jax.ShapeDtypeStruct((bsz, s, LANES), BF16),
        jax.ShapeDtypeStruct((bsz, s, LANES), BF16),
        jax.ShapeDtypeStruct((bsz, G_B, s, 2 * LANES), BF16),
        jax.ShapeDtypeStruct((bsz, s // TQ, LANES, TQ), BF16),
        jax.ShapeDtypeStruct((bsz, G_B, s, LANES), BF16),
        jax.ShapeDtypeStruct((bsz, s // TQ, LANES, TQ), BF16),
        jax.ShapeDtypeStruct((bsz, G_B, GATE_ROWS, s), F32),
    ]
    out_specs = [
        pl.BlockSpec((1, H_A, HEAD_DIM, tm), tr4),
        pl.BlockSpec((1, H_A, tm, LANES), tok4),
        pl.BlockSpec((1, tm // TQ, 512, TQ), tile4),
        pl.BlockSpec((1, tm // BLK_A, 1, 512), lambda b, t: (b, t, 0, 0)),
        pl.BlockSpec((1, H_B, HEAD_DIM, tm), tr4),
        pl.BlockSpec((1, tm, LANES), tok3),
        pl.BlockSpec((1, tm, LANES), tok3),
        pl.BlockSpec((1, G_B, tm, 2 * LANES), tok4),
        pl.BlockSpec((1, tm // TQ, LANES, TQ), tile4),
        pl.BlockSpec((1, G_B, tm, LANES), tok4),
        pl.BlockSpec((1, tm // TQ, LANES, TQ), tile4),
        pl.BlockSpec((1, G_B, GATE_ROWS, tm), tr4),
    ]
    return pl.pallas_call(
        _inproj_kernel,
        grid=(bsz, nt),
        in_specs=in_specs,
        out_specs=out_specs,
        out_shape=out_shape,
        compiler_params=_params(("parallel", "parallel")),
        name="inproj",
    )(x, sc, sh, gmix, wr, wt, bd, gqa, gka, gqb, gks, gkw)


def _compress_kernel(ck_ref, cv_ref, w1_ref, w2t_ref, pe_ref, gk_ref, ok_ref, ov_ref):
    half = CMP_STRIDE * HEAD_DIM
    for kv, c_ref in enumerate((ck_ref, cv_ref)):
        for g in range(G_B):
            c = c_ref[0, g]
            a = _dot(c, w1_ref[kv, 0:half, :])
            b = _dot(c, w1_ref[kv, half:2 * half, :])
            n = a.shape[0]
            b_next = pltpu.roll(b, n - 1, 0)
            pe_term = _dot(pe_ref[kv], w1_ref[kv])[0:1]
            hid = jax.nn.gelu(a + b_next + pe_term).astype(BF16)
            if kv == 0:
                y = _dot_nt(hid, w2t_ref[kv])
                ms = jnp.mean(y * y, axis=1, keepdims=True)
                y = y * lax.rsqrt(ms + EPS) * gk_ref[...]
                ok_ref[0, g] = jnp.concatenate([y, jnp.zeros_like(y)], axis=1).astype(BF16)
            else:
                ov_ref[0, g] = _dot_nt(w2t_ref[kv], hid).astype(BF16)


def _compress(ck, cv, w1, w2t, pe, gk):
    bsz, g, n, width = ck.shape
    blk = pl.BlockSpec((1, g, n, width), lambda b: (b, 0, 0, 0))
    full = lambda a: pl.BlockSpec(a.shape, lambda b: (0,) * a.ndim)
    return pl.pallas_call(
        _compress_kernel,
        grid=(bsz,),
        in_specs=[blk, blk, full(w1), full(w2t), full(pe), full(gk)],
        out_specs=[pl.BlockSpec((1, g, n, LANES), lambda b: (b, 0, 0, 0)),
                   pl.BlockSpec((1, g, HEAD_DIM, n), lambda b: (b, 0, 0, 0))],
        out_shape=[jax.ShapeDtypeStruct((bsz, g, n, LANES), BF16),
                   jax.ShapeDtypeStruct((bsz, g, HEAD_DIM, n), BF16)],
        compiler_params=_params(("parallel",)),
        name="compress",
    )(ck, cv, w1, w2t, pe, gk)


def _with_ones(vt):
    return jnp.concatenate([vt, jnp.ones((BF16_ROWS, vt.shape[1]), BF16)], axis=0)


def _col_max(s):
    while s.shape[0] > 8:
        half = s.shape[0] // 2
        s = jnp.maximum(s[0:half], s[half:2 * half])
    return jnp.max(s, axis=0, keepdims=True)


def _flash_update(carries, chains):
    m_news = []
    for carry, tiles in zip(carries, chains):
        tops = []
        for s, _, bias, top in tiles:
            top = _col_max(s) if top is None else top
            tops.append(top if bias is None else top + bias)
        m_news.append(functools.reduce(jnp.maximum, tops if carry is None else tops + [carry[0]]))
    pvs = [None] * len(chains)
    for t in range(max(len(tiles) for tiles in chains)):
        for c, tiles in enumerate(chains):
            if t < len(tiles):
                s, vt, bias, _ = tiles[t]
                p = jnp.exp2(s - (m_news[c] if bias is None else m_news[c] - bias)).astype(BF16)
                part = _dot(_with_ones(vt), p)
                pvs[c] = part if pvs[c] is None else pvs[c] + part
    outs = []
    for carry, m_new, pv in zip(carries, m_news, pvs):
        outs.append((m_new, pv if carry is None else jnp.exp2(carry[0] - m_new) * carry[1] + pv))
    return outs


def _flash_out(acc):
    return acc[0:HEAD_DIM] / acc[HEAD_DIM:HEAD_DIM + 1]


def _topk_rows(scores, index, k):
    scores = list(scores)
    picked = [jnp.zeros(sc.shape, F32) for sc in scores]
    for _ in range(k):
        mx = [jnp.max(sc, axis=0, keepdims=True) for sc in scores]
        cand = [jnp.where(sc == m, index, jnp.int32(1 << 20)) for sc, m in zip(scores, mx)]
        first = [jnp.min(c, axis=0, keepdims=True) for c in cand]
        hit = [index == f for f in first]
        picked = [jnp.where(h, 1.0, p) for h, p in zip(hit, picked)]
        scores = [jnp.where(h, -jnp.inf, sc) for h, sc in zip(hit, scores)]
    return picked


def _tile_rows(j):
    return pl.ds(pl.multiple_of(j * TQ, TQ), TQ)


def _moba_gate_kernel(q_ref, km_ref, o_ref, *, n_sel):
    nh, tg = q_ref.shape[1], q_ref.shape[3]
    t0 = pl.program_id(2) * tg
    shape = (km_ref.shape[2], tg)
    blk = lax.broadcasted_iota(jnp.int32, shape, 0)
    own = (lax.broadcasted_iota(jnp.int32, shape, 1) + t0) >> _LOG2_BLK_A
    valid = blk < own
    qs, gates = [], []
    for hh in range(nh):
        q = q_ref[0, hh]
        gate = _dot3(km_ref[0, hh], jnp.concatenate([q, jnp.zeros_like(q)], axis=0))
        qs.append(q)
        gates.append(jnp.where(valid, gate, -jnp.inf))
    for hh, picked in enumerate(_topk_rows(gates, blk, n_sel)):
        keep = jnp.where(valid, picked, 0.0) + jnp.where(blk == own, 1.0, 0.0)
        selb = jnp.where(keep > 0.0, 0.0, NEG)
        o_ref[0, hh] = jnp.concatenate([qs[hh], selb], axis=0).astype(BF16)


def _moba_gate(qa, km):
    bsz, nh, _, s = qa.shape
    tg = min(s, 2048)
    hpb = 2
    n_sel = max(1, min(TOPK_A, s // BLK_A - 1))
    return pl.pallas_call(
        functools.partial(_moba_gate_kernel, n_sel=n_sel),
        grid=(bsz, nh // hpb, s // tg),
        in_specs=[pl.BlockSpec((1, hpb, HEAD_DIM, tg), lambda b, h, t: (b, h, 0, t)),
                  pl.BlockSpec((1, hpb, HEAD_DIM, LANES), lambda b, h, t: (b, h, 0, 0))],
        out_specs=pl.BlockSpec((1, hpb, LANES, tg), lambda b, h, t: (b, h, 0, t)),
        out_shape=jax.ShapeDtypeStruct((bsz, nh, LANES, s), BF16),
        compiler_params=_params(("parallel", "parallel", "parallel")),
        name="moba_gate",
    )(qa, km)


def _far_tiles(qi):
    n_far = jnp.maximum(qi - 1, 0)
    left = jnp.maximum(n_far - 1, 0)
    off_left = jnp.where((n_far & 1) == 1, 0.0, NEG)
    return n_far >> 1, left, off_left


class _FarLoop:
    def __init__(self, n_pairs, heads, qk_tile, values, cbs, sa_ref, sb_ref, m_ref, acc_ref):
        self.n_pairs, self.heads, self.qk_tile, self.values, self.cbs = n_pairs, heads, qk_tile, values, cbs
        self.sa_ref, self.sb_ref, self.m_ref, self.acc_ref = sa_ref, sb_ref, m_ref, acc_ref
        self.last = jnp.maximum(n_pairs - 1, 0)

    def fetch(self, buf_ref, h, i):
        s_lo = self.qk_tile(h, 2 * i)
        s_hi = self.qk_tile(h, 2 * i + 1)
        buf_ref[h, 0:TQ, :] = s_lo
        buf_ref[h, TQ:2 * TQ, :] = s_hi
        return _col_max(s_lo), _col_max(s_hi)

    def consume(self, buf_ref, h, top, i):
        (m, acc), = _flash_update([(self.m_ref[h], self.acc_ref[h])], [[
            (buf_ref[h, 0:TQ, :], self.values(h, 2 * i), self.cbs[h], top[0]),
            (buf_ref[h, TQ:2 * TQ, :], self.values(h, 2 * i + 1), self.cbs[h], top[1])]])
        self.m_ref[h] = m
        self.acc_ref[h] = acc

    def first(self):
        return tuple(self.fetch(self.sa_ref, h, 0) for h in self.heads)

    def run(self, tops_first):
        def two_pairs(q, tops_a):
            ia = 2 * q
            tops_b, tops_next = [], []
            for h in self.heads:
                tops_b.append(self.fetch(self.sb_ref, h, ia + 1))
                self.consume(self.sa_ref, h, tops_a[h], ia)
            for h in self.heads:
                tops_next.append(self.fetch(self.sa_ref, h, jnp.minimum(ia + 2, self.last)))
                self.consume(self.sb_ref, h, tops_b[h], ia + 1)
            return tuple(tops_next)

        tops_last = lax.fori_loop(0, self.n_pairs >> 1, two_pairs, tops_first)

        @pl.when((self.n_pairs & 1) == 1)
        def _():
            for h in self.heads:
                self.consume(self.sa_ref, h, tops_last[h], self.last)


def _moba_kernel(q_ref, k_ref, vt_ref, tab_ref, cb_ref, o_ref, sa_ref, sb_ref, m_ref, acc_ref):
    qi = pl.program_id(2)
    jp = jnp.maximum(qi - 1, 0)
    off_p = jnp.where(qi >= 1, 0.0, NEG)
    n_pairs, jl, off_l = _far_tiles(qi)
    heads = range(MOBA_HEADS)
    qaug = [q_ref[0, hh] for hh in heads]
    cbs = [cb_ref[hh][0:1, 0:1] for hh in heads]

    def scores(hh, j):
        return _dot(k_ref[0, hh, _tile_rows(j), :], qaug[hh])

    def values(hh, j):
        return vt_ref[0, j, hh * HEAD_DIM:(hh + 1) * HEAD_DIM, :]

    far = _FarLoop(n_pairs, heads, scores, values, cbs, sa_ref, sb_ref, m_ref, acc_ref)

    s_own = [scores(hh, qi) for hh in heads]
    s_prev = [scores(hh, jp) for hh in heads]
    s_left = [scores(hh, jl) for hh in heads]
    tops_first = far.first()
    chains = [[(s_own[hh] + tab_ref[hh, TQ:2 * TQ, :], values(hh, qi), None, None),
               (s_prev[hh] + tab_ref[hh, 0:TQ, :] + off_p, values(hh, jp), None, None),
               (s_left[hh], values(hh, jl), cbs[hh] + off_l, None)] for hh in heads]
    for hh, (m, acc) in zip(heads, _flash_update([None] * len(chains), chains)):
        m_ref[hh] = m
        acc_ref[hh] = acc
    far.run(tops_first)
    out_t = jnp.concatenate([_flash_out(acc_ref[hh]) for hh in heads], axis=0)
    o_ref[0] = out_t.T.astype(BF16)


def _flash_scratch(n_heads):
    return [pltpu.VMEM((n_heads, 2 * TQ, TQ), F32),
            pltpu.VMEM((n_heads, 2 * TQ, TQ), F32),
            pltpu.VMEM((n_heads, 1, TQ), F32),
            pltpu.VMEM((n_heads, ACC_ROWS, TQ), F32)]


def _moba(qaug, kaug, vat, tab, cb):
    bsz, _, _, s = qaug.shape
    nq = s // TQ
    nh = MOBA_HEADS
    return pl.pallas_call(
        _moba_kernel,
        grid=(bsz, H_A // nh, nq),
        in_specs=[
            pl.BlockSpec((1, nh, LANES, TQ), lambda b, hp, qi: (b, hp, 0, qi)),
            pl.BlockSpec((1, nh, s, LANES), lambda b, hp, qi: (b, hp, 0, 0)),
            pl.BlockSpec((1, nq, nh * HEAD_DIM, TQ), lambda b, hp, qi: (b, 0, hp, 0)),
            pl.BlockSpec((nh, 2 * TQ, TQ), lambda b, hp, qi: (hp, 0, 0)),
            pl.BlockSpec((nh, 8, LANES), lambda b, hp, qi: (hp, 0, 0)),
        ],
        out_specs=pl.BlockSpec((1, TQ, nh * HEAD_DIM), lambda b, hp, qi: (b, qi, hp)),
        out_shape=jax.ShapeDtypeStruct((bsz, s, H_A * HEAD_DIM), BF16),
        scratch_shapes=_flash_scratch(nh),
        compiler_params=_params(("parallel", "parallel", "arbitrary")),
        name="moba",
    )(qaug, kaug, vat, tab, cb)


def _nsa_cmp_kernel(q_ref, kc_ref, vct_ref, ovt_ref, oc_ref, selb_ref, *, n_sel, n_cmp, n_parts):
    qi = pl.program_id(1)
    ncp = kc_ref.shape[2]
    t0 = qi * TQ
    zeros = jnp.zeros((HEAD_DIM, TQ), BF16)
    qs = [jnp.concatenate([q_ref[0, hd].astype(BF16), zeros], axis=0) for hd in range(H_B)]

    def body(nk, nb):
        n_idx = lax.broadcasted_iota(jnp.int32, (nk, TQ), 0)
        t_idx = lax.broadcasted_iota(jnp.int32, (nk, TQ), 1) + t0
        mask = (n_idx * CMP_STRIDE + (CMP_LEN - 1) <= t_idx) & (n_idx < n_cmp)
        any_key = t_idx[0:1] >= CMP_LEN - 1
        blk = lax.broadcasted_iota(jnp.int32, (nb, TQ), 0)
        cur = (lax.broadcasted_iota(jnp.int32, (nb, TQ), 1) + t0) >> _LOG2_SEL_BLK
        ok = blk <= cur
        forced = (blk == 0) | (blk == cur) | (blk == cur - 1)
        ovt = ovt_ref[0:nb, 0:nk]
        scores = []
        for g in range(G_B):
            kc = kc_ref[0, g, 0:nk, :]
            vct = _with_ones(vct_ref[0, g, :, 0:nk])
            psum = jnp.zeros((nk, TQ), F32)
            for hd in range(g * R_B, (g + 1) * R_B):
                z = jnp.where(mask, _dot(kc, qs[hd]), NEG)
                e = jnp.exp2(z - _col_max(z))
                acc = _dot(vct, e.astype(BF16))
                rinv = jnp.where(any_key, 1.0 / acc[HEAD_DIM:HEAD_DIM + 1], 0.0)
                oc_ref[0, hd] = acc[0:HEAD_DIM] * rinv
                psum = psum + e * rinv
            ph, pl_ = _split(psum)
            imp_t = _dot(ovt, ph) + _dot(ovt, pl_)
            scores.append(jnp.where(ok, jnp.where(forced, BIG, imp_t), -jnp.inf))
        for g, picked in enumerate(_topk_rows(scores, blk, n_sel)):
            selb_ref[0, g, 0:nb, :] = jnp.where(ok & (picked > 0.0), 0.0, NEG).astype(BF16)
            if nb < LANES:
                selb_ref[0, g, nb:LANES, :] = jnp.full((LANES - nb, TQ), NEG, BF16)

    part = ncp // n_parts
    need = jnp.minimum(((qi + 1) * (TQ // CMP_STRIDE) + part - 1) // part, n_parts)
    for v in range(1, n_parts + 1):
        pl.when(need == v)(functools.partial(body, v * part, min(LANES, v * part * CMP_STRIDE // SEL_BLK)))


def _cmp_parts(ncp):
    return 4 if ncp % (4 * LANES) == 0 else 1


def _nsa_cmp(qb, kcmp, vcmpt, ovt, n_cmp):
    bsz, _, _, s = qb.shape
    nq = s // TQ
    ncp = kcmp.shape[2]
    n_sel = min(SEL_TOPK, s // SEL_BLK)
    n_parts = _cmp_parts(ncp)
    return pl.pallas_call(
        functools.partial(_nsa_cmp_kernel, n_sel=n_sel, n_cmp=n_cmp, n_parts=n_parts),
        grid=(bsz, nq),
        in_specs=[
            pl.BlockSpec((1, H_B, HEAD_DIM, TQ), lambda b, qi: (b, 0, 0, qi)),
            pl.BlockSpec((1, G_B, ncp, LANES), lambda b, qi: (b, 0, 0, 0)),
            pl.BlockSpec((1, G_B, HEAD_DIM, ncp), lambda b, qi: (b, 0, 0, 0)),
            pl.BlockSpec((LANES, ncp), lambda b, qi: (0, 0)),
        ],
        out_specs=[
            pl.BlockSpec((1, H_B, HEAD_DIM, TQ), lambda b, qi: (b, 0, 0, qi)),
            pl.BlockSpec((1, G_B, LANES, TQ), lambda b, qi: (b, 0, 0, qi)),
        ],
        out_shape=[jax.ShapeDtypeStruct((bsz, H_B, HEAD_DIM, s), F32),
                   jax.ShapeDtypeStruct((bsz, G_B, LANES, s), BF16)],
        compiler_params=_params(("parallel", "parallel")),
        name="nsa_cmp",
    )(qb, kcmp, vcmpt, ovt)


def _nsa_main_kernel(q_ref, selb_ref, ks_ref, vst_ref, kw_ref, vwt_ref, oc_ref, gates_ref,
                     tabs_ref, tabw_ref, cb_ref, o_ref, sa_ref, sb_ref, m_ref, acc_ref):
    qi = pl.program_id(2)
    selb = selb_ref[0, 0]
    gates = gates_ref[0, 0]
    zeros = jnp.zeros((HEAD_DIM, TQ), BF16)
    j1 = jnp.maximum(qi - 1, 0)
    j2 = jnp.maximum(qi - 2, 0)
    off1 = jnp.where(qi >= 1, 0.0, NEG)
    off2 = jnp.where(qi >= 2, 0.0, NEG)
    n_pairs, jl, off_l = _far_tiles(qi)

    heads = range(R_B)
    qw = [jnp.concatenate([q_ref[0, r].astype(BF16), zeros], axis=0) for r in heads]
    qs = [jnp.concatenate([qw[r], selb], axis=0) for r in heads]
    cbs = [cb_ref[r][0:1, 0:1] for r in heads]

    def s_sel(r, j):
        return _dot(ks_ref[0, 0, _tile_rows(j), :], qs[r])

    def s_win(r, j):
        return _dot(kw_ref[0, 0, _tile_rows(j), :], qw[r])

    far = _FarLoop(n_pairs, heads, s_sel, lambda r, j: vst_ref[0, j], cbs, sa_ref, sb_ref, m_ref, acc_ref)

    sw = [[s_win(r, j) for r in heads] for j in (qi, j1, j2)]
    ss = [[s_sel(r, j) for r in heads] for j in (qi, j1, jl)]
    tops_first = far.first()

    chains = [[(sw[0][r] + tabw_ref[r, 2 * TQ:3 * TQ, :], vwt_ref[0, qi], None, None),
               (sw[1][r] + tabw_ref[r, TQ:2 * TQ, :] + off1, vwt_ref[0, j1], None, None),
               (sw[2][r] + tabw_ref[r, 0:TQ, :] + off2, vwt_ref[0, j2], None, None)] for r in heads]
    chains += [[(ss[0][r] + tabs_ref[r, TQ:2 * TQ, :], vst_ref[0, qi], None, None),
                (ss[1][r] + tabs_ref[r, 0:TQ, :] + off1, vst_ref[0, j1], None, None),
                (ss[2][r], vst_ref[0, jl], cbs[r] + off_l, None)] for r in heads]
    done = _flash_update([None] * len(chains), chains)
    win = [_flash_out(acc) for _, acc in done[:R_B]]
    for r, (m, acc) in zip(heads, done[R_B:]):
        m_ref[r] = m
        acc_ref[r] = acc
    far.run(tops_first)

    outs = [gates[3 * r:3 * r + 1] * oc_ref[0, r]
            + gates[3 * r + 1:3 * r + 2] * _flash_out(acc_ref[r])
            + gates[3 * r + 2:3 * r + 3] * win[r] for r in heads]
    o_ref[0] = jnp.concatenate(outs, axis=0).T.astype(BF16)


def _nsa_main(qb, selb, ksaug, vst, kw, vwt, oc, gates, tabs, tabw, cb):
    bsz, _, _, s = qb.shape
    nq = s // TQ
    return pl.pallas_call(
        _nsa_main_kernel,
        grid=(bsz, G_B, nq),
        in_specs=[
            pl.BlockSpec((1, R_B, HEAD_DIM, TQ), lambda b, g, qi: (b, g, 0, qi)),
            pl.BlockSpec((1, 1, LANES, TQ), lambda b, g, qi: (b, g, 0, qi)),
            pl.BlockSpec((1, 1, s, 2 * LANES), lambda b, g, qi: (b, g, 0, 0)),
            pl.BlockSpec((1, nq, HEAD_DIM, TQ), lambda b, g, qi: (b, 0, g, 0)),
            pl.BlockSpec((1, 1, s, LANES), lambda b, g, qi: (b, g, 0, 0)),
            pl.BlockSpec((1, nq, HEAD_DIM, TQ), lambda b, g, qi: (b, 0, g, 0)),
            pl.BlockSpec((1, R_B, HEAD_DIM, TQ), lambda b, g, qi: (b, g, 0, qi)),
            pl.BlockSpec((1, 1, GATE_ROWS, TQ), lambda b, g, qi: (b, g, 0, qi)),
            pl.BlockSpec((R_B, 2 * TQ, TQ), lambda b, g, qi: (g, 0, 0)),
            pl.BlockSpec((R_B, 3 * TQ, TQ), lambda b, g, qi: (g, 0, 0)),
            pl.BlockSpec((R_B, 8, LANES), lambda b, g, qi: (g, 0, 0)),
        ],
        out_specs=pl.BlockSpec((1, TQ, R_B * HEAD_DIM), lambda b, g, qi: (b, qi, g)),
        out_shape=jax.ShapeDtypeStruct((bsz, s, H_B * HEAD_DIM), BF16),
        scratch_shapes=_flash_scratch(R_B),
        compiler_params=_params(("parallel", "parallel", "arbitrary")),
        name="nsa_main",
    )(qb, selb, ksaug, vst, kw, vwt, oc, gates, tabs, tabw, cb)


def _out_ffn_kernel(x_ref, oa_ref, ob_ref, wo_ref, gtm_ref, gffn_ref, scf_ref, shf_ref, gtf_ref,
                    wg_ref, wu_ref, wd_ref, o_ref, x1_ref, h_ref, acc_ref):
    f = pl.program_id(2)
    half = oa_ref.shape[2]

    @pl.when(f == 0)
    def _():
        mix = _dot(oa_ref[0], wo_ref[0:half, :]) + _dot(ob_ref[0], wo_ref[half:2 * half, :])
        x1 = x_ref[0] + gtm_ref[0, 0] * mix
        x1_ref[...] = x1
        h = _rms_rows(x1, gffn_ref[...]) * (1.0 + scf_ref[0, 0]) + shf_ref[0, 0]
        h_ref[...] = h.astype(BF16)

    h = h_ref[...]
    gate = _dot(h, wg_ref[...])
    up = _dot(h, wu_ref[...])
    act = (gate * jax.nn.sigmoid(gate) * up).astype(BF16)
    part = _dot(act, wd_ref[...])

    @pl.when(f == 0)
    def _():
        acc_ref[...] = part

    @pl.when(f > 0)
    def _():
        acc_ref[...] += part

    @pl.when(f == pl.num_programs(2) - 1)
    def _():
        o_ref[0] = x1_ref[...] + gtf_ref[0, 0] * acc_ref[...]


def _out_ffn(x, oa, ob, wo, mod4, gffn, wgu, wd):
    bsz, s, d = x.shape
    fh = wd.shape[0]
    tm = TM_FFN
    tf = fh // 2 if (fh // 2) % LANES == 0 else fh
    nf = fh // tf
    tok = lambda b, t, f: (b, t, 0)
    modspec = lambda k: pl.BlockSpec((1, 1, 1, d), lambda b, t, f: (b, k, 0, 0))
    return pl.pallas_call(
        _out_ffn_kernel,
        grid=(bsz, s // tm, nf),
        in_specs=[
            pl.BlockSpec((1, tm, d), tok),
            pl.BlockSpec((1, tm, oa.shape[2]), tok),
            pl.BlockSpec((1, tm, ob.shape[2]), tok),
            pl.BlockSpec((d, d), lambda b, t, f: (0, 0)),
            modspec(2),
            pl.BlockSpec((1, d), lambda b, t, f: (0, 0)),
            modspec(4),
            modspec(3),
            modspec(5),
            pl.BlockSpec((d, tf), lambda b, t, f: (0, f)),
            pl.BlockSpec((d, tf), lambda b, t, f: (0, f + nf)),
            pl.BlockSpec((tf, d), lambda b, t, f: (f, 0)),
        ],
        out_specs=pl.BlockSpec((1, tm, d), tok),
        out_shape=jax.ShapeDtypeStruct((bsz, s, d), F32),
        scratch_shapes=[pltpu.VMEM((tm, d), F32), pltpu.VMEM((tm, d), BF16), pltpu.VMEM((tm, d), F32)],
        compiler_params=_params(("parallel", "parallel", "arbitrary")),
        name="out_ffn",
    )(x, oa, ob, wo, mod4, gffn, mod4, mod4, mod4, wgu, wgu, wd)


def _t5_bucket_np(d):
    max_exact = N_BUCKETS // 2
    d = np.maximum(d, 0)
    df = np.maximum(d, 1).astype(np.float64)
    large = max_exact + (np.log(df / max_exact) / math.log(MAX_DIST / max_exact)
                         * (N_BUCKETS - max_exact)).astype(np.int64)
    large = np.minimum(large, N_BUCKETS - 1)
    return np.where(d < max_exact, d, large).astype(np.int32)


def _bias_expand_kernel(tab_ref, bucket_ref, o_ref):
    hd = pl.program_id(0)
    bucket = bucket_ref[...]
    acc = jnp.full(bucket.shape, NEG, F32)
    for b in range(N_BUCKETS):
        acc = jnp.where(bucket == b, tab_ref[hd, b], acc)
    o_ref[0] = acc


def _bias_expand(tab, bucket):
    nh = tab.shape[0]
    return pl.pallas_call(
        _bias_expand_kernel,
        grid=(nh,),
        in_specs=[pl.BlockSpec(memory_space=pltpu.SMEM),
                  pl.BlockSpec(bucket.shape, lambda h: (0, 0))],
        out_specs=pl.BlockSpec((1,) + bucket.shape, lambda h: (h, 0, 0)),
        out_shape=jax.ShapeDtypeStruct((nh,) + bucket.shape, F32),
        compiler_params=_params(("parallel",)),
        name="bias_expand",
    )(tab, jnp.asarray(bucket, jnp.int32))


def _bias_tables(rel_bias):
    tab = rel_bias.T.astype(F32) * LOG2_E
    i = np.arange(TQ)[None, :]
    d_near = i + TQ - np.arange(2 * TQ)[:, None]
    near = _bias_expand(tab, np.where(d_near >= 0, _t5_bucket_np(d_near), -1))
    d_win = i + 2 * TQ - np.arange(3 * TQ)[:, None]
    ok_win = (d_win >= 0) & (d_win < WINDOW)
    win = _bias_expand(tab[H_A:], np.where(ok_win, _t5_bucket_np(d_win), -1))
    far = jnp.broadcast_to(tab[:, N_BUCKETS - 1][:, None, None], (tab.shape[0], 8, LANES))
    return near, win, far


def _overlap_t(n_cmp_pad, n_cmp):
    cs = np.arange(n_cmp_pad)[None, :] * CMP_STRIDE
    ss = np.arange(LANES)[:, None] * SEL_BLK
    ov = (cs < ss + SEL_BLK) & (cs + CMP_LEN > ss) & (np.arange(n_cmp_pad)[None, :] < n_cmp)
    return jnp.asarray(ov.astype(np.float32), BF16)


def _block_diag(n):
    m = (np.arange(n)[:, None] // HEAD_DIM == np.arange(n)[None, :] // HEAD_DIM)
    return jnp.asarray(m.astype(np.float32) / HEAD_DIM, BF16)


def kernel(x, c, rel_bias, w_ada, b_ada, g_mix, w_in, q_norm_a, k_norm_a, q_norm_b, k_norm_cmp,
           k_norm_sel, k_norm_win, cmp_pe_k, cmp_w1_k, cmp_w2_k, cmp_pe_v, cmp_w1_v, cmp_w2_v,
           w_out, g_ffn, w_gu, w_down):
    bsz, s, d = x.shape
    depth = w_ada.shape[0]
    assert s % TM_IN == 0 and s % TM_FFN == 0 and s % (2 * TQ) == 0
    assert s // BLK_A <= HEAD_DIM and s // SEL_BLK <= LANES
    assert WINDOW == 2 * TQ and BLK_A == TQ and MAX_DIST <= TQ
    n_chunks = s // CMP_STRIDE
    n_cmp = (s - CMP_LEN) // CMP_STRIDE + 1
    scale = HEAD_DIM ** -0.5 * LOG2_E
    hd = HEAD_DIM

    near, win, far = _bias_tables(rel_bias)
    ovt = _overlap_t(n_chunks, n_cmp)
    bd = _block_diag(512)
    tile = lambda g, n: jnp.tile(g.astype(F32), n).reshape(1, -1)
    tile_t = lambda g, n: jnp.broadcast_to(jnp.tile(g.astype(F32), n)[:, None], (n * hd, TM_IN))

    for l in range(depth):
        mod = _ada(c, w_ada[l], b_ada[l])
        mod4 = mod.reshape(bsz, ADA_CHUNKS, 1, d)

        wl = w_in[l]
        cols = np.cumsum([0, H_A * hd, H_A * hd, H_A * hd, H_B * hd] + [G_B * hd] * 6)
        qa_c, ka_c, va_c, qb_c, kc_c, vc_c, ks_c, vs_c, kw_c, vw_c = [
            wl[:, int(a):int(b)] for a, b in zip(cols[:-1], cols[1:])]
        gl = wl[:, int(cols[-1]):].reshape(d, G_B, 3 * R_B)
        gl = jnp.pad(gl, ((0, 0), (0, 0), (0, GATE_ROWS - 3 * R_B))).reshape(d, G_B * GATE_ROWS)
        w_rows = jnp.concatenate([ka_c, kc_c, vc_c, ks_c, kw_c], axis=1).astype(BF16)
        w_t = jnp.concatenate([qa_c, va_c, qb_c, vs_c, vw_c, gl], axis=1).T.astype(BF16)

        (qa, kaug, vat, kmean, qb, kc, vc, ksaug, vst, kw, vwt, gates) = _inproj(
            x, mod4, mod4, g_mix[l].reshape(1, d), w_rows, w_t, bd,
            tile_t(q_norm_a[l], H_A) * scale, tile(k_norm_a[l], H_A), tile_t(q_norm_b[l], H_B) * scale,
            tile(k_norm_sel[l], G_B), tile(k_norm_win[l], G_B))

        nba = s // BLK_A
        km = kmean.reshape(bsz, nba, H_A, hd).transpose(0, 2, 1, 3)
        km = jnp.pad(km, ((0, 0), (0, 0), (0, hd - nba), (0, LANES - hd)))
        o_a = _moba(_moba_gate(qa, km), kaug, vat, near[:H_A], far[:H_A])

        chunks = lambda t: t.reshape(bsz, s, G_B, hd).transpose(0, 2, 1, 3).reshape(
            bsz, G_B, n_chunks, CMP_STRIDE * hd)
        w1 = jnp.stack([cmp_w1_k[l], cmp_w1_v[l]]).astype(BF16)
        w2t = jnp.stack([cmp_w2_k[l].T, cmp_w2_v[l].T]).astype(BF16)
        pe = jnp.stack([cmp_pe_k[l], cmp_pe_v[l]]).reshape(2, 1, CMP_LEN * hd)
        pe = jnp.broadcast_to(pe, (2, 8, CMP_LEN * hd)).astype(BF16)
        kcmp, vcmpt = _compress(chunks(kc), chunks(vc), w1, w2t, pe,
                                k_norm_cmp[l].astype(F32).reshape(1, hd))

        oc, selb = _nsa_cmp(qb, kcmp, vcmpt, ovt, n_cmp)
        o_b = _nsa_main(qb, selb, ksaug, vst, kw, vwt, oc, gates, near[H_A:], win, far[H_A:])

        x = _out_ffn(x, o_a, o_b, w_out[l].astype(BF16), mod4, g_ffn[l].reshape(1, d),
                     w_gu[l].astype(BF16), w_down[l].astype(BF16))
    return x
```

```python
import functools
import math

import jax
import jax.numpy as jnp
import numpy as np
from jax import lax
from jax.experimental import pallas as pl
from jax.experimental.pallas import tpu as pltpu

F32 = jnp.float32
BF16 = jnp.bfloat16

HEAD_DIM = 64
LANES = 128
BF16_ROWS = 16
H_A = 8
H_B = 8
G_B = 2
R_B = H_B // G_B
BLK_A = 256
TOPK_A = 3
CMP_LEN = 32
CMP_STRIDE = 16
CMP_HIDDEN = 256
SEL_BLK = 64
SEL_TOPK = 16
WINDOW = 512
N_BUCKETS = 32
MAX_DIST = 128
ADA_CHUNKS = 6
NEG = -1e30
BIG = 1e9
EPS = 1e-6
LOG2_E = math.log2(math.e)

_LOG2_BLK_A = BLK_A.bit_length() - 1
_LOG2_SEL_BLK = SEL_BLK.bit_length() - 1

TQ = 256
TM_IN = 512
TM_FFN = 512
FAR_PAIRS_PER_TRIP = 8
MOBA_HEADS = 4
GATE_ROWS = 16
ACC_ROWS = HEAD_DIM + BF16_ROWS
VMEM_LIMIT = 56 * 1024 * 1024


def _dot(a, b):
    return jnp.dot(a, b, preferred_element_type=F32)


def _dot_nt(a, b):
    return lax.dot_general(a, b, (((1,), (1,)), ((), ())), preferred_element_type=F32)


def _split(a):
    hi = a.astype(BF16)
    lo = (a - hi.astype(F32)).astype(BF16)
    return hi, lo


def _dot3(a, b):
    ah, al = _split(a)
    bh, bl = _split(b)
    return _dot(ah, bh) + (_dot(al, bh) + _dot(ah, bl))


def _params(sem):
    return pltpu.CompilerParams(dimension_semantics=sem, vmem_limit_bytes=VMEM_LIMIT)


def _ada_kernel(c_ref, w_ref, b_ref, o_ref):
    c = c_ref[...]
    o_ref[...] = _dot3(c * jax.nn.sigmoid(c), w_ref[...]) + b_ref[...]


def _ada(c, w, b):
    bsz, d = c.shape
    n = w.shape[1]
    tn = 512
    return pl.pallas_call(
        _ada_kernel,
        grid=(n // tn,),
        in_specs=[pl.BlockSpec((bsz, d), lambda j: (0, 0)),
                  pl.BlockSpec((d, tn), lambda j: (0, j)),
                  pl.BlockSpec((1, tn), lambda j: (0, j))],
        out_specs=pl.BlockSpec((bsz, tn), lambda j: (0, j)),
        out_shape=jax.ShapeDtypeStruct((bsz, n), F32),
        compiler_params=_params(("arbitrary",)),
        name="ada",
    )(c, w, b.reshape(1, n))


def _rms_rows(xf, g):
    ms = jnp.mean(xf * xf, axis=-1, keepdims=True)
    return xf * lax.rsqrt(ms + EPS) * g


def _head_norm(t, bd, gain):
    hi, lo = _split(t * t)
    ms = _dot(hi, bd) + _dot(lo, bd)
    return t * lax.rsqrt(ms + EPS) * gain


def _head_norm_t(t, gain):
    heads = []
    for hd in range(t.shape[0] // HEAD_DIM):
        th = t[hd * HEAD_DIM:(hd + 1) * HEAD_DIM]
        ms = jnp.mean(th * th, axis=0, keepdims=True)
        heads.append(th * lax.rsqrt(ms + EPS) * gain[hd * HEAD_DIM:(hd + 1) * HEAD_DIM])
    return heads


def _inproj_kernel(x_ref, sc_ref, sh_ref, gmix_ref, wr_ref, wt_ref, bd_ref, gqa_ref, gka_ref, gqb_ref,
                   gks_ref, gkw_ref,
                   qa_ref, kaug_ref, va_ref, kmean_ref, qb_ref, kc_ref, vc_ref, ksaug_ref,
                   vs_ref, kw_ref, vw_ref, gates_ref):
    tm = x_ref.shape[1]
    ti = pl.program_id(1)
    xf = x_ref[0]
    h = _rms_rows(xf, gmix_ref[...]) * (1.0 + sc_ref[0, 0]) + sh_ref[0, 0]
    hb = h.astype(BF16)

    def proj(c0, c1):
        return _dot(hb, wr_ref[:, c0:c1])

    def proj_t(r0, r1):
        return _dot_nt(wt_ref[r0:r1, :], hb)

    bd = bd_ref[...]
    bd2 = bd_ref[0:LANES, 0:LANES]
    lane = lax.broadcasted_iota(jnp.int32, (tm, LANES), 1)
    row = lax.broadcasted_iota(jnp.int32, (tm, LANES), 0) + ti * tm
    low = lane < HEAD_DIM

    def k_in_low(pair, odd):
        return pltpu.roll(pair, HEAD_DIM, 1) if odd else pair

    for hd, qh in enumerate(_head_norm_t(proj_t(0, 512), gqa_ref[...])):
        qa_ref[0, hd] = qh

    def put_tiles(ref, vt):
        for i in range(tm // TQ):
            ref[0, i] = vt[:, i * TQ:(i + 1) * TQ].astype(BF16)

    put_tiles(va_ref, proj_t(512, 1024))
    for hd, qh in enumerate(_head_norm_t(proj_t(1024, 1536), gqb_ref[...])):
        qb_ref[0, hd] = qh
    put_tiles(vs_ref, proj_t(1536, 1664))
    put_tiles(vw_ref, proj_t(1664, 1792))
    gl = jax.nn.sigmoid(proj_t(1792, 1792 + G_B * GATE_ROWS))
    for g in range(G_B):
        gates_ref[0, g] = gl[g * GATE_ROWS:(g + 1) * GATE_ROWS]

    ka = _head_norm(proj(0, 512), bd, gka_ref[...])
    oh_a = jnp.where(lane - HEAD_DIM == (row >> _LOG2_BLK_A), 1.0, 0.0)
    for hd in range(H_A):
        pair = ka[:, (hd // 2) * LANES:(hd // 2 + 1) * LANES]
        kaug_ref[0, hd] = jnp.where(low, k_in_low(pair, hd % 2), oh_a).astype(BF16)
    for i in range(tm // BLK_A):
        kmean_ref[0, i] = jnp.mean(ka[i * BLK_A:(i + 1) * BLK_A], axis=0, keepdims=True)

    kc_ref[0] = proj(512, 640).astype(BF16)
    vc_ref[0] = proj(640, 768).astype(BF16)

    ks = _head_norm(proj(768, 896), bd2, gks_ref[...])
    kw = _head_norm(proj(896, 1024), bd2, gkw_ref[...])
    oh_s = jnp.where(lane == (row >> _LOG2_SEL_BLK), 1.0, 0.0).astype(BF16)
    for g in range(G_B):
        ksaug_ref[0, g] = jnp.concatenate(
            [jnp.where(low, k_in_low(ks, g), 0.0).astype(BF16), oh_s], axis=1)
        kw_ref[0, g] = jnp.where(low, k_in_low(kw, g), 0.0).astype(BF16)


def _inproj(x, sc, sh, gmix, wr, wt, bd, gqa, gka, gqb, gks, gkw):
    bsz, s, d = x.shape
    tm = TM_IN
    nt = s // tm
    nba = s // BLK_A
    const2 = lambda b, t: (0, 0)
    tok3 = lambda b, t: (b, t, 0)
    tok4 = lambda b, t: (b, 0, t, 0)
    tile4 = lambda b, t: (b, t, 0, 0)
    tr4 = lambda b, t: (b, 0, 0, t)
    in_specs = [
        pl.BlockSpec((1, tm, d), tok3),
        pl.BlockSpec((1, 1, 1, d), lambda b, t: (b, 1, 0, 0)),
        pl.BlockSpec((1, 1, 1, d), lambda b, t: (b, 0, 0, 0)),
        pl.BlockSpec((1, d), const2),
        pl.BlockSpec(wr.shape, const2),
        pl.BlockSpec(wt.shape, const2),
        pl.BlockSpec((512, 512), const2),
        pl.BlockSpec((512, tm), const2),
        pl.BlockSpec((1, 512), const2),
        pl.BlockSpec((512, tm), const2),
        pl.BlockSpec((1, LANES), const2),
        pl.BlockSpec((1, LANES), const2),
    ]
    out_shape = [
        jax.ShapeDtypeStruct((bsz, H_A, HEAD_DIM, s), F32),
        jax.ShapeDtypeStruct((bsz, H_A, s, LANES), BF16),
        jax.ShapeDtypeStruct((bsz, s // TQ, 512, TQ), BF16),
        jax.ShapeDtypeStruct((bsz, nba, 1, 512), F32),
        jax.ShapeDtypeStruct((bsz, H_B, HEAD_DIM, s), F32),
        jax.ShapeDtypeStruct((bsz, s, LANES), BF16),
        jax.ShapeDtypeStruct((bsz, s, LANES), BF16),
        jax.ShapeDtypeStruct((bsz, G_B, s, 2 * LANES), BF16),
        jax.ShapeDtypeStruct((bsz, s // TQ, LANES, TQ), BF16),
        jax.ShapeDtypeStruct((bsz, G_B, s, LANES), BF16),
        jax.ShapeDtypeStruct((bsz, s // TQ, LANES, TQ), BF16),
        jax.ShapeDtypeStruct((bsz, G_B, GATE_ROWS, s), F32),
    ]
    out_specs = [
        pl.BlockSpec((1, H_A, HEAD_DIM, tm), tr4),
        pl.BlockSpec((1, H_A, tm, LANES), tok4),
        pl.BlockSpec((1, tm // TQ, 512, TQ), tile4),
        pl.BlockSpec((1, tm // BLK_A, 1, 512), lambda b, t: (b, t, 0, 0)),
        pl.BlockSpec((1, H_B, HEAD_DIM, tm), tr4),
        pl.BlockSpec((1, tm, LANES), tok3),
        pl.BlockSpec((1, tm, LANES), tok3),
        pl.BlockSpec((1, G_B, tm, 2 * LANES), tok4),
        pl.BlockSpec((1, tm // TQ, LANES, TQ), tile4),
        pl.BlockSpec((1, G_B, tm, LANES), tok4),
        pl.BlockSpec((1, tm // TQ, LANES, TQ), tile4),
        pl.BlockSpec((1, G_B, GATE_ROWS, tm), tr4),
    ]
    return pl.pallas_call(
        _inproj_kernel,
        grid=(bsz, nt),
        in_specs=in_specs,
        out_specs=out_specs,
        out_shape=out_shape,
        compiler_params=_params(("parallel", "parallel")),
        name="inproj",
    )(x, sc, sh, gmix, wr, wt, bd, gqa, gka, gqb, gks, gkw)


def _compress_kernel(ck_ref, cv_ref, w1_ref, w2t_ref, pe_ref, gk_ref, ok_ref, ov_ref):
    half = CMP_STRIDE * HEAD_DIM
    for kv, c_ref in enumerate((ck_ref, cv_ref)):
        for g in range(G_B):
            c = c_ref[0, g]
            a = _dot(c, w1_ref[kv, 0:half, :])
            b = _dot(c, w1_ref[kv, half:2 * half, :])
            n = a.shape[0]
            b_next = pltpu.roll(b, n - 1, 0)
            pe_term = _dot(pe_ref[kv], w1_ref[kv])[0:1]
            hid = jax.nn.gelu(a + b_next + pe_term).astype(BF16)
            if kv == 0:
                y = _dot_nt(hid, w2t_ref[kv])
                ms = jnp.mean(y * y, axis=1, keepdims=True)
                y = y * lax.rsqrt(ms + EPS) * gk_ref[...]
                ok_ref[0, g] = jnp.concatenate([y, jnp.zeros_like(y)], axis=1).astype(BF16)
            else:
                ov_ref[0, g] = _dot_nt(w2t_ref[kv], hid).astype(BF16)


def _compress(ck, cv, w1, w2t, pe, gk):
    bsz, g, n, width = ck.shape
    blk = pl.BlockSpec((1, g, n, width), lambda b: (b, 0, 0, 0))
    full = lambda a: pl.BlockSpec(a.shape, lambda b: (0,) * a.ndim)
    return pl.pallas_call(
        _compress_kernel,
        grid=(bsz,),
        in_specs=[blk, blk, full(w1), full(w2t), full(pe), full(gk)],
        out_specs=[pl.BlockSpec((1, g, n, LANES), lambda b: (b, 0, 0, 0)),
                   pl.BlockSpec((1, g, HEAD_DIM, n), lambda b: (b, 0, 0, 0))],
        out_shape=[jax.ShapeDtypeStruct((bsz, g, n, LANES), BF16),
                   jax.ShapeDtypeStruct((bsz, g, HEAD_DIM, n), BF16)],
        compiler_params=_params(("parallel",)),
        name="compress",
    )(ck, cv, w1, w2t, pe, gk)


def _with_ones(vt):
    return jnp.concatenate([vt, jnp.ones((BF16_ROWS, vt.shape[1]), BF16)], axis=0)


def _col_max(s):
    while s.shape[0] > 8:
        half = s.shape[0] // 2
        s = jnp.maximum(s[0:half], s[half:2 * half])
    return jnp.max(s, axis=0, keepdims=True)


def _flash_update(carries, chains):
    m_news = []
    for carry, tiles in zip(carries, chains):
        tops = []
        for s, _, bias, top in tiles:
            top = _col_max(s) if top is None else top
            tops.append(top if bias is None else top + bias)
        m_news.append(functools.reduce(jnp.maximum, tops if carry is None else tops + [carry[0]]))
    pvs = [None] * len(chains)
    for t in range(max(len(tiles) for tiles in chains)):
        for c, tiles in enumerate(chains):
            if t < len(tiles):
                s, vt, bias, _ = tiles[t]
                p = jnp.exp2(s - (m_news[c] if bias is None else m_news[c] - bias)).astype(BF16)
                part = _dot(_with_ones(vt), p)
                pvs[c] = part if pvs[c] is None else pvs[c] + part
    outs = []
    for carry, m_new, pv in zip(carries, m_news, pvs):
        outs.append((m_new, pv if carry is None else jnp.exp2(carry[0] - m_new) * carry[1] + pv))
    return outs


def _flash_out(acc):
    return acc[0:HEAD_DIM] / acc[HEAD_DIM:HEAD_DIM + 1]


def _topk_rows(scores, index, k):
    scores = list(scores)
    picked = [jnp.zeros(sc.shape, F32) for sc in scores]
    for _ in range(k):
        mx = [jnp.max(sc, axis=0, keepdims=True) for sc in scores]
        cand = [jnp.where(sc == m, index, jnp.int32(1 << 20)) for sc, m in zip(scores, mx)]
        first = [jnp.min(c, axis=0, keepdims=True) for c in cand]
        hit = [index == f for f in first]
        picked = [jnp.where(h, 1.0, p) for h, p in zip(hit, picked)]
        scores = [jnp.where(h, -jnp.inf, sc) for h, sc in zip(hit, scores)]
    return picked


def _tile_rows(j):
    return pl.ds(pl.multiple_of(j * TQ, TQ), TQ)


def _moba_gate_kernel(q_ref, km_ref, o_ref, *, n_sel):
    nh, tg = q_ref.shape[1], q_ref.shape[3]
    t0 = pl.program_id(2) * tg
    shape = (km_ref.shape[2], tg)
    blk = lax.broadcasted_iota(jnp.int32, shape, 0)
    own = (lax.broadcasted_iota(jnp.int32, shape, 1) + t0) >> _LOG2_BLK_A
    valid = blk < own
    qs, gates = [], []
    for hh in range(nh):
        q = q_ref[0, hh]
        gate = _dot3(km_ref[0, hh], jnp.concatenate([q, jnp.zeros_like(q)], axis=0))
        qs.append(q)
        gates.append(jnp.where(valid, gate, -jnp.inf))
    for hh, picked in enumerate(_topk_rows(gates, blk, n_sel)):
        keep = jnp.where(valid, picked, 0.0) + jnp.where(blk == own, 1.0, 0.0)
        selb = jnp.where(keep > 0.0, 0.0, NEG)
        o_ref[0, hh] = jnp.concatenate([qs[hh], selb], axis=0).astype(BF16)


def _moba_gate(qa, km):
    bsz, nh, _, s = qa.shape
    tg = min(s, 2048)
    hpb = 2
    n_sel = max(1, min(TOPK_A, s // BLK_A - 1))
    return pl.pallas_call(
        functools.partial(_moba_gate_kernel, n_sel=n_sel),
        grid=(bsz, nh // hpb, s // tg),
        in_specs=[pl.BlockSpec((1, hpb, HEAD_DIM, tg), lambda b, h, t: (b, h, 0, t)),
                  pl.BlockSpec((1, hpb, HEAD_DIM, LANES), lambda b, h, t: (b, h, 0, 0))],
        out_specs=pl.BlockSpec((1, hpb, LANES, tg), lambda b, h, t: (b, h, 0, t)),
        out_shape=jax.ShapeDtypeStruct((bsz, nh, LANES, s), BF16),
        compiler_params=_params(("parallel", "parallel", "parallel")),
        name="moba_gate",
    )(qa, km)


def _far_tiles(qi):
    n_far = jnp.maximum(qi - 1, 0)
    left = jnp.maximum(n_far - 1, 0)
    off_left = jnp.where((n_far & 1) == 1, 0.0, NEG)
    return n_far >> 1, left, off_left


class _FarLoop:
    def __init__(self, n_pairs, heads, qk_tile, values, cbs, sa_ref, sb_ref, m_ref, acc_ref):
        self.n_pairs, self.heads, self.qk_tile, self.values, self.cbs = n_pairs, heads, qk_tile, values, cbs
        self.sa_ref, self.sb_ref, self.m_ref, self.acc_ref = sa_ref, sb_ref, m_ref, acc_ref
        self.last = jnp.maximum(n_pairs - 1, 0)

    def fetch(self, buf_ref, h, i):
        s_lo = self.qk_tile(h, 2 * i)
        s_hi = self.qk_tile(h, 2 * i + 1)
        buf_ref[h, 0:TQ, :] = s_lo
        buf_ref[h, TQ:2 * TQ, :] = s_hi
        return _col_max(s_lo), _col_max(s_hi)

    def consume(self, buf_ref, h, top, i):
        (m, acc), = _flash_update([(self.m_ref[h], self.acc_ref[h])], [[
            (buf_ref[h, 0:TQ, :], self.values(h, 2 * i), self.cbs[h], top[0]),
            (buf_ref[h, TQ:2 * TQ, :], self.values(h, 2 * i + 1), self.cbs[h], top[1])]])
        self.m_ref[h] = m
        self.acc_ref[h] = acc

    def first(self):
        return tuple(self.fetch(self.sa_ref, h, 0) for h in self.heads)

    def run(self, tops_first):
        def two_pairs(ia, tops_a):
            tops_b, tops_next = [], []
            for h in self.heads:
                tops_b.append(self.fetch(self.sb_ref, h, ia + 1))
                self.consume(self.sa_ref, h, tops_a[h], ia)
            for h in self.heads:
                tops_next.append(self.fetch(self.sa_ref, h, jnp.minimum(ia + 2, self.last)))
                self.consume(self.sb_ref, h, tops_b[h], ia + 1)
            return tuple(tops_next)

        def pairs(n, start, tops):
            for k in range(0, n, 2):
                tops = two_pairs(start + k, tops)
            return tops

        trip = FAR_PAIRS_PER_TRIP
        n_trips = self.n_pairs // trip
        tops = lax.fori_loop(0, n_trips, lambda q, t: pairs(trip, trip * q, t), tops_first)
        done = trip * n_trips
        n = trip // 2
        while n >= 2:
            has = ((self.n_pairs // n) & 1) == 1
            tops = lax.cond(has, functools.partial(pairs, n, done), lambda t: t, tops)
            done = done + jnp.where(has, n, 0)
            n //= 2
        tops_last = tops

        @pl.when((self.n_pairs & 1) == 1)
        def _():
            for h in self.heads:
                self.consume(self.sa_ref, h, tops_last[h], self.last)


def _moba_kernel(q_ref, k_ref, vt_ref, tab_ref, cb_ref, o_ref, sa_ref, sb_ref, m_ref, acc_ref):
    qi = pl.program_id(2)
    jp = jnp.maximum(qi - 1, 0)
    off_p = jnp.where(qi >= 1, 0.0, NEG)
    n_pairs, jl, off_l = _far_tiles(qi)
    heads = range(MOBA_HEADS)
    qaug = [q_ref[0, hh] for hh in heads]
    cbs = [cb_ref[hh][0:1, 0:1] for hh in heads]

    def scores(hh, j):
        return _dot(k_ref[0, hh, _tile_rows(j), :], qaug[hh])

    def values(hh, j):
        return vt_ref[0, j, hh * HEAD_DIM:(hh + 1) * HEAD_DIM, :]

    far = _FarLoop(n_pairs, heads, scores, values, cbs, sa_ref, sb_ref, m_ref, acc_ref)

    s_own = [scores(hh, qi) for hh in heads]
    s_prev = [scores(hh, jp) for hh in heads]
    s_left = [scores(hh, jl) for hh in heads]
    tops_first = far.first()
    chains = [[(s_own[hh] + tab_ref[hh, TQ:2 * TQ, :], values(hh, qi), None, None),
               (s_prev[hh] + tab_ref[hh, 0:TQ, :] + off_p, values(hh, jp), None, None),
               (s_left[hh], values(hh, jl), cbs[hh] + off_l, None)] for hh in heads]
    for hh, (m, acc) in zip(heads, _flash_update([None] * len(chains), chains)):
        m_ref[hh] = m
        acc_ref[hh] = acc
    far.run(tops_first)
    out_t = jnp.concatenate([_flash_out(acc_ref[hh]) for hh in heads], axis=0)
    o_ref[0] = out_t.T.astype(BF16)


def _flash_scratch(n_heads):
    return [pltpu.VMEM((n_heads, 2 * TQ, TQ), F32),
            pltpu.VMEM((n_heads, 2 * TQ, TQ), F32),
            pltpu.VMEM((n_heads, 1, TQ), F32),
            pltpu.VMEM((n_heads, ACC_ROWS, TQ), F32)]


def _moba(qaug, kaug, vat, tab, cb):
    bsz, _, _, s = qaug.shape
    nq = s // TQ
    nh = MOBA_HEADS
    return pl.pallas_call(
        _moba_kernel,
        grid=(bsz, H_A // nh, nq),
        in_specs=[
            pl.BlockSpec((1, nh, LANES, TQ), lambda b, hp, qi: (b, hp, 0, qi)),
            pl.BlockSpec((1, nh, s, LANES), lambda b, hp, qi: (b, hp, 0, 0)),
            pl.BlockSpec((1, nq, nh * HEAD_DIM, TQ), lambda b, hp, qi: (b, 0, hp, 0)),
            pl.BlockSpec((nh, 2 * TQ, TQ), lambda b, hp, qi: (hp, 0, 0)),
            pl.BlockSpec((nh, 8, LANES), lambda b, hp, qi: (hp, 0, 0)),
        ],
        out_specs=pl.BlockSpec((1, TQ, nh * HEAD_DIM), lambda b, hp, qi: (b, qi, hp)),
        out_shape=jax.ShapeDtypeStruct((bsz, s, H_A * HEAD_DIM), BF16),
        scratch_shapes=_flash_scratch(nh),
        compiler_params=_params(("parallel", "parallel", "arbitrary")),
        name="moba",
    )(qaug, kaug, vat, tab, cb)


def _nsa_cmp_kernel(q_ref, kc_ref, vct_ref, ovt_ref, oc_ref, selb_ref, *, n_sel, n_cmp, n_parts):
    qi = pl.program_id(1)
    ncp = kc_ref.shape[2]
    t0 = qi * TQ
    zeros = jnp.zeros((HEAD_DIM, TQ), BF16)
    qs = [jnp.concatenate([q_ref[0, hd].astype(BF16), zeros], axis=0) for hd in range(H_B)]

    def body(nk, nb):
        n_idx = lax.broadcasted_iota(jnp.int32, (nk, TQ), 0)
        t_idx = lax.broadcasted_iota(jnp.int32, (nk, TQ), 1) + t0
        mask = (n_idx * CMP_STRIDE + (CMP_LEN - 1) <= t_idx) & (n_idx < n_cmp)
        any_key = t_idx[0:1] >= CMP_LEN - 1
        blk = lax.broadcasted_iota(jnp.int32, (nb, TQ), 0)
        cur = (lax.broadcasted_iota(jnp.int32, (nb, TQ), 1) + t0) >> _LOG2_SEL_BLK
        ok = blk <= cur
        forced = (blk == 0) | (blk == cur) | (blk == cur - 1)
        ovt = ovt_ref[0:nb, 0:nk]
        scores = []
        for g in range(G_B):
            kc = kc_ref[0, g, 0:nk, :]
            vct = _with_ones(vct_ref[0, g, :, 0:nk])
            psum = jnp.zeros((nk, TQ), F32)
            for hd in range(g * R_B, (g + 1) * R_B):
                z = jnp.where(mask, _dot(kc, qs[hd]), NEG)
                e = jnp.exp2(z - _col_max(z))
                acc = _dot(vct, e.astype(BF16))
                rinv = jnp.where(any_key, 1.0 / acc[HEAD_DIM:HEAD_DIM + 1], 0.0)
                oc_ref[0, hd] = acc[0:HEAD_DIM] * rinv
                psum = psum + e * rinv
            ph, pl_ = _split(psum)
            imp_t = _dot(ovt, ph) + _dot(ovt, pl_)
            scores.append(jnp.where(ok, jnp.where(forced, BIG, imp_t), -jnp.inf))
        for g, picked in enumerate(_topk_rows(scores, blk, n_sel)):
            selb_ref[0, g, 0:nb, :] = jnp.where(ok & (picked > 0.0), 0.0, NEG).astype(BF16)
            if nb < LANES:
                selb_ref[0, g, nb:LANES, :] = jnp.full((LANES - nb, TQ), NEG, BF16)

    part = ncp // n_parts
    need = jnp.minimum(((qi + 1) * (TQ // CMP_STRIDE) + part - 1) // part, n_parts)
    for v in range(1, n_parts + 1):
        pl.when(need == v)(functools.partial(body, v * part, min(LANES, v * part * CMP_STRIDE // SEL_BLK)))


def _cmp_parts(ncp):
    return 4 if ncp % (4 * LANES) == 0 else 1


def _nsa_cmp(qb, kcmp, vcmpt, ovt, n_cmp):
    bsz, _, _, s = qb.shape
    nq = s // TQ
    ncp = kcmp.shape[2]
    n_sel = min(SEL_TOPK, s // SEL_BLK)
    n_parts = _cmp_parts(ncp)
    return pl.pallas_call(
        functools.partial(_nsa_cmp_kernel, n_sel=n_sel, n_cmp=n_cmp, n_parts=n_parts),
        grid=(bsz, nq),
        in_specs=[
            pl.BlockSpec((1, H_B, HEAD_DIM, TQ), lambda b, qi: (b, 0, 0, qi)),
            pl.BlockSpec((1, G_B, ncp, LANES), lambda b, qi: (b, 0, 0, 0)),
            pl.BlockSpec((1, G_B, HEAD_DIM, ncp), lambda b, qi: (b, 0, 0, 0)),
            pl.BlockSpec((LANES, ncp), lambda b, qi: (0, 0)),
        ],
        out_specs=[
            pl.BlockSpec((1, H_B, HEAD_DIM, TQ), lambda b, qi: (b, 0, 0, qi)),
            pl.BlockSpec((1, G_B, LANES, TQ), lambda b, qi: (b, 0, 0, qi)),
        ],
        out_shape=[jax.ShapeDtypeStruct((bsz, H_B, HEAD_DIM, s), F32),
                   jax.ShapeDtypeStruct((bsz, G_B, LANES, s), BF16)],
        compiler_params=_params(("parallel", "parallel")),
        name="nsa_cmp",
    )(qb, kcmp, vcmpt, ovt)


def _nsa_main_kernel(q_ref, selb_ref, ks_ref, vst_ref, kw_ref, vwt_ref, oc_ref, gates_ref,
                     tabs_ref, tabw_ref, cb_ref, o_ref, sa_ref, sb_ref, m_ref, acc_ref):
    qi = pl.program_id(2)
    selb = selb_ref[0, 0]
    gates = gates_ref[0, 0]
    zeros = jnp.zeros((HEAD_DIM, TQ), BF16)
    j1 = jnp.maximum(qi - 1, 0)
    j2 = jnp.maximum(qi - 2, 0)
    off1 = jnp.where(qi >= 1, 0.0, NEG)
    off2 = jnp.where(qi >= 2, 0.0, NEG)
    n_pairs, jl, off_l = _far_tiles(qi)

    heads = range(R_B)
    qw = [jnp.concatenate([q_ref[0, r].astype(BF16), zeros], axis=0) for r in heads]
    qs = [jnp.concatenate([qw[r], selb], axis=0) for r in heads]
    cbs = [cb_ref[r][0:1, 0:1] for r in heads]

    def s_sel(r, j):
        return _dot(ks_ref[0, 0, _tile_rows(j), :], qs[r])

    def s_win(r, j):
        return _dot(kw_ref[0, 0, _tile_rows(j), :], qw[r])

    far = _FarLoop(n_pairs, heads, s_sel, lambda r, j: vst_ref[0, j], cbs, sa_ref, sb_ref, m_ref, acc_ref)

    sw = [[s_win(r, j) for r in heads] for j in (qi, j1, j2)]
    ss = [[s_sel(r, j) for r in heads] for j in (qi, j1, jl)]
    tops_first = far.first()

    chains = [[(sw[0][r] + tabw_ref[r, 2 * TQ:3 * TQ, :], vwt_ref[0, qi], None, None),
               (sw[1][r] + tabw_ref[r, TQ:2 * TQ, :] + off1, vwt_ref[0, j1], None, None),
               (sw[2][r] + tabw_ref[r, 0:TQ, :] + off2, vwt_ref[0, j2], None, None)] for r in heads]
    chains += [[(ss[0][r] + tabs_ref[r, TQ:2 * TQ, :], vst_ref[0, qi], None, None),
                (ss[1][r] + tabs_ref[r, 0:TQ, :] + off1, vst_ref[0, j1], None, None),
                (ss[2][r], vst_ref[0, jl], cbs[r] + off_l, None)] for r in heads]
    done = _flash_update([None] * len(chains), chains)
    win = [_flash_out(acc) for _, acc in done[:R_B]]
    for r, (m, acc) in zip(heads, done[R_B:]):
        m_ref[r] = m
        acc_ref[r] = acc
    far.run(tops_first)

    outs = [gates[3 * r:3 * r + 1] * oc_ref[0, r]
            + gates[3 * r + 1:3 * r + 2] * _flash_out(acc_ref[r])
            + gates[3 * r + 2:3 * r + 3] * win[r] for r in heads]
    o_ref[0] = jnp.concatenate(outs, axis=0).T.astype(BF16)


def _nsa_main(qb, selb, ksaug, vst, kw, vwt, oc, gates, tabs, tabw, cb):
    bsz, _, _, s = qb.shape
    nq = s // TQ
    return pl.pallas_call(
        _nsa_main_kernel,
        grid=(bsz, G_B, nq),
        in_specs=[
            pl.BlockSpec((1, R_B, HEAD_DIM, TQ), lambda b, g, qi: (b, g, 0, qi)),
            pl.BlockSpec((1, 1, LANES, TQ), lambda b, g, qi: (b, g, 0, qi)),
            pl.BlockSpec((1, 1, s, 2 * LANES), lambda b, g, qi: (b, g, 0, 0)),
            pl.BlockSpec((1, nq, HEAD_DIM, TQ), lambda b, g, qi: (b, 0, g, 0)),
            pl.BlockSpec((1, 1, s, LANES), lambda b, g, qi: (b, g, 0, 0)),
            pl.BlockSpec((1, nq, HEAD_DIM, TQ), lambda b, g, qi: (b, 0, g, 0)),
            pl.BlockSpec((1, R_B, HEAD_DIM, TQ), lambda b, g, qi: (b, g, 0, qi)),
            pl.BlockSpec((1, 1, GATE_ROWS, TQ), lambda b, g, qi: (b, g, 0, qi)),
            pl.BlockSpec((R_B, 2 * TQ, TQ), lambda b, g, qi: (g, 0, 0)),
            pl.BlockSpec((R_B, 3 * TQ, TQ), lambda b, g, qi: (g, 0, 0)),
            pl.BlockSpec((R_B, 8, LANES), lambda b, g, qi: (g, 0, 0)),
        ],
        out_specs=pl.BlockSpec((1, TQ, R_B * HEAD_DIM), lambda b, g, qi: (b, qi, g)),
        out_shape=jax.ShapeDtypeStruct((bsz, s, H_B * HEAD_DIM), BF16),
        scratch_shapes=_flash_scratch(R_B),
        compiler_params=_params(("parallel", "parallel", "arbitrary")),
        name="nsa_main",
    )(qb, selb, ksaug, vst, kw, vwt, oc, gates, tabs, tabw, cb)


def _out_ffn_kernel(x_ref, oa_ref, ob_ref, wo_ref, gtm_ref, gffn_ref, scf_ref, shf_ref, gtf_ref,
                    wg_ref, wu_ref, wd_ref, o_ref, x1_ref, h_ref, acc_ref):
    f = pl.program_id(2)
    half = oa_ref.shape[2]

    @pl.when(f == 0)
    def _():
        mix = _dot(oa_ref[0], wo_ref[0:half, :]) + _dot(ob_ref[0], wo_ref[half:2 * half, :])
        x1 = x_ref[0] + gtm_ref[0, 0] * mix
        x1_ref[...] = x1
        h = _rms_rows(x1, gffn_ref[...]) * (1.0 + scf_ref[0, 0]) + shf_ref[0, 0]
        h_ref[...] = h.astype(BF16)

    h = h_ref[...]
    gate = _dot(h, wg_ref[...])
    up = _dot(h, wu_ref[...])
    act = (gate * jax.nn.sigmoid(gate) * up).astype(BF16)
    part = _dot(act, wd_ref[...])

    @pl.when(f == 0)
    def _():
        acc_ref[...] = part

    @pl.when(f > 0)
    def _():
        acc_ref[...] += part

    @pl.when(f == pl.num_programs(2) - 1)
    def _():
        o_ref[0] = x1_ref[...] + gtf_ref[0, 0] * acc_ref[...]


def _out_ffn(x, oa, ob, wo, mod4, gffn, wgu, wd):
    bsz, s, d = x.shape
    fh = wd.shape[0]
    tm = TM_FFN
    tf = fh // 2 if (fh // 2) % LANES == 0 else fh
    nf = fh // tf
    tok = lambda b, t, f: (b, t, 0)
    modspec = lambda k: pl.BlockSpec((1, 1, 1, d), lambda b, t, f: (b, k, 0, 0))
    return pl.pallas_call(
        _out_ffn_kernel,
        grid=(bsz, s // tm, nf),
        in_specs=[
            pl.BlockSpec((1, tm, d), tok),
            pl.BlockSpec((1, tm, oa.shape[2]), tok),
            pl.BlockSpec((1, tm, ob.shape[2]), tok),
            pl.BlockSpec((d, d), lambda b, t, f: (0, 0)),
            modspec(2),
            pl.BlockSpec((1, d), lambda b, t, f: (0, 0)),
            modspec(4),
            modspec(3),
            modspec(5),
            pl.BlockSpec((d, tf), lambda b, t, f: (0, f)),
            pl.BlockSpec((d, tf), lambda b, t, f: (0, f + nf)),
            pl.BlockSpec((tf, d), lambda b, t, f: (f, 0)),
        ],
        out_specs=pl.BlockSpec((1, tm, d), tok),
        out_shape=jax.ShapeDtypeStruct((bsz, s, d), F32),
        scratch_shapes=[pltpu.VMEM((tm, d), F32), pltpu.VMEM((tm, d), BF16), pltpu.VMEM((tm, d), F32)],
        compiler_params=_params(("parallel", "parallel", "arbitrary")),
        name="out_ffn",
    )(x, oa, ob, wo, mod4, gffn, mod4, mod4, mod4, wgu, wgu, wd)


def _t5_bucket_np(d):
    max_exact = N_BUCKETS // 2
    d = np.maximum(d, 0)
    df = np.maximum(d, 1).astype(np.float64)
    large = max_exact + (np.log(df / max_exact) / math.log(MAX_DIST / max_exact)
                         * (N_BUCKETS - max_exact)).astype(np.int64)
    large = np.minimum(large, N_BUCKETS - 1)
    return np.where(d < max_exact, d, large).astype(np.int32)


def _bias_expand_kernel(tab_ref, bucket_ref, o_ref):
    hd = pl.program_id(0)
    bucket = bucket_ref[...]
    acc = jnp.full(bucket.shape, NEG, F32)
    for b in range(N_BUCKETS):
        acc = jnp.where(bucket == b, tab_ref[hd, b], acc)
    o_ref[0] = acc


def _bias_expand(tab, bucket):
    nh = tab.shape[0]
    return pl.pallas_call(
        _bias_expand_kernel,
        grid=(nh,),
        in_specs=[pl.BlockSpec(memory_space=pltpu.SMEM),
                  pl.BlockSpec(bucket.shape, lambda h: (0, 0))],
        out_specs=pl.BlockSpec((1,) + bucket.shape, lambda h: (h, 0, 0)),
        out_shape=jax.ShapeDtypeStruct((nh,) + bucket.shape, F32),
        compiler_params=_params(("parallel",)),
        name="bias_expand",
    )(tab, jnp.asarray(bucket, jnp.int32))


def _bias_tables(rel_bias):
    tab = rel_bias.T.astype(F32) * LOG2_E
    i = np.arange(TQ)[None, :]
    d_near = i + TQ - np.arange(2 * TQ)[:, None]
    near = _bias_expand(tab, np.where(d_near >= 0, _t5_bucket_np(d_near), -1))
    d_win = i + 2 * TQ - np.arange(3 * TQ)[:, None]
    ok_win = (d_win >= 0) & (d_win < WINDOW)
    win = _bias_expand(tab[H_A:], np.where(ok_win, _t5_bucket_np(d_win), -1))
    far = jnp.broadcast_to(tab[:, N_BUCKETS - 1][:, None, None], (tab.shape[0], 8, LANES))
    return near, win, far


def _overlap_t(n_cmp_pad, n_cmp):
    cs = np.arange(n_cmp_pad)[None, :] * CMP_STRIDE
    ss = np.arange(LANES)[:, None] * SEL_BLK
    ov = (cs < ss + SEL_BLK) & (cs + CMP_LEN > ss) & (np.arange(n_cmp_pad)[None, :] < n_cmp)
    return jnp.asarray(ov.astype(np.float32), BF16)


def _block_diag(n):
    m = (np.arange(n)[:, None] // HEAD_DIM == np.arange(n)[None, :] // HEAD_DIM)
    return jnp.asarray(m.astype(np.float32) / HEAD_DIM, BF16)


def kernel(x, c, rel_bias, w_ada, b_ada, g_mix, w_in, q_norm_a, k_norm_a, q_norm_b, k_norm_cmp,
           k_norm_sel, k_norm_win, cmp_pe_k, cmp_w1_k, cmp_w2_k, cmp_pe_v, cmp_w1_v, cmp_w2_v,
           w_out, g_ffn, w_gu, w_down):
    bsz, s, d = x.shape
    depth = w_ada.shape[0]
    assert s % TM_IN == 0 and s % TM_FFN == 0 and s % (2 * TQ) == 0
    assert s // BLK_A <= HEAD_DIM and s // SEL_BLK <= LANES
    assert WINDOW == 2 * TQ and BLK_A == TQ and MAX_DIST <= TQ
    n_chunks = s // CMP_STRIDE
    n_cmp = (s - CMP_LEN) // CMP_STRIDE + 1
    scale = HEAD_DIM ** -0.5 * LOG2_E
    hd = HEAD_DIM

    near, win, far = _bias_tables(rel_bias)
    ovt = _overlap_t(n_chunks, n_cmp)
    bd = _block_diag(512)
    tile = lambda g, n: jnp.tile(g.astype(F32), n).reshape(1, -1)
    tile_t = lambda g, n: jnp.broadcast_to(jnp.tile(g.astype(F32), n)[:, None], (n * hd, TM_IN))

    for l in range(depth):
        mod = _ada(c, w_ada[l], b_ada[l])
        mod4 = mod.reshape(bsz, ADA_CHUNKS, 1, d)

        wl = w_in[l]
        cols = np.cumsum([0, H_A * hd, H_A * hd, H_A * hd, H_B * hd] + [G_B * hd] * 6)
        qa_c, ka_c, va_c, qb_c, kc_c, vc_c, ks_c, vs_c, kw_c, vw_c = [
            wl[:, int(a):int(b)] for a, b in zip(cols[:-1], cols[1:])]
        gl = wl[:, int(cols[-1]):].reshape(d, G_B, 3 * R_B)
        gl = jnp.pad(gl, ((0, 0), (0, 0), (0, GATE_ROWS - 3 * R_B))).reshape(d, G_B * GATE_ROWS)
        w_rows = jnp.concatenate([ka_c, kc_c, vc_c, ks_c, kw_c], axis=1).astype(BF16)
        w_t = jnp.concatenate([qa_c, va_c, qb_c, vs_c, vw_c, gl], axis=1).T.astype(BF16)

        (qa, kaug, vat, kmean, qb, kc, vc, ksaug, vst, kw, vwt, gates) = _inproj(
            x, mod4, mod4, g_mix[l].reshape(1, d), w_rows, w_t, bd,
            tile_t(q_norm_a[l], H_A) * scale, tile(k_norm_a[l], H_A), tile_t(q_norm_b[l], H_B) * scale,
            tile(k_norm_sel[l], G_B), tile(k_norm_win[l], G_B))

        nba = s // BLK_A
        km = kmean.reshape(bsz, nba, H_A, hd).transpose(0, 2, 1, 3)
        km = jnp.pad(km, ((0, 0), (0, 0), (0, hd - nba), (0, LANES - hd)))
        o_a = _moba(_moba_gate(qa, km), kaug, vat, near[:H_A], far[:H_A])

        chunks = lambda t: t.reshape(bsz, s, G_B, hd).transpose(0, 2, 1, 3).reshape(
            bsz, G_B, n_chunks, CMP_STRIDE * hd)
        w1 = jnp.stack([cmp_w1_k[l], cmp_w1_v[l]]).astype(BF16)
        w2t = jnp.stack([cmp_w2_k[l].T, cmp_w2_v[l].T]).astype(BF16)
        pe = jnp.stack([cmp_pe_k[l], cmp_pe_v[l]]).reshape(2, 1, CMP_LEN * hd)
        pe = jnp.broadcast_to(pe, (2, 8, CMP_LEN * hd)).astype(BF16)
        kcmp, vcmpt = _compress(chunks(kc), chunks(vc), w1, w2t, pe,
                                k_norm_cmp[l].astype(F32).reshape(1, hd))

        oc, selb = _nsa_cmp(qb, kcmp, vcmpt, ovt, n_cmp)
        o_b = _nsa_main(qb, selb, ksaug, vst, kw, vwt, oc, gates, near[H_A:], win, far[H_A:])

        x = _out_ffn(x, o_a, o_b, w_out[l].astype(BF16), mod4, g_ffn[l].reshape(1, d),
                     w_gu[l].astype(BF16), w_down[l].astype(BF16))
    return x
```

```python
import functools
import math

import jax
import jax.numpy as jnp
import numpy as np
from jax import lax
from jax.experimental import pallas as pl
from jax.experimental.pallas import tpu as pltpu

F32 = jnp.float32
BF16 = jnp.bfloat16

HEAD_DIM = 64
LANES = 128
BF16_ROWS = 16
H_A = 8
H_B = 8
G_B = 2
R_B = H_B // G_B
BLK_A = 256
TOPK_A = 3
CMP_LEN = 32
CMP_STRIDE = 16
CMP_HIDDEN = 256
SEL_BLK = 64
SEL_TOPK = 16
WINDOW = 512
N_BUCKETS = 32
MAX_DIST = 128
ADA_CHUNKS = 6
NEG = -1e30
BIG = 1e9
EPS = 1e-6
LOG2_E = math.log2(math.e)

_LOG2_BLK_A = BLK_A.bit_length() - 1
_LOG2_SEL_BLK = SEL_BLK.bit_length() - 1

TQ = 256
TM_IN = 1024
TM_FFN = 512
CMP_TQ = 512
FAR_PAIRS_PER_TRIP = 8
MOBA_HEADS = 4
GATE_ROWS = 16
ACC_ROWS = HEAD_DIM + BF16_ROWS
VMEM_LIMIT = 56 * 1024 * 1024


def _dot(a, b):
    return jnp.dot(a, b, preferred_element_type=F32)


def _dot_nt(a, b):
    return lax.dot_general(a, b, (((1,), (1,)), ((), ())), preferred_element_type=F32)


def _split(a):
    hi = a.astype(BF16)
    lo = (a - hi.astype(F32)).astype(BF16)
    return hi, lo


def _dot3(a, b):
    ah, al = _split(a)
    bh, bl = _split(b)
    return _dot(ah, bh) + (_dot(al, bh) + _dot(ah, bl))


def _params(sem):
    return pltpu.CompilerParams(dimension_semantics=sem, vmem_limit_bytes=VMEM_LIMIT)


def _ada_kernel(c_ref, w_ref, b_ref, o_ref):
    c = c_ref[...]
    o_ref[...] = _dot3(c * jax.nn.sigmoid(c), w_ref[...]) + b_ref[...]


def _ada(c, w, b):
    bsz, d = c.shape
    n = w.shape[1]
    tn = 512
    return pl.pallas_call(
        _ada_kernel,
        grid=(n // tn,),
        in_specs=[pl.BlockSpec((bsz, d), lambda j: (0, 0)),
                  pl.BlockSpec((d, tn), lambda j: (0, j)),
                  pl.BlockSpec((1, tn), lambda j: (0, j))],
        out_specs=pl.BlockSpec((bsz, tn), lambda j: (0, j)),
        out_shape=jax.ShapeDtypeStruct((bsz, n), F32),
        compiler_params=_params(("arbitrary",)),
        name="ada",
    )(c, w, b.reshape(1, n))


def _rms_rows(xf, g):
    ms = jnp.mean(xf * xf, axis=-1, keepdims=True)
    return xf * lax.rsqrt(ms + EPS) * g


def _head_norm(t, bd, gain):
    hi, lo = _split(t * t)
    ms = _dot(hi, bd) + _dot(lo, bd)
    return t * lax.rsqrt(ms + EPS) * gain


def _head_norm_t(t, gain):
    heads = []
    for hd in range(t.shape[0] // HEAD_DIM):
        th = t[hd * HEAD_DIM:(hd + 1) * HEAD_DIM]
        ms = jnp.mean(th * th, axis=0, keepdims=True)
        heads.append(th * lax.rsqrt(ms + EPS) * gain[hd * HEAD_DIM:(hd + 1) * HEAD_DIM])
    return heads


def _inproj_kernel(x_ref, sc_ref, sh_ref, gmix_ref, wr_ref, wt_ref, bd_ref, gqa_ref, gka_ref, gqb_ref,
                   gks_ref, gkw_ref,
                   qa_ref, kaug_ref, va_ref, kmean_ref, qb_ref, kc_ref, vc_ref, ksaug_ref,
                   vs_ref, kw_ref, vw_ref, gates_ref):
    tm = x_ref.shape[1]
    ti = pl.program_id(1)
    xf = x_ref[0]
    h = _rms_rows(xf, gmix_ref[...]) * (1.0 + sc_ref[0, 0]) + sh_ref[0, 0]
    hb = h.astype(BF16)

    def proj(c0, c1):
        return _dot(hb, wr_ref[:, c0:c1])

    def proj_t(r0, r1):
        return _dot_nt(wt_ref[r0:r1, :], hb)

    bd = bd_ref[...]
    bd2 = bd_ref[0:LANES, 0:LANES]
    lane = lax.broadcasted_iota(jnp.int32, (tm, LANES), 1)
    row = lax.broadcasted_iota(jnp.int32, (tm, LANES), 0) + ti * tm
    low = lane < HEAD_DIM

    def k_in_low(pair, odd):
        return pltpu.roll(pair, HEAD_DIM, 1) if odd else pair

    for hd, qh in enumerate(_head_norm_t(proj_t(0, 512), gqa_ref[...])):
        qa_ref[0, hd] = qh

    def put_tiles(ref, vt):
        for i in range(tm // TQ):
            ref[0, i] = vt[:, i * TQ:(i + 1) * TQ].astype(BF16)

    put_tiles(va_ref, proj_t(512, 1024))
    for hd, qh in enumerate(_head_norm_t(proj_t(1024, 1536), gqb_ref[...])):
        qb_ref[0, hd] = qh
    put_tiles(vs_ref, proj_t(1536, 1664))
    put_tiles(vw_ref, proj_t(1664, 1792))
    gl = jax.nn.sigmoid(proj_t(1792, 1792 + G_B * GATE_ROWS))
    for g in range(G_B):
        gates_ref[0, g] = gl[g * GATE_ROWS:(g + 1) * GATE_ROWS]

    ka = _head_norm(proj(0, 512), bd, gka_ref[...])
    oh_a = jnp.where(lane - HEAD_DIM == (row >> _LOG2_BLK_A), 1.0, 0.0)
    for hd in range(H_A):
        pair = ka[:, (hd // 2) * LANES:(hd // 2 + 1) * LANES]
        kaug_ref[0, hd] = jnp.where(low, k_in_low(pair, hd % 2), oh_a).astype(BF16)
    for i in range(tm // BLK_A):
        kmean_ref[0, i] = jnp.mean(ka[i * BLK_A:(i + 1) * BLK_A], axis=0, keepdims=True)

    kc_ref[0] = proj(512, 640).astype(BF16)
    vc_ref[0] = proj(640, 768).astype(BF16)

    ks = _head_norm(proj(768, 896), bd2, gks_ref[...])
    kw = _head_norm(proj(896, 1024), bd2, gkw_ref[...])
    oh_s = jnp.where(lane == (row >> _LOG2_SEL_BLK), 1.0, 0.0).astype(BF16)
    for g in range(G_B):
        ksaug_ref[0, g] = jnp.concatenate(
            [jnp.where(low, k_in_low(ks, g), 0.0).astype(BF16), oh_s], axis=1)
        kw_ref[0, g] = jnp.where(low, k_in_low(kw, g), 0.0).astype(BF16)


def _inproj(x, sc, sh, gmix, wr, wt, bd, gqa, gka, gqb, gks, gkw):
    bsz, s, d = x.shape
    tm = TM_IN
    nt = s // tm
    nba = s // BLK_A
    const2 = lambda b, t: (0, 0)
    tok3 = lambda b, t: (b, t, 0)
    tok4 = lambda b, t: (b, 0, t, 0)
    tile4 = lambda b, t: (b, t, 0, 0)
    tr4 = lambda b, t: (b, 0, 0, t)
    in_specs = [
        pl.BlockSpec((1, tm, d), tok3),
        pl.BlockSpec((1, 1, 1, d), lambda b, t: (b, 1, 0, 0)),
        pl.BlockSpec((1, 1, 1, d), lambda b, t: (b, 0, 0, 0)),
        pl.BlockSpec((1, d), const2),
        pl.BlockSpec(wr.shape, const2),
        pl.BlockSpec(wt.shape, const2),
        pl.BlockSpec((512, 512), const2),
        pl.BlockSpec((512, tm), const2),
        pl.BlockSpec((1, 512), const2),
        pl.BlockSpec((512, tm), const2),
        pl.BlockSpec((1, LANES), const2),
        pl.BlockSpec((1, LANES), const2),
    ]
    out_shape = [
        jax.ShapeDtypeStruct((bsz, H_A, HEAD_DIM, s), F32),
        jax.ShapeDtypeStruct((bsz, H_A, s, LANES), BF16),
        jax.ShapeDtypeStruct((bsz, s // TQ, 512, TQ), BF16),
        jax.ShapeDtypeStruct((bsz, nba, 1, 512), F32),
        jax.ShapeDtypeStruct((bsz, H_B, HEAD_DIM, s), F32),
        jax.ShapeDtypeStruct((bsz, s, LANES), BF16),
        jax.ShapeDtypeStruct((bsz, s, LANES), BF16),
        jax.ShapeDtypeStruct((bsz, G_B, s, 2 * LANES), BF16),
        jax.ShapeDtypeStruct((bsz, s // TQ, LANES, TQ), BF16),
        jax.ShapeDtypeStruct((bsz, G_B, s, LANES), BF16),
        jax.ShapeDtypeStruct((bsz, s // TQ, LANES, TQ), BF16),
        jax.ShapeDtypeStruct((bsz, G_B, GATE_ROWS, s), F32),
    ]
    out_specs = [
        pl.BlockSpec((1, H_A, HEAD_DIM, tm), tr4),
        pl.BlockSpec((1, H_A, tm, LANES), tok4),
        pl.BlockSpec((1, tm // TQ, 512, TQ), tile4),
        pl.BlockSpec((1, tm // BLK_A, 1, 512), lambda b, t: (b, t, 0, 0)),
        pl.BlockSpec((1, H_B, HEAD_DIM, tm), tr4),
        pl.BlockSpec((1, tm, LANES), tok3),
        pl.BlockSpec((1, tm, LANES), tok3),
        pl.BlockSpec((1, G_B, tm, 2 * LANES), tok4),
        pl.BlockSpec((1, tm // TQ, LANES, TQ), tile4),
        pl.BlockSpec((1, G_B, tm, LANES), tok4),
        pl.BlockSpec((1, tm // TQ, LANES, TQ), tile4),
        pl.BlockSpec((1, G_B, GATE_ROWS, tm), tr4),
    ]
    return pl.pallas_call(
        _inproj_kernel,
        grid=(bsz, nt),
        in_specs=in_specs,
        out_specs=out_specs,
        out_shape=out_shape,
        compiler_params=_params(("parallel", "parallel")),
        name="inproj",
    )(x, sc, sh, gmix, wr, wt, bd, gqa, gka, gqb, gks, gkw)


def _compress_kernel(ck_ref, cv_ref, w1_ref, w2t_ref, pe_ref, gk_ref, ok_ref, ov_ref):
    half = CMP_STRIDE * HEAD_DIM
    for kv, c_ref in enumerate((ck_ref, cv_ref)):
        for g in range(G_B):
            c = c_ref[0, g]
            a = _dot(c, w1_ref[kv, 0:half, :])
            b = _dot(c, w1_ref[kv, half:2 * half, :])
            n = a.shape[0]
            b_next = pltpu.roll(b, n - 1, 0)
            pe_term = _dot(pe_ref[kv], w1_ref[kv])[0:1]
            hid = jax.nn.gelu(a + b_next + pe_term).astype(BF16)
            if kv == 0:
                y = _dot_nt(hid, w2t_ref[kv])
                ms = jnp.mean(y * y, axis=1, keepdims=True)
                y = y * lax.rsqrt(ms + EPS) * gk_ref[...]
                ok_ref[0, g] = jnp.concatenate([y, jnp.zeros_like(y)], axis=1).astype(BF16)
            else:
                ov_ref[0, g] = _dot_nt(w2t_ref[kv], hid).astype(BF16)


def _compress(ck, cv, w1, w2t, pe, gk):
    bsz, g, n, width = ck.shape
    blk = pl.BlockSpec((1, g, n, width), lambda b: (b, 0, 0, 0))
    full = lambda a: pl.BlockSpec(a.shape, lambda b: (0,) * a.ndim)
    return pl.pallas_call(
        _compress_kernel,
        grid=(bsz,),
        in_specs=[blk, blk, full(w1), full(w2t), full(pe), full(gk)],
        out_specs=[pl.BlockSpec((1, g, n, LANES), lambda b: (b, 0, 0, 0)),
                   pl.BlockSpec((1, g, HEAD_DIM, n), lambda b: (b, 0, 0, 0))],
        out_shape=[jax.ShapeDtypeStruct((bsz, g, n, LANES), BF16),
                   jax.ShapeDtypeStruct((bsz, g, HEAD_DIM, n), BF16)],
        compiler_params=_params(("parallel",)),
        name="compress",
    )(ck, cv, w1, w2t, pe, gk)


def _with_ones(vt):
    return jnp.concatenate([vt, jnp.ones((BF16_ROWS, vt.shape[1]), BF16)], axis=0)


def _col_max(s):
    while s.shape[0] > 8:
        half = s.shape[0] // 2
        s = jnp.maximum(s[0:half], s[half:2 * half])
    return jnp.max(s, axis=0, keepdims=True)


def _flash_update(carries, chains):
    m_news = []
    for carry, tiles in zip(carries, chains):
        tops = []
        for s, _, bias, top in tiles:
            top = _col_max(s) if top is None else top
            tops.append(top if bias is None else top + bias)
        m_news.append(functools.reduce(jnp.maximum, tops if carry is None else tops + [carry[0]]))
    pvs = [None] * len(chains)
    for t in range(max(len(tiles) for tiles in chains)):
        for c, tiles in enumerate(chains):
            if t < len(tiles):
                s, vt, bias, _ = tiles[t]
                p = jnp.exp2(s - (m_news[c] if bias is None else m_news[c] - bias)).astype(BF16)
                part = _dot(_with_ones(vt), p)
                pvs[c] = part if pvs[c] is None else pvs[c] + part
    outs = []
    for carry, m_new, pv in zip(carries, m_news, pvs):
        outs.append((m_new, pv if carry is None else jnp.exp2(carry[0] - m_new) * carry[1] + pv))
    return outs


def _flash_out(acc):
    return acc[0:HEAD_DIM] / acc[HEAD_DIM:HEAD_DIM + 1]


def _topk_rows(scores, index, k):
    scores = list(scores)
    picked = [jnp.zeros(sc.shape, F32) for sc in scores]
    for _ in range(k):
        mx = [jnp.max(sc, axis=0, keepdims=True) for sc in scores]
        cand = [jnp.where(sc == m, index, jnp.int32(1 << 20)) for sc, m in zip(scores, mx)]
        first = [jnp.min(c, axis=0, keepdims=True) for c in cand]
        hit = [index == f for f in first]
        picked = [jnp.where(h, 1.0, p) for h, p in zip(hit, picked)]
        scores = [jnp.where(h, -jnp.inf, sc) for h, sc in zip(hit, scores)]
    return picked


def _tile_rows(j):
    return pl.ds(pl.multiple_of(j * TQ, TQ), TQ)


def _moba_gate_kernel(q_ref, km_ref, o_ref, *, n_sel):
    nh, tg = q_ref.shape[1], q_ref.shape[3]
    t0 = pl.program_id(2) * tg
    shape = (km_ref.shape[2], tg)
    blk = lax.broadcasted_iota(jnp.int32, shape, 0)
    own = (lax.broadcasted_iota(jnp.int32, shape, 1) + t0) >> _LOG2_BLK_A
    valid = blk < own
    qs, gates = [], []
    for hh in range(nh):
        q = q_ref[0, hh]
        gate = _dot3(km_ref[0, hh], jnp.concatenate([q, jnp.zeros_like(q)], axis=0))
        qs.append(q)
        gates.append(jnp.where(valid, gate, -jnp.inf))
    for hh, picked in enumerate(_topk_rows(gates, blk, n_sel)):
        keep = jnp.where(valid, picked, 0.0) + jnp.where(blk == own, 1.0, 0.0)
        selb = jnp.where(keep > 0.0, 0.0, NEG)
        o_ref[0, hh] = jnp.concatenate([qs[hh], selb], axis=0).astype(BF16)


def _moba_gate(qa, km):
    bsz, nh, _, s = qa.shape
    tg = min(s, 2048)
    hpb = 2
    n_sel = max(1, min(TOPK_A, s // BLK_A - 1))
    return pl.pallas_call(
        functools.partial(_moba_gate_kernel, n_sel=n_sel),
        grid=(bsz, nh // hpb, s // tg),
        in_specs=[pl.BlockSpec((1, hpb, HEAD_DIM, tg), lambda b, h, t: (b, h, 0, t)),
                  pl.BlockSpec((1, hpb, HEAD_DIM, LANES), lambda b, h, t: (b, h, 0, 0))],
        out_specs=pl.BlockSpec((1, hpb, LANES, tg), lambda b, h, t: (b, h, 0, t)),
        out_shape=jax.ShapeDtypeStruct((bsz, nh, LANES, s), BF16),
        compiler_params=_params(("parallel", "parallel", "parallel")),
        name="moba_gate",
    )(qa, km)


def _far_tiles(qi):
    n_far = jnp.maximum(qi - 1, 0)
    left = jnp.maximum(n_far - 1, 0)
    off_left = jnp.where((n_far & 1) == 1, 0.0, NEG)
    return n_far >> 1, left, off_left


class _FarLoop:
    def __init__(self, n_pairs, heads, qk_tile, values, cbs, sa_ref, sb_ref, m_ref, acc_ref):
        self.n_pairs, self.heads, self.qk_tile, self.values, self.cbs = n_pairs, heads, qk_tile, values, cbs
        self.sa_ref, self.sb_ref, self.m_ref, self.acc_ref = sa_ref, sb_ref, m_ref, acc_ref
        self.last = jnp.maximum(n_pairs - 1, 0)

    def fetch(self, buf_ref, h, i):
        s_lo = self.qk_tile(h, 2 * i)
        s_hi = self.qk_tile(h, 2 * i + 1)
        buf_ref[h, 0:TQ, :] = s_lo
        buf_ref[h, TQ:2 * TQ, :] = s_hi
        return _col_max(s_lo), _col_max(s_hi)

    def consume(self, buf_ref, h, top, i):
        (m, acc), = _flash_update([(self.m_ref[h], self.acc_ref[h])], [[
            (buf_ref[h, 0:TQ, :], self.values(h, 2 * i), self.cbs[h], top[0]),
            (buf_ref[h, TQ:2 * TQ, :], self.values(h, 2 * i + 1), self.cbs[h], top[1])]])
        self.m_ref[h] = m
        self.acc_ref[h] = acc

    def first(self):
        return tuple(self.fetch(self.sa_ref, h, 0) for h in self.heads)

    def run(self, tops_first):
        def two_pairs(ia, tops_a):
            tops_b, tops_next = [], []
            for h in self.heads:
                tops_b.append(self.fetch(self.sb_ref, h, ia + 1))
                self.consume(self.sa_ref, h, tops_a[h], ia)
            for h in self.heads:
                tops_next.append(self.fetch(self.sa_ref, h, jnp.minimum(ia + 2, self.last)))
                self.consume(self.sb_ref, h, tops_b[h], ia + 1)
            return tuple(tops_next)

        def pairs(n, start, tops):
            for k in range(0, n, 2):
                tops = two_pairs(start + k, tops)
            return tops

        trip = FAR_PAIRS_PER_TRIP
        n_trips = self.n_pairs // trip
        tops = lax.fori_loop(0, n_trips, lambda q, t: pairs(trip, trip * q, t), tops_first)
        done = trip * n_trips
        n = trip // 2
        while n >= 2:
            has = ((self.n_pairs // n) & 1) == 1
            tops = lax.cond(has, functools.partial(pairs, n, done), lambda t: t, tops)
            done = done + jnp.where(has, n, 0)
            n //= 2
        tops_last = tops

        @pl.when((self.n_pairs & 1) == 1)
        def _():
            for h in self.heads:
                self.consume(self.sa_ref, h, tops_last[h], self.last)


def _moba_kernel(q_ref, k_ref, vt_ref, tab_ref, cb_ref, o_ref, sa_ref, sb_ref, m_ref, acc_ref):
    qi = pl.program_id(2)
    jp = jnp.maximum(qi - 1, 0)
    off_p = jnp.where(qi >= 1, 0.0, NEG)
    n_pairs, jl, off_l = _far_tiles(qi)
    heads = range(MOBA_HEADS)
    qaug = [q_ref[0, hh] for hh in heads]
    cbs = [cb_ref[hh][0:1, 0:1] for hh in heads]

    def scores(hh, j):
        return _dot(k_ref[0, hh, _tile_rows(j), :], qaug[hh])

    def values(hh, j):
        return vt_ref[0, j, hh * HEAD_DIM:(hh + 1) * HEAD_DIM, :]

    far = _FarLoop(n_pairs, heads, scores, values, cbs, sa_ref, sb_ref, m_ref, acc_ref)

    s_own = [scores(hh, qi) for hh in heads]
    s_prev = [scores(hh, jp) for hh in heads]
    s_left = [scores(hh, jl) for hh in heads]
    tops_first = far.first()
    chains = [[(s_own[hh] + tab_ref[hh, TQ:2 * TQ, :], values(hh, qi), None, None),
               (s_prev[hh] + tab_ref[hh, 0:TQ, :] + off_p, values(hh, jp), None, None),
               (s_left[hh], values(hh, jl), cbs[hh] + off_l, None)] for hh in heads]
    for hh, (m, acc) in zip(heads, _flash_update([None] * len(chains), chains)):
        m_ref[hh] = m
        acc_ref[hh] = acc
    far.run(tops_first)
    out_t = jnp.concatenate([_flash_out(acc_ref[hh]) for hh in heads], axis=0)
    o_ref[0] = out_t.T.astype(BF16)


def _flash_scratch(n_heads):
    return [pltpu.VMEM((n_heads, 2 * TQ, TQ), F32),
            pltpu.VMEM((n_heads, 2 * TQ, TQ), F32),
            pltpu.VMEM((n_heads, 1, TQ), F32),
            pltpu.VMEM((n_heads, ACC_ROWS, TQ), F32)]


def _moba(qaug, kaug, vat, tab, cb):
    bsz, _, _, s = qaug.shape
    nq = s // TQ
    nh = MOBA_HEADS
    return pl.pallas_call(
        _moba_kernel,
        grid=(bsz, H_A // nh, nq),
        in_specs=[
            pl.BlockSpec((1, nh, LANES, TQ), lambda b, hp, qi: (b, hp, 0, qi)),
            pl.BlockSpec((1, nh, s, LANES), lambda b, hp, qi: (b, hp, 0, 0)),
            pl.BlockSpec((1, nq, nh * HEAD_DIM, TQ), lambda b, hp, qi: (b, 0, hp, 0)),
            pl.BlockSpec((nh, 2 * TQ, TQ), lambda b, hp, qi: (hp, 0, 0)),
            pl.BlockSpec((nh, 8, LANES), lambda b, hp, qi: (hp, 0, 0)),
        ],
        out_specs=pl.BlockSpec((1, TQ, nh * HEAD_DIM), lambda b, hp, qi: (b, qi, hp)),
        out_shape=jax.ShapeDtypeStruct((bsz, s, H_A * HEAD_DIM), BF16),
        scratch_shapes=_flash_scratch(nh),
        compiler_params=_params(("parallel", "parallel", "arbitrary")),
        name="moba",
    )(qaug, kaug, vat, tab, cb)


def _nsa_cmp_kernel(q_ref, kc_ref, vct_ref, ovt_ref, oc_ref, selb_ref, *, n_sel, n_cmp, n_parts):
    qi = pl.program_id(1)
    ncp = kc_ref.shape[2]
    tq = q_ref.shape[3]
    t0 = qi * tq
    zeros = jnp.zeros((HEAD_DIM, tq), BF16)
    qs = [jnp.concatenate([q_ref[0, hd].astype(BF16), zeros], axis=0) for hd in range(H_B)]

    def body(nk, nb):
        n_idx = lax.broadcasted_iota(jnp.int32, (nk, tq), 0)
        t_idx = lax.broadcasted_iota(jnp.int32, (nk, tq), 1) + t0
        mask = (n_idx * CMP_STRIDE + (CMP_LEN - 1) <= t_idx) & (n_idx < n_cmp)
        any_key = t_idx[0:1] >= CMP_LEN - 1
        blk = lax.broadcasted_iota(jnp.int32, (nb, tq), 0)
        cur = (lax.broadcasted_iota(jnp.int32, (nb, tq), 1) + t0) >> _LOG2_SEL_BLK
        ok = blk <= cur
        forced = (blk == 0) | (blk == cur) | (blk == cur - 1)
        ovt = ovt_ref[0:nb, 0:nk]
        scores = []
        for g in range(G_B):
            kc = kc_ref[0, g, 0:nk, :]
            vct = _with_ones(vct_ref[0, g, :, 0:nk])
            psum = jnp.zeros((nk, tq), F32)
            for hd in range(g * R_B, (g + 1) * R_B):
                z = jnp.where(mask, _dot(kc, qs[hd]), NEG)
                e = jnp.exp2(z - _col_max(z))
                acc = _dot(vct, e.astype(BF16))
                rinv = jnp.where(any_key, 1.0 / acc[HEAD_DIM:HEAD_DIM + 1], 0.0)
                oc_ref[0, hd] = acc[0:HEAD_DIM] * rinv
                psum = psum + e * rinv
            ph, pl_ = _split(psum)
            imp_t = _dot(ovt, ph) + _dot(ovt, pl_)
            scores.append(jnp.where(ok, jnp.where(forced, BIG, imp_t), -jnp.inf))
        for g, picked in enumerate(_topk_rows(scores, blk, n_sel)):
            selb_ref[0, g, 0:nb, :] = jnp.where(ok & (picked > 0.0), 0.0, NEG).astype(BF16)
            if nb < LANES:
                selb_ref[0, g, nb:LANES, :] = jnp.full((LANES - nb, tq), NEG, BF16)

    part = ncp // n_parts
    need = jnp.minimum(((qi + 1) * (tq // CMP_STRIDE) + part - 1) // part, n_parts)
    for v in range(1, n_parts + 1):
        pl.when(need == v)(functools.partial(body, v * part, min(LANES, v * part * CMP_STRIDE // SEL_BLK)))


def _cmp_parts(ncp):
    return 4 if ncp % (4 * LANES) == 0 else 1


def _nsa_cmp(qb, kcmp, vcmpt, ovt, n_cmp):
    bsz, _, _, s = qb.shape
    tq = CMP_TQ if s % CMP_TQ == 0 else TQ
    nq = s // tq
    ncp = kcmp.shape[2]
    n_sel = min(SEL_TOPK, s // SEL_BLK)
    n_parts = _cmp_parts(ncp)
    return pl.pallas_call(
        functools.partial(_nsa_cmp_kernel, n_sel=n_sel, n_cmp=n_cmp, n_parts=n_parts),
        grid=(bsz, nq),
        in_specs=[
            pl.BlockSpec((1, H_B, HEAD_DIM, tq), lambda b, qi: (b, 0, 0, qi)),
            pl.BlockSpec((1, G_B, ncp, LANES), lambda b, qi: (b, 0, 0, 0)),
            pl.BlockSpec((1, G_B, HEAD_DIM, ncp), lambda b, qi: (b, 0, 0, 0)),
            pl.BlockSpec((LANES, ncp), lambda b, qi: (0, 0)),
        ],
        out_specs=[
            pl.BlockSpec((1, H_B, HEAD_DIM, tq), lambda b, qi: (b, 0, 0, qi)),
            pl.BlockSpec((1, G_B, LANES, tq), lambda b, qi: (b, 0, 0, qi)),
        ],
        out_shape=[jax.ShapeDtypeStruct((bsz, H_B, HEAD_DIM, s), F32),
                   jax.ShapeDtypeStruct((bsz, G_B, LANES, s), BF16)],
        compiler_params=_params(("parallel", "parallel")),
        name="nsa_cmp",
    )(qb, kcmp, vcmpt, ovt)


def _nsa_main_kernel(q_ref, selb_ref, ks_ref, vst_ref, kw_ref, vwt_ref, oc_ref, gates_ref,
                     tabs_ref, tabw_ref, cb_ref, o_ref, sa_ref, sb_ref, m_ref, acc_ref):
    qi = pl.program_id(2)
    selb = selb_ref[0, 0]
    gates = gates_ref[0, 0]
    zeros = jnp.zeros((HEAD_DIM, TQ), BF16)
    j1 = jnp.maximum(qi - 1, 0)
    j2 = jnp.maximum(qi - 2, 0)
    off1 = jnp.where(qi >= 1, 0.0, NEG)
    off2 = jnp.where(qi >= 2, 0.0, NEG)
    n_pairs, jl, off_l = _far_tiles(qi)

    heads = range(R_B)
    qw = [jnp.concatenate([q_ref[0, r].astype(BF16), zeros], axis=0) for r in heads]
    qs = [jnp.concatenate([qw[r], selb], axis=0) for r in heads]
    cbs = [cb_ref[r][0:1, 0:1] for r in heads]

    def s_sel(r, j):
        return _dot(ks_ref[0, 0, _tile_rows(j), :], qs[r])

    def s_win(r, j):
        return _dot(kw_ref[0, 0, _tile_rows(j), :], qw[r])

    far = _FarLoop(n_pairs, heads, s_sel, lambda r, j: vst_ref[0, j], cbs, sa_ref, sb_ref, m_ref, acc_ref)

    sw = [[s_win(r, j) for r in heads] for j in (qi, j1, j2)]
    ss = [[s_sel(r, j) for r in heads] for j in (qi, j1, jl)]
    tops_first = far.first()

    chains = [[(sw[0][r] + tabw_ref[r, 2 * TQ:3 * TQ, :], vwt_ref[0, qi], None, None),
               (sw[1][r] + tabw_ref[r, TQ:2 * TQ, :] + off1, vwt_ref[0, j1], None, None),
               (sw[2][r] + tabw_ref[r, 0:TQ, :] + off2, vwt_ref[0, j2], None, None)] for r in heads]
    chains += [[(ss[0][r] + tabs_ref[r, TQ:2 * TQ, :], vst_ref[0, qi], None, None),
                (ss[1][r] + tabs_ref[r, 0:TQ, :] + off1, vst_ref[0, j1], None, None),
                (ss[2][r], vst_ref[0, jl], cbs[r] + off_l, None)] for r in heads]
    done = _flash_update([None] * len(chains), chains)
    win = [_flash_out(acc) for _, acc in done[:R_B]]
    for r, (m, acc) in zip(heads, done[R_B:]):
        m_ref[r] = m
        acc_ref[r] = acc
    far.run(tops_first)

    outs = [gates[3 * r:3 * r + 1] * oc_ref[0, r]
            + gates[3 * r + 1:3 * r + 2] * _flash_out(acc_ref[r])
            + gates[3 * r + 2:3 * r + 3] * win[r] for r in heads]
    o_ref[0] = jnp.concatenate(outs, axis=0).T.astype(BF16)


def _nsa_main(qb, selb, ksaug, vst, kw, vwt, oc, gates, tabs, tabw, cb):
    bsz, _, _, s = qb.shape
    nq = s // TQ
    return pl.pallas_call(
        _nsa_main_kernel,
        grid=(bsz, G_B, nq),
        in_specs=[
            pl.BlockSpec((1, R_B, HEAD_DIM, TQ), lambda b, g, qi: (b, g, 0, qi)),
            pl.BlockSpec((1, 1, LANES, TQ), lambda b, g, qi: (b, g, 0, qi)),
            pl.BlockSpec((1, 1, s, 2 * LANES), lambda b, g, qi: (b, g, 0, 0)),
            pl.BlockSpec((1, nq, HEAD_DIM, TQ), lambda b, g, qi: (b, 0, g, 0)),
            pl.BlockSpec((1, 1, s, LANES), lambda b, g, qi: (b, g, 0, 0)),
            pl.BlockSpec((1, nq, HEAD_DIM, TQ), lambda b, g, qi: (b, 0, g, 0)),
            pl.BlockSpec((1, R_B, HEAD_DIM, TQ), lambda b, g, qi: (b, g, 0, qi)),
            pl.BlockSpec((1, 1, GATE_ROWS, TQ), lambda b, g, qi: (b, g, 0, qi)),
            pl.BlockSpec((R_B, 2 * TQ, TQ), lambda b, g, qi: (g, 0, 0)),
            pl.BlockSpec((R_B, 3 * TQ, TQ), lambda b, g, qi: (g, 0, 0)),
            pl.BlockSpec((R_B, 8, LANES), lambda b, g, qi: (g, 0, 0)),
        ],
        out_specs=pl.BlockSpec((1, TQ, R_B * HEAD_DIM), lambda b, g, qi: (b, qi, g)),
        out_shape=jax.ShapeDtypeStruct((bsz, s, H_B * HEAD_DIM), BF16),
        scratch_shapes=_flash_scratch(R_B),
        compiler_params=_params(("parallel", "parallel", "arbitrary")),
        name="nsa_main",
    )(qb, selb, ksaug, vst, kw, vwt, oc, gates, tabs, tabw, cb)


def _out_ffn_kernel(x_ref, oa_ref, ob_ref, wo_ref, gtm_ref, gffn_ref, scf_ref, shf_ref, gtf_ref,
                    wg_ref, wu_ref, wd_ref, o_ref, x1_ref, h_ref, acc_ref):
    f = pl.program_id(2)
    half = oa_ref.shape[2]

    @pl.when(f == 0)
    def _():
        mix = _dot(oa_ref[0], wo_ref[0:half, :]) + _dot(ob_ref[0], wo_ref[half:2 * half, :])
        x1 = x_ref[0] + gtm_ref[0, 0] * mix
        x1_ref[...] = x1
        h = _rms_rows(x1, gffn_ref[...]) * (1.0 + scf_ref[0, 0]) + shf_ref[0, 0]
        h_ref[...] = h.astype(BF16)

    h = h_ref[...]
    gate = _dot(h, wg_ref[...])
    up = _dot(h, wu_ref[...])
    act = (gate * jax.nn.sigmoid(gate) * up).astype(BF16)
    part = _dot(act, wd_ref[...])

    @pl.when(f == 0)
    def _():
        acc_ref[...] = part

    @pl.when(f > 0)
    def _():
        acc_ref[...] += part

    @pl.when(f == pl.num_programs(2) - 1)
    def _():
        o_ref[0] = x1_ref[...] + gtf_ref[0, 0] * acc_ref[...]


def _out_ffn(x, oa, ob, wo, mod4, gffn, wgu, wd):
    bsz, s, d = x.shape
    fh = wd.shape[0]
    tm = TM_FFN
    tf = fh // 2 if (fh // 2) % LANES == 0 else fh
    nf = fh // tf
    tok = lambda b, t, f: (b, t, 0)
    modspec = lambda k: pl.BlockSpec((1, 1, 1, d), lambda b, t, f: (b, k, 0, 0))
    return pl.pallas_call(
        _out_ffn_kernel,
        grid=(bsz, s // tm, nf),
        in_specs=[
            pl.BlockSpec((1, tm, d), tok),
            pl.BlockSpec((1, tm, oa.shape[2]), tok),
            pl.BlockSpec((1, tm, ob.shape[2]), tok),
            pl.BlockSpec((d, d), lambda b, t, f: (0, 0)),
            modspec(2),
            pl.BlockSpec((1, d), lambda b, t, f: (0, 0)),
            modspec(4),
            modspec(3),
            modspec(5),
            pl.BlockSpec((d, tf), lambda b, t, f: (0, f)),
            pl.BlockSpec((d, tf), lambda b, t, f: (0, f + nf)),
            pl.BlockSpec((tf, d), lambda b, t, f: (f, 0)),
        ],
        out_specs=pl.BlockSpec((1, tm, d), tok),
        out_shape=jax.ShapeDtypeStruct((bsz, s, d), F32),
        scratch_shapes=[pltpu.VMEM((tm, d), F32), pltpu.VMEM((tm, d), BF16), pltpu.VMEM((tm, d), F32)],
        compiler_params=_params(("parallel", "parallel", "arbitrary")),
        name="out_ffn",
    )(x, oa, ob, wo, mod4, gffn, mod4, mod4, mod4, wgu, wgu, wd)


def _t5_bucket_np(d):
    max_exact = N_BUCKETS // 2
    d = np.maximum(d, 0)
    df = np.maximum(d, 1).astype(np.float64)
    large = max_exact + (np.log(df / max_exact) / math.log(MAX_DIST / max_exact)
                         * (N_BUCKETS - max_exact)).astype(np.int64)
    large = np.minimum(large, N_BUCKETS - 1)
    return np.where(d < max_exact, d, large).astype(np.int32)


def _bias_expand_kernel(tab_ref, bucket_ref, o_ref):
    hd = pl.program_id(0)
    bucket = bucket_ref[...]
    acc = jnp.full(bucket.shape, NEG, F32)
    for b in range(N_BUCKETS):
        acc = jnp.where(bucket == b, tab_ref[hd, b], acc)
    o_ref[0] = acc


def _bias_expand(tab, bucket):
    nh = tab.shape[0]
    return pl.pallas_call(
        _bias_expand_kernel,
        grid=(nh,),
        in_specs=[pl.BlockSpec(memory_space=pltpu.SMEM),
                  pl.BlockSpec(bucket.shape, lambda h: (0, 0))],
        out_specs=pl.BlockSpec((1,) + bucket.shape, lambda h: (h, 0, 0)),
        out_shape=jax.ShapeDtypeStruct((nh,) + bucket.shape, F32),
        compiler_params=_params(("parallel",)),
        name="bias_expand",
    )(tab, jnp.asarray(bucket, jnp.int32))


def _bias_tables(rel_bias):
    tab = rel_bias.T.astype(F32) * LOG2_E
    i = np.arange(TQ)[None, :]
    d_near = i + TQ - np.arange(2 * TQ)[:, None]
    near = _bias_expand(tab, np.where(d_near >= 0, _t5_bucket_np(d_near), -1))
    d_win = i + 2 * TQ - np.arange(3 * TQ)[:, None]
    ok_win = (d_win >= 0) & (d_win < WINDOW)
    win = _bias_expand(tab[H_A:], np.where(ok_win, _t5_bucket_np(d_win), -1))
    far = jnp.broadcast_to(tab[:, N_BUCKETS - 1][:, None, None], (tab.shape[0], 8, LANES))
    return near, win, far


def _overlap_t(n_cmp_pad, n_cmp):
    cs = np.arange(n_cmp_pad)[None, :] * CMP_STRIDE
    ss = np.arange(LANES)[:, None] * SEL_BLK
    ov = (cs < ss + SEL_BLK) & (cs + CMP_LEN > ss) & (np.arange(n_cmp_pad)[None, :] < n_cmp)
    return jnp.asarray(ov.astype(np.float32), BF16)


def _block_diag(n):
    m = (np.arange(n)[:, None] // HEAD_DIM == np.arange(n)[None, :] // HEAD_DIM)
    return jnp.asarray(m.astype(np.float32) / HEAD_DIM, BF16)


def kernel(x, c, rel_bias, w_ada, b_ada, g_mix, w_in, q_norm_a, k_norm_a, q_norm_b, k_norm_cmp,
           k_norm_sel, k_norm_win, cmp_pe_k, cmp_w1_k, cmp_w2_k, cmp_pe_v, cmp_w1_v, cmp_w2_v,
           w_out, g_ffn, w_gu, w_down):
    bsz, s, d = x.shape
    depth = w_ada.shape[0]
    assert s % TM_IN == 0 and s % TM_FFN == 0 and s % (2 * TQ) == 0
    assert s // BLK_A <= HEAD_DIM and s // SEL_BLK <= LANES
    assert WINDOW == 2 * TQ and BLK_A == TQ and MAX_DIST <= TQ
    n_chunks = s // CMP_STRIDE
    n_cmp = (s - CMP_LEN) // CMP_STRIDE + 1
    scale = HEAD_DIM ** -0.5 * LOG2_E
    hd = HEAD_DIM

    near, win, far = _bias_tables(rel_bias)
    ovt = _overlap_t(n_chunks, n_cmp)
    bd = _block_diag(512)
    tile = lambda g, n: jnp.tile(g.astype(F32), n).reshape(1, -1)
    tile_t = lambda g, n: jnp.broadcast_to(jnp.tile(g.astype(F32), n)[:, None], (n * hd, TM_IN))

    for l in range(depth):
        mod = _ada(c, w_ada[l], b_ada[l])
        mod4 = mod.reshape(bsz, ADA_CHUNKS, 1, d)

        wl = w_in[l]
        cols = np.cumsum([0, H_A * hd, H_A * hd, H_A * hd, H_B * hd] + [G_B * hd] * 6)
        qa_c, ka_c, va_c, qb_c, kc_c, vc_c, ks_c, vs_c, kw_c, vw_c = [
            wl[:, int(a):int(b)] for a, b in zip(cols[:-1], cols[1:])]
        gl = wl[:, int(cols[-1]):].reshape(d, G_B, 3 * R_B)
        gl = jnp.pad(gl, ((0, 0), (0, 0), (0, GATE_ROWS - 3 * R_B))).reshape(d, G_B * GATE_ROWS)
        w_rows = jnp.concatenate([ka_c, kc_c, vc_c, ks_c, kw_c], axis=1).astype(BF16)
        w_t = jnp.concatenate([qa_c, va_c, qb_c, vs_c, vw_c, gl], axis=1).T.astype(BF16)

        (qa, kaug, vat, kmean, qb, kc, vc, ksaug, vst, kw, vwt, gates) = _inproj(
            x, mod4, mod4, g_mix[l].reshape(1, d), w_rows, w_t, bd,
            tile_t(q_norm_a[l], H_A) * scale, tile(k_norm_a[l], H_A), tile_t(q_norm_b[l], H_B) * scale,
            tile(k_norm_sel[l], G_B), tile(k_norm_win[l], G_B))

        nba = s // BLK_A
        km = kmean.reshape(bsz, nba, H_A, hd).transpose(0, 2, 1, 3)
        km = jnp.pad(km, ((0, 0), (0, 0), (0, hd - nba), (0, LANES - hd)))
        o_a = _moba(_moba_gate(qa, km), kaug, vat, near[:H_A], far[:H_A])

        chunks = lambda t: t.reshape(bsz, s, G_B, hd).transpose(0, 2, 1, 3).reshape(
            bsz, G_B, n_chunks, CMP_STRIDE * hd)
        w1 = jnp.stack([cmp_w1_k[l], cmp_w1_v[l]]).astype(BF16)
        w2t = jnp.stack([cmp_w2_k[l].T, cmp_w2_v[l].T]).astype(BF16)
        pe = jnp.stack([cmp_pe_k[l], cmp_pe_v[l]]).reshape(2, 1, CMP_LEN * hd)
        pe = jnp.broadcast_to(pe, (2, 8, CMP_LEN * hd)).astype(BF16)
        kcmp, vcmpt = _compress(chunks(kc), chunks(vc), w1, w2t, pe,
                                k_norm_cmp[l].astype(F32).reshape(1, hd))

        oc, selb = _nsa_cmp(qb, kcmp, vcmpt, ovt, n_cmp)
        o_b = _nsa_main(qb, selb, ksaug, vst, kw, vwt, oc, gates, near[H_A:], win, far[H_A:])

        x = _out_ffn(x, o_a, o_b, w_out[l].astype(BF16), mod4, g_ffn[l].reshape(1, d),
                     w_gu[l].astype(BF16), w_down[l].astype(BF16))
    return x
```

```python
import functools
import math

import jax
import jax.numpy as jnp
import numpy as np
from jax import lax
from jax.experimental import pallas as pl
from jax.experimental.pallas import tpu as pltpu

F32 = jnp.float32
BF16 = jnp.bfloat16

HEAD_DIM = 64
LANES = 128
BF16_ROWS = 16
H_A = 8
H_B = 8
G_B = 2
R_B = H_B // G_B
BLK_A = 256
TOPK_A = 3
CMP_LEN = 32
CMP_STRIDE = 16
CMP_HIDDEN = 256
SEL_BLK = 64
SEL_TOPK = 16
WINDOW = 512
N_BUCKETS = 32
MAX_DIST = 128
ADA_CHUNKS = 6
NEG = -1e30
BIG = 1e9
EPS = 1e-6
LOG2_E = math.log2(math.e)

_LOG2_BLK_A = BLK_A.bit_length() - 1
_LOG2_SEL_BLK = SEL_BLK.bit_length() - 1

TQ = 256
TM_IN = 1024
TM_FFN = 512
CMP_TQ = 1024
FAR_PAIRS_PER_TRIP = 8
MOBA_HEADS = 4
GATE_ROWS = 16
ACC_ROWS = HEAD_DIM + BF16_ROWS
VMEM_LIMIT = 56 * 1024 * 1024


def _dot(a, b):
    return jnp.dot(a, b, preferred_element_type=F32)


def _dot_nt(a, b):
    return lax.dot_general(a, b, (((1,), (1,)), ((), ())), preferred_element_type=F32)


def _split(a):
    hi = a.astype(BF16)
    lo = (a - hi.astype(F32)).astype(BF16)
    return hi, lo


def _dot3(a, b):
    ah, al = _split(a)
    bh, bl = _split(b)
    return _dot(ah, bh) + (_dot(al, bh) + _dot(ah, bl))


def _params(sem):
    return pltpu.CompilerParams(dimension_semantics=sem, vmem_limit_bytes=VMEM_LIMIT)


def _ada_kernel(c_ref, w_ref, b_ref, o_ref):
    c = c_ref[...]
    o_ref[...] = _dot3(c * jax.nn.sigmoid(c), w_ref[...]) + b_ref[...]


def _ada(c, w, b):
    bsz, d = c.shape
    n = w.shape[1]
    tn = 512
    return pl.pallas_call(
        _ada_kernel,
        grid=(n // tn,),
        in_specs=[pl.BlockSpec((bsz, d), lambda j: (0, 0)),
                  pl.BlockSpec((d, tn), lambda j: (0, j)),
                  pl.BlockSpec((1, tn), lambda j: (0, j))],
        out_specs=pl.BlockSpec((bsz, tn), lambda j: (0, j)),
        out_shape=jax.ShapeDtypeStruct((bsz, n), F32),
        compiler_params=_params(("arbitrary",)),
        name="ada",
    )(c, w, b.reshape(1, n))


def _rms_rows(xf, g):
    ms = jnp.mean(xf * xf, axis=-1, keepdims=True)
    return xf * lax.rsqrt(ms + EPS) * g


def _head_norm(t, bd, gain):
    hi, lo = _split(t * t)
    ms = _dot(hi, bd) + _dot(lo, bd)
    return t * lax.rsqrt(ms + EPS) * gain


def _head_norm_t(t, gain):
    heads = []
    for hd in range(t.shape[0] // HEAD_DIM):
        th = t[hd * HEAD_DIM:(hd + 1) * HEAD_DIM]
        ms = jnp.mean(th * th, axis=0, keepdims=True)
        heads.append(th * lax.rsqrt(ms + EPS) * gain[hd * HEAD_DIM:(hd + 1) * HEAD_DIM])
    return heads


def _inproj_kernel(x_ref, sc_ref, sh_ref, gmix_ref, wr_ref, wt_ref, bd_ref, gqa_ref, gka_ref, gqb_ref,
                   gks_ref, gkw_ref,
                   qa_ref, kaug_ref, va_ref, kmean_ref, qb_ref, kc_ref, vc_ref, ksaug_ref,
                   vs_ref, kw_ref, vw_ref, gates_ref):
    tm = x_ref.shape[1]
    ti = pl.program_id(1)
    xf = x_ref[0]
    h = _rms_rows(xf, gmix_ref[...]) * (1.0 + sc_ref[0, 0]) + sh_ref[0, 0]
    hb = h.astype(BF16)

    def proj(c0, c1):
        return _dot(hb, wr_ref[:, c0:c1])

    def proj_t(r0, r1):
        return _dot_nt(wt_ref[r0:r1, :], hb)

    bd = bd_ref[...]
    bd2 = bd_ref[0:LANES, 0:LANES]
    lane = lax.broadcasted_iota(jnp.int32, (tm, LANES), 1)
    row = lax.broadcasted_iota(jnp.int32, (tm, LANES), 0) + ti * tm
    low = lane < HEAD_DIM

    def k_in_low(pair, odd):
        return pltpu.roll(pair, HEAD_DIM, 1) if odd else pair

    for hd, qh in enumerate(_head_norm_t(proj_t(0, 512), gqa_ref[...])):
        qa_ref[0, hd] = qh

    def put_tiles(ref, vt):
        for i in range(tm // TQ):
            ref[0, i] = vt[:, i * TQ:(i + 1) * TQ].astype(BF16)

    put_tiles(va_ref, proj_t(512, 1024))
    for hd, qh in enumerate(_head_norm_t(proj_t(1024, 1536), gqb_ref[...])):
        qb_ref[0, hd] = qh
    put_tiles(vs_ref, proj_t(1536, 1664))
    put_tiles(vw_ref, proj_t(1664, 1792))
    gl = jax.nn.sigmoid(proj_t(1792, 1792 + G_B * GATE_ROWS))
    for g in range(G_B):
        gates_ref[0, g] = gl[g * GATE_ROWS:(g + 1) * GATE_ROWS]

    ka = _head_norm(proj(0, 512), bd, gka_ref[...])
    oh_a = jnp.where(lane - HEAD_DIM == (row >> _LOG2_BLK_A), 1.0, 0.0)
    for hd in range(H_A):
        pair = ka[:, (hd // 2) * LANES:(hd // 2 + 1) * LANES]
        kaug_ref[0, hd] = jnp.where(low, k_in_low(pair, hd % 2), oh_a).astype(BF16)
    for i in range(tm // BLK_A):
        kmean_ref[0, i] = jnp.mean(ka[i * BLK_A:(i + 1) * BLK_A], axis=0, keepdims=True)

    kc_ref[0] = proj(512, 640).astype(BF16)
    vc_ref[0] = proj(640, 768).astype(BF16)

    ks = _head_norm(proj(768, 896), bd2, gks_ref[...])
    kw = _head_norm(proj(896, 1024), bd2, gkw_ref[...])
    oh_s = jnp.where(lane == (row >> _LOG2_SEL_BLK), 1.0, 0.0).astype(BF16)
    for g in range(G_B):
        ksaug_ref[0, g] = jnp.concatenate(
            [jnp.where(low, k_in_low(ks, g), 0.0).astype(BF16), oh_s], axis=1)
        kw_ref[0, g] = jnp.where(low, k_in_low(kw, g), 0.0).astype(BF16)


def _inproj(x, sc, sh, gmix, wr, wt, bd, gqa, gka, gqb, gks, gkw):
    bsz, s, d = x.shape
    tm = TM_IN
    nt = s // tm
    nba = s // BLK_A
    const2 = lambda b, t: (0, 0)
    tok3 = lambda b, t: (b, t, 0)
    tok4 = lambda b, t: (b, 0, t, 0)
    tile4 = lambda b, t: (b, t, 0, 0)
    tr4 = lambda b, t: (b, 0, 0, t)
    in_specs = [
        pl.BlockSpec((1, tm, d), tok3),
        pl.BlockSpec((1, 1, 1, d), lambda b, t: (b, 1, 0, 0)),
        pl.BlockSpec((1, 1, 1, d), lambda b, t: (b, 0, 0, 0)),
        pl.BlockSpec((1, d), const2),
        pl.BlockSpec(wr.shape, const2),
        pl.BlockSpec(wt.shape, const2),
        pl.BlockSpec((512, 512), const2),
        pl.BlockSpec((512, tm), const2),
        pl.BlockSpec((1, 512), const2),
        pl.BlockSpec((512, tm), const2),
        pl.BlockSpec((1, LANES), const2),
        pl.BlockSpec((1, LANES), const2),
    ]
    out_shape = [
        jax.ShapeDtypeStruct((bsz, H_A, HEAD_DIM, s), F32),
        jax.ShapeDtypeStruct((bsz, H_A, s, LANES), BF16),
        jax.ShapeDtypeStruct((bsz, s // TQ, 512, TQ), BF16),
        jax.ShapeDtypeStruct((bsz, nba, 1, 512), F32),
        jax.ShapeDtypeStruct((bsz, H_B, HEAD_DIM, s), F32),
        jax.ShapeDtypeStruct((bsz, s, LANES), BF16),
        jax.ShapeDtypeStruct((bsz, s, LANES), BF16),
        jax.ShapeDtypeStruct((bsz, G_B, s, 2 * LANES), BF16),
        jax.ShapeDtypeStruct((bsz, s // TQ, LANES, TQ), BF16),
        jax.ShapeDtypeStruct((bsz, G_B, s, LANES), BF16),
        jax.ShapeDtypeStruct((bsz, s // TQ, LANES, TQ), BF16),
        jax.ShapeDtypeStruct((bsz, G_B, GATE_ROWS, s), F32),
    ]
    out_specs = [
        pl.BlockSpec((1, H_A, HEAD_DIM, tm), tr4),
        pl.BlockSpec((1, H_A, tm, LANES), tok4),
        pl.BlockSpec((1, tm // TQ, 512, TQ), tile4),
        pl.BlockSpec((1, tm // BLK_A, 1, 512), lambda b, t: (b, t, 0, 0)),
        pl.BlockSpec((1, H_B, HEAD_DIM, tm), tr4),
        pl.BlockSpec((1, tm, LANES), tok3),
        pl.BlockSpec((1, tm, LANES), tok3),
        pl.BlockSpec((1, G_B, tm, 2 * LANES), tok4),
        pl.BlockSpec((1, tm // TQ, LANES, TQ), tile4),
        pl.BlockSpec((1, G_B, tm, LANES), tok4),
        pl.BlockSpec((1, tm // TQ, LANES, TQ), tile4),
        pl.BlockSpec((1, G_B, GATE_ROWS, tm), tr4),
    ]
    return pl.pallas_call(
        _inproj_kernel,
        grid=(bsz, nt),
        in_specs=in_specs,
        out_specs=out_specs,
        out_shape=out_shape,
        compiler_params=_params(("parallel", "parallel")),
        name="inproj",
    )(x, sc, sh, gmix, wr, wt, bd, gqa, gka, gqb, gks, gkw)


def _compress_kernel(ck_ref, cv_ref, w1_ref, w2t_ref, pe_ref, gk_ref, ok_ref, ov_ref):
    half = CMP_STRIDE * HEAD_DIM
    for kv, c_ref in enumerate((ck_ref, cv_ref)):
        for g in range(G_B):
            c = c_ref[0, g]
            a = _dot(c, w1_ref[kv, 0:half, :])
            b = _dot(c, w1_ref[kv, half:2 * half, :])
            n = a.shape[0]
            b_next = pltpu.roll(b, n - 1, 0)
            pe_term = _dot(pe_ref[kv], w1_ref[kv])[0:1]
            hid = jax.nn.gelu(a + b_next + pe_term).astype(BF16)
            if kv == 0:
                y = _dot_nt(hid, w2t_ref[kv])
                ms = jnp.mean(y * y, axis=1, keepdims=True)
                y = y * lax.rsqrt(ms + EPS) * gk_ref[...]
                ok_ref[0, g] = jnp.concatenate([y, jnp.zeros_like(y)], axis=1).astype(BF16)
            else:
                ov_ref[0, g] = _dot_nt(w2t_ref[kv], hid).astype(BF16)


def _compress(ck, cv, w1, w2t, pe, gk):
    bsz, g, n, width = ck.shape
    blk = pl.BlockSpec((1, g, n, width), lambda b: (b, 0, 0, 0))
    full = lambda a: pl.BlockSpec(a.shape, lambda b: (0,) * a.ndim)
    return pl.pallas_call(
        _compress_kernel,
        grid=(bsz,),
        in_specs=[blk, blk, full(w1), full(w2t), full(pe), full(gk)],
        out_specs=[pl.BlockSpec((1, g, n, LANES), lambda b: (b, 0, 0, 0)),
                   pl.BlockSpec((1, g, HEAD_DIM, n), lambda b: (b, 0, 0, 0))],
        out_shape=[jax.ShapeDtypeStruct((bsz, g, n, LANES), BF16),
                   jax.ShapeDtypeStruct((bsz, g, HEAD_DIM, n), BF16)],
        compiler_params=_params(("parallel",)),
        name="compress",
    )(ck, cv, w1, w2t, pe, gk)


def _with_ones(vt):
    return jnp.concatenate([vt, jnp.ones((BF16_ROWS, vt.shape[1]), BF16)], axis=0)


def _col_max(s):
    while s.shape[0] > 8:
        half = s.shape[0] // 2
        s = jnp.maximum(s[0:half], s[half:2 * half])
    return jnp.max(s, axis=0, keepdims=True)


def _flash_update(carries, chains):
    m_news = []
    for carry, tiles in zip(carries, chains):
        tops = []
        for s, _, bias, top in tiles:
            top = _col_max(s) if top is None else top
            tops.append(top if bias is None else top + bias)
        m_news.append(functools.reduce(jnp.maximum, tops if carry is None else tops + [carry[0]]))
    pvs = [None] * len(chains)
    for t in range(max(len(tiles) for tiles in chains)):
        for c, tiles in enumerate(chains):
            if t < len(tiles):
                s, vt, bias, _ = tiles[t]
                p = jnp.exp2(s - (m_news[c] if bias is None else m_news[c] - bias)).astype(BF16)
                part = _dot(_with_ones(vt), p)
                pvs[c] = part if pvs[c] is None else pvs[c] + part
    outs = []
    for carry, m_new, pv in zip(carries, m_news, pvs):
        outs.append((m_new, pv if carry is None else jnp.exp2(carry[0] - m_new) * carry[1] + pv))
    return outs


def _flash_out(acc):
    return acc[0:HEAD_DIM] / acc[HEAD_DIM:HEAD_DIM + 1]


def _topk_rows(scores, index, k):
    scores = list(scores)
    picked = [jnp.zeros(sc.shape, F32) for sc in scores]
    for _ in range(k):
        mx = [jnp.max(sc, axis=0, keepdims=True) for sc in scores]
        cand = [jnp.where(sc == m, index, jnp.int32(1 << 20)) for sc, m in zip(scores, mx)]
        first = [jnp.min(c, axis=0, keepdims=True) for c in cand]
        hit = [index == f for f in first]
        picked = [jnp.where(h, 1.0, p) for h, p in zip(hit, picked)]
        scores = [jnp.where(h, -jnp.inf, sc) for h, sc in zip(hit, scores)]
    return picked


def _tile_rows(j):
    return pl.ds(pl.multiple_of(j * TQ, TQ), TQ)


def _moba_gate_kernel(q_ref, km_ref, o_ref, *, n_sel):
    nh, tg = q_ref.shape[1], q_ref.shape[3]
    t0 = pl.program_id(2) * tg
    shape = (km_ref.shape[2], tg)
    blk = lax.broadcasted_iota(jnp.int32, shape, 0)
    own = (lax.broadcasted_iota(jnp.int32, shape, 1) + t0) >> _LOG2_BLK_A
    valid = blk < own
    qs, gates = [], []
    for hh in range(nh):
        q = q_ref[0, hh]
        gate = _dot3(km_ref[0, hh], jnp.concatenate([q, jnp.zeros_like(q)], axis=0))
        qs.append(q)
        gates.append(jnp.where(valid, gate, -jnp.inf))
    for hh, picked in enumerate(_topk_rows(gates, blk, n_sel)):
        keep = jnp.where(valid, picked, 0.0) + jnp.where(blk == own, 1.0, 0.0)
        selb = jnp.where(keep > 0.0, 0.0, NEG)
        o_ref[0, hh] = jnp.concatenate([qs[hh], selb], axis=0).astype(BF16)


def _moba_gate(qa, km):
    bsz, nh, _, s = qa.shape
    tg = min(s, 2048)
    hpb = 4
    n_sel = max(1, min(TOPK_A, s // BLK_A - 1))
    return pl.pallas_call(
        functools.partial(_moba_gate_kernel, n_sel=n_sel),
        grid=(bsz, nh // hpb, s // tg),
        in_specs=[pl.BlockSpec((1, hpb, HEAD_DIM, tg), lambda b, h, t: (b, h, 0, t)),
                  pl.BlockSpec((1, hpb, HEAD_DIM, LANES), lambda b, h, t: (b, h, 0, 0))],
        out_specs=pl.BlockSpec((1, hpb, LANES, tg), lambda b, h, t: (b, h, 0, t)),
        out_shape=jax.ShapeDtypeStruct((bsz, nh, LANES, s), BF16),
        compiler_params=_params(("parallel", "parallel", "parallel")),
        name="moba_gate",
    )(qa, km)


def _far_tiles(qi):
    n_far = jnp.maximum(qi - 1, 0)
    left = jnp.maximum(n_far - 1, 0)
    off_left = jnp.where((n_far & 1) == 1, 0.0, NEG)
    return n_far >> 1, left, off_left


class _FarLoop:
    def __init__(self, n_pairs, heads, qk_tile, values, cbs, sa_ref, sb_ref, m_ref, acc_ref):
        self.n_pairs, self.heads, self.qk_tile, self.values, self.cbs = n_pairs, heads, qk_tile, values, cbs
        self.sa_ref, self.sb_ref, self.m_ref, self.acc_ref = sa_ref, sb_ref, m_ref, acc_ref
        self.last = jnp.maximum(n_pairs - 1, 0)

    def fetch(self, buf_ref, h, i):
        s_lo = self.qk_tile(h, 2 * i)
        s_hi = self.qk_tile(h, 2 * i + 1)
        buf_ref[h, 0:TQ, :] = s_lo
        buf_ref[h, TQ:2 * TQ, :] = s_hi
        return _col_max(s_lo), _col_max(s_hi)

    def consume(self, buf_ref, h, top, i):
        (m, acc), = _flash_update([(self.m_ref[h], self.acc_ref[h])], [[
            (buf_ref[h, 0:TQ, :], self.values(h, 2 * i), self.cbs[h], top[0]),
            (buf_ref[h, TQ:2 * TQ, :], self.values(h, 2 * i + 1), self.cbs[h], top[1])]])
        self.m_ref[h] = m
        self.acc_ref[h] = acc

    def first(self):
        return tuple(self.fetch(self.sa_ref, h, 0) for h in self.heads)

    def run(self, tops_first):
        def two_pairs(ia, tops_a):
            tops_b, tops_next = [], []
            for h in self.heads:
                tops_b.append(self.fetch(self.sb_ref, h, ia + 1))
                self.consume(self.sa_ref, h, tops_a[h], ia)
            for h in self.heads:
                tops_next.append(self.fetch(self.sa_ref, h, jnp.minimum(ia + 2, self.last)))
                self.consume(self.sb_ref, h, tops_b[h], ia + 1)
            return tuple(tops_next)

        def pairs(n, start, tops):
            for k in range(0, n, 2):
                tops = two_pairs(start + k, tops)
            return tops

        trip = FAR_PAIRS_PER_TRIP
        n_trips = self.n_pairs // trip
        tops = lax.fori_loop(0, n_trips, lambda q, t: pairs(trip, trip * q, t), tops_first)
        done = trip * n_trips
        n = trip // 2
        while n >= 2:
            has = ((self.n_pairs // n) & 1) == 1
            tops = lax.cond(has, functools.partial(pairs, n, done), lambda t: t, tops)
            done = done + jnp.where(has, n, 0)
            n //= 2
        tops_last = tops

        @pl.when((self.n_pairs & 1) == 1)
        def _():
            for h in self.heads:
                self.consume(self.sa_ref, h, tops_last[h], self.last)


def _moba_kernel(q_ref, k_ref, vt_ref, tab_ref, cb_ref, o_ref, sa_ref, sb_ref, m_ref, acc_ref):
    qi = pl.program_id(2)
    jp = jnp.maximum(qi - 1, 0)
    off_p = jnp.where(qi >= 1, 0.0, NEG)
    n_pairs, jl, off_l = _far_tiles(qi)
    heads = range(MOBA_HEADS)
    qaug = [q_ref[0, hh] for hh in heads]
    cbs = [cb_ref[hh][0:1, 0:1] for hh in heads]

    def scores(hh, j):
        return _dot(k_ref[0, hh, _tile_rows(j), :], qaug[hh])

    def values(hh, j):
        return vt_ref[0, j, hh * HEAD_DIM:(hh + 1) * HEAD_DIM, :]

    far = _FarLoop(n_pairs, heads, scores, values, cbs, sa_ref, sb_ref, m_ref, acc_ref)

    s_own = [scores(hh, qi) for hh in heads]
    s_prev = [scores(hh, jp) for hh in heads]
    s_left = [scores(hh, jl) for hh in heads]
    tops_first = far.first()
    chains = [[(s_own[hh] + tab_ref[hh, TQ:2 * TQ, :], values(hh, qi), None, None),
               (s_prev[hh] + tab_ref[hh, 0:TQ, :], values(hh, jp), off_p, None),
               (s_left[hh], values(hh, jl), cbs[hh] + off_l, None)] for hh in heads]
    for hh, (m, acc) in zip(heads, _flash_update([None] * len(chains), chains)):
        m_ref[hh] = m
        acc_ref[hh] = acc
    far.run(tops_first)
    out_t = jnp.concatenate([_flash_out(acc_ref[hh]) for hh in heads], axis=0)
    o_ref[0] = out_t.T.astype(BF16)


def _flash_scratch(n_heads):
    return [pltpu.VMEM((n_heads, 2 * TQ, TQ), F32),
            pltpu.VMEM((n_heads, 2 * TQ, TQ), F32),
            pltpu.VMEM((n_heads, 1, TQ), F32),
            pltpu.VMEM((n_heads, ACC_ROWS, TQ), F32)]


def _moba(qaug, kaug, vat, tab, cb):
    bsz, _, _, s = qaug.shape
    nq = s // TQ
    nh = MOBA_HEADS
    return pl.pallas_call(
        _moba_kernel,
        grid=(bsz, H_A // nh, nq),
        in_specs=[
            pl.BlockSpec((1, nh, LANES, TQ), lambda b, hp, qi: (b, hp, 0, qi)),
            pl.BlockSpec((1, nh, s, LANES), lambda b, hp, qi: (b, hp, 0, 0)),
            pl.BlockSpec((1, nq, nh * HEAD_DIM, TQ), lambda b, hp, qi: (b, 0, hp, 0)),
            pl.BlockSpec((nh, 2 * TQ, TQ), lambda b, hp, qi: (hp, 0, 0)),
            pl.BlockSpec((nh, 8, LANES), lambda b, hp, qi: (hp, 0, 0)),
        ],
        out_specs=pl.BlockSpec((1, TQ, nh * HEAD_DIM), lambda b, hp, qi: (b, qi, hp)),
        out_shape=jax.ShapeDtypeStruct((bsz, s, H_A * HEAD_DIM), BF16),
        scratch_shapes=_flash_scratch(nh),
        compiler_params=_params(("parallel", "parallel", "arbitrary")),
        name="moba",
    )(qaug, kaug, vat, tab, cb)


def _nsa_cmp_kernel(q_ref, kc_ref, vct_ref, ovt_ref, oc_ref, selb_ref, *, n_sel, n_cmp, n_parts):
    qi = pl.program_id(1)
    ncp = kc_ref.shape[2]
    tq = q_ref.shape[3]
    t0 = qi * tq
    zeros = jnp.zeros((HEAD_DIM, tq), BF16)
    qs = [jnp.concatenate([q_ref[0, hd].astype(BF16), zeros], axis=0) for hd in range(H_B)]

    def body(nk, nb):
        n_idx = lax.broadcasted_iota(jnp.int32, (nk, tq), 0)
        t_idx = lax.broadcasted_iota(jnp.int32, (nk, tq), 1) + t0
        mask = (n_idx * CMP_STRIDE + (CMP_LEN - 1) <= t_idx) & (n_idx < n_cmp)
        any_key = t_idx[0:1] >= CMP_LEN - 1
        blk = lax.broadcasted_iota(jnp.int32, (nb, tq), 0)
        cur = (lax.broadcasted_iota(jnp.int32, (nb, tq), 1) + t0) >> _LOG2_SEL_BLK
        ok = blk <= cur
        forced = (blk == 0) | (blk == cur) | (blk == cur - 1)
        ovt = ovt_ref[0:nb, 0:nk]
        scores = []
        for g in range(G_B):
            kc = kc_ref[0, g, 0:nk, :]
            vct = _with_ones(vct_ref[0, g, :, 0:nk])
            psum = jnp.zeros((nk, tq), F32)
            for hd in range(g * R_B, (g + 1) * R_B):
                z = jnp.where(mask, _dot(kc, qs[hd]), NEG)
                e = jnp.exp2(z - _col_max(z))
                acc = _dot(vct, e.astype(BF16))
                rinv = jnp.where(any_key, 1.0 / acc[HEAD_DIM:HEAD_DIM + 1], 0.0)
                oc_ref[0, hd] = acc[0:HEAD_DIM] * rinv
                psum = psum + e * rinv
            ph, pl_ = _split(psum)
            imp_t = _dot(ovt, ph) + _dot(ovt, pl_)
            scores.append(jnp.where(ok, jnp.where(forced, BIG, imp_t), -jnp.inf))
        for g, picked in enumerate(_topk_rows(scores, blk, n_sel)):
            selb_ref[0, g, 0:nb, :] = jnp.where(ok & (picked > 0.0), 0.0, NEG).astype(BF16)
            if nb < LANES:
                selb_ref[0, g, nb:LANES, :] = jnp.full((LANES - nb, tq), NEG, BF16)

    part = ncp // n_parts
    need = jnp.minimum(((qi + 1) * (tq // CMP_STRIDE) + part - 1) // part, n_parts)
    for v in range(1, n_parts + 1):
        pl.when(need == v)(functools.partial(body, v * part, min(LANES, v * part * CMP_STRIDE // SEL_BLK)))


def _cmp_parts(ncp):
    return 4 if ncp % (4 * LANES) == 0 else 1


def _nsa_cmp(qb, kcmp, vcmpt, ovt, n_cmp):
    bsz, _, _, s = qb.shape
    tq = CMP_TQ if s % CMP_TQ == 0 else TQ
    nq = s // tq
    ncp = kcmp.shape[2]
    n_sel = min(SEL_TOPK, s // SEL_BLK)
    n_parts = _cmp_parts(ncp)
    return pl.pallas_call(
        functools.partial(_nsa_cmp_kernel, n_sel=n_sel, n_cmp=n_cmp, n_parts=n_parts),
        grid=(bsz, nq),
        in_specs=[
            pl.BlockSpec((1, H_B, HEAD_DIM, tq), lambda b, qi: (b, 0, 0, qi)),
            pl.BlockSpec((1, G_B, ncp, LANES), lambda b, qi: (b, 0, 0, 0)),
            pl.BlockSpec((1, G_B, HEAD_DIM, ncp), lambda b, qi: (b, 0, 0, 0)),
            pl.BlockSpec((LANES, ncp), lambda b, qi: (0, 0)),
        ],
        out_specs=[
            pl.BlockSpec((1, H_B, HEAD_DIM, tq), lambda b, qi: (b, 0, 0, qi)),
            pl.BlockSpec((1, G_B, LANES, tq), lambda b, qi: (b, 0, 0, qi)),
        ],
        out_shape=[jax.ShapeDtypeStruct((bsz, H_B, HEAD_DIM, s), F32),
                   jax.ShapeDtypeStruct((bsz, G_B, LANES, s), BF16)],
        compiler_params=_params(("parallel", "parallel")),
        name="nsa_cmp",
    )(qb, kcmp, vcmpt, ovt)


def _nsa_main_kernel(q_ref, selb_ref, ks_ref, vst_ref, kw_ref, vwt_ref, oc_ref, gates_ref,
                     tabs_ref, tabw_ref, cb_ref, o_ref, sa_ref, sb_ref, m_ref, acc_ref):
    qi = pl.program_id(2)
    selb = selb_ref[0, 0]
    gates = gates_ref[0, 0]
    zeros = jnp.zeros((HEAD_DIM, TQ), BF16)
    j1 = jnp.maximum(qi - 1, 0)
    j2 = jnp.maximum(qi - 2, 0)
    off1 = jnp.where(qi >= 1, 0.0, NEG)
    off2 = jnp.where(qi >= 2, 0.0, NEG)
    n_pairs, jl, off_l = _far_tiles(qi)

    heads = range(R_B)
    qw = [jnp.concatenate([q_ref[0, r].astype(BF16), zeros], axis=0) for r in heads]
    qs = [jnp.concatenate([qw[r], selb], axis=0) for r in heads]
    cbs = [cb_ref[r][0:1, 0:1] for r in heads]

    def s_sel(r, j):
        return _dot(ks_ref[0, 0, _tile_rows(j), :], qs[r])

    def s_win(r, j):
        return _dot(kw_ref[0, 0, _tile_rows(j), :], qw[r])

    far = _FarLoop(n_pairs, heads, s_sel, lambda r, j: vst_ref[0, j], cbs, sa_ref, sb_ref, m_ref, acc_ref)

    sw = [[s_win(r, j) for r in heads] for j in (qi, j1, j2)]
    ss = [[s_sel(r, j) for r in heads] for j in (qi, j1, jl)]
    tops_first = far.first()

    chains = [[(sw[0][r] + tabw_ref[r, 2 * TQ:3 * TQ, :], vwt_ref[0, qi], None, None),
               (sw[1][r] + tabw_ref[r, TQ:2 * TQ, :], vwt_ref[0, j1], off1, None),
               (sw[2][r] + tabw_ref[r, 0:TQ, :], vwt_ref[0, j2], off2, None)] for r in heads]
    chains += [[(ss[0][r] + tabs_ref[r, TQ:2 * TQ, :], vst_ref[0, qi], None, None),
                (ss[1][r] + tabs_ref[r, 0:TQ, :], vst_ref[0, j1], off1, None),
                (ss[2][r], vst_ref[0, jl], cbs[r] + off_l, None)] for r in heads]
    done = _flash_update([None] * len(chains), chains)
    win = [_flash_out(acc) for _, acc in done[:R_B]]
    for r, (m, acc) in zip(heads, done[R_B:]):
        m_ref[r] = m
        acc_ref[r] = acc
    far.run(tops_first)

    outs = [gates[3 * r:3 * r + 1] * oc_ref[0, r]
            + gates[3 * r + 1:3 * r + 2] * _flash_out(acc_ref[r])
            + gates[3 * r + 2:3 * r + 3] * win[r] for r in heads]
    o_ref[0] = jnp.concatenate(outs, axis=0).T.astype(BF16)


def _nsa_main(qb, selb, ksaug, vst, kw, vwt, oc, gates, tabs, tabw, cb):
    bsz, _, _, s = qb.shape
    nq = s // TQ
    return pl.pallas_call(
        _nsa_main_kernel,
        grid=(bsz, G_B, nq),
        in_specs=[
            pl.BlockSpec((1, R_B, HEAD_DIM, TQ), lambda b, g, qi: (b, g, 0, qi)),
            pl.BlockSpec((1, 1, LANES, TQ), lambda b, g, qi: (b, g, 0, qi)),
            pl.BlockSpec((1, 1, s, 2 * LANES), lambda b, g, qi: (b, g, 0, 0)),
            pl.BlockSpec((1, nq, HEAD_DIM, TQ), lambda b, g, qi: (b, 0, g, 0)),
            pl.BlockSpec((1, 1, s, LANES), lambda b, g, qi: (b, g, 0, 0)),
            pl.BlockSpec((1, nq, HEAD_DIM, TQ), lambda b, g, qi: (b, 0, g, 0)),
            pl.BlockSpec((1, R_B, HEAD_DIM, TQ), lambda b, g, qi: (b, g, 0, qi)),
            pl.BlockSpec((1, 1, GATE_ROWS, TQ), lambda b, g, qi: (b, g, 0, qi)),
            pl.BlockSpec((R_B, 2 * TQ, TQ), lambda b, g, qi: (g, 0, 0)),
            pl.BlockSpec((R_B, 3 * TQ, TQ), lambda b, g, qi: (g, 0, 0)),
            pl.BlockSpec((R_B, 8, LANES), lambda b, g, qi: (g, 0, 0)),
        ],
        out_specs=pl.BlockSpec((1, TQ, R_B * HEAD_DIM), lambda b, g, qi: (b, qi, g)),
        out_shape=jax.ShapeDtypeStruct((bsz, s, H_B * HEAD_DIM), BF16),
        scratch_shapes=_flash_scratch(R_B),
        compiler_params=_params(("parallel", "parallel", "arbitrary")),
        name="nsa_main",
    )(qb, selb, ksaug, vst, kw, vwt, oc, gates, tabs, tabw, cb)


def _out_ffn_kernel(x_ref, oa_ref, ob_ref, wo_ref, gtm_ref, gffn_ref, scf_ref, shf_ref, gtf_ref,
                    wg_ref, wu_ref, wd_ref, o_ref, x1_ref, h_ref, acc_ref):
    f = pl.program_id(2)
    half = oa_ref.shape[2]

    @pl.when(f == 0)
    def _():
        mix = _dot(oa_ref[0], wo_ref[0:half, :]) + _dot(ob_ref[0], wo_ref[half:2 * half, :])
        x1 = x_ref[0] + gtm_ref[0, 0] * mix
        x1_ref[...] = x1
        h = _rms_rows(x1, gffn_ref[...]) * (1.0 + scf_ref[0, 0]) + shf_ref[0, 0]
        h_ref[...] = h.astype(BF16)

    h = h_ref[...]
    gate = _dot(h, wg_ref[...])
    up = _dot(h, wu_ref[...])
    act = (gate * jax.nn.sigmoid(gate) * up).astype(BF16)
    part = _dot(act, wd_ref[...])

    @pl.when(f == 0)
    def _():
        acc_ref[...] = part

    @pl.when(f > 0)
    def _():
        acc_ref[...] += part

    @pl.when(f == pl.num_programs(2) - 1)
    def _():
        o_ref[0] = x1_ref[...] + gtf_ref[0, 0] * acc_ref[...]


def _out_ffn(x, oa, ob, wo, mod4, gffn, wgu, wd):
    bsz, s, d = x.shape
    fh = wd.shape[0]
    tm = TM_FFN
    tf = fh // 2 if (fh // 2) % LANES == 0 else fh
    nf = fh // tf
    tok = lambda b, t, f: (b, t, 0)
    modspec = lambda k: pl.BlockSpec((1, 1, 1, d), lambda b, t, f: (b, k, 0, 0))
    return pl.pallas_call(
        _out_ffn_kernel,
        grid=(bsz, s // tm, nf),
        in_specs=[
            pl.BlockSpec((1, tm, d), tok),
            pl.BlockSpec((1, tm, oa.shape[2]), tok),
            pl.BlockSpec((1, tm, ob.shape[2]), tok),
            pl.BlockSpec((d, d), lambda b, t, f: (0, 0)),
            modspec(2),
            pl.BlockSpec((1, d), lambda b, t, f: (0, 0)),
            modspec(4),
            modspec(3),
            modspec(5),
            pl.BlockSpec((d, tf), lambda b, t, f: (0, f)),
            pl.BlockSpec((d, tf), lambda b, t, f: (0, f + nf)),
            pl.BlockSpec((tf, d), lambda b, t, f: (f, 0)),
        ],
        out_specs=pl.BlockSpec((1, tm, d), tok),
        out_shape=jax.ShapeDtypeStruct((bsz, s, d), F32),
        scratch_shapes=[pltpu.VMEM((tm, d), F32), pltpu.VMEM((tm, d), BF16), pltpu.VMEM((tm, d), F32)],
        compiler_params=_params(("parallel", "parallel", "arbitrary")),
        name="out_ffn",
    )(x, oa, ob, wo, mod4, gffn, mod4, mod4, mod4, wgu, wgu, wd)


def _t5_bucket_np(d):
    max_exact = N_BUCKETS // 2
    d = np.maximum(d, 0)
    df = np.maximum(d, 1).astype(np.float64)
    large = max_exact + (np.log(df / max_exact) / math.log(MAX_DIST / max_exact)
                         * (N_BUCKETS - max_exact)).astype(np.int64)
    large = np.minimum(large, N_BUCKETS - 1)
    return np.where(d < max_exact, d, large).astype(np.int32)


def _bias_expand_kernel(tab_ref, bucket_ref, o_ref):
    hd = pl.program_id(0)
    bucket = bucket_ref[...]
    acc = jnp.full(bucket.shape, NEG, F32)
    for b in range(N_BUCKETS):
        acc = jnp.where(bucket == b, tab_ref[hd, b], acc)
    o_ref[0] = acc


def _bias_expand(tab, bucket):
    nh = tab.shape[0]
    return pl.pallas_call(
        _bias_expand_kernel,
        grid=(nh,),
        in_specs=[pl.BlockSpec(memory_space=pltpu.SMEM),
                  pl.BlockSpec(bucket.shape, lambda h: (0, 0))],
        out_specs=pl.BlockSpec((1,) + bucket.shape, lambda h: (h, 0, 0)),
        out_shape=jax.ShapeDtypeStruct((nh,) + bucket.shape, F32),
        compiler_params=_params(("parallel",)),
        name="bias_expand",
    )(tab, jnp.asarray(bucket, jnp.int32))


def _bias_tables(rel_bias):
    tab = rel_bias.T.astype(F32) * LOG2_E
    i = np.arange(TQ)[None, :]
    d_near = i + TQ - np.arange(2 * TQ)[:, None]
    near = _bias_expand(tab, np.where(d_near >= 0, _t5_bucket_np(d_near), -1))
    d_win = i + 2 * TQ - np.arange(3 * TQ)[:, None]
    ok_win = (d_win >= 0) & (d_win < WINDOW)
    win = _bias_expand(tab[H_A:], np.where(ok_win, _t5_bucket_np(d_win), -1))
    far = jnp.broadcast_to(tab[:, N_BUCKETS - 1][:, None, None], (tab.shape[0], 8, LANES))
    return near, win, far


def _overlap_t(n_cmp_pad, n_cmp):
    cs = np.arange(n_cmp_pad)[None, :] * CMP_STRIDE
    ss = np.arange(LANES)[:, None] * SEL_BLK
    ov = (cs < ss + SEL_BLK) & (cs + CMP_LEN > ss) & (np.arange(n_cmp_pad)[None, :] < n_cmp)
    return jnp.asarray(ov.astype(np.float32), BF16)


def _block_diag(n):
    m = (np.arange(n)[:, None] // HEAD_DIM == np.arange(n)[None, :] // HEAD_DIM)
    return jnp.asarray(m.astype(np.float32) / HEAD_DIM, BF16)


def kernel(x, c, rel_bias, w_ada, b_ada, g_mix, w_in, q_norm_a, k_norm_a, q_norm_b, k_norm_cmp,
           k_norm_sel, k_norm_win, cmp_pe_k, cmp_w1_k, cmp_w2_k, cmp_pe_v, cmp_w1_v, cmp_w2_v,
           w_out, g_ffn, w_gu, w_down):
    bsz, s, d = x.shape
    depth = w_ada.shape[0]
    assert s % TM_IN == 0 and s % TM_FFN == 0 and s % (2 * TQ) == 0
    assert s // BLK_A <= HEAD_DIM and s // SEL_BLK <= LANES
    assert WINDOW == 2 * TQ and BLK_A == TQ and MAX_DIST <= TQ
    n_chunks = s // CMP_STRIDE
    n_cmp = (s - CMP_LEN) // CMP_STRIDE + 1
    scale = HEAD_DIM ** -0.5 * LOG2_E
    hd = HEAD_DIM

    near, win, far = _bias_tables(rel_bias)
    ovt = _overlap_t(n_chunks, n_cmp)
    bd = _block_diag(512)
    tile = lambda g, n: jnp.tile(g.astype(F32), n).reshape(1, -1)
    tile_t = lambda g, n: jnp.broadcast_to(jnp.tile(g.astype(F32), n)[:, None], (n * hd, TM_IN))

    for l in range(depth):
        mod = _ada(c, w_ada[l], b_ada[l])
        mod4 = mod.reshape(bsz, ADA_CHUNKS, 1, d)

        wl = w_in[l]
        cols = np.cumsum([0, H_A * hd, H_A * hd, H_A * hd, H_B * hd] + [G_B * hd] * 6)
        qa_c, ka_c, va_c, qb_c, kc_c, vc_c, ks_c, vs_c, kw_c, vw_c = [
            wl[:, int(a):int(b)] for a, b in zip(cols[:-1], cols[1:])]
        gl = wl[:, int(cols[-1]):].reshape(d, G_B, 3 * R_B)
        gl = jnp.pad(gl, ((0, 0), (0, 0), (0, GATE_ROWS - 3 * R_B))).reshape(d, G_B * GATE_ROWS)
        w_rows = jnp.concatenate([ka_c, kc_c, vc_c, ks_c, kw_c], axis=1).astype(BF16)
        w_t = jnp.concatenate([qa_c, va_c, qb_c, vs_c, vw_c, gl], axis=1).T.astype(BF16)

        (qa, kaug, vat, kmean, qb, kc, vc, ksaug, vst, kw, vwt, gates) = _inproj(
            x, mod4, mod4, g_mix[l].reshape(1, d), w_rows, w_t, bd,
            tile_t(q_norm_a[l], H_A) * scale, tile(k_norm_a[l], H_A), tile_t(q_norm_b[l], H_B) * scale,
            tile(k_norm_sel[l], G_B), tile(k_norm_win[l], G_B))

        nba = s // BLK_A
        km = kmean.reshape(bsz, nba, H_A, hd).transpose(0, 2, 1, 3)
        km = jnp.pad(km, ((0, 0), (0, 0), (0, hd - nba), (0, LANES - hd)))
        o_a = _moba(_moba_gate(qa, km), kaug, vat, near[:H_A], far[:H_A])

        chunks = lambda t: t.reshape(bsz, s, G_B, hd).transpose(0, 2, 1, 3).reshape(
            bsz, G_B, n_chunks, CMP_STRIDE * hd)
        w1 = jnp.stack([cmp_w1_k[l], cmp_w1_v[l]]).astype(BF16)
        w2t = jnp.stack([cmp_w2_k[l].T, cmp_w2_v[l].T]).astype(BF16)
        pe = jnp.stack([cmp_pe_k[l], cmp_pe_v[l]]).reshape(2, 1, CMP_LEN * hd)
        pe = jnp.broadcast_to(pe, (2, 8, CMP_LEN * hd)).astype(BF16)
        kcmp, vcmpt = _compress(chunks(kc), chunks(vc), w1, w2t, pe,
                                k_norm_cmp[l].astype(F32).reshape(1, hd))

        oc, selb = _nsa_cmp(qb, kcmp, vcmpt, ovt, n_cmp)
        o_b = _nsa_main(qb, selb, ksaug, vst, kw, vwt, oc, gates, near[H_A:], win, far[H_A:])

        x = _out_ffn(x, o_a, o_b, w_out[l].astype(BF16), mod4, g_ffn[l].reshape(1, d),
                     w_gu[l].astype(BF16), w_down[l].astype(BF16))
    return x
```

```python
import functools
import math

import jax
import jax.numpy as jnp
import numpy as np
from jax import lax
from jax.experimental import pallas as pl
from jax.experimental.pallas import tpu as pltpu

F32 = jnp.float32
BF16 = jnp.bfloat16

HEAD_DIM = 64
LANES = 128
BF16_ROWS = 16
H_A = 8
H_B = 8
G_B = 2
R_B = H_B // G_B
BLK_A = 256
TOPK_A = 3
CMP_LEN = 32
CMP_STRIDE = 16
CMP_HIDDEN = 256
SEL_BLK = 64
SEL_TOPK = 16
WINDOW = 512
N_BUCKETS = 32
MAX_DIST = 128
ADA_CHUNKS = 6
NEG = -1e30
BIG = 1e9
EPS = 1e-6
LOG2_E = math.log2(math.e)

_LOG2_BLK_A = BLK_A.bit_length() - 1
_LOG2_SEL_BLK = SEL_BLK.bit_length() - 1

TQ = 256
TM_IN = 1024
TM_FFN = 512
CMP_TQ = 512
FAR_PAIRS_PER_TRIP = 8
MOBA_HEADS = 4
GATE_ROWS = 16
ACC_ROWS = HEAD_DIM + BF16_ROWS
VMEM_LIMIT = 56 * 1024 * 1024


def _dot(a, b):
    return jnp.dot(a, b, preferred_element_type=F32)


def _dot_nt(a, b):
    return lax.dot_general(a, b, (((1,), (1,)), ((), ())), preferred_element_type=F32)


def _split(a):
    hi = a.astype(BF16)
    lo = (a - hi.astype(F32)).astype(BF16)
    return hi, lo


def _dot3(a, b):
    ah, al = _split(a)
    bh, bl = _split(b)
    return _dot(ah, bh) + (_dot(al, bh) + _dot(ah, bl))


def _params(sem):
    return pltpu.CompilerParams(dimension_semantics=sem, vmem_limit_bytes=VMEM_LIMIT)


def _ada_kernel(c_ref, w_ref, b_ref, o_ref):
    c = c_ref[...]
    o_ref[...] = _dot3(c * jax.nn.sigmoid(c), w_ref[...]) + b_ref[...]


def _ada(c, w, b):
    bsz, d = c.shape
    n = w.shape[1]
    tn = 512
    return pl.pallas_call(
        _ada_kernel,
        grid=(n // tn,),
        in_specs=[pl.BlockSpec((bsz, d), lambda j: (0, 0)),
                  pl.BlockSpec((d, tn), lambda j: (0, j)),
                  pl.BlockSpec((1, tn), lambda j: (0, j))],
        out_specs=pl.BlockSpec((bsz, tn), lambda j: (0, j)),
        out_shape=jax.ShapeDtypeStruct((bsz, n), F32),
        compiler_params=_params(("arbitrary",)),
        name="ada",
    )(c, w, b.reshape(1, n))


def _rms_rows(xf, g):
    ms = jnp.mean(xf * xf, axis=-1, keepdims=True)
    return xf * lax.rsqrt(ms + EPS) * g


def _head_norm(t, bd, gain):
    hi, lo = _split(t * t)
    ms = _dot(hi, bd) + _dot(lo, bd)
    return t * lax.rsqrt(ms + EPS) * gain


def _head_norm_t(t, gain):
    heads = []
    for hd in range(t.shape[0] // HEAD_DIM):
        th = t[hd * HEAD_DIM:(hd + 1) * HEAD_DIM]
        ms = jnp.mean(th * th, axis=0, keepdims=True)
        heads.append(th * lax.rsqrt(ms + EPS) * gain[hd * HEAD_DIM:(hd + 1) * HEAD_DIM])
    return heads


def _inproj_kernel(x_ref, sc_ref, sh_ref, gmix_ref, wr_ref, wt_ref, bd_ref, gqa_ref, gka_ref, gqb_ref,
                   gks_ref, gkw_ref,
                   qa_ref, kaug_ref, va_ref, kmean_ref, qb_ref, kc_ref, vc_ref, ksaug_ref,
                   vs_ref, kw_ref, vw_ref, gates_ref):
    tm = x_ref.shape[1]
    ti = pl.program_id(1)
    xf = x_ref[0]
    h = _rms_rows(xf, gmix_ref[...]) * (1.0 + sc_ref[0, 0]) + sh_ref[0, 0]
    hb = h.astype(BF16)

    def proj(c0, c1):
        return _dot(hb, wr_ref[:, c0:c1])

    def proj_t(r0, r1):
        return _dot_nt(wt_ref[r0:r1, :], hb)

    bd = bd_ref[...]
    bd2 = bd_ref[0:LANES, 0:LANES]
    lane = lax.broadcasted_iota(jnp.int32, (tm, LANES), 1)
    row = lax.broadcasted_iota(jnp.int32, (tm, LANES), 0) + ti * tm
    low = lane < HEAD_DIM

    def k_in_low(pair, odd):
        return pltpu.roll(pair, HEAD_DIM, 1) if odd else pair

    for hd, qh in enumerate(_head_norm_t(proj_t(0, 512), gqa_ref[...])):
        qa_ref[0, hd] = qh

    def put_tiles(ref, vt):
        for i in range(tm // TQ):
            ref[0, i] = vt[:, i * TQ:(i + 1) * TQ].astype(BF16)

    put_tiles(va_ref, proj_t(512, 1024))
    for hd, qh in enumerate(_head_norm_t(proj_t(1024, 1536), gqb_ref[...])):
        qb_ref[0, hd] = qh
    put_tiles(vs_ref, proj_t(1536, 1664))
    put_tiles(vw_ref, proj_t(1664, 1792))
    gl = jax.nn.sigmoid(proj_t(1792, 1792 + G_B * GATE_ROWS))
    for g in range(G_B):
        gates_ref[0, g] = gl[g * GATE_ROWS:(g + 1) * GATE_ROWS]

    ka = _head_norm(proj(0, 512), bd, gka_ref[...])
    oh_a = jnp.where(lane - HEAD_DIM == (row >> _LOG2_BLK_A), 1.0, 0.0)
    for hd in range(H_A):
        pair = ka[:, (hd // 2) * LANES:(hd // 2 + 1) * LANES]
        kaug_ref[0, hd] = jnp.where(low, k_in_low(pair, hd % 2), oh_a).astype(BF16)
    for i in range(tm // BLK_A):
        kmean_ref[0, i] = jnp.mean(ka[i * BLK_A:(i + 1) * BLK_A], axis=0, keepdims=True)

    kc_ref[0] = proj(512, 640).astype(BF16)
    vc_ref[0] = proj(640, 768).astype(BF16)

    ks = _head_norm(proj(768, 896), bd2, gks_ref[...])
    kw = _head_norm(proj(896, 1024), bd2, gkw_ref[...])
    oh_s = jnp.where(lane == (row >> _LOG2_SEL_BLK), 1.0, 0.0).astype(BF16)
    for g in range(G_B):
        ksaug_ref[0, g] = jnp.concatenate(
            [jnp.where(low, k_in_low(ks, g), 0.0).astype(BF16), oh_s], axis=1)
        kw_ref[0, g] = jnp.where(low, k_in_low(kw, g), 0.0).astype(BF16)


def _inproj(x, sc, sh, gmix, wr, wt, bd, gqa, gka, gqb, gks, gkw):
    bsz, s, d = x.shape
    tm = TM_IN
    nt = s // tm
    nba = s // BLK_A
    const2 = lambda b, t: (0, 0)
    tok3 = lambda b, t: (b, t, 0)
    tok4 = lambda b, t: (b, 0, t, 0)
    tile4 = lambda b, t: (b, t, 0, 0)
    tr4 = lambda b, t: (b, 0, 0, t)
    in_specs = [
        pl.BlockSpec((1, tm, d), tok3),
        pl.BlockSpec((1, 1, 1, d), lambda b, t: (b, 1, 0, 0)),
        pl.BlockSpec((1, 1, 1, d), lambda b, t: (b, 0, 0, 0)),
        pl.BlockSpec((1, d), const2),
        pl.BlockSpec(wr.shape, const2),
        pl.BlockSpec(wt.shape, const2),
        pl.BlockSpec((512, 512), const2),
        pl.BlockSpec((512, tm), const2),
        pl.BlockSpec((1, 512), const2),
        pl.BlockSpec((512, tm), const2),
        pl.BlockSpec((1, LANES), const2),
        pl.BlockSpec((1, LANES), const2),
    ]
    out_shape = [
        jax.ShapeDtypeStruct((bsz, H_A, HEAD_DIM, s), F32),
        jax.ShapeDtypeStruct((bsz, H_A, s, LANES), BF16),
        jax.ShapeDtypeStruct((bsz, s // TQ, 512, TQ), BF16),
        jax.ShapeDtypeStruct((bsz, nba, 1, 512), F32),
        jax.ShapeDtypeStruct((bsz, H_B, HEAD_DIM, s), F32),
        jax.ShapeDtypeStruct((bsz, s, LANES), BF16),
        jax.ShapeDtypeStruct((bsz, s, LANES), BF16),
        jax.ShapeDtypeStruct((bsz, G_B, s, 2 * LANES), BF16),
        jax.ShapeDtypeStruct((bsz, s // TQ, LANES, TQ), BF16),
        jax.ShapeDtypeStruct((bsz, G_B, s, LANES), BF16),
        jax.ShapeDtypeStruct((bsz, s // TQ, LANES, TQ), BF16),
        jax.ShapeDtypeStruct((bsz, G_B, GATE_ROWS, s), F32),
    ]
    out_specs = [
        pl.BlockSpec((1, H_A, HEAD_DIM, tm), tr4),
        pl.BlockSpec((1, H_A, tm, LANES), tok4),
        pl.BlockSpec((1, tm // TQ, 512, TQ), tile4),
        pl.BlockSpec((1, tm // BLK_A, 1, 512), lambda b, t: (b, t, 0, 0)),
        pl.BlockSpec((1, H_B, HEAD_DIM, tm), tr4),
        pl.BlockSpec((1, tm, LANES), tok3),
        pl.BlockSpec((1, tm, LANES), tok3),
        pl.BlockSpec((1, G_B, tm, 2 * LANES), tok4),
        pl.BlockSpec((1, tm // TQ, LANES, TQ), tile4),
        pl.BlockSpec((1, G_B, tm, LANES), tok4),
        pl.BlockSpec((1, tm // TQ, LANES, TQ), tile4),
        pl.BlockSpec((1, G_B, GATE_ROWS, tm), tr4),
    ]
    return pl.pallas_call(
        _inproj_kernel,
        grid=(bsz, nt),
        in_specs=in_specs,
        out_specs=out_specs,
        out_shape=out_shape,
        compiler_params=_params(("parallel", "parallel")),
        name="inproj",
    )(x, sc, sh, gmix, wr, wt, bd, gqa, gka, gqb, gks, gkw)


def _compress_kernel(ck_ref, cv_ref, w1_ref, w2t_ref, pe_ref, gk_ref, ok_ref, ov_ref):
    half = CMP_STRIDE * HEAD_DIM
    for kv, c_ref in enumerate((ck_ref, cv_ref)):
        for g in range(G_B):
            c = c_ref[0, g]
            a = _dot(c, w1_ref[kv, 0:half, :])
            b = _dot(c, w1_ref[kv, half:2 * half, :])
            n = a.shape[0]
            b_next = pltpu.roll(b, n - 1, 0)
            pe_term = _dot(pe_ref[kv], w1_ref[kv])[0:1]
            hid = jax.nn.gelu(a + b_next + pe_term).astype(BF16)
            if kv == 0:
                y = _dot_nt(hid, w2t_ref[kv])
                ms = jnp.mean(y * y, axis=1, keepdims=True)
                y = y * lax.rsqrt(ms + EPS) * gk_ref[...]
                ok_ref[0, g] = jnp.concatenate([y, jnp.zeros_like(y)], axis=1).astype(BF16)
            else:
                ov_ref[0, g] = _dot_nt(w2t_ref[kv], hid).astype(BF16)


def _compress(ck, cv, w1, w2t, pe, gk):
    bsz, g, n, width = ck.shape
    blk = pl.BlockSpec((1, g, n, width), lambda b: (b, 0, 0, 0))
    full = lambda a: pl.BlockSpec(a.shape, lambda b: (0,) * a.ndim)
    return pl.pallas_call(
        _compress_kernel,
        grid=(bsz,),
        in_specs=[blk, blk, full(w1), full(w2t), full(pe), full(gk)],
        out_specs=[pl.BlockSpec((1, g, n, LANES), lambda b: (b, 0, 0, 0)),
                   pl.BlockSpec((1, g, HEAD_DIM, n), lambda b: (b, 0, 0, 0))],
        out_shape=[jax.ShapeDtypeStruct((bsz, g, n, LANES), BF16),
                   jax.ShapeDtypeStruct((bsz, g, HEAD_DIM, n), BF16)],
        compiler_params=_params(("parallel",)),
        name="compress",
    )(ck, cv, w1, w2t, pe, gk)


def _with_ones(vt):
    return jnp.concatenate([vt, jnp.ones((BF16_ROWS, vt.shape[1]), BF16)], axis=0)


def _col_max(s):
    while s.shape[0] > 8:
        half = s.shape[0] // 2
        s = jnp.maximum(s[0:half], s[half:2 * half])
    return jnp.max(s, axis=0, keepdims=True)


def _flash_update(carries, chains):
    m_news = []
    for carry, tiles in zip(carries, chains):
        tops = []
        for s, _, bias, top in tiles:
            top = _col_max(s) if top is None else top
            tops.append(top if bias is None else top + bias)
        m_news.append(functools.reduce(jnp.maximum, tops if carry is None else tops + [carry[0]]))
    pvs = [None] * len(chains)
    for t in range(max(len(tiles) for tiles in chains)):
        for c, tiles in enumerate(chains):
            if t < len(tiles):
                s, vt, bias, _ = tiles[t]
                p = jnp.exp2(s - (m_news[c] if bias is None else m_news[c] - bias)).astype(BF16)
                part = _dot(_with_ones(vt), p)
                pvs[c] = part if pvs[c] is None else pvs[c] + part
    outs = []
    for carry, m_new, pv in zip(carries, m_news, pvs):
        outs.append((m_new, pv if carry is None else jnp.exp2(carry[0] - m_new) * carry[1] + pv))
    return outs


def _flash_out(acc):
    return acc[0:HEAD_DIM] / acc[HEAD_DIM:HEAD_DIM + 1]


def _topk_rows(scores, index, k):
    scores = list(scores)
    picked = [jnp.zeros(sc.shape, F32) for sc in scores]
    for _ in range(k):
        mx = [jnp.max(sc, axis=0, keepdims=True) for sc in scores]
        cand = [jnp.where(sc == m, index, jnp.int32(1 << 20)) for sc, m in zip(scores, mx)]
        first = [jnp.min(c, axis=0, keepdims=True) for c in cand]
        hit = [index == f for f in first]
        picked = [jnp.where(h, 1.0, p) for h, p in zip(hit, picked)]
        scores = [jnp.where(h, -jnp.inf, sc) for h, sc in zip(hit, scores)]
    return picked


def _tile_rows(j):
    return pl.ds(pl.multiple_of(j * TQ, TQ), TQ)


def _moba_gate_kernel(q_ref, km_ref, o_ref, *, n_sel):
    nh, tg = q_ref.shape[1], q_ref.shape[3]
    t0 = pl.program_id(2) * tg
    shape = (km_ref.shape[2], tg)
    blk = lax.broadcasted_iota(jnp.int32, shape, 0)
    own = (lax.broadcasted_iota(jnp.int32, shape, 1) + t0) >> _LOG2_BLK_A
    valid = blk < own
    qs, gates = [], []
    for hh in range(nh):
        q = q_ref[0, hh]
        gate = _dot3(km_ref[0, hh], jnp.concatenate([q, jnp.zeros_like(q)], axis=0))
        qs.append(q)
        gates.append(jnp.where(valid, gate, -jnp.inf))
    for hh, picked in enumerate(_topk_rows(gates, blk, n_sel)):
        keep = jnp.where(valid, picked, 0.0) + jnp.where(blk == own, 1.0, 0.0)
        selb = jnp.where(keep > 0.0, 0.0, NEG)
        o_ref[0, hh] = jnp.concatenate([qs[hh], selb], axis=0).astype(BF16)


def _moba_gate(qa, km):
    bsz, nh, _, s = qa.shape
    tg = min(s, 2048)
    hpb = 4
    n_sel = max(1, min(TOPK_A, s // BLK_A - 1))
    return pl.pallas_call(
        functools.partial(_moba_gate_kernel, n_sel=n_sel),
        grid=(bsz, nh // hpb, s // tg),
        in_specs=[pl.BlockSpec((1, hpb, HEAD_DIM, tg), lambda b, h, t: (b, h, 0, t)),
                  pl.BlockSpec((1, hpb, HEAD_DIM, LANES), lambda b, h, t: (b, h, 0, 0))],
        out_specs=pl.BlockSpec((1, hpb, LANES, tg), lambda b, h, t: (b, h, 0, t)),
        out_shape=jax.ShapeDtypeStruct((bsz, nh, LANES, s), BF16),
        compiler_params=_params(("parallel", "parallel", "parallel")),
        name="moba_gate",
    )(qa, km)


def _far_tiles(qi):
    n_far = jnp.maximum(qi - 1, 0)
    left = jnp.maximum(n_far - 1, 0)
    off_left = jnp.where((n_far & 1) == 1, 0.0, NEG)
    return n_far >> 1, left, off_left


class _FarLoop:
    def __init__(self, n_pairs, heads, qk_tile, values, cbs, sa_ref, sb_ref, m_ref, acc_ref):
        self.n_pairs, self.heads, self.qk_tile, self.values, self.cbs = n_pairs, heads, qk_tile, values, cbs
        self.sa_ref, self.sb_ref, self.m_ref, self.acc_ref = sa_ref, sb_ref, m_ref, acc_ref
        self.last = jnp.maximum(n_pairs - 1, 0)

    def fetch(self, buf_ref, h, i):
        s_lo = self.qk_tile(h, 2 * i)
        s_hi = self.qk_tile(h, 2 * i + 1)
        buf_ref[h, 0:TQ, :] = s_lo
        buf_ref[h, TQ:2 * TQ, :] = s_hi
        return _col_max(s_lo), _col_max(s_hi)

    def consume(self, buf_ref, h, top, i):
        (m, acc), = _flash_update([(self.m_ref[h], self.acc_ref[h])], [[
            (buf_ref[h, 0:TQ, :], self.values(h, 2 * i), self.cbs[h], top[0]),
            (buf_ref[h, TQ:2 * TQ, :], self.values(h, 2 * i + 1), self.cbs[h], top[1])]])
        self.m_ref[h] = m
        self.acc_ref[h] = acc

    def first(self):
        return tuple(self.fetch(self.sa_ref, h, 0) for h in self.heads)

    def run(self, tops_first):
        def two_pairs(ia, tops_a):
            tops_b, tops_next = [], []
            for h in self.heads:
                tops_b.append(self.fetch(self.sb_ref, h, ia + 1))
                self.consume(self.sa_ref, h, tops_a[h], ia)
            for h in self.heads:
                tops_next.append(self.fetch(self.sa_ref, h, jnp.minimum(ia + 2, self.last)))
                self.consume(self.sb_ref, h, tops_b[h], ia + 1)
            return tuple(tops_next)

        def pairs(n, start, tops):
            for k in range(0, n, 2):
                tops = two_pairs(start + k, tops)
            return tops

        trip = FAR_PAIRS_PER_TRIP
        n_trips = self.n_pairs // trip
        tops = lax.fori_loop(0, n_trips, lambda q, t: pairs(trip, trip * q, t), tops_first)
        done = trip * n_trips
        n = trip // 2
        while n >= 2:
            has = ((self.n_pairs // n) & 1) == 1
            tops = lax.cond(has, functools.partial(pairs, n, done), lambda t: t, tops)
            done = done + jnp.where(has, n, 0)
            n //= 2
        tops_last = tops

        @pl.when((self.n_pairs & 1) == 1)
        def _():
            for h in self.heads:
                self.consume(self.sa_ref, h, tops_last[h], self.last)


def _moba_kernel(q_ref, k_ref, vt_ref, tab_ref, cb_ref, o_ref, sa_ref, sb_ref, m_ref, acc_ref):
    qi = pl.program_id(2)
    jp = jnp.maximum(qi - 1, 0)
    off_p = jnp.where(qi >= 1, 0.0, NEG)
    n_pairs, jl, off_l = _far_tiles(qi)
    heads = range(MOBA_HEADS)
    qaug = [q_ref[0, hh] for hh in heads]
    cbs = [cb_ref[hh][0:1, 0:1] for hh in heads]

    def scores(hh, j):
        return _dot(k_ref[0, hh, _tile_rows(j), :], qaug[hh])

    def values(hh, j):
        return vt_ref[0, j, hh * HEAD_DIM:(hh + 1) * HEAD_DIM, :]

    far = _FarLoop(n_pairs, heads, scores, values, cbs, sa_ref, sb_ref, m_ref, acc_ref)

    s_own = [scores(hh, qi) for hh in heads]
    s_prev = [scores(hh, jp) for hh in heads]
    s_left = [scores(hh, jl) for hh in heads]
    tops_first = far.first()
    chains = [[(s_own[hh] + tab_ref[hh, TQ:2 * TQ, :], values(hh, qi), None, None),
               (s_prev[hh] + tab_ref[hh, 0:TQ, :], values(hh, jp), off_p, None),
               (s_left[hh], values(hh, jl), cbs[hh] + off_l, None)] for hh in heads]
    for hh, (m, acc) in zip(heads, _flash_update([None] * len(chains), chains)):
        m_ref[hh] = m
        acc_ref[hh] = acc
    far.run(tops_first)
    out_t = jnp.concatenate([_flash_out(acc_ref[hh]) for hh in heads], axis=0)
    o_ref[0] = out_t.T.astype(BF16)


def _flash_scratch(n_heads):
    return [pltpu.VMEM((n_heads, 2 * TQ, TQ), F32),
            pltpu.VMEM((n_heads, 2 * TQ, TQ), F32),
            pltpu.VMEM((n_heads, 1, TQ), F32),
            pltpu.VMEM((n_heads, ACC_ROWS, TQ), F32)]


def _moba(qaug, kaug, vat, tab, cb):
    bsz, _, _, s = qaug.shape
    nq = s // TQ
    nh = MOBA_HEADS
    return pl.pallas_call(
        _moba_kernel,
        grid=(bsz, H_A // nh, nq),
        in_specs=[
            pl.BlockSpec((1, nh, LANES, TQ), lambda b, hp, qi: (b, hp, 0, qi)),
            pl.BlockSpec((1, nh, s, LANES), lambda b, hp, qi: (b, hp, 0, 0)),
            pl.BlockSpec((1, nq, nh * HEAD_DIM, TQ), lambda b, hp, qi: (b, 0, hp, 0)),
            pl.BlockSpec((nh, 2 * TQ, TQ), lambda b, hp, qi: (hp, 0, 0)),
            pl.BlockSpec((nh, 8, LANES), lambda b, hp, qi: (hp, 0, 0)),
        ],
        out_specs=pl.BlockSpec((1, TQ, nh * HEAD_DIM), lambda b, hp, qi: (b, qi, hp)),
        out_shape=jax.ShapeDtypeStruct((bsz, s, H_A * HEAD_DIM), BF16),
        scratch_shapes=_flash_scratch(nh),
        compiler_params=_params(("parallel", "parallel", "arbitrary")),
        name="moba",
    )(qaug, kaug, vat, tab, cb)


def _nsa_cmp_kernel(q_ref, kc_ref, vct_ref, ovt_ref, oc_ref, selb_ref, *, n_sel, n_cmp, n_parts):
    qi = pl.program_id(1)
    ncp = kc_ref.shape[2]
    tq = q_ref.shape[3]
    t0 = qi * tq
    zeros = jnp.zeros((HEAD_DIM, tq), BF16)
    qs = [jnp.concatenate([q_ref[0, hd].astype(BF16), zeros], axis=0) for hd in range(H_B)]

    def body(nk, nb):
        n_idx = lax.broadcasted_iota(jnp.int32, (nk, tq), 0)
        t_idx = lax.broadcasted_iota(jnp.int32, (nk, tq), 1) + t0
        mask = (n_idx * CMP_STRIDE + (CMP_LEN - 1) <= t_idx) & (n_idx < n_cmp)
        any_key = t_idx[0:1] >= CMP_LEN - 1
        blk = lax.broadcasted_iota(jnp.int32, (nb, tq), 0)
        cur = (lax.broadcasted_iota(jnp.int32, (nb, tq), 1) + t0) >> _LOG2_SEL_BLK
        ok = blk <= cur
        forced = (blk == 0) | (blk == cur) | (blk == cur - 1)
        ovt = ovt_ref[0:nb, 0:nk]
        scores = []
        for g in range(G_B):
            kc = kc_ref[0, g, 0:nk, :]
            vct = _with_ones(vct_ref[0, g, :, 0:nk])
            psum = jnp.zeros((nk, tq), F32)
            for hd in range(g * R_B, (g + 1) * R_B):
                z = jnp.where(mask, _dot(kc, qs[hd]), NEG)
                e = jnp.exp2(z - _col_max(z))
                acc = _dot(vct, e.astype(BF16))
                rinv = jnp.where(any_key, 1.0 / acc[HEAD_DIM:HEAD_DIM + 1], 0.0)
                oc_ref[0, hd] = acc[0:HEAD_DIM] * rinv
                psum = psum + e * rinv
            ph, pl_ = _split(psum)
            imp_t = _dot(ovt, ph) + _dot(ovt, pl_)
            scores.append(jnp.where(ok, jnp.where(forced, BIG, imp_t), -jnp.inf))
        for g, picked in enumerate(_topk_rows(scores, blk, n_sel)):
            selb_ref[0, g, 0:nb, :] = jnp.where(ok & (picked > 0.0), 0.0, NEG).astype(BF16)
            if nb < LANES:
                selb_ref[0, g, nb:LANES, :] = jnp.full((LANES - nb, tq), NEG, BF16)

    part = ncp // n_parts
    need = jnp.minimum(((qi + 1) * (tq // CMP_STRIDE) + part - 1) // part, n_parts)
    for v in range(1, n_parts + 1):
        pl.when(need == v)(functools.partial(body, v * part, min(LANES, v * part * CMP_STRIDE // SEL_BLK)))


def _cmp_parts(ncp):
    return 4 if ncp % (4 * LANES) == 0 else 1


def _nsa_cmp(qb, kcmp, vcmpt, ovt, n_cmp):
    bsz, _, _, s = qb.shape
    tq = CMP_TQ if s % CMP_TQ == 0 else TQ
    nq = s // tq
    ncp = kcmp.shape[2]
    n_sel = min(SEL_TOPK, s // SEL_BLK)
    n_parts = _cmp_parts(ncp)
    return pl.pallas_call(
        functools.partial(_nsa_cmp_kernel, n_sel=n_sel, n_cmp=n_cmp, n_parts=n_parts),
        grid=(bsz, nq),
        in_specs=[
            pl.BlockSpec((1, H_B, HEAD_DIM, tq), lambda b, qi: (b, 0, 0, qi)),
            pl.BlockSpec((1, G_B, ncp, LANES), lambda b, qi: (b, 0, 0, 0)),
            pl.BlockSpec((1, G_B, HEAD_DIM, ncp), lambda b, qi: (b, 0, 0, 0)),
            pl.BlockSpec((LANES, ncp), lambda b, qi: (0, 0)),
        ],
        out_specs=[
            pl.BlockSpec((1, H_B, HEAD_DIM, tq), lambda b, qi: (b, 0, 0, qi)),
            pl.BlockSpec((1, G_B, LANES, tq), lambda b, qi: (b, 0, 0, qi)),
        ],
        out_shape=[jax.ShapeDtypeStruct((bsz, H_B, HEAD_DIM, s), F32),
                   jax.ShapeDtypeStruct((bsz, G_B, LANES, s), BF16)],
        compiler_params=_params(("parallel", "parallel")),
        name="nsa_cmp",
    )(qb, kcmp, vcmpt, ovt)


def _nsa_main_kernel(q_ref, selb_ref, ks_ref, vst_ref, kw_ref, vwt_ref, oc_ref, gates_ref,
                     tabs_ref, tabw_ref, cb_ref, o_ref, sa_ref, sb_ref, m_ref, acc_ref):
    qi = pl.program_id(2)
    selb = selb_ref[0, 0]
    gates = gates_ref[0, 0]
    zeros = jnp.zeros((HEAD_DIM, TQ), BF16)
    j1 = jnp.maximum(qi - 1, 0)
    j2 = jnp.maximum(qi - 2, 0)
    off1 = jnp.where(qi >= 1, 0.0, NEG)
    off2 = jnp.where(qi >= 2, 0.0, NEG)
    n_pairs, jl, off_l = _far_tiles(qi)

    heads = range(R_B)
    qw = [jnp.concatenate([q_ref[0, r].astype(BF16), zeros], axis=0) for r in heads]
    qs = [jnp.concatenate([qw[r], selb], axis=0) for r in heads]
    cbs = [cb_ref[r][0:1, 0:1] for r in heads]

    def s_sel(r, j):
        return _dot(ks_ref[0, 0, _tile_rows(j), :], qs[r])

    def s_win(r, j):
        return _dot(kw_ref[0, 0, _tile_rows(j), :], qw[r])

    far = _FarLoop(n_pairs, heads, s_sel, lambda r, j: vst_ref[0, j], cbs, sa_ref, sb_ref, m_ref, acc_ref)

    sw = [[s_win(r, j) for r in heads] for j in (qi, j1, j2)]
    ss = [[s_sel(r, j) for r in heads] for j in (qi, j1, jl)]
    tops_first = far.first()

    chains = [[(sw[0][r] + tabw_ref[r, 2 * TQ:3 * TQ, :], vwt_ref[0, qi], None, None),
               (sw[1][r] + tabw_ref[r, TQ:2 * TQ, :], vwt_ref[0, j1], off1, None),
               (sw[2][r] + tabw_ref[r, 0:TQ, :], vwt_ref[0, j2], off2, None)] for r in heads]
    chains += [[(ss[0][r] + tabs_ref[r, TQ:2 * TQ, :], vst_ref[0, qi], None, None),
                (ss[1][r] + tabs_ref[r, 0:TQ, :], vst_ref[0, j1], off1, None),
                (ss[2][r], vst_ref[0, jl], cbs[r] + off_l, None)] for r in heads]
    done = _flash_update([None] * len(chains), chains)
    win = [_flash_out(acc) for _, acc in done[:R_B]]
    for r, (m, acc) in zip(heads, done[R_B:]):
        m_ref[r] = m
        acc_ref[r] = acc
    far.run(tops_first)

    outs = [gates[3 * r:3 * r + 1] * oc_ref[0, r]
            + gates[3 * r + 1:3 * r + 2] * _flash_out(acc_ref[r])
            + gates[3 * r + 2:3 * r + 3] * win[r] for r in heads]
    o_ref[0] = jnp.concatenate(outs, axis=0).T.astype(BF16)


def _nsa_main(qb, selb, ksaug, vst, kw, vwt, oc, gates, tabs, tabw, cb):
    bsz, _, _, s = qb.shape
    nq = s // TQ
    return pl.pallas_call(
        _nsa_main_kernel,
        grid=(bsz, G_B, nq),
        in_specs=[
            pl.BlockSpec((1, R_B, HEAD_DIM, TQ), lambda b, g, qi: (b, g, 0, qi)),
            pl.BlockSpec((1, 1, LANES, TQ), lambda b, g, qi: (b, g, 0, qi)),
            pl.BlockSpec((1, 1, s, 2 * LANES), lambda b, g, qi: (b, g, 0, 0)),
            pl.BlockSpec((1, nq, HEAD_DIM, TQ), lambda b, g, qi: (b, 0, g, 0)),
            pl.BlockSpec((1, 1, s, LANES), lambda b, g, qi: (b, g, 0, 0)),
            pl.BlockSpec((1, nq, HEAD_DIM, TQ), lambda b, g, qi: (b, 0, g, 0)),
            pl.BlockSpec((1, R_B, HEAD_DIM, TQ), lambda b, g, qi: (b, g, 0, qi)),
            pl.BlockSpec((1, 1, GATE_ROWS, TQ), lambda b, g, qi: (b, g, 0, qi)),
            pl.BlockSpec((R_B, 2 * TQ, TQ), lambda b, g, qi: (g, 0, 0)),
            pl.BlockSpec((R_B, 3 * TQ, TQ), lambda b, g, qi: (g, 0, 0)),
            pl.BlockSpec((R_B, 8, LANES), lambda b, g, qi: (g, 0, 0)),
        ],
        out_specs=pl.BlockSpec((1, TQ, R_B * HEAD_DIM), lambda b, g, qi: (b, qi, g)),
        out_shape=jax.ShapeDtypeStruct((bsz, s, H_B * HEAD_DIM), BF16),
        scratch_shapes=_flash_scratch(R_B),
        compiler_params=_params(("parallel", "parallel", "arbitrary")),
        name="nsa_main",
    )(qb, selb, ksaug, vst, kw, vwt, oc, gates, tabs, tabw, cb)


def _out_ffn_kernel(x_ref, oa_ref, ob_ref, wo_ref, gtm_ref, gffn_ref, scf_ref, shf_ref, gtf_ref,
                    wg_ref, wu_ref, wd_ref, o_ref, x1_ref, h_ref, acc_ref):
    f = pl.program_id(2)
    half = oa_ref.shape[2]

    @pl.when(f == 0)
    def _():
        mix = _dot(oa_ref[0], wo_ref[0:half, :]) + _dot(ob_ref[0], wo_ref[half:2 * half, :])
        x1 = x_ref[0] + gtm_ref[0, 0] * mix
        x1_ref[...] = x1
        h = _rms_rows(x1, gffn_ref[...]) * (1.0 + scf_ref[0, 0]) + shf_ref[0, 0]
        h_ref[...] = h.astype(BF16)

    h = h_ref[...]
    gate = _dot(h, wg_ref[...])
    up = _dot(h, wu_ref[...])
    act = (gate * jax.nn.sigmoid(gate) * up).astype(BF16)
    part = _dot(act, wd_ref[...])

    @pl.when(f == 0)
    def _():
        acc_ref[...] = part

    @pl.when(f > 0)
    def _():
        acc_ref[...] += part

    @pl.when(f == pl.num_programs(2) - 1)
    def _():
        o_ref[0] = x1_ref[...] + gtf_ref[0, 0] * acc_ref[...]


def _out_ffn(x, oa, ob, wo, mod4, gffn, wgu, wd):
    bsz, s, d = x.shape
    fh = wd.shape[0]
    tm = TM_FFN
    tf = fh // 2 if (fh // 2) % LANES == 0 else fh
    nf = fh // tf
    tok = lambda b, t, f: (b, t, 0)
    modspec = lambda k: pl.BlockSpec((1, 1, 1, d), lambda b, t, f: (b, k, 0, 0))
    return pl.pallas_call(
        _out_ffn_kernel,
        grid=(bsz, s // tm, nf),
        in_specs=[
            pl.BlockSpec((1, tm, d), tok),
            pl.BlockSpec((1, tm, oa.shape[2]), tok),
            pl.BlockSpec((1, tm, ob.shape[2]), tok),
            pl.BlockSpec((d, d), lambda b, t, f: (0, 0)),
            modspec(2),
            pl.BlockSpec((1, d), lambda b, t, f: (0, 0)),
            modspec(4),
            modspec(3),
            modspec(5),
            pl.BlockSpec((d, tf), lambda b, t, f: (0, f)),
            pl.BlockSpec((d, tf), lambda b, t, f: (0, f + nf)),
            pl.BlockSpec((tf, d), lambda b, t, f: (f, 0)),
        ],
        out_specs=pl.BlockSpec((1, tm, d), tok),
        out_shape=jax.ShapeDtypeStruct((bsz, s, d), F32),
        scratch_shapes=[pltpu.VMEM((tm, d), F32), pltpu.VMEM((tm, d), BF16), pltpu.VMEM((tm, d), F32)],
        compiler_params=_params(("parallel", "parallel", "arbitrary")),
        name="out_ffn",
    )(x, oa, ob, wo, mod4, gffn, mod4, mod4, mod4, wgu, wgu, wd)


def _t5_bucket_np(d):
    max_exact = N_BUCKETS // 2
    d = np.maximum(d, 0)
    df = np.maximum(d, 1).astype(np.float64)
    large = max_exact + (np.log(df / max_exact) / math.log(MAX_DIST / max_exact)
                         * (N_BUCKETS - max_exact)).astype(np.int64)
    large = np.minimum(large, N_BUCKETS - 1)
    return np.where(d < max_exact, d, large).astype(np.int32)


def _bias_expand_kernel(tab_ref, bucket_ref, o_ref):
    hd = pl.program_id(0)
    bucket = bucket_ref[...]
    acc = jnp.full(bucket.shape, NEG, F32)
    for b in range(N_BUCKETS):
        acc = jnp.where(bucket == b, tab_ref[hd, b], acc)
    o_ref[0] = acc


def _bias_expand(tab, bucket):
    nh = tab.shape[0]
    return pl.pallas_call(
        _bias_expand_kernel,
        grid=(nh,),
        in_specs=[pl.BlockSpec(memory_space=pltpu.SMEM),
                  pl.BlockSpec(bucket.shape, lambda h: (0, 0))],
        out_specs=pl.BlockSpec((1,) + bucket.shape, lambda h: (h, 0, 0)),
        out_shape=jax.ShapeDtypeStruct((nh,) + bucket.shape, F32),
        compiler_params=_params(("parallel",)),
        name="bias_expand",
    )(tab, jnp.asarray(bucket, jnp.int32))


def _bias_tables(rel_bias):
    tab = rel_bias.T.astype(F32) * LOG2_E
    i = np.arange(TQ)[None, :]
    d_near = i + TQ - np.arange(2 * TQ)[:, None]
    near = _bias_expand(tab, np.where(d_near >= 0, _t5_bucket_np(d_near), -1))
    d_win = i + 2 * TQ - np.arange(3 * TQ)[:, None]
    ok_win = (d_win >= 0) & (d_win < WINDOW)
    win = _bias_expand(tab[H_A:], np.where(ok_win, _t5_bucket_np(d_win), -1))
    far = jnp.broadcast_to(tab[:, N_BUCKETS - 1][:, None, None], (tab.shape[0], 8, LANES))
    return near, win, far


def _overlap_t(n_cmp_pad, n_cmp):
    cs = np.arange(n_cmp_pad)[None, :] * CMP_STRIDE
    ss = np.arange(LANES)[:, None] * SEL_BLK
    ov = (cs < ss + SEL_BLK) & (cs + CMP_LEN > ss) & (np.arange(n_cmp_pad)[None, :] < n_cmp)
    return jnp.asarray(ov.astype(np.float32), BF16)


def _block_diag(n):
    m = (np.arange(n)[:, None] // HEAD_DIM == np.arange(n)[None, :] // HEAD_DIM)
    return jnp.asarray(m.astype(np.float32) / HEAD_DIM, BF16)


def kernel(x, c, rel_bias, w_ada, b_ada, g_mix, w_in, q_norm_a, k_norm_a, q_norm_b, k_norm_cmp,
           k_norm_sel, k_norm_win, cmp_pe_k, cmp_w1_k, cmp_w2_k, cmp_pe_v, cmp_w1_v, cmp_w2_v,
           w_out, g_ffn, w_gu, w_down):
    bsz, s, d = x.shape
    depth = w_ada.shape[0]
    assert s % TM_IN == 0 and s % TM_FFN == 0 and s % (2 * TQ) == 0
    assert s // BLK_A <= HEAD_DIM and s // SEL_BLK <= LANES
    assert WINDOW == 2 * TQ and BLK_A == TQ and MAX_DIST <= TQ
    n_chunks = s // CMP_STRIDE
    n_cmp = (s - CMP_LEN) // CMP_STRIDE + 1
    scale = HEAD_DIM ** -0.5 * LOG2_E
    hd = HEAD_DIM

    near, win, far = _bias_tables(rel_bias)
    ovt = _overlap_t(n_chunks, n_cmp)
    bd = _block_diag(512)
    tile = lambda g, n: jnp.tile(g.astype(F32), n).reshape(1, -1)
    tile_t = lambda g, n: jnp.broadcast_to(jnp.tile(g.astype(F32), n)[:, None], (n * hd, TM_IN))

    for l in range(depth):
        mod = _ada(c, w_ada[l], b_ada[l])
        mod4 = mod.reshape(bsz, ADA_CHUNKS, 1, d)

        wl = w_in[l]
        cols = np.cumsum([0, H_A * hd, H_A * hd, H_A * hd, H_B * hd] + [G_B * hd] * 6)
        qa_c, ka_c, va_c, qb_c, kc_c, vc_c, ks_c, vs_c, kw_c, vw_c = [
            wl[:, int(a):int(b)] for a, b in zip(cols[:-1], cols[1:])]
        gl = wl[:, int(cols[-1]):].reshape(d, G_B, 3 * R_B)
        gl = jnp.pad(gl, ((0, 0), (0, 0), (0, GATE_ROWS - 3 * R_B))).reshape(d, G_B * GATE_ROWS)
        w_rows = jnp.concatenate([ka_c, kc_c, vc_c, ks_c, kw_c], axis=1).astype(BF16)
        w_t = jnp.concatenate([qa_c, va_c, qb_c, vs_c, vw_c, gl], axis=1).T.astype(BF16)

        (qa, kaug, vat, kmean, qb, kc, vc, ksaug, vst, kw, vwt, gates) = _inproj(
            x, mod4, mod4, g_mix[l].reshape(1, d), w_rows, w_t, bd,
            tile_t(q_norm_a[l], H_A) * scale, tile(k_norm_a[l], H_A), tile_t(q_norm_b[l], H_B) * scale,
            tile(k_norm_sel[l], G_B), tile(k_norm_win[l], G_B))

        nba = s // BLK_A
        km = kmean.reshape(bsz, nba, H_A, hd).transpose(0, 2, 1, 3)
        km = jnp.pad(km, ((0, 0), (0, 0), (0, hd - nba), (0, LANES - hd)))
        o_a = _moba(_moba_gate(qa, km), kaug, vat, near[:H_A], far[:H_A])

        chunks = lambda t: t.reshape(bsz, s, G_B, hd).transpose(0, 2, 1, 3).reshape(
            bsz, G_B, n_chunks, CMP_STRIDE * hd)
        w1 = jnp.stack([cmp_w1_k[l], cmp_w1_v[l]]).astype(BF16)
        w2t = jnp.stack([cmp_w2_k[l].T, cmp_w2_v[l].T]).astype(BF16)
        pe = jnp.stack([cmp_pe_k[l], cmp_pe_v[l]]).reshape(2, 1, CMP_LEN * hd)
        pe = jnp.broadcast_to(pe, (2, 8, CMP_LEN * hd)).astype(BF16)
        kcmp, vcmpt = _compress(chunks(kc), chunks(vc), w1, w2t, pe,
                                k_norm_cmp[l].astype(F32).reshape(1, hd))

        oc, selb = _nsa_cmp(qb, kcmp, vcmpt, ovt, n_cmp)
        o_b = _nsa_main(qb, selb, ksaug, vst, kw, vwt, oc, gates, near[H_A:], win, far[H_A:])

        x = _out_ffn(x, o_a, o_b, w_out[l].astype(BF16), mod4, g_ffn[l].reshape(1, d),
                     w_gu[l].astype(BF16), w_down[l].astype(BF16))
    return x
```

```python
import functools
import math

import jax
import jax.numpy as jnp
import numpy as np
from jax import lax
from jax.experimental import pallas as pl
from jax.experimental.pallas import tpu as pltpu

F32 = jnp.float32
BF16 = jnp.bfloat16

HEAD_DIM = 64
LANES = 128
BF16_ROWS = 16
H_A = 8
H_B = 8
G_B = 2
R_B = H_B // G_B
D_A = H_A * HEAD_DIM
D_B = H_B * HEAD_DIM
D_KV = G_B * HEAD_DIM
BLK_A = 256
TOPK_A = 3
CMP_LEN = 32
CMP_STRIDE = 16
CMP_HIDDEN = 256
SEL_BLK = 64
SEL_TOPK = 16
WINDOW = 512
N_BUCKETS = 32
MAX_DIST = 128
ADA_CHUNKS = 6
NEG = -1e30
BIG = 1e9
EPS = 1e-6
LOG2_E = math.log2(math.e)

_LOG2_BLK_A = BLK_A.bit_length() - 1
_LOG2_SEL_BLK = SEL_BLK.bit_length() - 1

TQ = 256
TM_IN = 1024
TM_FFN = 512
ADA_TN = 512
GATE_TQ = 2048
GATE_HEADS = 4
CMP_TQ = 512
FAR_PAIRS_PER_TRIP = 8
MOBA_HEADS = 4
GATE_ROWS = 16
ACC_ROWS = HEAD_DIM + BF16_ROWS
V7X_VMEM_BYTES = 64 * 1024 * 1024
VMEM_LIMIT = V7X_VMEM_BYTES * 7 // 8


def _dot(a, b):
    return jnp.dot(a, b, preferred_element_type=F32)


def _dot_nt(a, b):
    return lax.dot_general(a, b, (((1,), (1,)), ((), ())), preferred_element_type=F32)


def _split(a):
    hi = a.astype(BF16)
    lo = (a - hi.astype(F32)).astype(BF16)
    return hi, lo


def _dot3(a, b):
    ah, al = _split(a)
    bh, bl = _split(b)
    return _dot(ah, bh) + (_dot(al, bh) + _dot(ah, bl))


def _params(sem):
    return pltpu.CompilerParams(dimension_semantics=sem, vmem_limit_bytes=VMEM_LIMIT)


def _ada_kernel(c_ref, w_ref, b_ref, o_ref):
    c = c_ref[...]
    o_ref[...] = _dot3(c * jax.nn.sigmoid(c), w_ref[...]) + b_ref[...]


def _ada(c, w, b):
    bsz, d = c.shape
    n = w.shape[1]
    tn = ADA_TN
    return pl.pallas_call(
        _ada_kernel,
        grid=(n // tn,),
        in_specs=[pl.BlockSpec((bsz, d), lambda j: (0, 0)),
                  pl.BlockSpec((d, tn), lambda j: (0, j)),
                  pl.BlockSpec((1, tn), lambda j: (0, j))],
        out_specs=pl.BlockSpec((bsz, tn), lambda j: (0, j)),
        out_shape=jax.ShapeDtypeStruct((bsz, n), F32),
        compiler_params=_params(("arbitrary",)),
        name="ada",
    )(c, w, b.reshape(1, n))


def _rms_rows(xf, g):
    ms = jnp.mean(xf * xf, axis=-1, keepdims=True)
    return xf * lax.rsqrt(ms + EPS) * g


def _head_norm(t, bd, gain):
    hi, lo = _split(t * t)
    ms = _dot(hi, bd) + _dot(lo, bd)
    return t * lax.rsqrt(ms + EPS) * gain


def _head_norm_t(t, gain):
    heads = []
    for hd in range(t.shape[0] // HEAD_DIM):
        th = t[hd * HEAD_DIM:(hd + 1) * HEAD_DIM]
        ms = jnp.mean(th * th, axis=0, keepdims=True)
        heads.append(th * lax.rsqrt(ms + EPS) * gain[hd * HEAD_DIM:(hd + 1) * HEAD_DIM])
    return heads


def _inproj_kernel(x_ref, sc_ref, sh_ref, gmix_ref, wr_ref, wt_ref, bd_ref, gqa_ref, gka_ref, gqb_ref,
                   gks_ref, gkw_ref,
                   qa_ref, kaug_ref, va_ref, kmean_ref, qb_ref, kc_ref, vc_ref, ksaug_ref,
                   vs_ref, kw_ref, vw_ref, gates_ref):
    tm = x_ref.shape[1]
    ti = pl.program_id(1)
    xf = x_ref[0]
    h = _rms_rows(xf, gmix_ref[...]) * (1.0 + sc_ref[0, 0]) + sh_ref[0, 0]
    hb = h.astype(BF16)

    def proj(c0, c1):
        return _dot(hb, wr_ref[:, c0:c1])

    def proj_t(r0, r1):
        return _dot_nt(wt_ref[r0:r1, :], hb)

    bd = bd_ref[...]
    bd2 = bd_ref[0:LANES, 0:LANES]
    lane = lax.broadcasted_iota(jnp.int32, (tm, LANES), 1)
    row = lax.broadcasted_iota(jnp.int32, (tm, LANES), 0) + ti * tm
    low = lane < HEAD_DIM

    def k_in_low(pair, odd):
        return pltpu.roll(pair, HEAD_DIM, 1) if odd else pair

    t_qa, t_va, t_qb = 0, D_A, 2 * D_A
    t_vs, t_vw, t_gl = t_qb + D_B, t_qb + D_B + D_KV, t_qb + D_B + 2 * D_KV
    c_ka, c_kc, c_vc, c_ks, c_kw = 0, D_A, D_A + D_KV, D_A + 2 * D_KV, D_A + 3 * D_KV

    for hd, qh in enumerate(_head_norm_t(proj_t(t_qa, t_va), gqa_ref[...])):
        qa_ref[0, hd] = qh

    def put_tiles(ref, vt):
        for i in range(tm // TQ):
            ref[0, i] = vt[:, i * TQ:(i + 1) * TQ].astype(BF16)

    put_tiles(va_ref, proj_t(t_va, t_qb))
    for hd, qh in enumerate(_head_norm_t(proj_t(t_qb, t_vs), gqb_ref[...])):
        qb_ref[0, hd] = qh
    put_tiles(vs_ref, proj_t(t_vs, t_vw))
    put_tiles(vw_ref, proj_t(t_vw, t_gl))
    gl = jax.nn.sigmoid(proj_t(t_gl, t_gl + G_B * GATE_ROWS))
    for g in range(G_B):
        gates_ref[0, g] = gl[g * GATE_ROWS:(g + 1) * GATE_ROWS]

    ka = _head_norm(proj(c_ka, c_kc), bd, gka_ref[...])
    oh_a = jnp.where(lane - HEAD_DIM == (row >> _LOG2_BLK_A), 1.0, 0.0)
    for hd in range(H_A):
        pair = ka[:, (hd // 2) * LANES:(hd // 2 + 1) * LANES]
        kaug_ref[0, hd] = jnp.where(low, k_in_low(pair, hd % 2), oh_a).astype(BF16)
    for i in range(tm // BLK_A):
        kmean_ref[0, i] = jnp.mean(ka[i * BLK_A:(i + 1) * BLK_A], axis=0, keepdims=True)

    kc_ref[0] = proj(c_kc, c_vc).astype(BF16)
    vc_ref[0] = proj(c_vc, c_ks).astype(BF16)

    ks = _head_norm(proj(c_ks, c_kw), bd2, gks_ref[...])
    kw = _head_norm(proj(c_kw, c_kw + D_KV), bd2, gkw_ref[...])
    oh_s = jnp.where(lane == (row >> _LOG2_SEL_BLK), 1.0, 0.0).astype(BF16)
    for g in range(G_B):
        ksaug_ref[0, g] = jnp.concatenate(
            [jnp.where(low, k_in_low(ks, g), 0.0).astype(BF16), oh_s], axis=1)
        kw_ref[0, g] = jnp.where(low, k_in_low(kw, g), 0.0).astype(BF16)


def _inproj(x, sc, sh, gmix, wr, wt, bd, gqa, gka, gqb, gks, gkw):
    bsz, s, d = x.shape
    tm = TM_IN
    nt = s // tm
    nba = s // BLK_A
    const2 = lambda b, t: (0, 0)
    tok3 = lambda b, t: (b, t, 0)
    tok4 = lambda b, t: (b, 0, t, 0)
    tile4 = lambda b, t: (b, t, 0, 0)
    tr4 = lambda b, t: (b, 0, 0, t)
    in_specs = [
        pl.BlockSpec((1, tm, d), tok3),
        pl.BlockSpec((1, 1, 1, d), lambda b, t: (b, 1, 0, 0)),
        pl.BlockSpec((1, 1, 1, d), lambda b, t: (b, 0, 0, 0)),
        pl.BlockSpec((1, d), const2),
        pl.BlockSpec(wr.shape, const2),
        pl.BlockSpec(wt.shape, const2),
        pl.BlockSpec((D_A, D_A), const2),
        pl.BlockSpec((D_A, tm), const2),
        pl.BlockSpec((1, D_A), const2),
        pl.BlockSpec((D_B, tm), const2),
        pl.BlockSpec((1, LANES), const2),
        pl.BlockSpec((1, LANES), const2),
    ]
    out_shape = [
        jax.ShapeDtypeStruct((bsz, H_A, HEAD_DIM, s), F32),
        jax.ShapeDtypeStruct((bsz, H_A, s, LANES), BF16),
        jax.ShapeDtypeStruct((bsz, s // TQ, D_A, TQ), BF16),
        jax.ShapeDtypeStruct((bsz, nba, 1, D_A), F32),
        jax.ShapeDtypeStruct((bsz, H_B, HEAD_DIM, s), F32),
        jax.ShapeDtypeStruct((bsz, s, LANES), BF16),
        jax.ShapeDtypeStruct((bsz, s, LANES), BF16),
        jax.ShapeDtypeStruct((bsz, G_B, s, 2 * LANES), BF16),
        jax.ShapeDtypeStruct((bsz, s // TQ, LANES, TQ), BF16),
        jax.ShapeDtypeStruct((bsz, G_B, s, LANES), BF16),
        jax.ShapeDtypeStruct((bsz, s // TQ, LANES, TQ), BF16),
        jax.ShapeDtypeStruct((bsz, G_B, GATE_ROWS, s), F32),
    ]
    out_specs = [
        pl.BlockSpec((1, H_A, HEAD_DIM, tm), tr4),
        pl.BlockSpec((1, H_A, tm, LANES), tok4),
        pl.BlockSpec((1, tm // TQ, D_A, TQ), tile4),
        pl.BlockSpec((1, tm // BLK_A, 1, D_A), lambda b, t: (b, t, 0, 0)),
        pl.BlockSpec((1, H_B, HEAD_DIM, tm), tr4),
        pl.BlockSpec((1, tm, LANES), tok3),
        pl.BlockSpec((1, tm, LANES), tok3),
        pl.BlockSpec((1, G_B, tm, 2 * LANES), tok4),
        pl.BlockSpec((1, tm // TQ, LANES, TQ), tile4),
        pl.BlockSpec((1, G_B, tm, LANES), tok4),
        pl.BlockSpec((1, tm // TQ, LANES, TQ), tile4),
        pl.BlockSpec((1, G_B, GATE_ROWS, tm), tr4),
    ]
    return pl.pallas_call(
        _inproj_kernel,
        grid=(bsz, nt),
        in_specs=in_specs,
        out_specs=out_specs,
        out_shape=out_shape,
        compiler_params=_params(("parallel", "parallel")),
        name="inproj",
    )(x, sc, sh, gmix, wr, wt, bd, gqa, gka, gqb, gks, gkw)


def _compress_kernel(ck_ref, cv_ref, w1_ref, w2t_ref, pe_ref, gk_ref, ok_ref, ov_ref):
    half = CMP_STRIDE * HEAD_DIM
    for kv, c_ref in enumerate((ck_ref, cv_ref)):
        for g in range(G_B):
            c = c_ref[0, g]
            a = _dot(c, w1_ref[kv, 0:half, :])
            b = _dot(c, w1_ref[kv, half:2 * half, :])
            n = a.shape[0]
            b_next = pltpu.roll(b, n - 1, 0)
            pe_term = _dot(pe_ref[kv], w1_ref[kv])[0:1]
            hid = jax.nn.gelu(a + b_next + pe_term).astype(BF16)
            if kv == 0:
                y = _dot_nt(hid, w2t_ref[kv])
                ms = jnp.mean(y * y, axis=1, keepdims=True)
                y = y * lax.rsqrt(ms + EPS) * gk_ref[...]
                ok_ref[0, g] = jnp.concatenate([y, jnp.zeros_like(y)], axis=1).astype(BF16)
            else:
                ov_ref[0, g] = _dot_nt(w2t_ref[kv], hid).astype(BF16)


def _compress(ck, cv, w1, w2t, pe, gk):
    bsz, g, n, width = ck.shape
    blk = pl.BlockSpec((1, g, n, width), lambda b: (b, 0, 0, 0))
    full = lambda a: pl.BlockSpec(a.shape, lambda b: (0,) * a.ndim)
    return pl.pallas_call(
        _compress_kernel,
        grid=(bsz,),
        in_specs=[blk, blk, full(w1), full(w2t), full(pe), full(gk)],
        out_specs=[pl.BlockSpec((1, g, n, LANES), lambda b: (b, 0, 0, 0)),
                   pl.BlockSpec((1, g, HEAD_DIM, n), lambda b: (b, 0, 0, 0))],
        out_shape=[jax.ShapeDtypeStruct((bsz, g, n, LANES), BF16),
                   jax.ShapeDtypeStruct((bsz, g, HEAD_DIM, n), BF16)],
        compiler_params=_params(("parallel",)),
        name="compress",
    )(ck, cv, w1, w2t, pe, gk)


def _with_ones(vt):
    return jnp.concatenate([vt, jnp.ones((BF16_ROWS, vt.shape[1]), BF16)], axis=0)


def _col_max(s):
    while s.shape[0] > 8:
        half = s.shape[0] // 2
        s = jnp.maximum(s[0:half], s[half:2 * half])
    return jnp.max(s, axis=0, keepdims=True)


def _flash_update(carries, chains):
    m_news = []
    for carry, tiles in zip(carries, chains):
        tops = []
        for s, _, bias, top in tiles:
            top = _col_max(s) if top is None else top
            tops.append(top if bias is None else top + bias)
        m_news.append(functools.reduce(jnp.maximum, tops if carry is None else tops + [carry[0]]))
    pvs = [None] * len(chains)
    for t in range(max(len(tiles) for tiles in chains)):
        for c, tiles in enumerate(chains):
            if t < len(tiles):
                s, vt, bias, _ = tiles[t]
                p = jnp.exp2(s - (m_news[c] if bias is None else m_news[c] - bias)).astype(BF16)
                part = _dot(_with_ones(vt), p)
                pvs[c] = part if pvs[c] is None else pvs[c] + part
    outs = []
    for carry, m_new, pv in zip(carries, m_news, pvs):
        outs.append((m_new, pv if carry is None else jnp.exp2(carry[0] - m_new) * carry[1] + pv))
    return outs


def _flash_out(acc):
    return acc[0:HEAD_DIM] / acc[HEAD_DIM:HEAD_DIM + 1]


def _topk_rows(scores, index, k):
    scores = list(scores)
    picked = [jnp.zeros(sc.shape, F32) for sc in scores]
    for _ in range(k):
        mx = [jnp.max(sc, axis=0, keepdims=True) for sc in scores]
        cand = [jnp.where(sc == m, index, jnp.int32(1 << 20)) for sc, m in zip(scores, mx)]
        first = [jnp.min(c, axis=0, keepdims=True) for c in cand]
        hit = [index == f for f in first]
        picked = [jnp.where(h, 1.0, p) for h, p in zip(hit, picked)]
        scores = [jnp.where(h, -jnp.inf, sc) for h, sc in zip(hit, scores)]
    return picked


def _tile_rows(j):
    return pl.ds(pl.multiple_of(j * TQ, TQ), TQ)


def _moba_gate_kernel(q_ref, km_ref, o_ref, *, n_sel):
    nh, tg = q_ref.shape[1], q_ref.shape[3]
    t0 = pl.program_id(2) * tg
    shape = (km_ref.shape[2], tg)
    blk = lax.broadcasted_iota(jnp.int32, shape, 0)
    own = (lax.broadcasted_iota(jnp.int32, shape, 1) + t0) >> _LOG2_BLK_A
    valid = blk < own
    qs, gates = [], []
    for hh in range(nh):
        q = q_ref[0, hh]
        gate = _dot3(km_ref[0, hh], jnp.concatenate([q, jnp.zeros_like(q)], axis=0))
        qs.append(q)
        gates.append(jnp.where(valid, gate, -jnp.inf))
    for hh, picked in enumerate(_topk_rows(gates, blk, n_sel)):
        keep = jnp.where(valid, picked, 0.0) + jnp.where(blk == own, 1.0, 0.0)
        selb = jnp.where(keep > 0.0, 0.0, NEG)
        o_ref[0, hh] = jnp.concatenate([qs[hh], selb], axis=0).astype(BF16)


def _moba_gate(qa, km):
    bsz, nh, _, s = qa.shape
    tg = min(s, GATE_TQ)
    hpb = GATE_HEADS
    n_sel = max(1, min(TOPK_A, s // BLK_A - 1))
    return pl.pallas_call(
        functools.partial(_moba_gate_kernel, n_sel=n_sel),
        grid=(bsz, nh // hpb, s // tg),
        in_specs=[pl.BlockSpec((1, hpb, HEAD_DIM, tg), lambda b, h, t: (b, h, 0, t)),
                  pl.BlockSpec((1, hpb, HEAD_DIM, LANES), lambda b, h, t: (b, h, 0, 0))],
        out_specs=pl.BlockSpec((1, hpb, LANES, tg), lambda b, h, t: (b, h, 0, t)),
        out_shape=jax.ShapeDtypeStruct((bsz, nh, LANES, s), BF16),
        compiler_params=_params(("parallel", "parallel", "parallel")),
        name="moba_gate",
    )(qa, km)


def _far_tiles(qi):
    n_far = jnp.maximum(qi - 1, 0)
    left = jnp.maximum(n_far - 1, 0)
    off_left = jnp.where((n_far & 1) == 1, 0.0, NEG)
    return n_far >> 1, left, off_left


class _FarLoop:
    def __init__(self, n_pairs, heads, qk_tile, values, cbs, sa_ref, sb_ref, m_ref, acc_ref):
        self.n_pairs, self.heads, self.qk_tile, self.values, self.cbs = n_pairs, heads, qk_tile, values, cbs
        self.sa_ref, self.sb_ref, self.m_ref, self.acc_ref = sa_ref, sb_ref, m_ref, acc_ref
        self.last = jnp.maximum(n_pairs - 1, 0)

    def fetch(self, buf_ref, h, i):
        s_lo = self.qk_tile(h, 2 * i)
        s_hi = self.qk_tile(h, 2 * i + 1)
        buf_ref[h, 0:TQ, :] = s_lo
        buf_ref[h, TQ:2 * TQ, :] = s_hi
        return _col_max(s_lo), _col_max(s_hi)

    def consume(self, buf_ref, h, top, i):
        (m, acc), = _flash_update([(self.m_ref[h], self.acc_ref[h])], [[
            (buf_ref[h, 0:TQ, :], self.values(h, 2 * i), self.cbs[h], top[0]),
            (buf_ref[h, TQ:2 * TQ, :], self.values(h, 2 * i + 1), self.cbs[h], top[1])]])
        self.m_ref[h] = m
        self.acc_ref[h] = acc

    def first(self):
        return tuple(self.fetch(self.sa_ref, h, 0) for h in self.heads)

    def run(self, tops_first):
        def two_pairs(ia, tops_a):
            tops_b, tops_next = [], []
            for h in self.heads:
                tops_b.append(self.fetch(self.sb_ref, h, ia + 1))
                self.consume(self.sa_ref, h, tops_a[h], ia)
            for h in self.heads:
                tops_next.append(self.fetch(self.sa_ref, h, jnp.minimum(ia + 2, self.last)))
                self.consume(self.sb_ref, h, tops_b[h], ia + 1)
            return tuple(tops_next)

        def pairs(n, start, tops):
            for k in range(0, n, 2):
                tops = two_pairs(start + k, tops)
            return tops

        trip = FAR_PAIRS_PER_TRIP
        n_trips = self.n_pairs // trip
        tops = lax.fori_loop(0, n_trips, lambda q, t: pairs(trip, trip * q, t), tops_first)
        done = trip * n_trips
        n = trip // 2
        while n >= 2:
            has = ((self.n_pairs // n) & 1) == 1
            tops = lax.cond(has, functools.partial(pairs, n, done), lambda t: t, tops)
            done = done + jnp.where(has, n, 0)
            n //= 2
        tops_last = tops

        @pl.when((self.n_pairs & 1) == 1)
        def _():
            for h in self.heads:
                self.consume(self.sa_ref, h, tops_last[h], self.last)


def _moba_kernel(q_ref, k_ref, vt_ref, tab_ref, cb_ref, o_ref, sa_ref, sb_ref, m_ref, acc_ref):
    qi = pl.program_id(2)
    jp = jnp.maximum(qi - 1, 0)
    off_p = jnp.where(qi >= 1, 0.0, NEG)
    n_pairs, jl, off_l = _far_tiles(qi)
    heads = range(MOBA_HEADS)
    qaug = [q_ref[0, hh] for hh in heads]
    cbs = [cb_ref[hh][0:1, 0:1] for hh in heads]

    def scores(hh, j):
        return _dot(k_ref[0, hh, _tile_rows(j), :], qaug[hh])

    def values(hh, j):
        return vt_ref[0, j, hh * HEAD_DIM:(hh + 1) * HEAD_DIM, :]

    far = _FarLoop(n_pairs, heads, scores, values, cbs, sa_ref, sb_ref, m_ref, acc_ref)

    s_own = [scores(hh, qi) for hh in heads]
    s_prev = [scores(hh, jp) for hh in heads]
    s_left = [scores(hh, jl) for hh in heads]
    tops_first = far.first()
    chains = [[(s_own[hh] + tab_ref[hh, TQ:2 * TQ, :], values(hh, qi), None, None),
               (s_prev[hh] + tab_ref[hh, 0:TQ, :], values(hh, jp), off_p, None),
               (s_left[hh], values(hh, jl), cbs[hh] + off_l, None)] for hh in heads]
    for hh, (m, acc) in zip(heads, _flash_update([None] * len(chains), chains)):
        m_ref[hh] = m
        acc_ref[hh] = acc
    far.run(tops_first)
    out_t = jnp.concatenate([_flash_out(acc_ref[hh]) for hh in heads], axis=0)
    o_ref[0] = out_t.T.astype(BF16)


def _flash_scratch(n_heads):
    return [pltpu.VMEM((n_heads, 2 * TQ, TQ), F32),
            pltpu.VMEM((n_heads, 2 * TQ, TQ), F32),
            pltpu.VMEM((n_heads, 1, TQ), F32),
            pltpu.VMEM((n_heads, ACC_ROWS, TQ), F32)]


def _moba(qaug, kaug, vat, tab, cb):
    bsz, _, _, s = qaug.shape
    nq = s // TQ
    nh = MOBA_HEADS
    return pl.pallas_call(
        _moba_kernel,
        grid=(bsz, H_A // nh, nq),
        in_specs=[
            pl.BlockSpec((1, nh, LANES, TQ), lambda b, hp, qi: (b, hp, 0, qi)),
            pl.BlockSpec((1, nh, s, LANES), lambda b, hp, qi: (b, hp, 0, 0)),
            pl.BlockSpec((1, nq, nh * HEAD_DIM, TQ), lambda b, hp, qi: (b, 0, hp, 0)),
            pl.BlockSpec((nh, 2 * TQ, TQ), lambda b, hp, qi: (hp, 0, 0)),
            pl.BlockSpec((nh, 8, LANES), lambda b, hp, qi: (hp, 0, 0)),
        ],
        out_specs=pl.BlockSpec((1, TQ, nh * HEAD_DIM), lambda b, hp, qi: (b, qi, hp)),
        out_shape=jax.ShapeDtypeStruct((bsz, s, H_A * HEAD_DIM), BF16),
        scratch_shapes=_flash_scratch(nh),
        compiler_params=_params(("parallel", "parallel", "arbitrary")),
        name="moba",
    )(qaug, kaug, vat, tab, cb)


def _nsa_cmp_kernel(q_ref, kc_ref, vct_ref, ovt_ref, oc_ref, selb_ref, *, n_sel, n_cmp, n_parts):
    qi = pl.program_id(1)
    ncp = kc_ref.shape[2]
    tq = q_ref.shape[3]
    t0 = qi * tq
    zeros = jnp.zeros((HEAD_DIM, tq), BF16)
    qs = [jnp.concatenate([q_ref[0, hd].astype(BF16), zeros], axis=0) for hd in range(H_B)]

    def body(nk, nb):
        n_idx = lax.broadcasted_iota(jnp.int32, (nk, tq), 0)
        t_idx = lax.broadcasted_iota(jnp.int32, (nk, tq), 1) + t0
        mask = (n_idx * CMP_STRIDE + (CMP_LEN - 1) <= t_idx) & (n_idx < n_cmp)
        any_key = t_idx[0:1] >= CMP_LEN - 1
        blk = lax.broadcasted_iota(jnp.int32, (nb, tq), 0)
        cur = (lax.broadcasted_iota(jnp.int32, (nb, tq), 1) + t0) >> _LOG2_SEL_BLK
        ok = blk <= cur
        forced = (blk == 0) | (blk == cur) | (blk == cur - 1)
        ovt = ovt_ref[0:nb, 0:nk]
        scores = []
        for g in range(G_B):
            kc = kc_ref[0, g, 0:nk, :]
            vct = _with_ones(vct_ref[0, g, :, 0:nk])
            psum = jnp.zeros((nk, tq), F32)
            for hd in range(g * R_B, (g + 1) * R_B):
                z = jnp.where(mask, _dot(kc, qs[hd]), NEG)
                e = jnp.exp2(z - _col_max(z))
                acc = _dot(vct, e.astype(BF16))
                rinv = jnp.where(any_key, 1.0 / acc[HEAD_DIM:HEAD_DIM + 1], 0.0)
                oc_ref[0, hd] = acc[0:HEAD_DIM] * rinv
                psum = psum + e * rinv
            ph, pl_ = _split(psum)
            imp_t = _dot(ovt, ph) + _dot(ovt, pl_)
            scores.append(jnp.where(ok, jnp.where(forced, BIG, imp_t), -jnp.inf))
        for g, picked in enumerate(_topk_rows(scores, blk, n_sel)):
            selb_ref[0, g, 0:nb, :] = jnp.where(ok & (picked > 0.0), 0.0, NEG).astype(BF16)
            if nb < LANES:
                selb_ref[0, g, nb:LANES, :] = jnp.full((LANES - nb, tq), NEG, BF16)

    part = ncp // n_parts
    need = jnp.minimum(((qi + 1) * (tq // CMP_STRIDE) + part - 1) // part, n_parts)
    for v in range(1, n_parts + 1):
        pl.when(need == v)(functools.partial(body, v * part, min(LANES, v * part * CMP_STRIDE // SEL_BLK)))


def _cmp_parts(ncp):
    return 4 if ncp % (4 * LANES) == 0 else 1


def _nsa_cmp(qb, kcmp, vcmpt, ovt, n_cmp):
    bsz, _, _, s = qb.shape
    tq = CMP_TQ if s % CMP_TQ == 0 else TQ
    nq = s // tq
    ncp = kcmp.shape[2]
    n_sel = min(SEL_TOPK, s // SEL_BLK)
    n_parts = _cmp_parts(ncp)
    return pl.pallas_call(
        functools.partial(_nsa_cmp_kernel, n_sel=n_sel, n_cmp=n_cmp, n_parts=n_parts),
        grid=(bsz, nq),
        in_specs=[
            pl.BlockSpec((1, H_B, HEAD_DIM, tq), lambda b, qi: (b, 0, 0, qi)),
            pl.BlockSpec((1, G_B, ncp, LANES), lambda b, qi: (b, 0, 0, 0)),
            pl.BlockSpec((1, G_B, HEAD_DIM, ncp), lambda b, qi: (b, 0, 0, 0)),
            pl.BlockSpec((LANES, ncp), lambda b, qi: (0, 0)),
        ],
        out_specs=[
            pl.BlockSpec((1, H_B, HEAD_DIM, tq), lambda b, qi: (b, 0, 0, qi)),
            pl.BlockSpec((1, G_B, LANES, tq), lambda b, qi: (b, 0, 0, qi)),
        ],
        out_shape=[jax.ShapeDtypeStruct((bsz, H_B, HEAD_DIM, s), F32),
                   jax.ShapeDtypeStruct((bsz, G_B, LANES, s), BF16)],
        compiler_params=_params(("parallel", "parallel")),
        name="nsa_cmp",
    )(qb, kcmp, vcmpt, ovt)


def _nsa_main_kernel(q_ref, selb_ref, ks_ref, vst_ref, kw_ref, vwt_ref, oc_ref, gates_ref,
                     tabs_ref, tabw_ref, cb_ref, o_ref, sa_ref, sb_ref, m_ref, acc_ref):
    qi = pl.program_id(2)
    selb = selb_ref[0, 0]
    gates = gates_ref[0, 0]
    zeros = jnp.zeros((HEAD_DIM, TQ), BF16)
    j1 = jnp.maximum(qi - 1, 0)
    j2 = jnp.maximum(qi - 2, 0)
    off1 = jnp.where(qi >= 1, 0.0, NEG)
    off2 = jnp.where(qi >= 2, 0.0, NEG)
    n_pairs, jl, off_l = _far_tiles(qi)

    heads = range(R_B)
    qw = [jnp.concatenate([q_ref[0, r].astype(BF16), zeros], axis=0) for r in heads]
    qs = [jnp.concatenate([qw[r], selb], axis=0) for r in heads]
    cbs = [cb_ref[r][0:1, 0:1] for r in heads]

    def s_sel(r, j):
        return _dot(ks_ref[0, 0, _tile_rows(j), :], qs[r])

    def s_win(r, j):
        return _dot(kw_ref[0, 0, _tile_rows(j), :], qw[r])

    far = _FarLoop(n_pairs, heads, s_sel, lambda r, j: vst_ref[0, j], cbs, sa_ref, sb_ref, m_ref, acc_ref)

    sw = [[s_win(r, j) for r in heads] for j in (qi, j1, j2)]
    ss = [[s_sel(r, j) for r in heads] for j in (qi, j1, jl)]
    tops_first = far.first()

    chains = [[(sw[0][r] + tabw_ref[r, 2 * TQ:3 * TQ, :], vwt_ref[0, qi], None, None),
               (sw[1][r] + tabw_ref[r, TQ:2 * TQ, :], vwt_ref[0, j1], off1, None),
               (sw[2][r] + tabw_ref[r, 0:TQ, :], vwt_ref[0, j2], off2, None)] for r in heads]
    chains += [[(ss[0][r] + tabs_ref[r, TQ:2 * TQ, :], vst_ref[0, qi], None, None),
                (ss[1][r] + tabs_ref[r, 0:TQ, :], vst_ref[0, j1], off1, None),
                (ss[2][r], vst_ref[0, jl], cbs[r] + off_l, None)] for r in heads]
    done = _flash_update([None] * len(chains), chains)
    win = [_flash_out(acc) for _, acc in done[:R_B]]
    for r, (m, acc) in zip(heads, done[R_B:]):
        m_ref[r] = m
        acc_ref[r] = acc
    far.run(tops_first)

    outs = [gates[3 * r:3 * r + 1] * oc_ref[0, r]
            + gates[3 * r + 1:3 * r + 2] * _flash_out(acc_ref[r])
            + gates[3 * r + 2:3 * r + 3] * win[r] for r in heads]
    o_ref[0] = jnp.concatenate(outs, axis=0).T.astype(BF16)


def _nsa_main(qb, selb, ksaug, vst, kw, vwt, oc, gates, tabs, tabw, cb):
    bsz, _, _, s = qb.shape
    nq = s // TQ
    return pl.pallas_call(
        _nsa_main_kernel,
        grid=(bsz, G_B, nq),
        in_specs=[
            pl.BlockSpec((1, R_B, HEAD_DIM, TQ), lambda b, g, qi: (b, g, 0, qi)),
            pl.BlockSpec((1, 1, LANES, TQ), lambda b, g, qi: (b, g, 0, qi)),
            pl.BlockSpec((1, 1, s, 2 * LANES), lambda b, g, qi: (b, g, 0, 0)),
            pl.BlockSpec((1, nq, HEAD_DIM, TQ), lambda b, g, qi: (b, 0, g, 0)),
            pl.BlockSpec((1, 1, s, LANES), lambda b, g, qi: (b, g, 0, 0)),
            pl.BlockSpec((1, nq, HEAD_DIM, TQ), lambda b, g, qi: (b, 0, g, 0)),
            pl.BlockSpec((1, R_B, HEAD_DIM, TQ), lambda b, g, qi: (b, g, 0, qi)),
            pl.BlockSpec((1, 1, GATE_ROWS, TQ), lambda b, g, qi: (b, g, 0, qi)),
            pl.BlockSpec((R_B, 2 * TQ, TQ), lambda b, g, qi: (g, 0, 0)),
            pl.BlockSpec((R_B, 3 * TQ, TQ), lambda b, g, qi: (g, 0, 0)),
            pl.BlockSpec((R_B, 8, LANES), lambda b, g, qi: (g, 0, 0)),
        ],
        out_specs=pl.BlockSpec((1, TQ, R_B * HEAD_DIM), lambda b, g, qi: (b, qi, g)),
        out_shape=jax.ShapeDtypeStruct((bsz, s, H_B * HEAD_DIM), BF16),
        scratch_shapes=_flash_scratch(R_B),
        compiler_params=_params(("parallel", "parallel", "arbitrary")),
        name="nsa_main",
    )(qb, selb, ksaug, vst, kw, vwt, oc, gates, tabs, tabw, cb)


def _out_ffn_kernel(x_ref, oa_ref, ob_ref, wo_ref, gtm_ref, gffn_ref, scf_ref, shf_ref, gtf_ref,
                    wg_ref, wu_ref, wd_ref, o_ref, x1_ref, h_ref, acc_ref):
    f = pl.program_id(2)
    half = oa_ref.shape[2]

    @pl.when(f == 0)
    def _():
        mix = _dot(oa_ref[0], wo_ref[0:half, :]) + _dot(ob_ref[0], wo_ref[half:2 * half, :])
        x1 = x_ref[0] + gtm_ref[0, 0] * mix
        x1_ref[...] = x1
        h = _rms_rows(x1, gffn_ref[...]) * (1.0 + scf_ref[0, 0]) + shf_ref[0, 0]
        h_ref[...] = h.astype(BF16)

    h = h_ref[...]
    gate = _dot(h, wg_ref[...])
    up = _dot(h, wu_ref[...])
    act = (gate * jax.nn.sigmoid(gate) * up).astype(BF16)
    part = _dot(act, wd_ref[...])

    @pl.when(f == 0)
    def _():
        acc_ref[...] = part

    @pl.when(f > 0)
    def _():
        acc_ref[...] += part

    @pl.when(f == pl.num_programs(2) - 1)
    def _():
        o_ref[0] = x1_ref[...] + gtf_ref[0, 0] * acc_ref[...]


def _out_ffn(x, oa, ob, wo, mod4, gffn, wgu, wd):
    bsz, s, d = x.shape
    fh = wd.shape[0]
    tm = TM_FFN
    tf = fh // 2 if (fh // 2) % LANES == 0 else fh
    nf = fh // tf
    tok = lambda b, t, f: (b, t, 0)
    modspec = lambda k: pl.BlockSpec((1, 1, 1, d), lambda b, t, f: (b, k, 0, 0))
    return pl.pallas_call(
        _out_ffn_kernel,
        grid=(bsz, s // tm, nf),
        in_specs=[
            pl.BlockSpec((1, tm, d), tok),
            pl.BlockSpec((1, tm, oa.shape[2]), tok),
            pl.BlockSpec((1, tm, ob.shape[2]), tok),
            pl.BlockSpec((d, d), lambda b, t, f: (0, 0)),
            modspec(2),
            pl.BlockSpec((1, d), lambda b, t, f: (0, 0)),
            modspec(4),
            modspec(3),
            modspec(5),
            pl.BlockSpec((d, tf), lambda b, t, f: (0, f)),
            pl.BlockSpec((d, tf), lambda b, t, f: (0, f + nf)),
            pl.BlockSpec((tf, d), lambda b, t, f: (f, 0)),
        ],
        out_specs=pl.BlockSpec((1, tm, d), tok),
        out_shape=jax.ShapeDtypeStruct((bsz, s, d), F32),
        scratch_shapes=[pltpu.VMEM((tm, d), F32), pltpu.VMEM((tm, d), BF16), pltpu.VMEM((tm, d), F32)],
        compiler_params=_params(("parallel", "parallel", "arbitrary")),
        name="out_ffn",
    )(x, oa, ob, wo, mod4, gffn, mod4, mod4, mod4, wgu, wgu, wd)


def _t5_bucket_np(d):
    max_exact = N_BUCKETS // 2
    d = np.maximum(d, 0)
    df = np.maximum(d, 1).astype(np.float64)
    large = max_exact + (np.log(df / max_exact) / math.log(MAX_DIST / max_exact)
                         * (N_BUCKETS - max_exact)).astype(np.int64)
    large = np.minimum(large, N_BUCKETS - 1)
    return np.where(d < max_exact, d, large).astype(np.int32)


def _bias_expand_kernel(tab_ref, bucket_ref, o_ref):
    hd = pl.program_id(0)
    bucket = bucket_ref[...]
    acc = jnp.full(bucket.shape, NEG, F32)
    for b in range(N_BUCKETS):
        acc = jnp.where(bucket == b, tab_ref[hd, b], acc)
    o_ref[0] = acc


def _bias_expand(tab, bucket):
    nh = tab.shape[0]
    return pl.pallas_call(
        _bias_expand_kernel,
        grid=(nh,),
        in_specs=[pl.BlockSpec(memory_space=pltpu.SMEM),
                  pl.BlockSpec(bucket.shape, lambda h: (0, 0))],
        out_specs=pl.BlockSpec((1,) + bucket.shape, lambda h: (h, 0, 0)),
        out_shape=jax.ShapeDtypeStruct((nh,) + bucket.shape, F32),
        compiler_params=_params(("parallel",)),
        name="bias_expand",
    )(tab, jnp.asarray(bucket, jnp.int32))


def _bias_tables(rel_bias):
    tab = rel_bias.T.astype(F32) * LOG2_E
    i = np.arange(TQ)[None, :]
    d_near = i + TQ - np.arange(2 * TQ)[:, None]
    near = _bias_expand(tab, np.where(d_near >= 0, _t5_bucket_np(d_near), -1))
    d_win = i + 2 * TQ - np.arange(3 * TQ)[:, None]
    ok_win = (d_win >= 0) & (d_win < WINDOW)
    win = _bias_expand(tab[H_A:], np.where(ok_win, _t5_bucket_np(d_win), -1))
    far = jnp.broadcast_to(tab[:, N_BUCKETS - 1][:, None, None], (tab.shape[0], 8, LANES))
    return near, win, far


def _overlap_t(n_cmp_pad, n_cmp):
    cs = np.arange(n_cmp_pad)[None, :] * CMP_STRIDE
    ss = np.arange(LANES)[:, None] * SEL_BLK
    ov = (cs < ss + SEL_BLK) & (cs + CMP_LEN > ss) & (np.arange(n_cmp_pad)[None, :] < n_cmp)
    return jnp.asarray(ov.astype(np.float32), BF16)


def _block_diag(n):
    m = (np.arange(n)[:, None] // HEAD_DIM == np.arange(n)[None, :] // HEAD_DIM)
    return jnp.asarray(m.astype(np.float32) / HEAD_DIM, BF16)


def kernel(x, c, rel_bias, w_ada, b_ada, g_mix, w_in, q_norm_a, k_norm_a, q_norm_b, k_norm_cmp,
           k_norm_sel, k_norm_win, cmp_pe_k, cmp_w1_k, cmp_w2_k, cmp_pe_v, cmp_w1_v, cmp_w2_v,
           w_out, g_ffn, w_gu, w_down):
    bsz, s, d = x.shape
    depth = w_ada.shape[0]
    assert s % TM_IN == 0 and s % TM_FFN == 0 and s % (2 * TQ) == 0
    assert s // BLK_A <= HEAD_DIM and s // SEL_BLK <= LANES
    assert WINDOW == 2 * TQ and BLK_A == TQ and MAX_DIST <= TQ
    n_chunks = s // CMP_STRIDE
    n_cmp = (s - CMP_LEN) // CMP_STRIDE + 1
    scale = HEAD_DIM ** -0.5 * LOG2_E
    hd = HEAD_DIM

    near, win, far = _bias_tables(rel_bias)
    ovt = _overlap_t(n_chunks, n_cmp)
    bd = _block_diag(D_A)
    tile = lambda g, n: jnp.tile(g.astype(F32), n).reshape(1, -1)
    tile_t = lambda g, n: jnp.broadcast_to(jnp.tile(g.astype(F32), n)[:, None], (n * hd, TM_IN))

    for l in range(depth):
        mod = _ada(c, w_ada[l], b_ada[l])
        mod4 = mod.reshape(bsz, ADA_CHUNKS, 1, d)

        wl = w_in[l]
        cols = np.cumsum([0, H_A * hd, H_A * hd, H_A * hd, H_B * hd] + [G_B * hd] * 6)
        qa_c, ka_c, va_c, qb_c, kc_c, vc_c, ks_c, vs_c, kw_c, vw_c = [
            wl[:, int(a):int(b)] for a, b in zip(cols[:-1], cols[1:])]
        gl = wl[:, int(cols[-1]):].reshape(d, G_B, 3 * R_B)
        gl = jnp.pad(gl, ((0, 0), (0, 0), (0, GATE_ROWS - 3 * R_B))).reshape(d, G_B * GATE_ROWS)
        w_rows = jnp.concatenate([ka_c, kc_c, vc_c, ks_c, kw_c], axis=1).astype(BF16)
        w_t = jnp.concatenate([qa_c, va_c, qb_c, vs_c, vw_c, gl], axis=1).T.astype(BF16)

        (qa, kaug, vat, kmean, qb, kc, vc, ksaug, vst, kw, vwt, gates) = _inproj(
            x, mod4, mod4, g_mix[l].reshape(1, d), w_rows, w_t, bd,
            tile_t(q_norm_a[l], H_A) * scale, tile(k_norm_a[l], H_A), tile_t(q_norm_b[l], H_B) * scale,
            tile(k_norm_sel[l], G_B), tile(k_norm_win[l], G_B))

        nba = s // BLK_A
        km = kmean.reshape(bsz, nba, H_A, hd).transpose(0, 2, 1, 3)
        km = jnp.pad(km, ((0, 0), (0, 0), (0, hd - nba), (0, LANES - hd)))
        o_a = _moba(_moba_gate(qa, km), kaug, vat, near[:H_A], far[:H_A])

        chunks = lambda t: t.reshape(bsz, s, G_B, hd).transpose(0, 2, 1, 3).reshape(
            bsz, G_B, n_chunks, CMP_STRIDE * hd)
        w1 = jnp.stack([cmp_w1_k[l], cmp_w1_v[l]]).astype(BF16)
        w2t = jnp.stack([cmp_w2_k[l].T, cmp_w2_v[l].T]).astype(BF16)
        pe = jnp.stack([cmp_pe_k[l], cmp_pe_v[l]]).reshape(2, 1, CMP_LEN * hd)
        pe = jnp.broadcast_to(pe, (2, 8, CMP_LEN * hd)).astype(BF16)
        kcmp, vcmpt = _compress(chunks(kc), chunks(vc), w1, w2t, pe,
                                k_norm_cmp[l].astype(F32).reshape(1, hd))

        oc, selb = _nsa_cmp(qb, kcmp, vcmpt, ovt, n_cmp)
        o_b = _nsa_main(qb, selb, ksaug, vst, kw, vwt, oc, gates, near[H_A:], win, far[H_A:])

        x = _out_ffn(x, o_a, o_b, w_out[l].astype(BF16), mod4, g_ffn[l].reshape(1, d),
                     w_gu[l].astype(BF16), w_down[l].astype(BF16))
    return x
```

```python
import functools
import math

import jax
import jax.numpy as jnp
import numpy as np
from jax import lax
from jax.experimental import pallas as pl
from jax.experimental.pallas import tpu as pltpu

F32 = jnp.float32
BF16 = jnp.bfloat16

HEAD_DIM = 64
LANES = 128
BF16_ROWS = 16
H_A = 8
H_B = 8
G_B = 2
R_B = H_B // G_B
D_A = H_A * HEAD_DIM
D_B = H_B * HEAD_DIM
D_KV = G_B * HEAD_DIM
BLK_A = 256
TOPK_A = 3
CMP_LEN = 32
CMP_STRIDE = 16
CMP_HIDDEN = 256
SEL_BLK = 64
SEL_TOPK = 16
WINDOW = 512
N_BUCKETS = 32
MAX_DIST = 128
ADA_CHUNKS = 6
NEG = -1e30
N_FORCED = 3
EPS = 1e-6
LOG2_E = math.log2(math.e)

_LOG2_BLK_A = BLK_A.bit_length() - 1
_LOG2_SEL_BLK = SEL_BLK.bit_length() - 1

TQ = 256
TM_IN = 1024
TM_FFN = 512
ADA_TN = 512
GATE_TQ = 2048
GATE_HEADS = 4
CMP_TQ = 512
FAR_PAIRS_PER_TRIP = 8
MOBA_HEADS = 4
GATE_ROWS = 16
ACC_ROWS = HEAD_DIM + BF16_ROWS
V7X_VMEM_BYTES = 64 * 1024 * 1024
VMEM_LIMIT = V7X_VMEM_BYTES * 7 // 8


def _dot(a, b):
    return jnp.dot(a, b, preferred_element_type=F32)


def _dot_nt(a, b):
    return lax.dot_general(a, b, (((1,), (1,)), ((), ())), preferred_element_type=F32)


def _split(a):
    hi = a.astype(BF16)
    lo = (a - hi.astype(F32)).astype(BF16)
    return hi, lo


def _dot3(a, b):
    ah, al = _split(a)
    bh, bl = _split(b)
    return _dot(ah, bh) + (_dot(al, bh) + _dot(ah, bl))


def _params(sem):
    return pltpu.CompilerParams(dimension_semantics=sem, vmem_limit_bytes=VMEM_LIMIT)


def _ada_kernel(c_ref, w_ref, b_ref, o_ref):
    c = c_ref[...]
    o_ref[...] = _dot3(c * jax.nn.sigmoid(c), w_ref[...]) + b_ref[...]


def _ada(c, w, b):
    bsz, d = c.shape
    n = w.shape[1]
    tn = ADA_TN
    return pl.pallas_call(
        _ada_kernel,
        grid=(n // tn,),
        in_specs=[pl.BlockSpec((bsz, d), lambda j: (0, 0)),
                  pl.BlockSpec((d, tn), lambda j: (0, j)),
                  pl.BlockSpec((1, tn), lambda j: (0, j))],
        out_specs=pl.BlockSpec((bsz, tn), lambda j: (0, j)),
        out_shape=jax.ShapeDtypeStruct((bsz, n), F32),
        compiler_params=_params(("arbitrary",)),
        name="ada",
    )(c, w, b.reshape(1, n))


def _rms_rows(xf, g):
    ms = jnp.mean(xf * xf, axis=-1, keepdims=True)
    return xf * lax.rsqrt(ms + EPS) * g


def _head_norm(t, bd, gain):
    hi, lo = _split(t * t)
    ms = _dot(hi, bd) + _dot(lo, bd)
    return t * lax.rsqrt(ms + EPS) * gain


def _head_norm_t(t, gain):
    heads = []
    for hd in range(t.shape[0] // HEAD_DIM):
        th = t[hd * HEAD_DIM:(hd + 1) * HEAD_DIM]
        ms = jnp.mean(th * th, axis=0, keepdims=True)
        heads.append(th * lax.rsqrt(ms + EPS) * gain[hd * HEAD_DIM:(hd + 1) * HEAD_DIM])
    return heads


def _inproj_kernel(x_ref, sc_ref, sh_ref, gmix_ref, wr_ref, wt_ref, bd_ref, gqa_ref, gka_ref, gqb_ref,
                   gks_ref, gkw_ref,
                   qa_ref, kaug_ref, va_ref, kmean_ref, qb_ref, kc_ref, vc_ref, ksaug_ref,
                   vs_ref, kw_ref, vw_ref, gates_ref):
    tm = x_ref.shape[1]
    ti = pl.program_id(1)
    xf = x_ref[0]
    h = _rms_rows(xf, gmix_ref[...]) * (1.0 + sc_ref[0, 0]) + sh_ref[0, 0]
    hb = h.astype(BF16)

    def proj(c0, c1):
        return _dot(hb, wr_ref[:, c0:c1])

    def proj_t(r0, r1):
        return _dot_nt(wt_ref[r0:r1, :], hb)

    bd = bd_ref[...]
    bd2 = bd_ref[0:LANES, 0:LANES]
    lane = lax.broadcasted_iota(jnp.int32, (tm, LANES), 1)
    row = lax.broadcasted_iota(jnp.int32, (tm, LANES), 0) + ti * tm
    low = lane < HEAD_DIM

    def k_in_low(pair, odd):
        return pltpu.roll(pair, HEAD_DIM, 1) if odd else pair

    t_qa, t_va, t_qb = 0, D_A, 2 * D_A
    t_vs, t_vw, t_gl = t_qb + D_B, t_qb + D_B + D_KV, t_qb + D_B + 2 * D_KV
    c_ka, c_kc, c_vc, c_ks, c_kw = 0, D_A, D_A + D_KV, D_A + 2 * D_KV, D_A + 3 * D_KV

    for hd, qh in enumerate(_head_norm_t(proj_t(t_qa, t_va), gqa_ref[...])):
        qa_ref[0, hd] = qh

    def put_tiles(ref, vt):
        for i in range(tm // TQ):
            ref[0, i] = vt[:, i * TQ:(i + 1) * TQ].astype(BF16)

    put_tiles(va_ref, proj_t(t_va, t_qb))
    for hd, qh in enumerate(_head_norm_t(proj_t(t_qb, t_vs), gqb_ref[...])):
        qb_ref[0, hd] = qh
    put_tiles(vs_ref, proj_t(t_vs, t_vw))
    put_tiles(vw_ref, proj_t(t_vw, t_gl))
    gl = jax.nn.sigmoid(proj_t(t_gl, t_gl + G_B * GATE_ROWS))
    for g in range(G_B):
        gates_ref[0, g] = gl[g * GATE_ROWS:(g + 1) * GATE_ROWS]

    ka = _head_norm(proj(c_ka, c_kc), bd, gka_ref[...])
    oh_a = jnp.where(lane - HEAD_DIM == (row >> _LOG2_BLK_A), 1.0, 0.0)
    for hd in range(H_A):
        pair = ka[:, (hd // 2) * LANES:(hd // 2 + 1) * LANES]
        kaug_ref[0, hd] = jnp.where(low, k_in_low(pair, hd % 2), oh_a).astype(BF16)
    for i in range(tm // BLK_A):
        kmean_ref[0, i] = jnp.mean(ka[i * BLK_A:(i + 1) * BLK_A], axis=0, keepdims=True)

    kc_ref[0] = proj(c_kc, c_vc).astype(BF16)
    vc_ref[0] = proj(c_vc, c_ks).astype(BF16)

    ks = _head_norm(proj(c_ks, c_kw), bd2, gks_ref[...])
    kw = _head_norm(proj(c_kw, c_kw + D_KV), bd2, gkw_ref[...])
    oh_s = jnp.where(lane == (row >> _LOG2_SEL_BLK), 1.0, 0.0).astype(BF16)
    for g in range(G_B):
        ksaug_ref[0, g] = jnp.concatenate(
            [jnp.where(low, k_in_low(ks, g), 0.0).astype(BF16), oh_s], axis=1)
        kw_ref[0, g] = jnp.where(low, k_in_low(kw, g), 0.0).astype(BF16)


def _inproj(x, sc, sh, gmix, wr, wt, bd, gqa, gka, gqb, gks, gkw):
    bsz, s, d = x.shape
    tm = TM_IN
    nt = s // tm
    nba = s // BLK_A
    const2 = lambda b, t: (0, 0)
    tok3 = lambda b, t: (b, t, 0)
    tok4 = lambda b, t: (b, 0, t, 0)
    tile4 = lambda b, t: (b, t, 0, 0)
    tr4 = lambda b, t: (b, 0, 0, t)
    in_specs = [
        pl.BlockSpec((1, tm, d), tok3),
        pl.BlockSpec((1, 1, 1, d), lambda b, t: (b, 1, 0, 0)),
        pl.BlockSpec((1, 1, 1, d), lambda b, t: (b, 0, 0, 0)),
        pl.BlockSpec((1, d), const2),
        pl.BlockSpec(wr.shape, const2),
        pl.BlockSpec(wt.shape, const2),
        pl.BlockSpec((D_A, D_A), const2),
        pl.BlockSpec((D_A, tm), const2),
        pl.BlockSpec((1, D_A), const2),
        pl.BlockSpec((D_B, tm), const2),
        pl.BlockSpec((1, LANES), const2),
        pl.BlockSpec((1, LANES), const2),
    ]
    out_shape = [
        jax.ShapeDtypeStruct((bsz, H_A, HEAD_DIM, s), F32),
        jax.ShapeDtypeStruct((bsz, H_A, s, LANES), BF16),
        jax.ShapeDtypeStruct((bsz, s // TQ, D_A, TQ), BF16),
        jax.ShapeDtypeStruct((bsz, nba, 1, D_A), F32),
        jax.ShapeDtypeStruct((bsz, H_B, HEAD_DIM, s), F32),
        jax.ShapeDtypeStruct((bsz, s, LANES), BF16),
        jax.ShapeDtypeStruct((bsz, s, LANES), BF16),
        jax.ShapeDtypeStruct((bsz, G_B, s, 2 * LANES), BF16),
        jax.ShapeDtypeStruct((bsz, s // TQ, LANES, TQ), BF16),
        jax.ShapeDtypeStruct((bsz, G_B, s, LANES), BF16),
        jax.ShapeDtypeStruct((bsz, s // TQ, LANES, TQ), BF16),
        jax.ShapeDtypeStruct((bsz, G_B, GATE_ROWS, s), F32),
    ]
    out_specs = [
        pl.BlockSpec((1, H_A, HEAD_DIM, tm), tr4),
        pl.BlockSpec((1, H_A, tm, LANES), tok4),
        pl.BlockSpec((1, tm // TQ, D_A, TQ), tile4),
        pl.BlockSpec((1, tm // BLK_A, 1, D_A), lambda b, t: (b, t, 0, 0)),
        pl.BlockSpec((1, H_B, HEAD_DIM, tm), tr4),
        pl.BlockSpec((1, tm, LANES), tok3),
        pl.BlockSpec((1, tm, LANES), tok3),
        pl.BlockSpec((1, G_B, tm, 2 * LANES), tok4),
        pl.BlockSpec((1, tm // TQ, LANES, TQ), tile4),
        pl.BlockSpec((1, G_B, tm, LANES), tok4),
        pl.BlockSpec((1, tm // TQ, LANES, TQ), tile4),
        pl.BlockSpec((1, G_B, GATE_ROWS, tm), tr4),
    ]
    return pl.pallas_call(
        _inproj_kernel,
        grid=(bsz, nt),
        in_specs=in_specs,
        out_specs=out_specs,
        out_shape=out_shape,
        compiler_params=_params(("parallel", "parallel")),
        name="inproj",
    )(x, sc, sh, gmix, wr, wt, bd, gqa, gka, gqb, gks, gkw)


def _compress_kernel(ck_ref, cv_ref, w1_ref, w2t_ref, pe_ref, gk_ref, ok_ref, ov_ref):
    half = CMP_STRIDE * HEAD_DIM
    for kv, c_ref in enumerate((ck_ref, cv_ref)):
        for g in range(G_B):
            c = c_ref[0, g]
            a = _dot(c, w1_ref[kv, 0:half, :])
            b = _dot(c, w1_ref[kv, half:2 * half, :])
            n = a.shape[0]
            b_next = pltpu.roll(b, n - 1, 0)
            pe_term = _dot(pe_ref[kv], w1_ref[kv])[0:1]
            hid = jax.nn.gelu(a + b_next + pe_term).astype(BF16)
            if kv == 0:
                y = _dot_nt(hid, w2t_ref[kv])
                ms = jnp.mean(y * y, axis=1, keepdims=True)
                y = y * lax.rsqrt(ms + EPS) * gk_ref[...]
                ok_ref[0, g] = jnp.concatenate([y, jnp.zeros_like(y)], axis=1).astype(BF16)
            else:
                ov_ref[0, g] = _dot_nt(w2t_ref[kv], hid).astype(BF16)


def _compress(ck, cv, w1, w2t, pe, gk):
    bsz, g, n, width = ck.shape
    blk = pl.BlockSpec((1, g, n, width), lambda b: (b, 0, 0, 0))
    full = lambda a: pl.BlockSpec(a.shape, lambda b: (0,) * a.ndim)
    return pl.pallas_call(
        _compress_kernel,
        grid=(bsz,),
        in_specs=[blk, blk, full(w1), full(w2t), full(pe), full(gk)],
        out_specs=[pl.BlockSpec((1, g, n, LANES), lambda b: (b, 0, 0, 0)),
                   pl.BlockSpec((1, g, HEAD_DIM, n), lambda b: (b, 0, 0, 0))],
        out_shape=[jax.ShapeDtypeStruct((bsz, g, n, LANES), BF16),
                   jax.ShapeDtypeStruct((bsz, g, HEAD_DIM, n), BF16)],
        compiler_params=_params(("parallel",)),
        name="compress",
    )(ck, cv, w1, w2t, pe, gk)


def _with_ones(vt):
    return jnp.concatenate([vt, jnp.ones((BF16_ROWS, vt.shape[1]), BF16)], axis=0)


def _col_max(s):
    while s.shape[0] > 8:
        half = s.shape[0] // 2
        s = jnp.maximum(s[0:half], s[half:2 * half])
    return jnp.max(s, axis=0, keepdims=True)


def _flash_update(carries, chains):
    m_news = []
    for carry, tiles in zip(carries, chains):
        tops = []
        for s, _, bias, top in tiles:
            top = _col_max(s) if top is None else top
            tops.append(top if bias is None else top + bias)
        m_news.append(functools.reduce(jnp.maximum, tops if carry is None else tops + [carry[0]]))
    pvs = [None] * len(chains)
    for t in range(max(len(tiles) for tiles in chains)):
        for c, tiles in enumerate(chains):
            if t < len(tiles):
                s, vt, bias, _ = tiles[t]
                p = jnp.exp2(s - (m_news[c] if bias is None else m_news[c] - bias)).astype(BF16)
                part = _dot(_with_ones(vt), p)
                pvs[c] = part if pvs[c] is None else pvs[c] + part
    outs = []
    for carry, m_new, pv in zip(carries, m_news, pvs):
        outs.append((m_new, pv if carry is None else jnp.exp2(carry[0] - m_new) * carry[1] + pv))
    return outs


def _flash_out(acc):
    return acc[0:HEAD_DIM] / acc[HEAD_DIM:HEAD_DIM + 1]


def _topk_rows(scores, index, k):
    scores = list(scores)
    picked = [jnp.zeros(sc.shape, F32) for sc in scores]
    for _ in range(k):
        mx = [jnp.max(sc, axis=0, keepdims=True) for sc in scores]
        cand = [jnp.where(sc == m, index, jnp.int32(1 << 20)) for sc, m in zip(scores, mx)]
        first = [jnp.min(c, axis=0, keepdims=True) for c in cand]
        hit = [index == f for f in first]
        picked = [jnp.where(h, 1.0, p) for h, p in zip(hit, picked)]
        scores = [jnp.where(h, -jnp.inf, sc) for h, sc in zip(hit, scores)]
    return picked


def _tile_rows(j):
    return pl.ds(pl.multiple_of(j * TQ, TQ), TQ)


def _moba_gate_kernel(q_ref, km_ref, o_ref, *, n_sel):
    nh, tg = q_ref.shape[1], q_ref.shape[3]
    t0 = pl.program_id(2) * tg
    shape = (km_ref.shape[2], tg)
    blk = lax.broadcasted_iota(jnp.int32, shape, 0)
    own = (lax.broadcasted_iota(jnp.int32, shape, 1) + t0) >> _LOG2_BLK_A
    valid = blk < own
    qs, gates = [], []
    for hh in range(nh):
        q = q_ref[0, hh]
        gate = _dot3(km_ref[0, hh], jnp.concatenate([q, jnp.zeros_like(q)], axis=0))
        qs.append(q)
        gates.append(jnp.where(valid, gate, -jnp.inf))
    for hh, picked in enumerate(_topk_rows(gates, blk, n_sel)):
        keep = jnp.where(valid, picked, 0.0) + jnp.where(blk == own, 1.0, 0.0)
        selb = jnp.where(keep > 0.0, 0.0, NEG)
        o_ref[0, hh] = jnp.concatenate([qs[hh], selb], axis=0).astype(BF16)


def _moba_gate(qa, km):
    bsz, nh, _, s = qa.shape
    tg = min(s, GATE_TQ)
    hpb = GATE_HEADS
    n_sel = max(1, min(TOPK_A, s // BLK_A - 1))
    return pl.pallas_call(
        functools.partial(_moba_gate_kernel, n_sel=n_sel),
        grid=(bsz, nh // hpb, s // tg),
        in_specs=[pl.BlockSpec((1, hpb, HEAD_DIM, tg), lambda b, h, t: (b, h, 0, t)),
                  pl.BlockSpec((1, hpb, HEAD_DIM, LANES), lambda b, h, t: (b, h, 0, 0))],
        out_specs=pl.BlockSpec((1, hpb, LANES, tg), lambda b, h, t: (b, h, 0, t)),
        out_shape=jax.ShapeDtypeStruct((bsz, nh, LANES, s), BF16),
        compiler_params=_params(("parallel", "parallel", "parallel")),
        name="moba_gate",
    )(qa, km)


def _far_tiles(qi):
    n_far = jnp.maximum(qi - 1, 0)
    left = jnp.maximum(n_far - 1, 0)
    off_left = jnp.where((n_far & 1) == 1, 0.0, NEG)
    return n_far >> 1, left, off_left


class _FarLoop:
    def __init__(self, n_pairs, heads, qk_tile, values, cbs, sa_ref, sb_ref, m_ref, acc_ref):
        self.n_pairs, self.heads, self.qk_tile, self.values, self.cbs = n_pairs, heads, qk_tile, values, cbs
        self.sa_ref, self.sb_ref, self.m_ref, self.acc_ref = sa_ref, sb_ref, m_ref, acc_ref
        self.last = jnp.maximum(n_pairs - 1, 0)

    def fetch(self, buf_ref, h, i):
        s_lo = self.qk_tile(h, 2 * i)
        s_hi = self.qk_tile(h, 2 * i + 1)
        buf_ref[h, 0:TQ, :] = s_lo
        buf_ref[h, TQ:2 * TQ, :] = s_hi
        return _col_max(s_lo), _col_max(s_hi)

    def consume(self, buf_ref, h, top, i):
        (m, acc), = _flash_update([(self.m_ref[h], self.acc_ref[h])], [[
            (buf_ref[h, 0:TQ, :], self.values(h, 2 * i), self.cbs[h], top[0]),
            (buf_ref[h, TQ:2 * TQ, :], self.values(h, 2 * i + 1), self.cbs[h], top[1])]])
        self.m_ref[h] = m
        self.acc_ref[h] = acc

    def first(self):
        return tuple(self.fetch(self.sa_ref, h, 0) for h in self.heads)

    def run(self, tops_first):
        def two_pairs(ia, tops_a):
            tops_b, tops_next = [], []
            for h in self.heads:
                tops_b.append(self.fetch(self.sb_ref, h, ia + 1))
                self.consume(self.sa_ref, h, tops_a[h], ia)
            for h in self.heads:
                tops_next.append(self.fetch(self.sa_ref, h, jnp.minimum(ia + 2, self.last)))
                self.consume(self.sb_ref, h, tops_b[h], ia + 1)
            return tuple(tops_next)

        def pairs(n, start, tops):
            for k in range(0, n, 2):
                tops = two_pairs(start + k, tops)
            return tops

        trip = FAR_PAIRS_PER_TRIP
        n_trips = self.n_pairs // trip
        tops = lax.fori_loop(0, n_trips, lambda q, t: pairs(trip, trip * q, t), tops_first)
        done = trip * n_trips
        n = trip // 2
        while n >= 2:
            has = ((self.n_pairs // n) & 1) == 1
            tops = lax.cond(has, functools.partial(pairs, n, done), lambda t: t, tops)
            done = done + jnp.where(has, n, 0)
            n //= 2
        tops_last = tops

        @pl.when((self.n_pairs & 1) == 1)
        def _():
            for h in self.heads:
                self.consume(self.sa_ref, h, tops_last[h], self.last)


def _moba_kernel(q_ref, k_ref, vt_ref, tab_ref, cb_ref, o_ref, sa_ref, sb_ref, m_ref, acc_ref):
    qi = pl.program_id(2)
    jp = jnp.maximum(qi - 1, 0)
    off_p = jnp.where(qi >= 1, 0.0, NEG)
    n_pairs, jl, off_l = _far_tiles(qi)
    heads = range(MOBA_HEADS)
    qaug = [q_ref[0, hh] for hh in heads]
    cbs = [cb_ref[hh][0:1, 0:1] for hh in heads]

    def scores(hh, j):
        return _dot(k_ref[0, hh, _tile_rows(j), :], qaug[hh])

    def values(hh, j):
        return vt_ref[0, j, hh * HEAD_DIM:(hh + 1) * HEAD_DIM, :]

    far = _FarLoop(n_pairs, heads, scores, values, cbs, sa_ref, sb_ref, m_ref, acc_ref)

    s_own = [scores(hh, qi) for hh in heads]
    s_prev = [scores(hh, jp) for hh in heads]
    s_left = [scores(hh, jl) for hh in heads]
    tops_first = far.first()
    chains = [[(s_own[hh] + tab_ref[hh, TQ:2 * TQ, :], values(hh, qi), None, None),
               (s_prev[hh] + tab_ref[hh, 0:TQ, :], values(hh, jp), off_p, None),
               (s_left[hh], values(hh, jl), cbs[hh] + off_l, None)] for hh in heads]
    for hh, (m, acc) in zip(heads, _flash_update([None] * len(chains), chains)):
        m_ref[hh] = m
        acc_ref[hh] = acc
    far.run(tops_first)
    out_t = jnp.concatenate([_flash_out(acc_ref[hh]) for hh in heads], axis=0)
    o_ref[0] = out_t.T.astype(BF16)


def _flash_scratch(n_heads):
    return [pltpu.VMEM((n_heads, 2 * TQ, TQ), F32),
            pltpu.VMEM((n_heads, 2 * TQ, TQ), F32),
            pltpu.VMEM((n_heads, 1, TQ), F32),
            pltpu.VMEM((n_heads, ACC_ROWS, TQ), F32)]


def _moba(qaug, kaug, vat, tab, cb):
    bsz, _, _, s = qaug.shape
    nq = s // TQ
    nh = MOBA_HEADS
    return pl.pallas_call(
        _moba_kernel,
        grid=(bsz, H_A // nh, nq),
        in_specs=[
            pl.BlockSpec((1, nh, LANES, TQ), lambda b, hp, qi: (b, hp, 0, qi)),
            pl.BlockSpec((1, nh, s, LANES), lambda b, hp, qi: (b, hp, 0, 0)),
            pl.BlockSpec((1, nq, nh * HEAD_DIM, TQ), lambda b, hp, qi: (b, 0, hp, 0)),
            pl.BlockSpec((nh, 2 * TQ, TQ), lambda b, hp, qi: (hp, 0, 0)),
            pl.BlockSpec((nh, 8, LANES), lambda b, hp, qi: (hp, 0, 0)),
        ],
        out_specs=pl.BlockSpec((1, TQ, nh * HEAD_DIM), lambda b, hp, qi: (b, qi, hp)),
        out_shape=jax.ShapeDtypeStruct((bsz, s, H_A * HEAD_DIM), BF16),
        scratch_shapes=_flash_scratch(nh),
        compiler_params=_params(("parallel", "parallel", "arbitrary")),
        name="moba",
    )(qaug, kaug, vat, tab, cb)


def _nsa_cmp_kernel(q_ref, kc_ref, vct_ref, ovt_ref, oc_ref, selb_ref, *, n_sel, n_cmp, n_parts):
    qi = pl.program_id(1)
    ncp = kc_ref.shape[2]
    tq = q_ref.shape[3]
    t0 = qi * tq
    zeros = jnp.zeros((HEAD_DIM, tq), BF16)
    qs = [jnp.concatenate([q_ref[0, hd].astype(BF16), zeros], axis=0) for hd in range(H_B)]

    def body(nk, nb):
        n_idx = lax.broadcasted_iota(jnp.int32, (nk, tq), 0)
        t_idx = lax.broadcasted_iota(jnp.int32, (nk, tq), 1) + t0
        mask = (n_idx * CMP_STRIDE + (CMP_LEN - 1) <= t_idx) & (n_idx < n_cmp)
        any_key = t_idx[0:1] >= CMP_LEN - 1
        blk = lax.broadcasted_iota(jnp.int32, (nb, tq), 0)
        cur = (lax.broadcasted_iota(jnp.int32, (nb, tq), 1) + t0) >> _LOG2_SEL_BLK
        ok = blk <= cur
        forced = (blk == 0) | (blk == cur) | (blk == cur - 1)
        ovt = ovt_ref[0:nb, 0:nk]
        scores = []
        for g in range(G_B):
            kc = kc_ref[0, g, 0:nk, :]
            vct = _with_ones(vct_ref[0, g, :, 0:nk])
            psum = jnp.zeros((nk, tq), F32)
            for hd in range(g * R_B, (g + 1) * R_B):
                z = jnp.where(mask, _dot(kc, qs[hd]), NEG)
                e = jnp.exp2(z - _col_max(z))
                acc = _dot(vct, e.astype(BF16))
                rinv = jnp.where(any_key, 1.0 / acc[HEAD_DIM:HEAD_DIM + 1], 0.0)
                oc_ref[0, hd] = acc[0:HEAD_DIM] * rinv
                psum = psum + e * rinv
            ph, pl_ = _split(psum)
            imp_t = _dot(ovt, ph) + _dot(ovt, pl_)
            scores.append(jnp.where(ok & jnp.logical_not(forced), imp_t, -jnp.inf))
        for g, picked in enumerate(_topk_rows(scores, blk, max(n_sel - N_FORCED, 0))):
            selb_ref[0, g, 0:nb, :] = jnp.where(ok & (forced | (picked > 0.0)), 0.0, NEG).astype(BF16)
            if nb < LANES:
                selb_ref[0, g, nb:LANES, :] = jnp.full((LANES - nb, tq), NEG, BF16)

    part = ncp // n_parts
    need = jnp.minimum(((qi + 1) * (tq // CMP_STRIDE) + part - 1) // part, n_parts)
    for v in range(1, n_parts + 1):
        pl.when(need == v)(functools.partial(body, v * part, min(LANES, v * part * CMP_STRIDE // SEL_BLK)))


def _cmp_parts(ncp):
    return 4 if ncp % (4 * LANES) == 0 else 1


def _nsa_cmp(qb, kcmp, vcmpt, ovt, n_cmp):
    bsz, _, _, s = qb.shape
    tq = CMP_TQ if s % CMP_TQ == 0 else TQ
    nq = s // tq
    ncp = kcmp.shape[2]
    n_sel = min(SEL_TOPK, s // SEL_BLK)
    assert n_sel >= N_FORCED
    n_parts = _cmp_parts(ncp)
    return pl.pallas_call(
        functools.partial(_nsa_cmp_kernel, n_sel=n_sel, n_cmp=n_cmp, n_parts=n_parts),
        grid=(bsz, nq),
        in_specs=[
            pl.BlockSpec((1, H_B, HEAD_DIM, tq), lambda b, qi: (b, 0, 0, qi)),
            pl.BlockSpec((1, G_B, ncp, LANES), lambda b, qi: (b, 0, 0, 0)),
            pl.BlockSpec((1, G_B, HEAD_DIM, ncp), lambda b, qi: (b, 0, 0, 0)),
            pl.BlockSpec((LANES, ncp), lambda b, qi: (0, 0)),
        ],
        out_specs=[
            pl.BlockSpec((1, H_B, HEAD_DIM, tq), lambda b, qi: (b, 0, 0, qi)),
            pl.BlockSpec((1, G_B, LANES, tq), lambda b, qi: (b, 0, 0, qi)),
        ],
        out_shape=[jax.ShapeDtypeStruct((bsz, H_B, HEAD_DIM, s), F32),
                   jax.ShapeDtypeStruct((bsz, G_B, LANES, s), BF16)],
        compiler_params=_params(("parallel", "parallel")),
        name="nsa_cmp",
    )(qb, kcmp, vcmpt, ovt)


def _nsa_main_kernel(q_ref, selb_ref, ks_ref, vst_ref, kw_ref, vwt_ref, oc_ref, gates_ref,
                     tabs_ref, tabw_ref, cb_ref, o_ref, sa_ref, sb_ref, m_ref, acc_ref):
    qi = pl.program_id(2)
    selb = selb_ref[0, 0]
    gates = gates_ref[0, 0]
    zeros = jnp.zeros((HEAD_DIM, TQ), BF16)
    j1 = jnp.maximum(qi - 1, 0)
    j2 = jnp.maximum(qi - 2, 0)
    off1 = jnp.where(qi >= 1, 0.0, NEG)
    off2 = jnp.where(qi >= 2, 0.0, NEG)
    n_pairs, jl, off_l = _far_tiles(qi)

    heads = range(R_B)
    qw = [jnp.concatenate([q_ref[0, r].astype(BF16), zeros], axis=0) for r in heads]
    qs = [jnp.concatenate([qw[r], selb], axis=0) for r in heads]
    cbs = [cb_ref[r][0:1, 0:1] for r in heads]

    def s_sel(r, j):
        return _dot(ks_ref[0, 0, _tile_rows(j), :], qs[r])

    def s_win(r, j):
        return _dot(kw_ref[0, 0, _tile_rows(j), :], qw[r])

    far = _FarLoop(n_pairs, heads, s_sel, lambda r, j: vst_ref[0, j], cbs, sa_ref, sb_ref, m_ref, acc_ref)

    sw = [[s_win(r, j) for r in heads] for j in (qi, j1, j2)]
    ss = [[s_sel(r, j) for r in heads] for j in (qi, j1, jl)]
    tops_first = far.first()

    chains = [[(sw[0][r] + tabw_ref[r, 2 * TQ:3 * TQ, :], vwt_ref[0, qi], None, None),
               (sw[1][r] + tabw_ref[r, TQ:2 * TQ, :], vwt_ref[0, j1], off1, None),
               (sw[2][r] + tabw_ref[r, 0:TQ, :], vwt_ref[0, j2], off2, None)] for r in heads]
    chains += [[(ss[0][r] + tabs_ref[r, TQ:2 * TQ, :], vst_ref[0, qi], None, None),
                (ss[1][r] + tabs_ref[r, 0:TQ, :], vst_ref[0, j1], off1, None),
                (ss[2][r], vst_ref[0, jl], cbs[r] + off_l, None)] for r in heads]
    done = _flash_update([None] * len(chains), chains)
    win = [_flash_out(acc) for _, acc in done[:R_B]]
    for r, (m, acc) in zip(heads, done[R_B:]):
        m_ref[r] = m
        acc_ref[r] = acc
    far.run(tops_first)

    outs = [gates[3 * r:3 * r + 1] * oc_ref[0, r]
            + gates[3 * r + 1:3 * r + 2] * _flash_out(acc_ref[r])
            + gates[3 * r + 2:3 * r + 3] * win[r] for r in heads]
    o_ref[0] = jnp.concatenate(outs, axis=0).T.astype(BF16)


def _nsa_main(qb, selb, ksaug, vst, kw, vwt, oc, gates, tabs, tabw, cb):
    bsz, _, _, s = qb.shape
    nq = s // TQ
    return pl.pallas_call(
        _nsa_main_kernel,
        grid=(bsz, G_B, nq),
        in_specs=[
            pl.BlockSpec((1, R_B, HEAD_DIM, TQ), lambda b, g, qi: (b, g, 0, qi)),
            pl.BlockSpec((1, 1, LANES, TQ), lambda b, g, qi: (b, g, 0, qi)),
            pl.BlockSpec((1, 1, s, 2 * LANES), lambda b, g, qi: (b, g, 0, 0)),
            pl.BlockSpec((1, nq, HEAD_DIM, TQ), lambda b, g, qi: (b, 0, g, 0)),
            pl.BlockSpec((1, 1, s, LANES), lambda b, g, qi: (b, g, 0, 0)),
            pl.BlockSpec((1, nq, HEAD_DIM, TQ), lambda b, g, qi: (b, 0, g, 0)),
            pl.BlockSpec((1, R_B, HEAD_DIM, TQ), lambda b, g, qi: (b, g, 0, qi)),
            pl.BlockSpec((1, 1, GATE_ROWS, TQ), lambda b, g, qi: (b, g, 0, qi)),
            pl.BlockSpec((R_B, 2 * TQ, TQ), lambda b, g, qi: (g, 0, 0)),
            pl.BlockSpec((R_B, 3 * TQ, TQ), lambda b, g, qi: (g, 0, 0)),
            pl.BlockSpec((R_B, 8, LANES), lambda b, g, qi: (g, 0, 0)),
        ],
        out_specs=pl.BlockSpec((1, TQ, R_B * HEAD_DIM), lambda b, g, qi: (b, qi, g)),
        out_shape=jax.ShapeDtypeStruct((bsz, s, H_B * HEAD_DIM), BF16),
        scratch_shapes=_flash_scratch(R_B),
        compiler_params=_params(("parallel", "parallel", "arbitrary")),
        name="nsa_main",
    )(qb, selb, ksaug, vst, kw, vwt, oc, gates, tabs, tabw, cb)


def _out_ffn_kernel(x_ref, oa_ref, ob_ref, wo_ref, gtm_ref, gffn_ref, scf_ref, shf_ref, gtf_ref,
                    wg_ref, wu_ref, wd_ref, o_ref, x1_ref, h_ref, acc_ref):
    f = pl.program_id(2)
    half = oa_ref.shape[2]

    @pl.when(f == 0)
    def _():
        mix = _dot(oa_ref[0], wo_ref[0:half, :]) + _dot(ob_ref[0], wo_ref[half:2 * half, :])
        x1 = x_ref[0] + gtm_ref[0, 0] * mix
        x1_ref[...] = x1
        h = _rms_rows(x1, gffn_ref[...]) * (1.0 + scf_ref[0, 0]) + shf_ref[0, 0]
        h_ref[...] = h.astype(BF16)

    h = h_ref[...]
    gate = _dot(h, wg_ref[...])
    up = _dot(h, wu_ref[...])
    act = (gate * jax.nn.sigmoid(gate) * up).astype(BF16)
    part = _dot(act, wd_ref[...])

    @pl.when(f == 0)
    def _():
        acc_ref[...] = part

    @pl.when(f > 0)
    def _():
        acc_ref[...] += part

    @pl.when(f == pl.num_programs(2) - 1)
    def _():
        o_ref[0] = x1_ref[...] + gtf_ref[0, 0] * acc_ref[...]


def _out_ffn(x, oa, ob, wo, mod4, gffn, wgu, wd):
    bsz, s, d = x.shape
    fh = wd.shape[0]
    tm = TM_FFN
    tf = fh // 2 if (fh // 2) % LANES == 0 else fh
    nf = fh // tf
    tok = lambda b, t, f: (b, t, 0)
    modspec = lambda k: pl.BlockSpec((1, 1, 1, d), lambda b, t, f: (b, k, 0, 0))
    return pl.pallas_call(
        _out_ffn_kernel,
        grid=(bsz, s // tm, nf),
        in_specs=[
            pl.BlockSpec((1, tm, d), tok),
            pl.BlockSpec((1, tm, oa.shape[2]), tok),
            pl.BlockSpec((1, tm, ob.shape[2]), tok),
            pl.BlockSpec((d, d), lambda b, t, f: (0, 0)),
            modspec(2),
            pl.BlockSpec((1, d), lambda b, t, f: (0, 0)),
            modspec(4),
            modspec(3),
            modspec(5),
            pl.BlockSpec((d, tf), lambda b, t, f: (0, f)),
            pl.BlockSpec((d, tf), lambda b, t, f: (0, f + nf)),
            pl.BlockSpec((tf, d), lambda b, t, f: (f, 0)),
        ],
        out_specs=pl.BlockSpec((1, tm, d), tok),
        out_shape=jax.ShapeDtypeStruct((bsz, s, d), F32),
        scratch_shapes=[pltpu.VMEM((tm, d), F32), pltpu.VMEM((tm, d), BF16), pltpu.VMEM((tm, d), F32)],
        compiler_params=_params(("parallel", "parallel", "arbitrary")),
        name="out_ffn",
    )(x, oa, ob, wo, mod4, gffn, mod4, mod4, mod4, wgu, wgu, wd)


def _t5_bucket_np(d):
    max_exact = N_BUCKETS // 2
    d = np.maximum(d, 0)
    df = np.maximum(d, 1).astype(np.float64)
    large = max_exact + (np.log(df / max_exact) / math.log(MAX_DIST / max_exact)
                         * (N_BUCKETS - max_exact)).astype(np.int64)
    large = np.minimum(large, N_BUCKETS - 1)
    return np.where(d < max_exact, d, large).astype(np.int32)


def _bias_expand_kernel(tab_ref, bucket_ref, o_ref):
    hd = pl.program_id(0)
    bucket = bucket_ref[...]
    acc = jnp.full(bucket.shape, NEG, F32)
    for b in range(N_BUCKETS):
        acc = jnp.where(bucket == b, tab_ref[hd, b], acc)
    o_ref[0] = acc


def _bias_expand(tab, bucket):
    nh = tab.shape[0]
    return pl.pallas_call(
        _bias_expand_kernel,
        grid=(nh,),
        in_specs=[pl.BlockSpec(memory_space=pltpu.SMEM),
                  pl.BlockSpec(bucket.shape, lambda h: (0, 0))],
        out_specs=pl.BlockSpec((1,) + bucket.shape, lambda h: (h, 0, 0)),
        out_shape=jax.ShapeDtypeStruct((nh,) + bucket.shape, F32),
        compiler_params=_params(("parallel",)),
        name="bias_expand",
    )(tab, jnp.asarray(bucket, jnp.int32))


def _bias_tables(rel_bias):
    tab = rel_bias.T.astype(F32) * LOG2_E
    i = np.arange(TQ)[None, :]
    d_near = i + TQ - np.arange(2 * TQ)[:, None]
    near = _bias_expand(tab, np.where(d_near >= 0, _t5_bucket_np(d_near), -1))
    d_win = i + 2 * TQ - np.arange(3 * TQ)[:, None]
    ok_win = (d_win >= 0) & (d_win < WINDOW)
    win = _bias_expand(tab[H_A:], np.where(ok_win, _t5_bucket_np(d_win), -1))
    far = jnp.broadcast_to(tab[:, N_BUCKETS - 1][:, None, None], (tab.shape[0], 8, LANES))
    return near, win, far


def _overlap_t(n_cmp_pad, n_cmp):
    cs = np.arange(n_cmp_pad)[None, :] * CMP_STRIDE
    ss = np.arange(LANES)[:, None] * SEL_BLK
    ov = (cs < ss + SEL_BLK) & (cs + CMP_LEN > ss) & (np.arange(n_cmp_pad)[None, :] < n_cmp)
    return jnp.asarray(ov.astype(np.float32), BF16)


def _block_diag(n):
    m = (np.arange(n)[:, None] // HEAD_DIM == np.arange(n)[None, :] // HEAD_DIM)
    return jnp.asarray(m.astype(np.float32) / HEAD_DIM, BF16)


def kernel(x, c, rel_bias, w_ada, b_ada, g_mix, w_in, q_norm_a, k_norm_a, q_norm_b, k_norm_cmp,
           k_norm_sel, k_norm_win, cmp_pe_k, cmp_w1_k, cmp_w2_k, cmp_pe_v, cmp_w1_v, cmp_w2_v,
           w_out, g_ffn, w_gu, w_down):
    bsz, s, d = x.shape
    depth = w_ada.shape[0]
    assert s % TM_IN == 0 and s % TM_FFN == 0 and s % (2 * TQ) == 0
    assert s // BLK_A <= HEAD_DIM and s // SEL_BLK <= LANES
    assert WINDOW == 2 * TQ and BLK_A == TQ and MAX_DIST <= TQ
    n_chunks = s // CMP_STRIDE
    n_cmp = (s - CMP_LEN) // CMP_STRIDE + 1
    scale = HEAD_DIM ** -0.5 * LOG2_E
    hd = HEAD_DIM

    near, win, far = _bias_tables(rel_bias)
    ovt = _overlap_t(n_chunks, n_cmp)
    bd = _block_diag(D_A)
    tile = lambda g, n: jnp.tile(g.astype(F32), n).reshape(1, -1)
    tile_t = lambda g, n: jnp.broadcast_to(jnp.tile(g.astype(F32), n)[:, None], (n * hd, TM_IN))

    for l in range(depth):
        mod = _ada(c, w_ada[l], b_ada[l])
        mod4 = mod.reshape(bsz, ADA_CHUNKS, 1, d)

        wl = w_in[l]
        cols = np.cumsum([0, H_A * hd, H_A * hd, H_A * hd, H_B * hd] + [G_B * hd] * 6)
        qa_c, ka_c, va_c, qb_c, kc_c, vc_c, ks_c, vs_c, kw_c, vw_c = [
            wl[:, int(a):int(b)] for a, b in zip(cols[:-1], cols[1:])]
        gl = wl[:, int(cols[-1]):].reshape(d, G_B, 3 * R_B)
        gl = jnp.pad(gl, ((0, 0), (0, 0), (0, GATE_ROWS - 3 * R_B))).reshape(d, G_B * GATE_ROWS)
        w_rows = jnp.concatenate([ka_c, kc_c, vc_c, ks_c, kw_c], axis=1).astype(BF16)
        w_t = jnp.concatenate([qa_c, va_c, qb_c, vs_c, vw_c, gl], axis=1).T.astype(BF16)

        (qa, kaug, vat, kmean, qb, kc, vc, ksaug, vst, kw, vwt, gates) = _inproj(
            x, mod4, mod4, g_mix[l].reshape(1, d), w_rows, w_t, bd,
            tile_t(q_norm_a[l], H_A) * scale, tile(k_norm_a[l], H_A), tile_t(q_norm_b[l], H_B) * scale,
            tile(k_norm_sel[l], G_B), tile(k_norm_win[l], G_B))

        nba = s // BLK_A
        km = kmean.reshape(bsz, nba, H_A, hd).transpose(0, 2, 1, 3)
        km = jnp.pad(km, ((0, 0), (0, 0), (0, hd - nba), (0, LANES - hd)))
        o_a = _moba(_moba_gate(qa, km), kaug, vat, near[:H_A], far[:H_A])

        chunks = lambda t: t.reshape(bsz, s, G_B, hd).transpose(0, 2, 1, 3).reshape(
            bsz, G_B, n_chunks, CMP_STRIDE * hd)
        w1 = jnp.stack([cmp_w1_k[l], cmp_w1_v[l]]).astype(BF16)
        w2t = jnp.stack([cmp_w2_k[l].T, cmp_w2_v[l].T]).astype(BF16)
        pe = jnp.stack([cmp_pe_k[l], cmp_pe_v[l]]).reshape(2, 1, CMP_LEN * hd)
        pe = jnp.broadcast_to(pe, (2, 8, CMP_LEN * hd)).astype(BF16)
        kcmp, vcmpt = _compress(chunks(kc), chunks(vc), w1, w2t, pe,
                                k_norm_cmp[l].astype(F32).reshape(1, hd))

        oc, selb = _nsa_cmp(qb, kcmp, vcmpt, ovt, n_cmp)
        o_b = _nsa_main(qb, selb, ksaug, vst, kw, vwt, oc, gates, near[H_A:], win, far[H_A:])

        x = _out_ffn(x, o_a, o_b, w_out[l].astype(BF16), mod4, g_ffn[l].reshape(1, d),
                     w_gu[l].astype(BF16), w_down[l].astype(BF16))
    return x
```

```python
import functools
import math

import jax
import jax.numpy as jnp
import numpy as np
from jax import lax
from jax.experimental import pallas as pl
from jax.experimental.pallas import tpu as pltpu

F32 = jnp.float32
BF16 = jnp.bfloat16

HEAD_DIM = 64
LANES = 128
BF16_ROWS = 16
H_A = 8
H_B = 8
G_B = 2
R_B = H_B // G_B
D_A = H_A * HEAD_DIM
D_B = H_B * HEAD_DIM
D_KV = G_B * HEAD_DIM
BLK_A = 256
TOPK_A = 3
CMP_LEN = 32
CMP_STRIDE = 16
CMP_HIDDEN = 256
SEL_BLK = 64
SEL_TOPK = 16
WINDOW = 512
N_BUCKETS = 32
MAX_DIST = 128
ADA_CHUNKS = 6
NEG = -1e30
N_FORCED = 3
EPS = 1e-6
LOG2_E = math.log2(math.e)

_LOG2_BLK_A = BLK_A.bit_length() - 1
_LOG2_SEL_BLK = SEL_BLK.bit_length() - 1

TQ = 256
TM_IN = 1024
TM_FFN = 512
ADA_TN = 512
GATE_TQ = 2048
GATE_HEADS = 4
CMP_TQ = 512
FAR_PAIRS_PER_TRIP = 8
MOBA_HEADS = 4
GATE_ROWS = 16
ACC_ROWS = HEAD_DIM + BF16_ROWS
V7X_VMEM_BYTES = 64 * 1024 * 1024
VMEM_LIMIT = V7X_VMEM_BYTES * 7 // 8


def _dot(a, b):
    return jnp.dot(a, b, preferred_element_type=F32)


def _dot_nt(a, b):
    return lax.dot_general(a, b, (((1,), (1,)), ((), ())), preferred_element_type=F32)


def _split(a):
    hi = a.astype(BF16)
    lo = (a - hi.astype(F32)).astype(BF16)
    return hi, lo


def _dot3(a, b):
    ah, al = _split(a)
    bh, bl = _split(b)
    return _dot(ah, bh) + (_dot(al, bh) + _dot(ah, bl))


def _params(sem):
    return pltpu.CompilerParams(dimension_semantics=sem, vmem_limit_bytes=VMEM_LIMIT)


def _ada_kernel(c_ref, w_ref, b_ref, o_ref):
    c = c_ref[...]
    o_ref[...] = _dot3(c * jax.nn.sigmoid(c), w_ref[...]) + b_ref[...]


def _ada(c, w, b):
    bsz, d = c.shape
    n = w.shape[1]
    tn = ADA_TN
    return pl.pallas_call(
        _ada_kernel,
        grid=(n // tn,),
        in_specs=[pl.BlockSpec((bsz, d), lambda j: (0, 0)),
                  pl.BlockSpec((d, tn), lambda j: (0, j)),
                  pl.BlockSpec((1, tn), lambda j: (0, j))],
        out_specs=pl.BlockSpec((bsz, tn), lambda j: (0, j)),
        out_shape=jax.ShapeDtypeStruct((bsz, n), F32),
        compiler_params=_params(("arbitrary",)),
        name="ada",
    )(c, w, b.reshape(1, n))


def _rms_rows(xf, g):
    ms = jnp.mean(xf * xf, axis=-1, keepdims=True)
    return xf * lax.rsqrt(ms + EPS) * g


def _head_norm(t, bd, gain):
    ms = _dot((t * t).astype(BF16), bd)
    return t * lax.rsqrt(ms + EPS) * gain


def _head_norm_t(t, gain):
    heads = []
    for hd in range(t.shape[0] // HEAD_DIM):
        th = t[hd * HEAD_DIM:(hd + 1) * HEAD_DIM]
        ms = jnp.mean(th * th, axis=0, keepdims=True)
        heads.append(th * lax.rsqrt(ms + EPS) * gain[hd * HEAD_DIM:(hd + 1) * HEAD_DIM])
    return heads


def _inproj_kernel(x_ref, sc_ref, sh_ref, gmix_ref, wr_ref, wt_ref, bd_ref, gqa_ref, gka_ref, gqb_ref,
                   gks_ref, gkw_ref,
                   qa_ref, kaug_ref, va_ref, kmean_ref, qb_ref, kc_ref, vc_ref, ksaug_ref,
                   vs_ref, kw_ref, vw_ref, gates_ref):
    tm = x_ref.shape[1]
    ti = pl.program_id(1)
    xf = x_ref[0]
    h = _rms_rows(xf, gmix_ref[...]) * (1.0 + sc_ref[0, 0]) + sh_ref[0, 0]
    hb = h.astype(BF16)

    def proj(c0, c1):
        return _dot(hb, wr_ref[:, c0:c1])

    def proj_t(r0, r1):
        return _dot_nt(wt_ref[r0:r1, :], hb)

    bd = bd_ref[...]
    bd2 = bd_ref[0:LANES, 0:LANES]
    lane = lax.broadcasted_iota(jnp.int32, (tm, LANES), 1)
    row = lax.broadcasted_iota(jnp.int32, (tm, LANES), 0) + ti * tm
    low = lane < HEAD_DIM

    def k_in_low(pair, odd):
        return pltpu.roll(pair, HEAD_DIM, 1) if odd else pair

    t_qa, t_va, t_qb = 0, D_A, 2 * D_A
    t_vs, t_vw, t_gl = t_qb + D_B, t_qb + D_B + D_KV, t_qb + D_B + 2 * D_KV
    c_ka, c_kc, c_vc, c_ks, c_kw = 0, D_A, D_A + D_KV, D_A + 2 * D_KV, D_A + 3 * D_KV

    for hd, qh in enumerate(_head_norm_t(proj_t(t_qa, t_va), gqa_ref[...])):
        qa_ref[0, hd] = qh

    def put_tiles(ref, vt):
        for i in range(tm // TQ):
            ref[0, i] = vt[:, i * TQ:(i + 1) * TQ].astype(BF16)

    put_tiles(va_ref, proj_t(t_va, t_qb))
    for hd, qh in enumerate(_head_norm_t(proj_t(t_qb, t_vs), gqb_ref[...])):
        qb_ref[0, hd] = qh
    put_tiles(vs_ref, proj_t(t_vs, t_vw))
    put_tiles(vw_ref, proj_t(t_vw, t_gl))
    gl = jax.nn.sigmoid(proj_t(t_gl, t_gl + G_B * GATE_ROWS))
    for g in range(G_B):
        gates_ref[0, g] = gl[g * GATE_ROWS:(g + 1) * GATE_ROWS]

    ka = _head_norm(proj(c_ka, c_kc), bd, gka_ref[...])
    oh_a = jnp.where(lane - HEAD_DIM == (row >> _LOG2_BLK_A), 1.0, 0.0)
    for hd in range(H_A):
        pair = ka[:, (hd // 2) * LANES:(hd // 2 + 1) * LANES]
        kaug_ref[0, hd] = jnp.where(low, k_in_low(pair, hd % 2), oh_a).astype(BF16)
    for i in range(tm // BLK_A):
        kmean_ref[0, i] = jnp.mean(ka[i * BLK_A:(i + 1) * BLK_A], axis=0, keepdims=True)

    kc_ref[0] = proj(c_kc, c_vc).astype(BF16)
    vc_ref[0] = proj(c_vc, c_ks).astype(BF16)

    ks = _head_norm(proj(c_ks, c_kw), bd2, gks_ref[...])
    kw = _head_norm(proj(c_kw, c_kw + D_KV), bd2, gkw_ref[...])
    oh_s = jnp.where(lane == (row >> _LOG2_SEL_BLK), 1.0, 0.0).astype(BF16)
    for g in range(G_B):
        ksaug_ref[0, g] = jnp.concatenate(
            [jnp.where(low, k_in_low(ks, g), 0.0).astype(BF16), oh_s], axis=1)
        kw_ref[0, g] = jnp.where(low, k_in_low(kw, g), 0.0).astype(BF16)


def _inproj(x, sc, sh, gmix, wr, wt, bd, gqa, gka, gqb, gks, gkw):
    bsz, s, d = x.shape
    tm = TM_IN
    nt = s // tm
    nba = s // BLK_A
    const2 = lambda b, t: (0, 0)
    tok3 = lambda b, t: (b, t, 0)
    tok4 = lambda b, t: (b, 0, t, 0)
    tile4 = lambda b, t: (b, t, 0, 0)
    tr4 = lambda b, t: (b, 0, 0, t)
    in_specs = [
        pl.BlockSpec((1, tm, d), tok3),
        pl.BlockSpec((1, 1, 1, d), lambda b, t: (b, 1, 0, 0)),
        pl.BlockSpec((1, 1, 1, d), lambda b, t: (b, 0, 0, 0)),
        pl.BlockSpec((1, d), const2),
        pl.BlockSpec(wr.shape, const2),
        pl.BlockSpec(wt.shape, const2),
        pl.BlockSpec((D_A, D_A), const2),
        pl.BlockSpec((D_A, tm), const2),
        pl.BlockSpec((1, D_A), const2),
        pl.BlockSpec((D_B, tm), const2),
        pl.BlockSpec((1, LANES), const2),
        pl.BlockSpec((1, LANES), const2),
    ]
    out_shape = [
        jax.ShapeDtypeStruct((bsz, H_A, HEAD_DIM, s), F32),
        jax.ShapeDtypeStruct((bsz, H_A, s, LANES), BF16),
        jax.ShapeDtypeStruct((bsz, s // TQ, D_A, TQ), BF16),
        jax.ShapeDtypeStruct((bsz, nba, 1, D_A), F32),
        jax.ShapeDtypeStruct((bsz, H_B, HEAD_DIM, s), F32),
        jax.ShapeDtypeStruct((bsz, s, LANES), BF16),
        jax.ShapeDtypeStruct((bsz, s, LANES), BF16),
        jax.ShapeDtypeStruct((bsz, G_B, s, 2 * LANES), BF16),
        jax.ShapeDtypeStruct((bsz, s // TQ, LANES, TQ), BF16),
        jax.ShapeDtypeStruct((bsz, G_B, s, LANES), BF16),
        jax.ShapeDtypeStruct((bsz, s // TQ, LANES, TQ), BF16),
        jax.ShapeDtypeStruct((bsz, G_B, GATE_ROWS, s), F32),
    ]
    out_specs = [
        pl.BlockSpec((1, H_A, HEAD_DIM, tm), tr4),
        pl.BlockSpec((1, H_A, tm, LANES), tok4),
        pl.BlockSpec((1, tm // TQ, D_A, TQ), tile4),
        pl.BlockSpec((1, tm // BLK_A, 1, D_A), lambda b, t: (b, t, 0, 0)),
        pl.BlockSpec((1, H_B, HEAD_DIM, tm), tr4),
        pl.BlockSpec((1, tm, LANES), tok3),
        pl.BlockSpec((1, tm, LANES), tok3),
        pl.BlockSpec((1, G_B, tm, 2 * LANES), tok4),
        pl.BlockSpec((1, tm // TQ, LANES, TQ), tile4),
        pl.BlockSpec((1, G_B, tm, LANES), tok4),
        pl.BlockSpec((1, tm // TQ, LANES, TQ), tile4),
        pl.BlockSpec((1, G_B, GATE_ROWS, tm), tr4),
    ]
    return pl.pallas_call(
        _inproj_kernel,
        grid=(bsz, nt),
        in_specs=in_specs,
        out_specs=out_specs,
        out_shape=out_shape,
        compiler_params=_params(("parallel", "parallel")),
        name="inproj",
    )(x, sc, sh, gmix, wr, wt, bd, gqa, gka, gqb, gks, gkw)


def _compress_kernel(ck_ref, cv_ref, w1_ref, w2t_ref, pe_ref, gk_ref, ok_ref, ov_ref):
    half = CMP_STRIDE * HEAD_DIM
    for kv, c_ref in enumerate((ck_ref, cv_ref)):
        for g in range(G_B):
            c = c_ref[0, g]
            a = _dot(c, w1_ref[kv, 0:half, :])
            b = _dot(c, w1_ref[kv, half:2 * half, :])
            n = a.shape[0]
            b_next = pltpu.roll(b, n - 1, 0)
            pe_term = _dot(pe_ref[kv], w1_ref[kv])[0:1]
            hid = jax.nn.gelu(a + b_next + pe_term).astype(BF16)
            if kv == 0:
                y = _dot_nt(hid, w2t_ref[kv])
                ms = jnp.mean(y * y, axis=1, keepdims=True)
                y = y * lax.rsqrt(ms + EPS) * gk_ref[...]
                ok_ref[0, g] = jnp.concatenate([y, jnp.zeros_like(y)], axis=1).astype(BF16)
            else:
                ov_ref[0, g] = _dot_nt(w2t_ref[kv], hid).astype(BF16)


def _compress(ck, cv, w1, w2t, pe, gk):
    bsz, g, n, width = ck.shape
    blk = pl.BlockSpec((1, g, n, width), lambda b: (b, 0, 0, 0))
    full = lambda a: pl.BlockSpec(a.shape, lambda b: (0,) * a.ndim)
    return pl.pallas_call(
        _compress_kernel,
        grid=(bsz,),
        in_specs=[blk, blk, full(w1), full(w2t), full(pe), full(gk)],
        out_specs=[pl.BlockSpec((1, g, n, LANES), lambda b: (b, 0, 0, 0)),
                   pl.BlockSpec((1, g, HEAD_DIM, n), lambda b: (b, 0, 0, 0))],
        out_shape=[jax.ShapeDtypeStruct((bsz, g, n, LANES), BF16),
                   jax.ShapeDtypeStruct((bsz, g, HEAD_DIM, n), BF16)],
        compiler_params=_params(("parallel",)),
        name="compress",
    )(ck, cv, w1, w2t, pe, gk)


def _with_ones(vt):
    return jnp.concatenate([vt, jnp.ones((BF16_ROWS, vt.shape[1]), BF16)], axis=0)


def _col_max(s):
    while s.shape[0] > 8:
        half = s.shape[0] // 2
        s = jnp.maximum(s[0:half], s[half:2 * half])
    return jnp.max(s, axis=0, keepdims=True)


def _flash_update(carries, chains):
    m_news = []
    for carry, tiles in zip(carries, chains):
        tops = []
        for s, _, bias, top in tiles:
            top = _col_max(s) if top is None else top
            tops.append(top if bias is None else top + bias)
        m_news.append(functools.reduce(jnp.maximum, tops if carry is None else tops + [carry[0]]))
    pvs = [None] * len(chains)
    for t in range(max(len(tiles) for tiles in chains)):
        for c, tiles in enumerate(chains):
            if t < len(tiles):
                s, vt, bias, _ = tiles[t]
                p = jnp.exp2(s - (m_news[c] if bias is None else m_news[c] - bias)).astype(BF16)
                part = _dot(_with_ones(vt), p)
                pvs[c] = part if pvs[c] is None else pvs[c] + part
    outs = []
    for carry, m_new, pv in zip(carries, m_news, pvs):
        outs.append((m_new, pv if carry is None else jnp.exp2(carry[0] - m_new) * carry[1] + pv))
    return outs


def _flash_out(acc):
    return acc[0:HEAD_DIM] / acc[HEAD_DIM:HEAD_DIM + 1]


def _topk_rows(scores, index, k):
    scores = list(scores)
    picked = [jnp.zeros(sc.shape, F32) for sc in scores]
    for _ in range(k):
        mx = [jnp.max(sc, axis=0, keepdims=True) for sc in scores]
        cand = [jnp.where(sc == m, index, jnp.int32(1 << 20)) for sc, m in zip(scores, mx)]
        first = [jnp.min(c, axis=0, keepdims=True) for c in cand]
        hit = [index == f for f in first]
        picked = [jnp.where(h, 1.0, p) for h, p in zip(hit, picked)]
        scores = [jnp.where(h, -jnp.inf, sc) for h, sc in zip(hit, scores)]
    return picked


def _tile_rows(j):
    return pl.ds(pl.multiple_of(j * TQ, TQ), TQ)


def _moba_gate_kernel(q_ref, km_ref, o_ref, *, n_sel, n_rows):
    nh, tg = q_ref.shape[1], q_ref.shape[3]
    t0 = pl.program_id(2) * tg
    shape = (n_rows, tg)
    blk = lax.broadcasted_iota(jnp.int32, shape, 0)
    own = (lax.broadcasted_iota(jnp.int32, shape, 1) + t0) >> _LOG2_BLK_A
    valid = blk < own
    qs, gates = [], []
    for hh in range(nh):
        q = q_ref[0, hh]
        gate = _dot3(km_ref[0, hh, 0:n_rows, :], jnp.concatenate([q, jnp.zeros_like(q)], axis=0))
        qs.append(q)
        gates.append(jnp.where(valid, gate, -jnp.inf))
    unused = jnp.full((HEAD_DIM - n_rows, tg), NEG, F32)
    for hh, picked in enumerate(_topk_rows(gates, blk, n_sel)):
        keep = jnp.where(valid, picked, 0.0) + jnp.where(blk == own, 1.0, 0.0)
        selb = jnp.where(keep > 0.0, 0.0, NEG)
        o_ref[0, hh] = jnp.concatenate([qs[hh], selb, unused], axis=0).astype(BF16)


def _moba_gate(qa, km):
    bsz, nh, _, s = qa.shape
    tg = min(s, GATE_TQ)
    hpb = GATE_HEADS
    n_sel = max(1, min(TOPK_A, s // BLK_A - 1))
    return pl.pallas_call(
        functools.partial(_moba_gate_kernel, n_sel=n_sel, n_rows=-(-(s // BLK_A) // 8) * 8),
        grid=(bsz, nh // hpb, s // tg),
        in_specs=[pl.BlockSpec((1, hpb, HEAD_DIM, tg), lambda b, h, t: (b, h, 0, t)),
                  pl.BlockSpec((1, hpb, HEAD_DIM, LANES), lambda b, h, t: (b, h, 0, 0))],
        out_specs=pl.BlockSpec((1, hpb, LANES, tg), lambda b, h, t: (b, h, 0, t)),
        out_shape=jax.ShapeDtypeStruct((bsz, nh, LANES, s), BF16),
        compiler_params=_params(("parallel", "parallel", "parallel")),
        name="moba_gate",
    )(qa, km)


def _far_tiles(qi):
    n_far = jnp.maximum(qi - 1, 0)
    left = jnp.maximum(n_far - 1, 0)
    off_left = jnp.where((n_far & 1) == 1, 0.0, NEG)
    return n_far >> 1, left, off_left


class _FarLoop:
    def __init__(self, n_pairs, heads, qk_tile, values, cbs, sa_ref, sb_ref, m_ref, acc_ref):
        self.n_pairs, self.heads, self.qk_tile, self.values, self.cbs = n_pairs, heads, qk_tile, values, cbs
        self.sa_ref, self.sb_ref, self.m_ref, self.acc_ref = sa_ref, sb_ref, m_ref, acc_ref
        self.last = jnp.maximum(n_pairs - 1, 0)

    def fetch(self, buf_ref, h, i):
        s_lo = self.qk_tile(h, 2 * i)
        s_hi = self.qk_tile(h, 2 * i + 1)
        buf_ref[h, 0:TQ, :] = s_lo
        buf_ref[h, TQ:2 * TQ, :] = s_hi
        return _col_max(s_lo), _col_max(s_hi)

    def consume(self, buf_ref, h, top, i):
        (m, acc), = _flash_update([(self.m_ref[h], self.acc_ref[h])], [[
            (buf_ref[h, 0:TQ, :], self.values(h, 2 * i), self.cbs[h], top[0]),
            (buf_ref[h, TQ:2 * TQ, :], self.values(h, 2 * i + 1), self.cbs[h], top[1])]])
        self.m_ref[h] = m
        self.acc_ref[h] = acc

    def first(self):
        return tuple(self.fetch(self.sa_ref, h, 0) for h in self.heads)

    def run(self, tops_first):
        def two_pairs(ia, tops_a):
            tops_b, tops_next = [], []
            for h in self.heads:
                tops_b.append(self.fetch(self.sb_ref, h, ia + 1))
                self.consume(self.sa_ref, h, tops_a[h], ia)
            for h in self.heads:
                tops_next.append(self.fetch(self.sa_ref, h, jnp.minimum(ia + 2, self.last)))
                self.consume(self.sb_ref, h, tops_b[h], ia + 1)
            return tuple(tops_next)

        def pairs(n, start, tops):
            for k in range(0, n, 2):
                tops = two_pairs(start + k, tops)
            return tops

        trip = FAR_PAIRS_PER_TRIP
        n_trips = self.n_pairs // trip
        tops = lax.fori_loop(0, n_trips, lambda q, t: pairs(trip, trip * q, t), tops_first)
        done = trip * n_trips
        n = trip // 2
        while n >= 2:
            has = ((self.n_pairs // n) & 1) == 1
            tops = lax.cond(has, functools.partial(pairs, n, done), lambda t: t, tops)
            done = done + jnp.where(has, n, 0)
            n //= 2
        tops_last = tops

        @pl.when((self.n_pairs & 1) == 1)
        def _():
            for h in self.heads:
                self.consume(self.sa_ref, h, tops_last[h], self.last)


def _moba_kernel(q_ref, k_ref, vt_ref, tab_ref, cb_ref, o_ref, sa_ref, sb_ref, m_ref, acc_ref):
    qi = pl.program_id(2)
    jp = jnp.maximum(qi - 1, 0)
    off_p = jnp.where(qi >= 1, 0.0, NEG)
    n_pairs, jl, off_l = _far_tiles(qi)
    heads = range(MOBA_HEADS)
    qaug = [q_ref[0, hh] for hh in heads]
    cbs = [cb_ref[hh][0:1, 0:1] for hh in heads]

    def scores(hh, j):
        return _dot(k_ref[0, hh, _tile_rows(j), :], qaug[hh])

    def values(hh, j):
        return vt_ref[0, j, hh * HEAD_DIM:(hh + 1) * HEAD_DIM, :]

    far = _FarLoop(n_pairs, heads, scores, values, cbs, sa_ref, sb_ref, m_ref, acc_ref)

    s_own = [scores(hh, qi) for hh in heads]
    s_prev = [scores(hh, jp) for hh in heads]
    s_left = [scores(hh, jl) for hh in heads]
    tops_first = far.first()
    chains = [[(s_own[hh] + tab_ref[hh, TQ:2 * TQ, :], values(hh, qi), None, None),
               (s_prev[hh] + tab_ref[hh, 0:TQ, :], values(hh, jp), off_p, None),
               (s_left[hh], values(hh, jl), cbs[hh] + off_l, None)] for hh in heads]
    for hh, (m, acc) in zip(heads, _flash_update([None] * len(chains), chains)):
        m_ref[hh] = m
        acc_ref[hh] = acc
    far.run(tops_first)
    out_t = jnp.concatenate([_flash_out(acc_ref[hh]) for hh in heads], axis=0)
    o_ref[0] = out_t.T.astype(BF16)


def _flash_scratch(n_heads):
    return [pltpu.VMEM((n_heads, 2 * TQ, TQ), F32),
            pltpu.VMEM((n_heads, 2 * TQ, TQ), F32),
            pltpu.VMEM((n_heads, 1, TQ), F32),
            pltpu.VMEM((n_heads, ACC_ROWS, TQ), F32)]


def _moba(qaug, kaug, vat, tab, cb):
    bsz, _, _, s = qaug.shape
    nq = s // TQ
    nh = MOBA_HEADS
    return pl.pallas_call(
        _moba_kernel,
        grid=(bsz, H_A // nh, nq),
        in_specs=[
            pl.BlockSpec((1, nh, LANES, TQ), lambda b, hp, qi: (b, hp, 0, qi)),
            pl.BlockSpec((1, nh, s, LANES), lambda b, hp, qi: (b, hp, 0, 0)),
            pl.BlockSpec((1, nq, nh * HEAD_DIM, TQ), lambda b, hp, qi: (b, 0, hp, 0)),
            pl.BlockSpec((nh, 2 * TQ, TQ), lambda b, hp, qi: (hp, 0, 0)),
            pl.BlockSpec((nh, 8, LANES), lambda b, hp, qi: (hp, 0, 0)),
        ],
        out_specs=pl.BlockSpec((1, TQ, nh * HEAD_DIM), lambda b, hp, qi: (b, qi, hp)),
        out_shape=jax.ShapeDtypeStruct((bsz, s, H_A * HEAD_DIM), BF16),
        scratch_shapes=_flash_scratch(nh),
        compiler_params=_params(("parallel", "parallel", "arbitrary")),
        name="moba",
    )(qaug, kaug, vat, tab, cb)


def _nsa_cmp_kernel(q_ref, kc_ref, vct_ref, ovt_ref, oc_ref, selb_ref, *, n_sel, n_cmp, n_parts):
    qi = pl.program_id(1)
    ncp = kc_ref.shape[2]
    tq = q_ref.shape[3]
    t0 = qi * tq
    zeros = jnp.zeros((HEAD_DIM, tq), BF16)
    qs = [jnp.concatenate([q_ref[0, hd].astype(BF16), zeros], axis=0) for hd in range(H_B)]

    def body(nk, nb):
        n_idx = lax.broadcasted_iota(jnp.int32, (nk, tq), 0)
        t_idx = lax.broadcasted_iota(jnp.int32, (nk, tq), 1) + t0
        mask = (n_idx * CMP_STRIDE + (CMP_LEN - 1) <= t_idx) & (n_idx < n_cmp)
        any_key = t_idx[0:1] >= CMP_LEN - 1
        blk = lax.broadcasted_iota(jnp.int32, (nb, tq), 0)
        cur = (lax.broadcasted_iota(jnp.int32, (nb, tq), 1) + t0) >> _LOG2_SEL_BLK
        ok = blk <= cur
        forced = (blk == 0) | (blk == cur) | (blk == cur - 1)
        ovt = ovt_ref[0:nb, 0:nk]
        scores = []
        for g in range(G_B):
            kc = kc_ref[0, g, 0:nk, :]
            vct = _with_ones(vct_ref[0, g, :, 0:nk])
            psum = jnp.zeros((nk, tq), F32)
            for hd in range(g * R_B, (g + 1) * R_B):
                z = jnp.where(mask, _dot(kc, qs[hd]), NEG)
                e = jnp.exp2(z - _col_max(z))
                acc = _dot(vct, e.astype(BF16))
                rinv = jnp.where(any_key, 1.0 / acc[HEAD_DIM:HEAD_DIM + 1], 0.0)
                oc_ref[0, hd] = acc[0:HEAD_DIM] * rinv
                psum = psum + e * rinv
            ph, pl_ = _split(psum)
            imp_t = _dot(ovt, ph) + _dot(ovt, pl_)
            scores.append(jnp.where(ok & jnp.logical_not(forced), imp_t, -jnp.inf))
        for g, picked in enumerate(_topk_rows(scores, blk, max(n_sel - N_FORCED, 0))):
            selb_ref[0, g, 0:nb, :] = jnp.where(ok & (forced | (picked > 0.0)), 0.0, NEG).astype(BF16)
            if nb < LANES:
                selb_ref[0, g, nb:LANES, :] = jnp.full((LANES - nb, tq), NEG, BF16)

    part = ncp // n_parts
    need = jnp.minimum(((qi + 1) * (tq // CMP_STRIDE) + part - 1) // part, n_parts)
    for v in range(1, n_parts + 1):
        pl.when(need == v)(functools.partial(body, v * part, min(LANES, v * part * CMP_STRIDE // SEL_BLK)))


def _cmp_parts(ncp):
    return 4 if ncp % (4 * LANES) == 0 else 1


def _nsa_cmp(qb, kcmp, vcmpt, ovt, n_cmp):
    bsz, _, _, s = qb.shape
    tq = CMP_TQ if s % CMP_TQ == 0 else TQ
    nq = s // tq
    ncp = kcmp.shape[2]
    n_sel = min(SEL_TOPK, s // SEL_BLK)
    assert n_sel >= N_FORCED
    n_parts = _cmp_parts(ncp)
    return pl.pallas_call(
        functools.partial(_nsa_cmp_kernel, n_sel=n_sel, n_cmp=n_cmp, n_parts=n_parts),
        grid=(bsz, nq),
        in_specs=[
            pl.BlockSpec((1, H_B, HEAD_DIM, tq), lambda b, qi: (b, 0, 0, qi)),
            pl.BlockSpec((1, G_B, ncp, LANES), lambda b, qi: (b, 0, 0, 0)),
            pl.BlockSpec((1, G_B, HEAD_DIM, ncp), lambda b, qi: (b, 0, 0, 0)),
            pl.BlockSpec((LANES, ncp), lambda b, qi: (0, 0)),
        ],
        out_specs=[
            pl.BlockSpec((1, H_B, HEAD_DIM, tq), lambda b, qi: (b, 0, 0, qi)),
            pl.BlockSpec((1, G_B, LANES, tq), lambda b, qi: (b, 0, 0, qi)),
        ],
        out_shape=[jax.ShapeDtypeStruct((bsz, H_B, HEAD_DIM, s), F32),
                   jax.ShapeDtypeStruct((bsz, G_B, LANES, s), BF16)],
        compiler_params=_params(("parallel", "parallel")),
        name="nsa_cmp",
    )(qb, kcmp, vcmpt, ovt)


def _nsa_main_kernel(q_ref, selb_ref, ks_ref, vst_ref, kw_ref, vwt_ref, oc_ref, gates_ref,
                     tabs_ref, tabw_ref, cb_ref, o_ref, sa_ref, sb_ref, m_ref, acc_ref):
    qi = pl.program_id(2)
    selb = selb_ref[0, 0]
    gates = gates_ref[0, 0]
    zeros = jnp.zeros((HEAD_DIM, TQ), BF16)
    j1 = jnp.maximum(qi - 1, 0)
    j2 = jnp.maximum(qi - 2, 0)
    off1 = jnp.where(qi >= 1, 0.0, NEG)
    off2 = jnp.where(qi >= 2, 0.0, NEG)
    n_pairs, jl, off_l = _far_tiles(qi)

    heads = range(R_B)
    qw = [jnp.concatenate([q_ref[0, r].astype(BF16), zeros], axis=0) for r in heads]
    qs = [jnp.concatenate([qw[r], selb], axis=0) for r in heads]
    cbs = [cb_ref[r][0:1, 0:1] for r in heads]

    def s_sel(r, j):
        return _dot(ks_ref[0, 0, _tile_rows(j), :], qs[r])

    def s_win(r, j):
        return _dot(kw_ref[0, 0, _tile_rows(j), :], qw[r])

    far = _FarLoop(n_pairs, heads, s_sel, lambda r, j: vst_ref[0, j], cbs, sa_ref, sb_ref, m_ref, acc_ref)

    sw = [[s_win(r, j) for r in heads] for j in (qi, j1, j2)]
    ss = [[s_sel(r, j) for r in heads] for j in (qi, j1, jl)]
    tops_first = far.first()

    chains = [[(sw[0][r] + tabw_ref[r, 2 * TQ:3 * TQ, :], vwt_ref[0, qi], None, None),
               (sw[1][r] + tabw_ref[r, TQ:2 * TQ, :], vwt_ref[0, j1], off1, None),
               (sw[2][r] + tabw_ref[r, 0:TQ, :], vwt_ref[0, j2], off2, None)] for r in heads]
    chains += [[(ss[0][r] + tabs_ref[r, TQ:2 * TQ, :], vst_ref[0, qi], None, None),
                (ss[1][r] + tabs_ref[r, 0:TQ, :], vst_ref[0, j1], off1, None),
                (ss[2][r], vst_ref[0, jl], cbs[r] + off_l, None)] for r in heads]
    done = _flash_update([None] * len(chains), chains)
    win = [_flash_out(acc) for _, acc in done[:R_B]]
    for r, (m, acc) in zip(heads, done[R_B:]):
        m_ref[r] = m
        acc_ref[r] = acc
    far.run(tops_first)

    outs = [gates[3 * r:3 * r + 1] * oc_ref[0, r]
            + gates[3 * r + 1:3 * r + 2] * _flash_out(acc_ref[r])
            + gates[3 * r + 2:3 * r + 3] * win[r] for r in heads]
    o_ref[0] = jnp.concatenate(outs, axis=0).T.astype(BF16)


def _nsa_main(qb, selb, ksaug, vst, kw, vwt, oc, gates, tabs, tabw, cb):
    bsz, _, _, s = qb.shape
    nq = s // TQ
    return pl.pallas_call(
        _nsa_main_kernel,
        grid=(bsz, G_B, nq),
        in_specs=[
            pl.BlockSpec((1, R_B, HEAD_DIM, TQ), lambda b, g, qi: (b, g, 0, qi)),
            pl.BlockSpec((1, 1, LANES, TQ), lambda b, g, qi: (b, g, 0, qi)),
            pl.BlockSpec((1, 1, s, 2 * LANES), lambda b, g, qi: (b, g, 0, 0)),
            pl.BlockSpec((1, nq, HEAD_DIM, TQ), lambda b, g, qi: (b, 0, g, 0)),
            pl.BlockSpec((1, 1, s, LANES), lambda b, g, qi: (b, g, 0, 0)),
            pl.BlockSpec((1, nq, HEAD_DIM, TQ), lambda b, g, qi: (b, 0, g, 0)),
            pl.BlockSpec((1, R_B, HEAD_DIM, TQ), lambda b, g, qi: (b, g, 0, qi)),
            pl.BlockSpec((1, 1, GATE_ROWS, TQ), lambda b, g, qi: (b, g, 0, qi)),
            pl.BlockSpec((R_B, 2 * TQ, TQ), lambda b, g, qi: (g, 0, 0)),
            pl.BlockSpec((R_B, 3 * TQ, TQ), lambda b, g, qi: (g, 0, 0)),
            pl.BlockSpec((R_B, 8, LANES), lambda b, g, qi: (g, 0, 0)),
        ],
        out_specs=pl.BlockSpec((1, TQ, R_B * HEAD_DIM), lambda b, g, qi: (b, qi, g)),
        out_shape=jax.ShapeDtypeStruct((bsz, s, H_B * HEAD_DIM), BF16),
        scratch_shapes=_flash_scratch(R_B),
        compiler_params=_params(("parallel", "parallel", "arbitrary")),
        name="nsa_main",
    )(qb, selb, ksaug, vst, kw, vwt, oc, gates, tabs, tabw, cb)


def _out_ffn_kernel(x_ref, oa_ref, ob_ref, wo_ref, gtm_ref, gffn_ref, scf_ref, shf_ref, gtf_ref,
                    wg_ref, wu_ref, wd_ref, o_ref, x1_ref, h_ref, acc_ref):
    f = pl.program_id(2)
    half = oa_ref.shape[2]

    @pl.when(f == 0)
    def _():
        mix = _dot(oa_ref[0], wo_ref[0:half, :]) + _dot(ob_ref[0], wo_ref[half:2 * half, :])
        x1 = x_ref[0] + gtm_ref[0, 0] * mix
        x1_ref[...] = x1
        h = _rms_rows(x1, gffn_ref[...]) * (1.0 + scf_ref[0, 0]) + shf_ref[0, 0]
        h_ref[...] = h.astype(BF16)

    h = h_ref[...]
    gate = _dot(h, wg_ref[...])
    up = _dot(h, wu_ref[...])
    act = (gate * jax.nn.sigmoid(gate) * up).astype(BF16)
    part = _dot(act, wd_ref[...])

    @pl.when(f == 0)
    def _():
        acc_ref[...] = part

    @pl.when(f > 0)
    def _():
        acc_ref[...] += part

    @pl.when(f == pl.num_programs(2) - 1)
    def _():
        o_ref[0] = x1_ref[...] + gtf_ref[0, 0] * acc_ref[...]


def _out_ffn(x, oa, ob, wo, mod4, gffn, wgu, wd):
    bsz, s, d = x.shape
    fh = wd.shape[0]
    tm = TM_FFN
    tf = fh // 2 if (fh // 2) % LANES == 0 else fh
    nf = fh // tf
    tok = lambda b, t, f: (b, t, 0)
    modspec = lambda k: pl.BlockSpec((1, 1, 1, d), lambda b, t, f: (b, k, 0, 0))
    return pl.pallas_call(
        _out_ffn_kernel,
        grid=(bsz, s // tm, nf),
        in_specs=[
            pl.BlockSpec((1, tm, d), tok),
            pl.BlockSpec((1, tm, oa.shape[2]), tok),
            pl.BlockSpec((1, tm, ob.shape[2]), tok),
            pl.BlockSpec((d, d), lambda b, t, f: (0, 0)),
            modspec(2),
            pl.BlockSpec((1, d), lambda b, t, f: (0, 0)),
            modspec(4),
            modspec(3),
            modspec(5),
            pl.BlockSpec((d, tf), lambda b, t, f: (0, f)),
            pl.BlockSpec((d, tf), lambda b, t, f: (0, f + nf)),
            pl.BlockSpec((tf, d), lambda b, t, f: (f, 0)),
        ],
        out_specs=pl.BlockSpec((1, tm, d), tok),
        out_shape=jax.ShapeDtypeStruct((bsz, s, d), F32),
        scratch_shapes=[pltpu.VMEM((tm, d), F32), pltpu.VMEM((tm, d), BF16), pltpu.VMEM((tm, d), F32)],
        compiler_params=_params(("parallel", "parallel", "arbitrary")),
        name="out_ffn",
    )(x, oa, ob, wo, mod4, gffn, mod4, mod4, mod4, wgu, wgu, wd)


def _t5_bucket_np(d):
    max_exact = N_BUCKETS // 2
    d = np.maximum(d, 0)
    df = np.maximum(d, 1).astype(np.float64)
    large = max_exact + (np.log(df / max_exact) / math.log(MAX_DIST / max_exact)
                         * (N_BUCKETS - max_exact)).astype(np.int64)
    large = np.minimum(large, N_BUCKETS - 1)
    return np.where(d < max_exact, d, large).astype(np.int32)


def _bias_expand_kernel(tab_ref, bucket_ref, o_ref):
    hd = pl.program_id(0)
    bucket = bucket_ref[...]
    acc = jnp.full(bucket.shape, NEG, F32)
    for b in range(N_BUCKETS):
        acc = jnp.where(bucket == b, tab_ref[hd, b], acc)
    o_ref[0] = acc


def _bias_expand(tab, bucket):
    nh = tab.shape[0]
    return pl.pallas_call(
        _bias_expand_kernel,
        grid=(nh,),
        in_specs=[pl.BlockSpec(memory_space=pltpu.SMEM),
                  pl.BlockSpec(bucket.shape, lambda h: (0, 0))],
        out_specs=pl.BlockSpec((1,) + bucket.shape, lambda h: (h, 0, 0)),
        out_shape=jax.ShapeDtypeStruct((nh,) + bucket.shape, F32),
        compiler_params=_params(("parallel",)),
        name="bias_expand",
    )(tab, jnp.asarray(bucket, jnp.int32))


def _bias_tables(rel_bias):
    tab = rel_bias.T.astype(F32) * LOG2_E
    i = np.arange(TQ)[None, :]
    d_near = i + TQ - np.arange(2 * TQ)[:, None]
    near = _bias_expand(tab, np.where(d_near >= 0, _t5_bucket_np(d_near), -1))
    d_win = i + 2 * TQ - np.arange(3 * TQ)[:, None]
    ok_win = (d_win >= 0) & (d_win < WINDOW)
    win = _bias_expand(tab[H_A:], np.where(ok_win, _t5_bucket_np(d_win), -1))
    far = jnp.broadcast_to(tab[:, N_BUCKETS - 1][:, None, None], (tab.shape[0], 8, LANES))
    return near, win, far


def _overlap_t(n_cmp_pad, n_cmp):
    cs = np.arange(n_cmp_pad)[None, :] * CMP_STRIDE
    ss = np.arange(LANES)[:, None] * SEL_BLK
    ov = (cs < ss + SEL_BLK) & (cs + CMP_LEN > ss) & (np.arange(n_cmp_pad)[None, :] < n_cmp)
    return jnp.asarray(ov.astype(np.float32), BF16)


def _block_diag(n):
    m = (np.arange(n)[:, None] // HEAD_DIM == np.arange(n)[None, :] // HEAD_DIM)
    return jnp.asarray(m.astype(np.float32) / HEAD_DIM, BF16)


def kernel(x, c, rel_bias, w_ada, b_ada, g_mix, w_in, q_norm_a, k_norm_a, q_norm_b, k_norm_cmp,
           k_norm_sel, k_norm_win, cmp_pe_k, cmp_w1_k, cmp_w2_k, cmp_pe_v, cmp_w1_v, cmp_w2_v,
           w_out, g_ffn, w_gu, w_down):
    bsz, s, d = x.shape
    depth = w_ada.shape[0]
    assert s % TM_IN == 0 and s % TM_FFN == 0 and s % (2 * TQ) == 0
    assert s // BLK_A <= HEAD_DIM and s // SEL_BLK <= LANES
    assert WINDOW == 2 * TQ and BLK_A == TQ and MAX_DIST <= TQ
    n_chunks = s // CMP_STRIDE
    n_cmp = (s - CMP_LEN) // CMP_STRIDE + 1
    scale = HEAD_DIM ** -0.5 * LOG2_E
    hd = HEAD_DIM

    near, win, far = _bias_tables(rel_bias)
    ovt = _overlap_t(n_chunks, n_cmp)
    bd = _block_diag(D_A)
    tile = lambda g, n: jnp.tile(g.astype(F32), n).reshape(1, -1)
    tile_t = lambda g, n: jnp.broadcast_to(jnp.tile(g.astype(F32), n)[:, None], (n * hd, TM_IN))

    for l in range(depth):
        mod = _ada(c, w_ada[l], b_ada[l])
        mod4 = mod.reshape(bsz, ADA_CHUNKS, 1, d)

        wl = w_in[l]
        cols = np.cumsum([0, H_A * hd, H_A * hd, H_A * hd, H_B * hd] + [G_B * hd] * 6)
        qa_c, ka_c, va_c, qb_c, kc_c, vc_c, ks_c, vs_c, kw_c, vw_c = [
            wl[:, int(a):int(b)] for a, b in zip(cols[:-1], cols[1:])]
        gl = wl[:, int(cols[-1]):].reshape(d, G_B, 3 * R_B)
        gl = jnp.pad(gl, ((0, 0), (0, 0), (0, GATE_ROWS - 3 * R_B))).reshape(d, G_B * GATE_ROWS)
        w_rows = jnp.concatenate([ka_c, kc_c, vc_c, ks_c, kw_c], axis=1).astype(BF16)
        w_t = jnp.concatenate([qa_c, va_c, qb_c, vs_c, vw_c, gl], axis=1).T.astype(BF16)

        (qa, kaug, vat, kmean, qb, kc, vc, ksaug, vst, kw, vwt, gates) = _inproj(
            x, mod4, mod4, g_mix[l].reshape(1, d), w_rows, w_t, bd,
            tile_t(q_norm_a[l], H_A) * scale, tile(k_norm_a[l], H_A), tile_t(q_norm_b[l], H_B) * scale,
            tile(k_norm_sel[l], G_B), tile(k_norm_win[l], G_B))

        nba = s // BLK_A
        km = kmean.reshape(bsz, nba, H_A, hd).transpose(0, 2, 1, 3)
        km = jnp.pad(km, ((0, 0), (0, 0), (0, hd - nba), (0, LANES - hd)))
        o_a = _moba(_moba_gate(qa, km), kaug, vat, near[:H_A], far[:H_A])

        chunks = lambda t: t.reshape(bsz, s, G_B, hd).transpose(0, 2, 1, 3).reshape(
            bsz, G_B, n_chunks, CMP_STRIDE * hd)
        w1 = jnp.stack([cmp_w1_k[l], cmp_w1_v[l]]).astype(BF16)
        w2t = jnp.stack([cmp_w2_k[l].T, cmp_w2_v[l].T]).astype(BF16)
        pe = jnp.stack([cmp_pe_k[l], cmp_pe_v[l]]).reshape(2, 1, CMP_LEN * hd)
        pe = jnp.broadcast_to(pe, (2, 8, CMP_LEN * hd)).astype(BF16)
        kcmp, vcmpt = _compress(chunks(kc), chunks(vc), w1, w2t, pe,
                                k_norm_cmp[l].astype(F32).reshape(1, hd))

        oc, selb = _nsa_cmp(qb, kcmp, vcmpt, ovt, n_cmp)
        o_b = _nsa_main(qb, selb, ksaug, vst, kw, vwt, oc, gates, near[H_A:], win, far[H_A:])

        x = _out_ffn(x, o_a, o_b, w_out[l].astype(BF16), mod4, g_ffn[l].reshape(1, d),
                     w_gu[l].astype(BF16), w_down[l].astype(BF16))
    return x
```

```python
import functools
import math

import jax
import jax.numpy as jnp
import numpy as np
from jax import lax
from jax.experimental import pallas as pl
from jax.experimental.pallas import tpu as pltpu

F32 = jnp.float32
BF16 = jnp.bfloat16

HEAD_DIM = 64
LANES = 128
BF16_ROWS = 16
H_A = 8
H_B = 8
G_B = 2
R_B = H_B // G_B
D_A = H_A * HEAD_DIM
D_B = H_B * HEAD_DIM
D_KV = G_B * HEAD_DIM
BLK_A = 256
TOPK_A = 3
CMP_LEN = 32
CMP_STRIDE = 16
CMP_HIDDEN = 256
SEL_BLK = 64
SEL_TOPK = 16
WINDOW = 512
N_BUCKETS = 32
MAX_DIST = 128
ADA_CHUNKS = 6
NEG = -1e30
N_FORCED = 3
EPS = 1e-6
LOG2_E = math.log2(math.e)

_LOG2_BLK_A = BLK_A.bit_length() - 1
_LOG2_SEL_BLK = SEL_BLK.bit_length() - 1

TQ = 256
TM_IN = 1024
TM_FFN = 512
ADA_TN = 512
GATE_TQ = 2048
GATE_HEADS = 4
CMP_TQ = 512
FAR_PAIRS_PER_TRIP = 8
MOBA_HEADS = 4
GATE_ROWS = 16
ACC_ROWS = HEAD_DIM + BF16_ROWS
V7X_VMEM_BYTES = 64 * 1024 * 1024
VMEM_LIMIT = V7X_VMEM_BYTES * 7 // 8


def _dot(a, b):
    return jnp.dot(a, b, preferred_element_type=F32)


def _dot_nt(a, b):
    return lax.dot_general(a, b, (((1,), (1,)), ((), ())), preferred_element_type=F32)


def _split(a):
    hi = a.astype(BF16)
    lo = (a - hi.astype(F32)).astype(BF16)
    return hi, lo


def _dot3(a, b):
    ah, al = _split(a)
    bh, bl = _split(b)
    return _dot(ah, bh) + (_dot(al, bh) + _dot(ah, bl))


def _params(sem):
    return pltpu.CompilerParams(dimension_semantics=sem, vmem_limit_bytes=VMEM_LIMIT)


def _ada_kernel(c_ref, w_ref, b_ref, o_ref):
    c = c_ref[...]
    o_ref[...] = _dot3(c * jax.nn.sigmoid(c), w_ref[...]) + b_ref[...]


def _ada(c, w, b):
    bsz, d = c.shape
    n = w.shape[1]
    tn = ADA_TN
    return pl.pallas_call(
        _ada_kernel,
        grid=(n // tn,),
        in_specs=[pl.BlockSpec((bsz, d), lambda j: (0, 0)),
                  pl.BlockSpec((d, tn), lambda j: (0, j)),
                  pl.BlockSpec((1, tn), lambda j: (0, j))],
        out_specs=pl.BlockSpec((bsz, tn), lambda j: (0, j)),
        out_shape=jax.ShapeDtypeStruct((bsz, n), F32),
        compiler_params=_params(("arbitrary",)),
        name="ada",
    )(c, w, b.reshape(1, n))


def _rms_rows(xf, g):
    ms = jnp.mean(xf * xf, axis=-1, keepdims=True)
    return xf * lax.rsqrt(ms + EPS) * g


def _head_norm(t, bd, gain):
    ms = _dot((t * t).astype(BF16), bd)
    return t * lax.rsqrt(ms + EPS) * gain


def _head_norm_t(t, gain):
    heads = []
    for hd in range(t.shape[0] // HEAD_DIM):
        th = t[hd * HEAD_DIM:(hd + 1) * HEAD_DIM]
        ms = jnp.mean(th * th, axis=0, keepdims=True)
        heads.append(th * lax.rsqrt(ms + EPS) * gain[hd * HEAD_DIM:(hd + 1) * HEAD_DIM])
    return heads


def _inproj_kernel(x_ref, sc_ref, sh_ref, gmix_ref, wr_ref, wt_ref, bd_ref, gqa_ref, gka_ref, gqb_ref,
                   gks_ref, gkw_ref,
                   qa_ref, kaug_ref, va_ref, kmean_ref, qb_ref, kc_ref, vc_ref, ksaug_ref,
                   vs_ref, kw_ref, vw_ref, gates_ref, chunk_ref):
    tm = x_ref.shape[1]
    ti = pl.program_id(1)
    xf = x_ref[0]
    h = _rms_rows(xf, gmix_ref[...]) * (1.0 + sc_ref[0, 0]) + sh_ref[0, 0]
    hb = h.astype(BF16)

    def proj(c0, c1):
        return _dot(hb, wr_ref[:, c0:c1])

    def proj_t(r0, r1):
        return _dot_nt(wt_ref[r0:r1, :], hb)

    bd = bd_ref[...]
    bd2 = bd_ref[0:LANES, 0:LANES]
    lane = lax.broadcasted_iota(jnp.int32, (tm, LANES), 1)
    row = lax.broadcasted_iota(jnp.int32, (tm, LANES), 0) + ti * tm
    low = lane < HEAD_DIM

    def k_in_low(pair, odd):
        return pltpu.roll(pair, HEAD_DIM, 1) if odd else pair

    t_qa, t_va, t_qb = 0, D_A, 2 * D_A
    t_vs, t_vw, t_gl = t_qb + D_B, t_qb + D_B + D_KV, t_qb + D_B + 2 * D_KV
    c_ka, c_kc, c_vc, c_ks, c_kw = 0, D_A, D_A + D_KV, D_A + 2 * D_KV, D_A + 3 * D_KV

    for hd, qh in enumerate(_head_norm_t(proj_t(t_qa, t_va), gqa_ref[...])):
        qa_ref[0, hd] = qh

    def put_tiles(ref, vt):
        for i in range(tm // TQ):
            ref[0, i] = vt[:, i * TQ:(i + 1) * TQ].astype(BF16)

    put_tiles(va_ref, proj_t(t_va, t_qb))
    for hd, qh in enumerate(_head_norm_t(proj_t(t_qb, t_vs), gqb_ref[...])):
        qb_ref[0, hd] = qh
    put_tiles(vs_ref, proj_t(t_vs, t_vw))
    put_tiles(vw_ref, proj_t(t_vw, t_gl))
    gl = jax.nn.sigmoid(proj_t(t_gl, t_gl + G_B * GATE_ROWS))
    for g in range(G_B):
        gates_ref[0, g] = gl[g * GATE_ROWS:(g + 1) * GATE_ROWS]

    ka = _head_norm(proj(c_ka, c_kc), bd, gka_ref[...])
    oh_a = jnp.where(lane - HEAD_DIM == (row >> _LOG2_BLK_A), 1.0, 0.0)
    for hd in range(H_A):
        pair = ka[:, (hd // 2) * LANES:(hd // 2 + 1) * LANES]
        kaug_ref[0, hd] = jnp.where(low, k_in_low(pair, hd % 2), oh_a).astype(BF16)
    for i in range(tm // BLK_A):
        kmean_ref[0, i] = jnp.mean(ka[i * BLK_A:(i + 1) * BLK_A], axis=0, keepdims=True)

    def put_chunks(ref, kv):
        chunk_ref[...] = kv
        rows = [chunk_ref[pl.ds(p, tm // CMP_STRIDE, stride=CMP_STRIDE), :] for p in range(CMP_STRIDE)]
        for g in range(G_B):
            ref[0, g] = jnp.concatenate(
                [r[:, g * HEAD_DIM:(g + 1) * HEAD_DIM] for r in rows], axis=1).astype(BF16)

    put_chunks(kc_ref, proj(c_kc, c_vc))
    put_chunks(vc_ref, proj(c_vc, c_ks))

    ks = _head_norm(proj(c_ks, c_kw), bd2, gks_ref[...])
    kw = _head_norm(proj(c_kw, c_kw + D_KV), bd2, gkw_ref[...])
    oh_s = jnp.where(lane == (row >> _LOG2_SEL_BLK), 1.0, 0.0).astype(BF16)
    for g in range(G_B):
        ksaug_ref[0, g] = jnp.concatenate(
            [jnp.where(low, k_in_low(ks, g), 0.0).astype(BF16), oh_s], axis=1)
        kw_ref[0, g] = jnp.where(low, k_in_low(kw, g), 0.0).astype(BF16)


def _inproj(x, sc, sh, gmix, wr, wt, bd, gqa, gka, gqb, gks, gkw):
    bsz, s, d = x.shape
    tm = TM_IN
    nt = s // tm
    nba = s // BLK_A
    const2 = lambda b, t: (0, 0)
    tok3 = lambda b, t: (b, t, 0)
    tok4 = lambda b, t: (b, 0, t, 0)
    tile4 = lambda b, t: (b, t, 0, 0)
    tr4 = lambda b, t: (b, 0, 0, t)
    in_specs = [
        pl.BlockSpec((1, tm, d), tok3),
        pl.BlockSpec((1, 1, 1, d), lambda b, t: (b, 1, 0, 0)),
        pl.BlockSpec((1, 1, 1, d), lambda b, t: (b, 0, 0, 0)),
        pl.BlockSpec((1, d), const2),
        pl.BlockSpec(wr.shape, const2),
        pl.BlockSpec(wt.shape, const2),
        pl.BlockSpec((D_A, D_A), const2),
        pl.BlockSpec((D_A, tm), const2),
        pl.BlockSpec((1, D_A), const2),
        pl.BlockSpec((D_B, tm), const2),
        pl.BlockSpec((1, LANES), const2),
        pl.BlockSpec((1, LANES), const2),
    ]
    out_shape = [
        jax.ShapeDtypeStruct((bsz, H_A, HEAD_DIM, s), F32),
        jax.ShapeDtypeStruct((bsz, H_A, s, LANES), BF16),
        jax.ShapeDtypeStruct((bsz, s // TQ, D_A, TQ), BF16),
        jax.ShapeDtypeStruct((bsz, nba, 1, D_A), F32),
        jax.ShapeDtypeStruct((bsz, H_B, HEAD_DIM, s), F32),
        jax.ShapeDtypeStruct((bsz, G_B, s // CMP_STRIDE, CMP_STRIDE * HEAD_DIM), BF16),
        jax.ShapeDtypeStruct((bsz, G_B, s // CMP_STRIDE, CMP_STRIDE * HEAD_DIM), BF16),
        jax.ShapeDtypeStruct((bsz, G_B, s, 2 * LANES), BF16),
        jax.ShapeDtypeStruct((bsz, s // TQ, LANES, TQ), BF16),
        jax.ShapeDtypeStruct((bsz, G_B, s, LANES), BF16),
        jax.ShapeDtypeStruct((bsz, s // TQ, LANES, TQ), BF16),
        jax.ShapeDtypeStruct((bsz, G_B, GATE_ROWS, s), F32),
    ]
    out_specs = [
        pl.BlockSpec((1, H_A, HEAD_DIM, tm), tr4),
        pl.BlockSpec((1, H_A, tm, LANES), tok4),
        pl.BlockSpec((1, tm // TQ, D_A, TQ), tile4),
        pl.BlockSpec((1, tm // BLK_A, 1, D_A), lambda b, t: (b, t, 0, 0)),
        pl.BlockSpec((1, H_B, HEAD_DIM, tm), tr4),
        pl.BlockSpec((1, G_B, tm // CMP_STRIDE, CMP_STRIDE * HEAD_DIM), tok4),
        pl.BlockSpec((1, G_B, tm // CMP_STRIDE, CMP_STRIDE * HEAD_DIM), tok4),
        pl.BlockSpec((1, G_B, tm, 2 * LANES), tok4),
        pl.BlockSpec((1, tm // TQ, LANES, TQ), tile4),
        pl.BlockSpec((1, G_B, tm, LANES), tok4),
        pl.BlockSpec((1, tm // TQ, LANES, TQ), tile4),
        pl.BlockSpec((1, G_B, GATE_ROWS, tm), tr4),
    ]
    return pl.pallas_call(
        _inproj_kernel,
        grid=(bsz, nt),
        in_specs=in_specs,
        out_specs=out_specs,
        out_shape=out_shape,
        scratch_shapes=[pltpu.VMEM((tm, LANES), F32)],
        compiler_params=_params(("parallel", "parallel")),
        name="inproj",
    )(x, sc, sh, gmix, wr, wt, bd, gqa, gka, gqb, gks, gkw)


def _compress_kernel(ck_ref, cv_ref, w1_ref, w2t_ref, pe_ref, gk_ref, ok_ref, ov_ref):
    half = CMP_STRIDE * HEAD_DIM
    for kv, c_ref in enumerate((ck_ref, cv_ref)):
        for g in range(G_B):
            c = c_ref[0, g]
            a = _dot(c, w1_ref[kv, 0:half, :])
            b = _dot(c, w1_ref[kv, half:2 * half, :])
            n = a.shape[0]
            b_next = pltpu.roll(b, n - 1, 0)
            pe_term = _dot(pe_ref[kv], w1_ref[kv])[0:1]
            hid = jax.nn.gelu(a + b_next + pe_term).astype(BF16)
            if kv == 0:
                y = _dot_nt(hid, w2t_ref[kv])
                ms = jnp.mean(y * y, axis=1, keepdims=True)
                y = y * lax.rsqrt(ms + EPS) * gk_ref[...]
                ok_ref[0, g] = jnp.concatenate([y, jnp.zeros_like(y)], axis=1).astype(BF16)
            else:
                ov_ref[0, g] = _dot_nt(w2t_ref[kv], hid).astype(BF16)


def _compress(ck, cv, w1, w2t, pe, gk):
    bsz, g, n, width = ck.shape
    blk = pl.BlockSpec((1, g, n, width), lambda b: (b, 0, 0, 0))
    full = lambda a: pl.BlockSpec(a.shape, lambda b: (0,) * a.ndim)
    return pl.pallas_call(
        _compress_kernel,
        grid=(bsz,),
        in_specs=[blk, blk, full(w1), full(w2t), full(pe), full(gk)],
        out_specs=[pl.BlockSpec((1, g, n, LANES), lambda b: (b, 0, 0, 0)),
                   pl.BlockSpec((1, g, HEAD_DIM, n), lambda b: (b, 0, 0, 0))],
        out_shape=[jax.ShapeDtypeStruct((bsz, g, n, LANES), BF16),
                   jax.ShapeDtypeStruct((bsz, g, HEAD_DIM, n), BF16)],
        compiler_params=_params(("parallel",)),
        name="compress",
    )(ck, cv, w1, w2t, pe, gk)


def _with_ones(vt):
    return jnp.concatenate([vt, jnp.ones((BF16_ROWS, vt.shape[1]), BF16)], axis=0)


def _col_max(s):
    while s.shape[0] > 8:
        half = s.shape[0] // 2
        s = jnp.maximum(s[0:half], s[half:2 * half])
    return jnp.max(s, axis=0, keepdims=True)


def _flash_update(carries, chains):
    m_news = []
    for carry, tiles in zip(carries, chains):
        tops = []
        for s, _, bias, top in tiles:
            top = _col_max(s) if top is None else top
            tops.append(top if bias is None else top + bias)
        m_news.append(functools.reduce(jnp.maximum, tops if carry is None else tops + [carry[0]]))
    pvs = [None] * len(chains)
    for t in range(max(len(tiles) for tiles in chains)):
        for c, tiles in enumerate(chains):
            if t < len(tiles):
                s, vt, bias, _ = tiles[t]
                p = jnp.exp2(s - (m_news[c] if bias is None else m_news[c] - bias)).astype(BF16)
                part = _dot(_with_ones(vt), p)
                pvs[c] = part if pvs[c] is None else pvs[c] + part
    outs = []
    for carry, m_new, pv in zip(carries, m_news, pvs):
        outs.append((m_new, pv if carry is None else jnp.exp2(carry[0] - m_new) * carry[1] + pv))
    return outs


def _flash_out(acc):
    return acc[0:HEAD_DIM] / acc[HEAD_DIM:HEAD_DIM + 1]


def _topk_rows(scores, index, k):
    scores = list(scores)
    picked = [jnp.zeros(sc.shape, F32) for sc in scores]
    for _ in range(k):
        mx = [jnp.max(sc, axis=0, keepdims=True) for sc in scores]
        cand = [jnp.where(sc == m, index, jnp.int32(1 << 20)) for sc, m in zip(scores, mx)]
        first = [jnp.min(c, axis=0, keepdims=True) for c in cand]
        hit = [index == f for f in first]
        picked = [jnp.where(h, 1.0, p) for h, p in zip(hit, picked)]
        scores = [jnp.where(h, -jnp.inf, sc) for h, sc in zip(hit, scores)]
    return picked


def _tile_rows(j):
    return pl.ds(pl.multiple_of(j * TQ, TQ), TQ)


def _moba_gate_kernel(q_ref, km_ref, o_ref, *, n_sel, n_rows):
    nh, tg = q_ref.shape[1], q_ref.shape[3]
    t0 = pl.program_id(2) * tg
    shape = (n_rows, tg)
    blk = lax.broadcasted_iota(jnp.int32, shape, 0)
    own = (lax.broadcasted_iota(jnp.int32, shape, 1) + t0) >> _LOG2_BLK_A
    valid = blk < own
    qs, gates = [], []
    for hh in range(nh):
        q = q_ref[0, hh]
        gate = _dot3(km_ref[0, hh, 0:n_rows, :], jnp.concatenate([q, jnp.zeros_like(q)], axis=0))
        qs.append(q)
        gates.append(jnp.where(valid, gate, -jnp.inf))
    unused = jnp.full((HEAD_DIM - n_rows, tg), NEG, F32)
    for hh, picked in enumerate(_topk_rows(gates, blk, n_sel)):
        keep = jnp.where(valid, picked, 0.0) + jnp.where(blk == own, 1.0, 0.0)
        selb = jnp.where(keep > 0.0, 0.0, NEG)
        o_ref[0, hh] = jnp.concatenate([qs[hh], selb, unused], axis=0).astype(BF16)


def _moba_gate(qa, km):
    bsz, nh, _, s = qa.shape
    tg = min(s, GATE_TQ)
    hpb = GATE_HEADS
    n_sel = max(1, min(TOPK_A, s // BLK_A - 1))
    return pl.pallas_call(
        functools.partial(_moba_gate_kernel, n_sel=n_sel, n_rows=-(-(s // BLK_A) // 8) * 8),
        grid=(bsz, nh // hpb, s // tg),
        in_specs=[pl.BlockSpec((1, hpb, HEAD_DIM, tg), lambda b, h, t: (b, h, 0, t)),
                  pl.BlockSpec((1, hpb, HEAD_DIM, LANES), lambda b, h, t: (b, h, 0, 0))],
        out_specs=pl.BlockSpec((1, hpb, LANES, tg), lambda b, h, t: (b, h, 0, t)),
        out_shape=jax.ShapeDtypeStruct((bsz, nh, LANES, s), BF16),
        compiler_params=_params(("parallel", "parallel", "parallel")),
        name="moba_gate",
    )(qa, km)


def _far_tiles(qi):
    n_far = jnp.maximum(qi - 1, 0)
    left = jnp.maximum(n_far - 1, 0)
    off_left = jnp.where((n_far & 1) == 1, 0.0, NEG)
    return n_far >> 1, left, off_left


class _FarLoop:
    def __init__(self, n_pairs, heads, qk_tile, values, cbs, sa_ref, sb_ref, m_ref, acc_ref):
        self.n_pairs, self.heads, self.qk_tile, self.values, self.cbs = n_pairs, heads, qk_tile, values, cbs
        self.sa_ref, self.sb_ref, self.m_ref, self.acc_ref = sa_ref, sb_ref, m_ref, acc_ref
        self.last = jnp.maximum(n_pairs - 1, 0)

    def fetch(self, buf_ref, h, i):
        s_lo = self.qk_tile(h, 2 * i)
        s_hi = self.qk_tile(h, 2 * i + 1)
        buf_ref[h, 0:TQ, :] = s_lo
        buf_ref[h, TQ:2 * TQ, :] = s_hi
        return _col_max(s_lo), _col_max(s_hi)

    def consume(self, buf_ref, h, top, i):
        (m, acc), = _flash_update([(self.m_ref[h], self.acc_ref[h])], [[
            (buf_ref[h, 0:TQ, :], self.values(h, 2 * i), self.cbs[h], top[0]),
            (buf_ref[h, TQ:2 * TQ, :], self.values(h, 2 * i + 1), self.cbs[h], top[1])]])
        self.m_ref[h] = m
        self.acc_ref[h] = acc

    def first(self):
        return tuple(self.fetch(self.sa_ref, h, 0) for h in self.heads)

    def run(self, tops_first):
        def two_pairs(ia, tops_a):
            tops_b, tops_next = [], []
            for h in self.heads:
                tops_b.append(self.fetch(self.sb_ref, h, ia + 1))
                self.consume(self.sa_ref, h, tops_a[h], ia)
            for h in self.heads:
                tops_next.append(self.fetch(self.sa_ref, h, jnp.minimum(ia + 2, self.last)))
                self.consume(self.sb_ref, h, tops_b[h], ia + 1)
            return tuple(tops_next)

        def pairs(n, start, tops):
            for k in range(0, n, 2):
                tops = two_pairs(start + k, tops)
            return tops

        trip = FAR_PAIRS_PER_TRIP
        n_trips = self.n_pairs // trip
        tops = lax.fori_loop(0, n_trips, lambda q, t: pairs(trip, trip * q, t), tops_first)
        done = trip * n_trips
        n = trip // 2
        while n >= 2:
            has = ((self.n_pairs // n) & 1) == 1
            tops = lax.cond(has, functools.partial(pairs, n, done), lambda t: t, tops)
            done = done + jnp.where(has, n, 0)
            n //= 2
        tops_last = tops

        @pl.when((self.n_pairs & 1) == 1)
        def _():
            for h in self.heads:
                self.consume(self.sa_ref, h, tops_last[h], self.last)


def _moba_kernel(q_ref, k_ref, vt_ref, tab_ref, cb_ref, o_ref, sa_ref, sb_ref, m_ref, acc_ref):
    qi = pl.program_id(2)
    jp = jnp.maximum(qi - 1, 0)
    off_p = jnp.where(qi >= 1, 0.0, NEG)
    n_pairs, jl, off_l = _far_tiles(qi)
    heads = range(MOBA_HEADS)
    qaug = [q_ref[0, hh] for hh in heads]
    cbs = [cb_ref[hh][0:1, 0:1] for hh in heads]

    def scores(hh, j):
        return _dot(k_ref[0, hh, _tile_rows(j), :], qaug[hh])

    def values(hh, j):
        return vt_ref[0, j, hh * HEAD_DIM:(hh + 1) * HEAD_DIM, :]

    far = _FarLoop(n_pairs, heads, scores, values, cbs, sa_ref, sb_ref, m_ref, acc_ref)

    s_own = [scores(hh, qi) for hh in heads]
    s_prev = [scores(hh, jp) for hh in heads]
    s_left = [scores(hh, jl) for hh in heads]
    tops_first = far.first()
    chains = [[(s_own[hh] + tab_ref[hh, TQ:2 * TQ, :], values(hh, qi), None, None),
               (s_prev[hh] + tab_ref[hh, 0:TQ, :], values(hh, jp), off_p, None),
               (s_left[hh], values(hh, jl), cbs[hh] + off_l, None)] for hh in heads]
    for hh, (m, acc) in zip(heads, _flash_update([None] * len(chains), chains)):
        m_ref[hh] = m
        acc_ref[hh] = acc
    far.run(tops_first)
    out_t = jnp.concatenate([_flash_out(acc_ref[hh]) for hh in heads], axis=0)
    o_ref[0] = out_t.T.astype(BF16)


def _flash_scratch(n_heads):
    return [pltpu.VMEM((n_heads, 2 * TQ, TQ), F32),
            pltpu.VMEM((n_heads, 2 * TQ, TQ), F32),
            pltpu.VMEM((n_heads, 1, TQ), F32),
            pltpu.VMEM((n_heads, ACC_ROWS, TQ), F32)]


def _moba(qaug, kaug, vat, tab, cb):
    bsz, _, _, s = qaug.shape
    nq = s // TQ
    nh = MOBA_HEADS
    return pl.pallas_call(
        _moba_kernel,
        grid=(bsz, H_A // nh, nq),
        in_specs=[
            pl.BlockSpec((1, nh, LANES, TQ), lambda b, hp, qi: (b, hp, 0, qi)),
            pl.BlockSpec((1, nh, s, LANES), lambda b, hp, qi: (b, hp, 0, 0)),
            pl.BlockSpec((1, nq, nh * HEAD_DIM, TQ), lambda b, hp, qi: (b, 0, hp, 0)),
            pl.BlockSpec((nh, 2 * TQ, TQ), lambda b, hp, qi: (hp, 0, 0)),
            pl.BlockSpec((nh, 8, LANES), lambda b, hp, qi: (hp, 0, 0)),
        ],
        out_specs=pl.BlockSpec((1, TQ, nh * HEAD_DIM), lambda b, hp, qi: (b, qi, hp)),
        out_shape=jax.ShapeDtypeStruct((bsz, s, H_A * HEAD_DIM), BF16),
        scratch_shapes=_flash_scratch(nh),
        compiler_params=_params(("parallel", "parallel", "arbitrary")),
        name="moba",
    )(qaug, kaug, vat, tab, cb)


def _nsa_cmp_kernel(q_ref, kc_ref, vct_ref, ovt_ref, oc_ref, selb_ref, *, n_sel, n_cmp, n_parts):
    qi = pl.program_id(1)
    ncp = kc_ref.shape[2]
    tq = q_ref.shape[3]
    t0 = qi * tq
    zeros = jnp.zeros((HEAD_DIM, tq), BF16)
    qs = [jnp.concatenate([q_ref[0, hd].astype(BF16), zeros], axis=0) for hd in range(H_B)]

    def body(nk, nb):
        n_idx = lax.broadcasted_iota(jnp.int32, (nk, tq), 0)
        t_idx = lax.broadcasted_iota(jnp.int32, (nk, tq), 1) + t0
        mask = (n_idx * CMP_STRIDE + (CMP_LEN - 1) <= t_idx) & (n_idx < n_cmp)
        any_key = t_idx[0:1] >= CMP_LEN - 1
        blk = lax.broadcasted_iota(jnp.int32, (nb, tq), 0)
        cur = (lax.broadcasted_iota(jnp.int32, (nb, tq), 1) + t0) >> _LOG2_SEL_BLK
        ok = blk <= cur
        forced = (blk == 0) | (blk == cur) | (blk == cur - 1)
        ovt = ovt_ref[0:nb, 0:nk]
        scores = []
        for g in range(G_B):
            kc = kc_ref[0, g, 0:nk, :]
            vct = _with_ones(vct_ref[0, g, :, 0:nk])
            psum = jnp.zeros((nk, tq), F32)
            for hd in range(g * R_B, (g + 1) * R_B):
                z = jnp.where(mask, _dot(kc, qs[hd]), NEG)
                e = jnp.exp2(z - _col_max(z))
                acc = _dot(vct, e.astype(BF16))
                rinv = jnp.where(any_key, 1.0 / acc[HEAD_DIM:HEAD_DIM + 1], 0.0)
                oc_ref[0, hd] = acc[0:HEAD_DIM] * rinv
                psum = psum + e * rinv
            ph, pl_ = _split(psum)
            imp_t = _dot(ovt, ph) + _dot(ovt, pl_)
            scores.append(jnp.where(ok & jnp.logical_not(forced), imp_t, -jnp.inf))
        for g, picked in enumerate(_topk_rows(scores, blk, max(n_sel - N_FORCED, 0))):
            selb_ref[0, g, 0:nb, :] = jnp.where(ok & (forced | (picked > 0.0)), 0.0, NEG).astype(BF16)
            if nb < LANES:
                selb_ref[0, g, nb:LANES, :] = jnp.full((LANES - nb, tq), NEG, BF16)

    part = ncp // n_parts
    need = jnp.minimum(((qi + 1) * (tq // CMP_STRIDE) + part - 1) // part, n_parts)
    for v in range(1, n_parts + 1):
        pl.when(need == v)(functools.partial(body, v * part, min(LANES, v * part * CMP_STRIDE // SEL_BLK)))


def _cmp_parts(ncp):
    return 4 if ncp % (4 * LANES) == 0 else 1


def _nsa_cmp(qb, kcmp, vcmpt, ovt, n_cmp):
    bsz, _, _, s = qb.shape
    tq = CMP_TQ if s % CMP_TQ == 0 else TQ
    nq = s // tq
    ncp = kcmp.shape[2]
    n_sel = min(SEL_TOPK, s // SEL_BLK)
    assert n_sel >= N_FORCED
    n_parts = _cmp_parts(ncp)
    return pl.pallas_call(
        functools.partial(_nsa_cmp_kernel, n_sel=n_sel, n_cmp=n_cmp, n_parts=n_parts),
        grid=(bsz, nq),
        in_specs=[
            pl.BlockSpec((1, H_B, HEAD_DIM, tq), lambda b, qi: (b, 0, 0, qi)),
            pl.BlockSpec((1, G_B, ncp, LANES), lambda b, qi: (b, 0, 0, 0)),
            pl.BlockSpec((1, G_B, HEAD_DIM, ncp), lambda b, qi: (b, 0, 0, 0)),
            pl.BlockSpec((LANES, ncp), lambda b, qi: (0, 0)),
        ],
        out_specs=[
            pl.BlockSpec((1, H_B, HEAD_DIM, tq), lambda b, qi: (b, 0, 0, qi)),
            pl.BlockSpec((1, G_B, LANES, tq), lambda b, qi: (b, 0, 0, qi)),
        ],
        out_shape=[jax.ShapeDtypeStruct((bsz, H_B, HEAD_DIM, s), F32),
                   jax.ShapeDtypeStruct((bsz, G_B, LANES, s), BF16)],
        compiler_params=_params(("parallel", "parallel")),
        name="nsa_cmp",
    )(qb, kcmp, vcmpt, ovt)


def _nsa_main_kernel(q_ref, selb_ref, ks_ref, vst_ref, kw_ref, vwt_ref, oc_ref, gates_ref,
                     tabs_ref, tabw_ref, cb_ref, o_ref, sa_ref, sb_ref, m_ref, acc_ref):
    qi = pl.program_id(2)
    selb = selb_ref[0, 0]
    gates = gates_ref[0, 0]
    zeros = jnp.zeros((HEAD_DIM, TQ), BF16)
    j1 = jnp.maximum(qi - 1, 0)
    j2 = jnp.maximum(qi - 2, 0)
    off1 = jnp.where(qi >= 1, 0.0, NEG)
    off2 = jnp.where(qi >= 2, 0.0, NEG)
    n_pairs, jl, off_l = _far_tiles(qi)

    heads = range(R_B)
    qw = [jnp.concatenate([q_ref[0, r].astype(BF16), zeros], axis=0) for r in heads]
    qs = [jnp.concatenate([qw[r], selb], axis=0) for r in heads]
    cbs = [cb_ref[r][0:1, 0:1] for r in heads]

    def s_sel(r, j):
        return _dot(ks_ref[0, 0, _tile_rows(j), :], qs[r])

    def s_win(r, j):
        return _dot(kw_ref[0, 0, _tile_rows(j), :], qw[r])

    far = _FarLoop(n_pairs, heads, s_sel, lambda r, j: vst_ref[0, j], cbs, sa_ref, sb_ref, m_ref, acc_ref)

    sw = [[s_win(r, j) for r in heads] for j in (qi, j1, j2)]
    ss = [[s_sel(r, j) for r in heads] for j in (qi, j1, jl)]
    tops_first = far.first()

    chains = [[(sw[0][r] + tabw_ref[r, 2 * TQ:3 * TQ, :], vwt_ref[0, qi], None, None),
               (sw[1][r] + tabw_ref[r, TQ:2 * TQ, :], vwt_ref[0, j1], off1, None),
               (sw[2][r] + tabw_ref[r, 0:TQ, :], vwt_ref[0, j2], off2, None)] for r in heads]
    chains += [[(ss[0][r] + tabs_ref[r, TQ:2 * TQ, :], vst_ref[0, qi], None, None),
                (ss[1][r] + tabs_ref[r, 0:TQ, :], vst_ref[0, j1], off1, None),
                (ss[2][r], vst_ref[0, jl], cbs[r] + off_l, None)] for r in heads]
    done = _flash_update([None] * len(chains), chains)
    win = [_flash_out(acc) for _, acc in done[:R_B]]
    for r, (m, acc) in zip(heads, done[R_B:]):
        m_ref[r] = m
        acc_ref[r] = acc
    far.run(tops_first)

    outs = [gates[3 * r:3 * r + 1] * oc_ref[0, r]
            + gates[3 * r + 1:3 * r + 2] * _flash_out(acc_ref[r])
            + gates[3 * r + 2:3 * r + 3] * win[r] for r in heads]
    o_ref[0] = jnp.concatenate(outs, axis=0).T.astype(BF16)


def _nsa_main(qb, selb, ksaug, vst, kw, vwt, oc, gates, tabs, tabw, cb):
    bsz, _, _, s = qb.shape
    nq = s // TQ
    return pl.pallas_call(
        _nsa_main_kernel,
        grid=(bsz, G_B, nq),
        in_specs=[
            pl.BlockSpec((1, R_B, HEAD_DIM, TQ), lambda b, g, qi: (b, g, 0, qi)),
            pl.BlockSpec((1, 1, LANES, TQ), lambda b, g, qi: (b, g, 0, qi)),
            pl.BlockSpec((1, 1, s, 2 * LANES), lambda b, g, qi: (b, g, 0, 0)),
            pl.BlockSpec((1, nq, HEAD_DIM, TQ), lambda b, g, qi: (b, 0, g, 0)),
            pl.BlockSpec((1, 1, s, LANES), lambda b, g, qi: (b, g, 0, 0)),
            pl.BlockSpec((1, nq, HEAD_DIM, TQ), lambda b, g, qi: (b, 0, g, 0)),
            pl.BlockSpec((1, R_B, HEAD_DIM, TQ), lambda b, g, qi: (b, g, 0, qi)),
            pl.BlockSpec((1, 1, GATE_ROWS, TQ), lambda b, g, qi: (b, g, 0, qi)),
            pl.BlockSpec((R_B, 2 * TQ, TQ), lambda b, g, qi: (g, 0, 0)),
            pl.BlockSpec((R_B, 3 * TQ, TQ), lambda b, g, qi: (g, 0, 0)),
            pl.BlockSpec((R_B, 8, LANES), lambda b, g, qi: (g, 0, 0)),
        ],
        out_specs=pl.BlockSpec((1, TQ, R_B * HEAD_DIM), lambda b, g, qi: (b, qi, g)),
        out_shape=jax.ShapeDtypeStruct((bsz, s, H_B * HEAD_DIM), BF16),
        scratch_shapes=_flash_scratch(R_B),
        compiler_params=_params(("parallel", "parallel", "arbitrary")),
        name="nsa_main",
    )(qb, selb, ksaug, vst, kw, vwt, oc, gates, tabs, tabw, cb)


def _out_ffn_kernel(x_ref, oa_ref, ob_ref, wo_ref, gtm_ref, gffn_ref, scf_ref, shf_ref, gtf_ref,
                    wg_ref, wu_ref, wd_ref, o_ref, x1_ref, h_ref, acc_ref):
    f = pl.program_id(2)
    half = oa_ref.shape[2]

    @pl.when(f == 0)
    def _():
        mix = _dot(oa_ref[0], wo_ref[0:half, :]) + _dot(ob_ref[0], wo_ref[half:2 * half, :])
        x1 = x_ref[0] + gtm_ref[0, 0] * mix
        x1_ref[...] = x1
        h = _rms_rows(x1, gffn_ref[...]) * (1.0 + scf_ref[0, 0]) + shf_ref[0, 0]
        h_ref[...] = h.astype(BF16)

    h = h_ref[...]
    gate = _dot(h, wg_ref[...])
    up = _dot(h, wu_ref[...])
    act = (gate * jax.nn.sigmoid(gate) * up).astype(BF16)
    part = _dot(act, wd_ref[...])

    @pl.when(f == 0)
    def _():
        acc_ref[...] = part

    @pl.when(f > 0)
    def _():
        acc_ref[...] += part

    @pl.when(f == pl.num_programs(2) - 1)
    def _():
        o_ref[0] = x1_ref[...] + gtf_ref[0, 0] * acc_ref[...]


def _out_ffn(x, oa, ob, wo, mod4, gffn, wgu, wd):
    bsz, s, d = x.shape
    fh = wd.shape[0]
    tm = TM_FFN
    tf = fh // 2 if (fh // 2) % LANES == 0 else fh
    nf = fh // tf
    tok = lambda b, t, f: (b, t, 0)
    modspec = lambda k: pl.BlockSpec((1, 1, 1, d), lambda b, t, f: (b, k, 0, 0))
    return pl.pallas_call(
        _out_ffn_kernel,
        grid=(bsz, s // tm, nf),
        in_specs=[
            pl.BlockSpec((1, tm, d), tok),
            pl.BlockSpec((1, tm, oa.shape[2]), tok),
            pl.BlockSpec((1, tm, ob.shape[2]), tok),
            pl.BlockSpec((d, d), lambda b, t, f: (0, 0)),
            modspec(2),
            pl.BlockSpec((1, d), lambda b, t, f: (0, 0)),
            modspec(4),
            modspec(3),
            modspec(5),
            pl.BlockSpec((d, tf), lambda b, t, f: (0, f)),
            pl.BlockSpec((d, tf), lambda b, t, f: (0, f + nf)),
            pl.BlockSpec((tf, d), lambda b, t, f: (f, 0)),
        ],
        out_specs=pl.BlockSpec((1, tm, d), tok),
        out_shape=jax.ShapeDtypeStruct((bsz, s, d), F32),
        scratch_shapes=[pltpu.VMEM((tm, d), F32), pltpu.VMEM((tm, d), BF16), pltpu.VMEM((tm, d), F32)],
        compiler_params=_params(("parallel", "parallel", "arbitrary")),
        name="out_ffn",
    )(x, oa, ob, wo, mod4, gffn, mod4, mod4, mod4, wgu, wgu, wd)


def _t5_bucket_np(d):
    max_exact = N_BUCKETS // 2
    d = np.maximum(d, 0)
    df = np.maximum(d, 1).astype(np.float64)
    large = max_exact + (np.log(df / max_exact) / math.log(MAX_DIST / max_exact)
                         * (N_BUCKETS - max_exact)).astype(np.int64)
    large = np.minimum(large, N_BUCKETS - 1)
    return np.where(d < max_exact, d, large).astype(np.int32)


def _bias_expand_kernel(tab_ref, bucket_ref, o_ref):
    hd = pl.program_id(0)
    bucket = bucket_ref[...]
    acc = jnp.full(bucket.shape, NEG, F32)
    for b in range(N_BUCKETS):
        acc = jnp.where(bucket == b, tab_ref[hd, b], acc)
    o_ref[0] = acc


def _bias_expand(tab, bucket):
    nh = tab.shape[0]
    return pl.pallas_call(
        _bias_expand_kernel,
        grid=(nh,),
        in_specs=[pl.BlockSpec(memory_space=pltpu.SMEM),
                  pl.BlockSpec(bucket.shape, lambda h: (0, 0))],
        out_specs=pl.BlockSpec((1,) + bucket.shape, lambda h: (h, 0, 0)),
        out_shape=jax.ShapeDtypeStruct((nh,) + bucket.shape, F32),
        compiler_params=_params(("parallel",)),
        name="bias_expand",
    )(tab, jnp.asarray(bucket, jnp.int32))


def _bias_tables(rel_bias):
    tab = rel_bias.T.astype(F32) * LOG2_E
    i = np.arange(TQ)[None, :]
    d_near = i + TQ - np.arange(2 * TQ)[:, None]
    near = _bias_expand(tab, np.where(d_near >= 0, _t5_bucket_np(d_near), -1))
    d_win = i + 2 * TQ - np.arange(3 * TQ)[:, None]
    ok_win = (d_win >= 0) & (d_win < WINDOW)
    win = _bias_expand(tab[H_A:], np.where(ok_win, _t5_bucket_np(d_win), -1))
    far = jnp.broadcast_to(tab[:, N_BUCKETS - 1][:, None, None], (tab.shape[0], 8, LANES))
    return near, win, far


def _overlap_t(n_cmp_pad, n_cmp):
    cs = np.arange(n_cmp_pad)[None, :] * CMP_STRIDE
    ss = np.arange(LANES)[:, None] * SEL_BLK
    ov = (cs < ss + SEL_BLK) & (cs + CMP_LEN > ss) & (np.arange(n_cmp_pad)[None, :] < n_cmp)
    return jnp.asarray(ov.astype(np.float32), BF16)


def _block_diag(n):
    m = (np.arange(n)[:, None] // HEAD_DIM == np.arange(n)[None, :] // HEAD_DIM)
    return jnp.asarray(m.astype(np.float32) / HEAD_DIM, BF16)


def kernel(x, c, rel_bias, w_ada, b_ada, g_mix, w_in, q_norm_a, k_norm_a, q_norm_b, k_norm_cmp,
           k_norm_sel, k_norm_win, cmp_pe_k, cmp_w1_k, cmp_w2_k, cmp_pe_v, cmp_w1_v, cmp_w2_v,
           w_out, g_ffn, w_gu, w_down):
    bsz, s, d = x.shape
    depth = w_ada.shape[0]
    assert s % TM_IN == 0 and s % TM_FFN == 0 and s % (2 * TQ) == 0
    assert s // BLK_A <= HEAD_DIM and s // SEL_BLK <= LANES
    assert WINDOW == 2 * TQ and BLK_A == TQ and MAX_DIST <= TQ
    n_chunks = s // CMP_STRIDE
    n_cmp = (s - CMP_LEN) // CMP_STRIDE + 1
    scale = HEAD_DIM ** -0.5 * LOG2_E
    hd = HEAD_DIM

    near, win, far = _bias_tables(rel_bias)
    ovt = _overlap_t(n_chunks, n_cmp)
    bd = _block_diag(D_A)
    tile = lambda g, n: jnp.tile(g.astype(F32), n).reshape(1, -1)
    tile_t = lambda g, n: jnp.broadcast_to(jnp.tile(g.astype(F32), n)[:, None], (n * hd, TM_IN))

    for l in range(depth):
        mod = _ada(c, w_ada[l], b_ada[l])
        mod4 = mod.reshape(bsz, ADA_CHUNKS, 1, d)

        wl = w_in[l]
        cols = np.cumsum([0, H_A * hd, H_A * hd, H_A * hd, H_B * hd] + [G_B * hd] * 6)
        qa_c, ka_c, va_c, qb_c, kc_c, vc_c, ks_c, vs_c, kw_c, vw_c = [
            wl[:, int(a):int(b)] for a, b in zip(cols[:-1], cols[1:])]
        gl = wl[:, int(cols[-1]):].reshape(d, G_B, 3 * R_B)
        gl = jnp.pad(gl, ((0, 0), (0, 0), (0, GATE_ROWS - 3 * R_B))).reshape(d, G_B * GATE_ROWS)
        w_rows = jnp.concatenate([ka_c, kc_c, vc_c, ks_c, kw_c], axis=1).astype(BF16)
        w_t = jnp.concatenate([qa_c, va_c, qb_c, vs_c, vw_c, gl], axis=1).T.astype(BF16)

        (qa, kaug, vat, kmean, qb, kc, vc, ksaug, vst, kw, vwt, gates) = _inproj(
            x, mod4, mod4, g_mix[l].reshape(1, d), w_rows, w_t, bd,
            tile_t(q_norm_a[l], H_A) * scale, tile(k_norm_a[l], H_A), tile_t(q_norm_b[l], H_B) * scale,
            tile(k_norm_sel[l], G_B), tile(k_norm_win[l], G_B))

        nba = s // BLK_A
        km = kmean.reshape(bsz, nba, H_A, hd).transpose(0, 2, 1, 3)
        km = jnp.pad(km, ((0, 0), (0, 0), (0, hd - nba), (0, LANES - hd)))
        o_a = _moba(_moba_gate(qa, km), kaug, vat, near[:H_A], far[:H_A])

        w1 = jnp.stack([cmp_w1_k[l], cmp_w1_v[l]]).astype(BF16)
        w2t = jnp.stack([cmp_w2_k[l].T, cmp_w2_v[l].T]).astype(BF16)
        pe = jnp.stack([cmp_pe_k[l], cmp_pe_v[l]]).reshape(2, 1, CMP_LEN * hd)
        pe = jnp.broadcast_to(pe, (2, 8, CMP_LEN * hd)).astype(BF16)
        kcmp, vcmpt = _compress(kc, vc, w1, w2t, pe,
                                k_norm_cmp[l].astype(F32).reshape(1, hd))

        oc, selb = _nsa_cmp(qb, kcmp, vcmpt, ovt, n_cmp)
        o_b = _nsa_main(qb, selb, ksaug, vst, kw, vwt, oc, gates, near[H_A:], win, far[H_A:])

        x = _out_ffn(x, o_a, o_b, w_out[l].astype(BF16), mod4, g_ffn[l].reshape(1, d),
                     w_gu[l].astype(BF16), w_down[l].astype(BF16))
    return x
```

```python
import functools
import math

import jax
import jax.numpy as jnp
import numpy as np
from jax import lax
from jax.experimental import pallas as pl
from jax.experimental.pallas import tpu as pltpu

F32 = jnp.float32
BF16 = jnp.bfloat16

HEAD_DIM = 64
LANES = 128
BF16_ROWS = 16
H_A = 8
H_B = 8
G_B = 2
R_B = H_B // G_B
D_A = H_A * HEAD_DIM
D_B = H_B * HEAD_DIM
D_KV = G_B * HEAD_DIM
BLK_A = 256
TOPK_A = 3
CMP_LEN = 32
CMP_STRIDE = 16
CMP_HIDDEN = 256
SEL_BLK = 64
SEL_TOPK = 16
WINDOW = 512
N_BUCKETS = 32
MAX_DIST = 128
ADA_CHUNKS = 6
NEG = -1e30
N_FORCED = 3
EPS = 1e-6
LOG2_E = math.log2(math.e)

_LOG2_BLK_A = BLK_A.bit_length() - 1
_LOG2_SEL_BLK = SEL_BLK.bit_length() - 1

TQ = 256
TM_IN = 1024
TM_FFN = 512
ADA_TN = 512
GATE_TQ = 2048
GATE_HEADS = 4
CMP_TQ = 512
FAR_PAIRS_PER_TRIP = 4
MOBA_HEADS = 4
GATE_ROWS = 16
ACC_ROWS = HEAD_DIM + BF16_ROWS
V7X_VMEM_BYTES = 64 * 1024 * 1024
VMEM_LIMIT = V7X_VMEM_BYTES * 7 // 8


def _dot(a, b):
    return jnp.dot(a, b, preferred_element_type=F32)


def _dot_nt(a, b):
    return lax.dot_general(a, b, (((1,), (1,)), ((), ())), preferred_element_type=F32)


def _split(a):
    hi = a.astype(BF16)
    lo = (a - hi.astype(F32)).astype(BF16)
    return hi, lo


def _dot3(a, b):
    ah, al = _split(a)
    bh, bl = _split(b)
    return _dot(ah, bh) + (_dot(al, bh) + _dot(ah, bl))


def _params(sem):
    return pltpu.CompilerParams(dimension_semantics=sem, vmem_limit_bytes=VMEM_LIMIT)


def _ada_kernel(c_ref, w_ref, b_ref, o_ref):
    c = c_ref[...]
    o_ref[...] = _dot3(c * jax.nn.sigmoid(c), w_ref[...]) + b_ref[...]


def _ada(c, w, b):
    bsz, d = c.shape
    n = w.shape[1]
    tn = ADA_TN
    return pl.pallas_call(
        _ada_kernel,
        grid=(n // tn,),
        in_specs=[pl.BlockSpec((bsz, d), lambda j: (0, 0)),
                  pl.BlockSpec((d, tn), lambda j: (0, j)),
                  pl.BlockSpec((1, tn), lambda j: (0, j))],
        out_specs=pl.BlockSpec((bsz, tn), lambda j: (0, j)),
        out_shape=jax.ShapeDtypeStruct((bsz, n), F32),
        compiler_params=_params(("arbitrary",)),
        name="ada",
    )(c, w, b.reshape(1, n))


def _rms_rows(xf, g):
    ms = jnp.mean(xf * xf, axis=-1, keepdims=True)
    return xf * lax.rsqrt(ms + EPS) * g


def _head_norm(t, bd, gain):
    ms = _dot((t * t).astype(BF16), bd)
    return t * lax.rsqrt(ms + EPS) * gain


def _head_norm_t(t, gain):
    heads = []
    for hd in range(t.shape[0] // HEAD_DIM):
        th = t[hd * HEAD_DIM:(hd + 1) * HEAD_DIM]
        ms = jnp.mean(th * th, axis=0, keepdims=True)
        heads.append(th * lax.rsqrt(ms + EPS) * gain[hd * HEAD_DIM:(hd + 1) * HEAD_DIM])
    return heads


def _inproj_kernel(x_ref, sc_ref, sh_ref, gmix_ref, wr_ref, wt_ref, bd_ref, gqa_ref, gka_ref, gqb_ref,
                   gks_ref, gkw_ref,
                   qa_ref, kaug_ref, va_ref, kmean_ref, qb_ref, kc_ref, vc_ref, ksaug_ref,
                   vs_ref, kw_ref, vw_ref, gates_ref, chunk_ref):
    tm = x_ref.shape[1]
    ti = pl.program_id(1)
    xf = x_ref[0]
    h = _rms_rows(xf, gmix_ref[...]) * (1.0 + sc_ref[0, 0]) + sh_ref[0, 0]
    hb = h.astype(BF16)

    def proj(c0, c1):
        return _dot(hb, wr_ref[:, c0:c1])

    def proj_t(r0, r1):
        return _dot_nt(wt_ref[r0:r1, :], hb)

    bd = bd_ref[...]
    bd2 = bd_ref[0:LANES, 0:LANES]
    lane = lax.broadcasted_iota(jnp.int32, (tm, LANES), 1)
    row = lax.broadcasted_iota(jnp.int32, (tm, LANES), 0) + ti * tm
    low = lane < HEAD_DIM

    def k_in_low(pair, odd):
        return pltpu.roll(pair, HEAD_DIM, 1) if odd else pair

    t_qa, t_va, t_qb = 0, D_A, 2 * D_A
    t_vs, t_vw, t_gl = t_qb + D_B, t_qb + D_B + D_KV, t_qb + D_B + 2 * D_KV
    c_ka, c_kc, c_vc, c_ks, c_kw = 0, D_A, D_A + D_KV, D_A + 2 * D_KV, D_A + 3 * D_KV

    for hd, qh in enumerate(_head_norm_t(proj_t(t_qa, t_va), gqa_ref[...])):
        qa_ref[0, hd] = qh

    def put_tiles(ref, vt):
        for i in range(tm // TQ):
            ref[0, i] = vt[:, i * TQ:(i + 1) * TQ].astype(BF16)

    put_tiles(va_ref, proj_t(t_va, t_qb))
    for hd, qh in enumerate(_head_norm_t(proj_t(t_qb, t_vs), gqb_ref[...])):
        qb_ref[0, hd] = qh
    put_tiles(vs_ref, proj_t(t_vs, t_vw))
    put_tiles(vw_ref, proj_t(t_vw, t_gl))
    gl = jax.nn.sigmoid(proj_t(t_gl, t_gl + G_B * GATE_ROWS))
    for g in range(G_B):
        gates_ref[0, g] = gl[g * GATE_ROWS:(g + 1) * GATE_ROWS]

    ka = _head_norm(proj(c_ka, c_kc), bd, gka_ref[...])
    oh_a = jnp.where(lane - HEAD_DIM == (row >> _LOG2_BLK_A), 1.0, 0.0)
    for hd in range(H_A):
        pair = ka[:, (hd // 2) * LANES:(hd // 2 + 1) * LANES]
        kaug_ref[0, hd] = jnp.where(low, k_in_low(pair, hd % 2), oh_a).astype(BF16)
    for i in range(tm // BLK_A):
        kmean_ref[0, i] = jnp.mean(ka[i * BLK_A:(i + 1) * BLK_A], axis=0, keepdims=True)

    def put_chunks(ref, kv):
        chunk_ref[...] = kv
        rows = [chunk_ref[pl.ds(p, tm // CMP_STRIDE, stride=CMP_STRIDE), :] for p in range(CMP_STRIDE)]
        for g in range(G_B):
            ref[0, g] = jnp.concatenate(
                [r[:, g * HEAD_DIM:(g + 1) * HEAD_DIM] for r in rows], axis=1).astype(BF16)

    put_chunks(kc_ref, proj(c_kc, c_vc))
    put_chunks(vc_ref, proj(c_vc, c_ks))

    ks = _head_norm(proj(c_ks, c_kw), bd2, gks_ref[...])
    kw = _head_norm(proj(c_kw, c_kw + D_KV), bd2, gkw_ref[...])
    oh_s = jnp.where(lane == (row >> _LOG2_SEL_BLK), 1.0, 0.0).astype(BF16)
    for g in range(G_B):
        ksaug_ref[0, g] = jnp.concatenate(
            [jnp.where(low, k_in_low(ks, g), 0.0).astype(BF16), oh_s], axis=1)
        kw_ref[0, g] = jnp.where(low, k_in_low(kw, g), 0.0).astype(BF16)


def _inproj(x, sc, sh, gmix, wr, wt, bd, gqa, gka, gqb, gks, gkw):
    bsz, s, d = x.shape
    tm = TM_IN
    nt = s // tm
    nba = s // BLK_A
    const2 = lambda b, t: (0, 0)
    tok3 = lambda b, t: (b, t, 0)
    tok4 = lambda b, t: (b, 0, t, 0)
    tile4 = lambda b, t: (b, t, 0, 0)
    tr4 = lambda b, t: (b, 0, 0, t)
    in_specs = [
        pl.BlockSpec((1, tm, d), tok3),
        pl.BlockSpec((1, 1, 1, d), lambda b, t: (b, 1, 0, 0)),
        pl.BlockSpec((1, 1, 1, d), lambda b, t: (b, 0, 0, 0)),
        pl.BlockSpec((1, d), const2),
        pl.BlockSpec(wr.shape, const2),
        pl.BlockSpec(wt.shape, const2),
        pl.BlockSpec((D_A, D_A), const2),
        pl.BlockSpec((D_A, tm), const2),
        pl.BlockSpec((1, D_A), const2),
        pl.BlockSpec((D_B, tm), const2),
        pl.BlockSpec((1, LANES), const2),
        pl.BlockSpec((1, LANES), const2),
    ]
    out_shape = [
        jax.ShapeDtypeStruct((bsz, H_A, HEAD_DIM, s), F32),
        jax.ShapeDtypeStruct((bsz, H_A, s, LANES), BF16),
        jax.ShapeDtypeStruct((bsz, s // TQ, D_A, TQ), BF16),
        jax.ShapeDtypeStruct((bsz, nba, 1, D_A), F32),
        jax.ShapeDtypeStruct((bsz, H_B, HEAD_DIM, s), F32),
        jax.ShapeDtypeStruct((bsz, G_B, s // CMP_STRIDE, CMP_STRIDE * HEAD_DIM), BF16),
        jax.ShapeDtypeStruct((bsz, G_B, s // CMP_STRIDE, CMP_STRIDE * HEAD_DIM), BF16),
        jax.ShapeDtypeStruct((bsz, G_B, s, 2 * LANES), BF16),
        jax.ShapeDtypeStruct((bsz, s // TQ, LANES, TQ), BF16),
        jax.ShapeDtypeStruct((bsz, G_B, s, LANES), BF16),
        jax.ShapeDtypeStruct((bsz, s // TQ, LANES, TQ), BF16),
        jax.ShapeDtypeStruct((bsz, G_B, GATE_ROWS, s), F32),
    ]
    out_specs = [
        pl.BlockSpec((1, H_A, HEAD_DIM, tm), tr4),
        pl.BlockSpec((1, H_A, tm, LANES), tok4),
        pl.BlockSpec((1, tm // TQ, D_A, TQ), tile4),
        pl.BlockSpec((1, tm // BLK_A, 1, D_A), lambda b, t: (b, t, 0, 0)),
        pl.BlockSpec((1, H_B, HEAD_DIM, tm), tr4),
        pl.BlockSpec((1, G_B, tm // CMP_STRIDE, CMP_STRIDE * HEAD_DIM), tok4),
        pl.BlockSpec((1, G_B, tm // CMP_STRIDE, CMP_STRIDE * HEAD_DIM), tok4),
        pl.BlockSpec((1, G_B, tm, 2 * LANES), tok4),
        pl.BlockSpec((1, tm // TQ, LANES, TQ), tile4),
        pl.BlockSpec((1, G_B, tm, LANES), tok4),
        pl.BlockSpec((1, tm // TQ, LANES, TQ), tile4),
        pl.BlockSpec((1, G_B, GATE_ROWS, tm), tr4),
    ]
    return pl.pallas_call(
        _inproj_kernel,
        grid=(bsz, nt),
        in_specs=in_specs,
        out_specs=out_specs,
        out_shape=out_shape,
        scratch_shapes=[pltpu.VMEM((tm, LANES), F32)],
        compiler_params=_params(("parallel", "parallel")),
        name="inproj",
    )(x, sc, sh, gmix, wr, wt, bd, gqa, gka, gqb, gks, gkw)


def _compress_kernel(ck_ref, cv_ref, w1_ref, w2t_ref, pe_ref, gk_ref, ok_ref, ov_ref):
    half = CMP_STRIDE * HEAD_DIM
    for kv, c_ref in enumerate((ck_ref, cv_ref)):
        for g in range(G_B):
            c = c_ref[0, g]
            a = _dot(c, w1_ref[kv, 0:half, :])
            b = _dot(c, w1_ref[kv, half:2 * half, :])
            n = a.shape[0]
            b_next = pltpu.roll(b, n - 1, 0)
            pe_term = _dot(pe_ref[kv], w1_ref[kv])[0:1]
            hid = jax.nn.gelu(a + b_next + pe_term).astype(BF16)
            if kv == 0:
                y = _dot_nt(hid, w2t_ref[kv])
                ms = jnp.mean(y * y, axis=1, keepdims=True)
                y = y * lax.rsqrt(ms + EPS) * gk_ref[...]
                ok_ref[0, g] = jnp.concatenate([y, jnp.zeros_like(y)], axis=1).astype(BF16)
            else:
                ov_ref[0, g] = _dot_nt(w2t_ref[kv], hid).astype(BF16)


def _compress(ck, cv, w1, w2t, pe, gk):
    bsz, g, n, width = ck.shape
    blk = pl.BlockSpec((1, g, n, width), lambda b: (b, 0, 0, 0))
    full = lambda a: pl.BlockSpec(a.shape, lambda b: (0,) * a.ndim)
    return pl.pallas_call(
        _compress_kernel,
        grid=(bsz,),
        in_specs=[blk, blk, full(w1), full(w2t), full(pe), full(gk)],
        out_specs=[pl.BlockSpec((1, g, n, LANES), lambda b: (b, 0, 0, 0)),
                   pl.BlockSpec((1, g, HEAD_DIM, n), lambda b: (b, 0, 0, 0))],
        out_shape=[jax.ShapeDtypeStruct((bsz, g, n, LANES), BF16),
                   jax.ShapeDtypeStruct((bsz, g, HEAD_DIM, n), BF16)],
        compiler_params=_params(("parallel",)),
        name="compress",
    )(ck, cv, w1, w2t, pe, gk)


def _with_ones(vt):
    return jnp.concatenate([vt, jnp.ones((BF16_ROWS, vt.shape[1]), BF16)], axis=0)


def _col_max(s):
    while s.shape[0] > 8:
        half = s.shape[0] // 2
        s = jnp.maximum(s[0:half], s[half:2 * half])
    return jnp.max(s, axis=0, keepdims=True)


def _flash_update(carries, chains):
    m_news = []
    for carry, tiles in zip(carries, chains):
        tops = []
        for s, _, bias, top in tiles:
            top = _col_max(s) if top is None else top
            tops.append(top if bias is None else top + bias)
        m_news.append(functools.reduce(jnp.maximum, tops if carry is None else tops + [carry[0]]))
    pvs = [None] * len(chains)
    for t in range(max(len(tiles) for tiles in chains)):
        for c, tiles in enumerate(chains):
            if t < len(tiles):
                s, vt, bias, _ = tiles[t]
                p = jnp.exp2(s - (m_news[c] if bias is None else m_news[c] - bias)).astype(BF16)
                part = _dot(_with_ones(vt), p)
                pvs[c] = part if pvs[c] is None else pvs[c] + part
    outs = []
    for carry, m_new, pv in zip(carries, m_news, pvs):
        outs.append((m_new, pv if carry is None else jnp.exp2(carry[0] - m_new) * carry[1] + pv))
    return outs


def _flash_out(acc):
    return acc[0:HEAD_DIM] / acc[HEAD_DIM:HEAD_DIM + 1]


def _topk_rows(scores, index, k):
    scores = list(scores)
    picked = [jnp.zeros(sc.shape, F32) for sc in scores]
    for _ in range(k):
        mx = [jnp.max(sc, axis=0, keepdims=True) for sc in scores]
        cand = [jnp.where(sc == m, index, jnp.int32(1 << 20)) for sc, m in zip(scores, mx)]
        first = [jnp.min(c, axis=0, keepdims=True) for c in cand]
        hit = [index == f for f in first]
        picked = [jnp.where(h, 1.0, p) for h, p in zip(hit, picked)]
        scores = [jnp.where(h, -jnp.inf, sc) for h, sc in zip(hit, scores)]
    return picked


def _tile_rows(j):
    return pl.ds(pl.multiple_of(j * TQ, TQ), TQ)


def _moba_gate_kernel(q_ref, km_ref, o_ref, *, n_sel, n_rows):
    nh, tg = q_ref.shape[1], q_ref.shape[3]
    t0 = pl.program_id(2) * tg
    shape = (n_rows, tg)
    blk = lax.broadcasted_iota(jnp.int32, shape, 0)
    own = (lax.broadcasted_iota(jnp.int32, shape, 1) + t0) >> _LOG2_BLK_A
    valid = blk < own
    qs, gates = [], []
    for hh in range(nh):
        q = q_ref[0, hh]
        gate = _dot3(km_ref[0, hh, 0:n_rows, :], jnp.concatenate([q, jnp.zeros_like(q)], axis=0))
        qs.append(q)
        gates.append(jnp.where(valid, gate, -jnp.inf))
    unused = jnp.full((HEAD_DIM - n_rows, tg), NEG, F32)
    for hh, picked in enumerate(_topk_rows(gates, blk, n_sel)):
        keep = jnp.where(valid, picked, 0.0) + jnp.where(blk == own, 1.0, 0.0)
        selb = jnp.where(keep > 0.0, 0.0, NEG)
        o_ref[0, hh] = jnp.concatenate([qs[hh], selb, unused], axis=0).astype(BF16)


def _moba_gate(qa, km):
    bsz, nh, _, s = qa.shape
    tg = min(s, GATE_TQ)
    hpb = GATE_HEADS
    n_sel = max(1, min(TOPK_A, s // BLK_A - 1))
    return pl.pallas_call(
        functools.partial(_moba_gate_kernel, n_sel=n_sel, n_rows=-(-(s // BLK_A) // 8) * 8),
        grid=(bsz, nh // hpb, s // tg),
        in_specs=[pl.BlockSpec((1, hpb, HEAD_DIM, tg), lambda b, h, t: (b, h, 0, t)),
                  pl.BlockSpec((1, hpb, HEAD_DIM, LANES), lambda b, h, t: (b, h, 0, 0))],
        out_specs=pl.BlockSpec((1, hpb, LANES, tg), lambda b, h, t: (b, h, 0, t)),
        out_shape=jax.ShapeDtypeStruct((bsz, nh, LANES, s), BF16),
        compiler_params=_params(("parallel", "parallel", "parallel")),
        name="moba_gate",
    )(qa, km)


def _far_tiles(qi):
    n_far = jnp.maximum(qi - 1, 0)
    left = jnp.maximum(n_far - 1, 0)
    off_left = jnp.where((n_far & 1) == 1, 0.0, NEG)
    return n_far >> 1, left, off_left


class _FarLoop:
    def __init__(self, n_pairs, heads, qk_tile, values, cbs, sa_ref, sb_ref, m_ref, acc_ref):
        self.n_pairs, self.heads, self.qk_tile, self.values, self.cbs = n_pairs, heads, qk_tile, values, cbs
        self.sa_ref, self.sb_ref, self.m_ref, self.acc_ref = sa_ref, sb_ref, m_ref, acc_ref
        self.last = jnp.maximum(n_pairs - 1, 0)

    def fetch(self, buf_ref, h, i):
        s_lo = self.qk_tile(h, 2 * i)
        s_hi = self.qk_tile(h, 2 * i + 1)
        buf_ref[h, 0:TQ, :] = s_lo
        buf_ref[h, TQ:2 * TQ, :] = s_hi
        return _col_max(s_lo), _col_max(s_hi)

    def consume(self, buf_ref, h, top, i):
        (m, acc), = _flash_update([(self.m_ref[h], self.acc_ref[h])], [[
            (buf_ref[h, 0:TQ, :], self.values(h, 2 * i), self.cbs[h], top[0]),
            (buf_ref[h, TQ:2 * TQ, :], self.values(h, 2 * i + 1), self.cbs[h], top[1])]])
        self.m_ref[h] = m
        self.acc_ref[h] = acc

    def first(self):
        return tuple(self.fetch(self.sa_ref, h, 0) for h in self.heads)

    def run(self, tops_first):
        def two_pairs(ia, tops_a):
            tops_b, tops_next = [], []
            for h in self.heads:
                tops_b.append(self.fetch(self.sb_ref, h, ia + 1))
                self.consume(self.sa_ref, h, tops_a[h], ia)
            for h in self.heads:
                tops_next.append(self.fetch(self.sa_ref, h, jnp.minimum(ia + 2, self.last)))
                self.consume(self.sb_ref, h, tops_b[h], ia + 1)
            return tuple(tops_next)

        def pairs(n, start, tops):
            for k in range(0, n, 2):
                tops = two_pairs(start + k, tops)
            return tops

        trip = FAR_PAIRS_PER_TRIP
        n_trips = self.n_pairs // trip
        tops = lax.fori_loop(0, n_trips, lambda q, t: pairs(trip, trip * q, t), tops_first)
        done = trip * n_trips
        n = trip // 2
        while n >= 2:
            has = ((self.n_pairs // n) & 1) == 1
            tops = lax.cond(has, functools.partial(pairs, n, done), lambda t: t, tops)
            done = done + jnp.where(has, n, 0)
            n //= 2
        tops_last = tops

        @pl.when((self.n_pairs & 1) == 1)
        def _():
            for h in self.heads:
                self.consume(self.sa_ref, h, tops_last[h], self.last)


def _moba_kernel(q_ref, k_ref, vt_ref, tab_ref, cb_ref, o_ref, sa_ref, sb_ref, m_ref, acc_ref):
    qi = pl.program_id(2)
    jp = jnp.maximum(qi - 1, 0)
    off_p = jnp.where(qi >= 1, 0.0, NEG)
    n_pairs, jl, off_l = _far_tiles(qi)
    heads = range(MOBA_HEADS)
    qaug = [q_ref[0, hh] for hh in heads]
    cbs = [cb_ref[hh][0:1, 0:1] for hh in heads]

    def scores(hh, j):
        return _dot(k_ref[0, hh, _tile_rows(j), :], qaug[hh])

    def values(hh, j):
        return vt_ref[0, j, hh * HEAD_DIM:(hh + 1) * HEAD_DIM, :]

    far = _FarLoop(n_pairs, heads, scores, values, cbs, sa_ref, sb_ref, m_ref, acc_ref)

    s_own = [scores(hh, qi) for hh in heads]
    s_prev = [scores(hh, jp) for hh in heads]
    s_left = [scores(hh, jl) for hh in heads]
    tops_first = far.first()
    chains = [[(s_own[hh] + tab_ref[hh, TQ:2 * TQ, :], values(hh, qi), None, None),
               (s_prev[hh] + tab_ref[hh, 0:TQ, :], values(hh, jp), off_p, None),
               (s_left[hh], values(hh, jl), cbs[hh] + off_l, None)] for hh in heads]
    for hh, (m, acc) in zip(heads, _flash_update([None] * len(chains), chains)):
        m_ref[hh] = m
        acc_ref[hh] = acc
    far.run(tops_first)
    out_t = jnp.concatenate([_flash_out(acc_ref[hh]) for hh in heads], axis=0)
    o_ref[0] = out_t.T.astype(BF16)


def _flash_scratch(n_heads):
    return [pltpu.VMEM((n_heads, 2 * TQ, TQ), F32),
            pltpu.VMEM((n_heads, 2 * TQ, TQ), F32),
            pltpu.VMEM((n_heads, 1, TQ), F32),
            pltpu.VMEM((n_heads, ACC_ROWS, TQ), F32)]


def _moba(qaug, kaug, vat, tab, cb):
    bsz, _, _, s = qaug.shape
    nq = s // TQ
    nh = MOBA_HEADS
    return pl.pallas_call(
        _moba_kernel,
        grid=(bsz, H_A // nh, nq),
        in_specs=[
            pl.BlockSpec((1, nh, LANES, TQ), lambda b, hp, qi: (b, hp, 0, qi)),
            pl.BlockSpec((1, nh, s, LANES), lambda b, hp, qi: (b, hp, 0, 0)),
            pl.BlockSpec((1, nq, nh * HEAD_DIM, TQ), lambda b, hp, qi: (b, 0, hp, 0)),
            pl.BlockSpec((nh, 2 * TQ, TQ), lambda b, hp, qi: (hp, 0, 0)),
            pl.BlockSpec((nh, 8, LANES), lambda b, hp, qi: (hp, 0, 0)),
        ],
        out_specs=pl.BlockSpec((1, TQ, nh * HEAD_DIM), lambda b, hp, qi: (b, qi, hp)),
        out_shape=jax.ShapeDtypeStruct((bsz, s, H_A * HEAD_DIM), BF16),
        scratch_shapes=_flash_scratch(nh),
        compiler_params=_params(("parallel", "parallel", "arbitrary")),
        name="moba",
    )(qaug, kaug, vat, tab, cb)


def _nsa_cmp_kernel(q_ref, kc_ref, vct_ref, ovt_ref, oc_ref, selb_ref, *, n_sel, n_cmp, n_parts):
    qi = pl.program_id(1)
    ncp = kc_ref.shape[2]
    tq = q_ref.shape[3]
    t0 = qi * tq
    zeros = jnp.zeros((HEAD_DIM, tq), BF16)
    qs = [jnp.concatenate([q_ref[0, hd].astype(BF16), zeros], axis=0) for hd in range(H_B)]

    def body(nk, nb):
        n_idx = lax.broadcasted_iota(jnp.int32, (nk, tq), 0)
        t_idx = lax.broadcasted_iota(jnp.int32, (nk, tq), 1) + t0
        mask = (n_idx * CMP_STRIDE + (CMP_LEN - 1) <= t_idx) & (n_idx < n_cmp)
        any_key = t_idx[0:1] >= CMP_LEN - 1
        blk = lax.broadcasted_iota(jnp.int32, (nb, tq), 0)
        cur = (lax.broadcasted_iota(jnp.int32, (nb, tq), 1) + t0) >> _LOG2_SEL_BLK
        ok = blk <= cur
        forced = (blk == 0) | (blk == cur) | (blk == cur - 1)
        ovt = ovt_ref[0:nb, 0:nk]
        scores = []
        for g in range(G_B):
            kc = kc_ref[0, g, 0:nk, :]
            vct = _with_ones(vct_ref[0, g, :, 0:nk])
            psum = jnp.zeros((nk, tq), F32)
            for hd in range(g * R_B, (g + 1) * R_B):
                z = jnp.where(mask, _dot(kc, qs[hd]), NEG)
                e = jnp.exp2(z - _col_max(z))
                acc = _dot(vct, e.astype(BF16))
                rinv = jnp.where(any_key, 1.0 / acc[HEAD_DIM:HEAD_DIM + 1], 0.0)
                oc_ref[0, hd] = acc[0:HEAD_DIM] * rinv
                psum = psum + e * rinv
            ph, pl_ = _split(psum)
            imp_t = _dot(ovt, ph) + _dot(ovt, pl_)
            scores.append(jnp.where(ok & jnp.logical_not(forced), imp_t, -jnp.inf))
        for g, picked in enumerate(_topk_rows(scores, blk, max(n_sel - N_FORCED, 0))):
            selb_ref[0, g, 0:nb, :] = jnp.where(ok & (forced | (picked > 0.0)), 0.0, NEG).astype(BF16)
            if nb < LANES:
                selb_ref[0, g, nb:LANES, :] = jnp.full((LANES - nb, tq), NEG, BF16)

    part = ncp // n_parts
    need = jnp.minimum(((qi + 1) * (tq // CMP_STRIDE) + part - 1) // part, n_parts)
    for v in range(1, n_parts + 1):
        pl.when(need == v)(functools.partial(body, v * part, min(LANES, v * part * CMP_STRIDE // SEL_BLK)))


def _cmp_parts(ncp):
    return 4 if ncp % (4 * LANES) == 0 else 1


def _nsa_cmp(qb, kcmp, vcmpt, ovt, n_cmp):
    bsz, _, _, s = qb.shape
    tq = CMP_TQ if s % CMP_TQ == 0 else TQ
    nq = s // tq
    ncp = kcmp.shape[2]
    n_sel = min(SEL_TOPK, s // SEL_BLK)
    assert n_sel >= N_FORCED
    n_parts = _cmp_parts(ncp)
    return pl.pallas_call(
        functools.partial(_nsa_cmp_kernel, n_sel=n_sel, n_cmp=n_cmp, n_parts=n_parts),
        grid=(bsz, nq),
        in_specs=[
            pl.BlockSpec((1, H_B, HEAD_DIM, tq), lambda b, qi: (b, 0, 0, qi)),
            pl.BlockSpec((1, G_B, ncp, LANES), lambda b, qi: (b, 0, 0, 0)),
            pl.BlockSpec((1, G_B, HEAD_DIM, ncp), lambda b, qi: (b, 0, 0, 0)),
            pl.BlockSpec((LANES, ncp), lambda b, qi: (0, 0)),
        ],
        out_specs=[
            pl.BlockSpec((1, H_B, HEAD_DIM, tq), lambda b, qi: (b, 0, 0, qi)),
            pl.BlockSpec((1, G_B, LANES, tq), lambda b, qi: (b, 0, 0, qi)),
        ],
        out_shape=[jax.ShapeDtypeStruct((bsz, H_B, HEAD_DIM, s), F32),
                   jax.ShapeDtypeStruct((bsz, G_B, LANES, s), BF16)],
        compiler_params=_params(("parallel", "parallel")),
        name="nsa_cmp",
    )(qb, kcmp, vcmpt, ovt)


def _nsa_main_kernel(q_ref, selb_ref, ks_ref, vst_ref, kw_ref, vwt_ref, oc_ref, gates_ref,
                     tabs_ref, tabw_ref, cb_ref, o_ref, sa_ref, sb_ref, m_ref, acc_ref):
    qi = pl.program_id(2)
    selb = selb_ref[0, 0]
    gates = gates_ref[0, 0]
    zeros = jnp.zeros((HEAD_DIM, TQ), BF16)
    j1 = jnp.maximum(qi - 1, 0)
    j2 = jnp.maximum(qi - 2, 0)
    off1 = jnp.where(qi >= 1, 0.0, NEG)
    off2 = jnp.where(qi >= 2, 0.0, NEG)
    n_pairs, jl, off_l = _far_tiles(qi)

    heads = range(R_B)
    qw = [jnp.concatenate([q_ref[0, r].astype(BF16), zeros], axis=0) for r in heads]
    qs = [jnp.concatenate([qw[r], selb], axis=0) for r in heads]
    cbs = [cb_ref[r][0:1, 0:1] for r in heads]

    def s_sel(r, j):
        return _dot(ks_ref[0, 0, _tile_rows(j), :], qs[r])

    def s_win(r, j):
        return _dot(kw_ref[0, 0, _tile_rows(j), :], qw[r])

    far = _FarLoop(n_pairs, heads, s_sel, lambda r, j: vst_ref[0, j], cbs, sa_ref, sb_ref, m_ref, acc_ref)

    sw = [[s_win(r, j) for r in heads] for j in (qi, j1, j2)]
    ss = [[s_sel(r, j) for r in heads] for j in (qi, j1, jl)]
    tops_first = far.first()

    chains = [[(sw[0][r] + tabw_ref[r, 2 * TQ:3 * TQ, :], vwt_ref[0, qi], None, None),
               (sw[1][r] + tabw_ref[r, TQ:2 * TQ, :], vwt_ref[0, j1], off1, None),
               (sw[2][r] + tabw_ref[r, 0:TQ, :], vwt_ref[0, j2], off2, None)] for r in heads]
    chains += [[(ss[0][r] + tabs_ref[r, TQ:2 * TQ, :], vst_ref[0, qi], None, None),
                (ss[1][r] + tabs_ref[r, 0:TQ, :], vst_ref[0, j1], off1, None),
                (ss[2][r], vst_ref[0, jl], cbs[r] + off_l, None)] for r in heads]
    done = _flash_update([None] * len(chains), chains)
    win = [_flash_out(acc) for _, acc in done[:R_B]]
    for r, (m, acc) in zip(heads, done[R_B:]):
        m_ref[r] = m
        acc_ref[r] = acc
    far.run(tops_first)

    outs = [gates[3 * r:3 * r + 1] * oc_ref[0, r]
            + gates[3 * r + 1:3 * r + 2] * _flash_out(acc_ref[r])
            + gates[3 * r + 2:3 * r + 3] * win[r] for r in heads]
    o_ref[0] = jnp.concatenate(outs, axis=0).T.astype(BF16)


def _nsa_main(qb, selb, ksaug, vst, kw, vwt, oc, gates, tabs, tabw, cb):
    bsz, _, _, s = qb.shape
    nq = s // TQ
    return pl.pallas_call(
        _nsa_main_kernel,
        grid=(bsz, G_B, nq),
        in_specs=[
            pl.BlockSpec((1, R_B, HEAD_DIM, TQ), lambda b, g, qi: (b, g, 0, qi)),
            pl.BlockSpec((1, 1, LANES, TQ), lambda b, g, qi: (b, g, 0, qi)),
            pl.BlockSpec((1, 1, s, 2 * LANES), lambda b, g, qi: (b, g, 0, 0)),
            pl.BlockSpec((1, nq, HEAD_DIM, TQ), lambda b, g, qi: (b, 0, g, 0)),
            pl.BlockSpec((1, 1, s, LANES), lambda b, g, qi: (b, g, 0, 0)),
            pl.BlockSpec((1, nq, HEAD_DIM, TQ), lambda b, g, qi: (b, 0, g, 0)),
            pl.BlockSpec((1, R_B, HEAD_DIM, TQ), lambda b, g, qi: (b, g, 0, qi)),
            pl.BlockSpec((1, 1, GATE_ROWS, TQ), lambda b, g, qi: (b, g, 0, qi)),
            pl.BlockSpec((R_B, 2 * TQ, TQ), lambda b, g, qi: (g, 0, 0)),
            pl.BlockSpec((R_B, 3 * TQ, TQ), lambda b, g, qi: (g, 0, 0)),
            pl.BlockSpec((R_B, 8, LANES), lambda b, g, qi: (g, 0, 0)),
        ],
        out_specs=pl.BlockSpec((1, TQ, R_B * HEAD_DIM), lambda b, g, qi: (b, qi, g)),
        out_shape=jax.ShapeDtypeStruct((bsz, s, H_B * HEAD_DIM), BF16),
        scratch_shapes=_flash_scratch(R_B),
        compiler_params=_params(("parallel", "parallel", "arbitrary")),
        name="nsa_main",
    )(qb, selb, ksaug, vst, kw, vwt, oc, gates, tabs, tabw, cb)


def _out_ffn_kernel(x_ref, oa_ref, ob_ref, wo_ref, gtm_ref, gffn_ref, scf_ref, shf_ref, gtf_ref,
                    wg_ref, wu_ref, wd_ref, o_ref, x1_ref, h_ref, acc_ref):
    f = pl.program_id(2)
    half = oa_ref.shape[2]

    @pl.when(f == 0)
    def _():
        mix = _dot(oa_ref[0], wo_ref[0:half, :]) + _dot(ob_ref[0], wo_ref[half:2 * half, :])
        x1 = x_ref[0] + gtm_ref[0, 0] * mix
        x1_ref[...] = x1
        h = _rms_rows(x1, gffn_ref[...]) * (1.0 + scf_ref[0, 0]) + shf_ref[0, 0]
        h_ref[...] = h.astype(BF16)

    h = h_ref[...]
    gate = _dot(h, wg_ref[...])
    up = _dot(h, wu_ref[...])
    act = (gate * jax.nn.sigmoid(gate) * up).astype(BF16)
    part = _dot(act, wd_ref[...])

    @pl.when(f == 0)
    def _():
        acc_ref[...] = part

    @pl.when(f > 0)
    def _():
        acc_ref[...] += part

    @pl.when(f == pl.num_programs(2) - 1)
    def _():
        o_ref[0] = x1_ref[...] + gtf_ref[0, 0] * acc_ref[...]


def _out_ffn(x, oa, ob, wo, mod4, gffn, wgu, wd):
    bsz, s, d = x.shape
    fh = wd.shape[0]
    tm = TM_FFN
    tf = fh // 2 if (fh // 2) % LANES == 0 else fh
    nf = fh // tf
    tok = lambda b, t, f: (b, t, 0)
    modspec = lambda k: pl.BlockSpec((1, 1, 1, d), lambda b, t, f: (b, k, 0, 0))
    return pl.pallas_call(
        _out_ffn_kernel,
        grid=(bsz, s // tm, nf),
        in_specs=[
            pl.BlockSpec((1, tm, d), tok),
            pl.BlockSpec((1, tm, oa.shape[2]), tok),
            pl.BlockSpec((1, tm, ob.shape[2]), tok),
            pl.BlockSpec((d, d), lambda b, t, f: (0, 0)),
            modspec(2),
            pl.BlockSpec((1, d), lambda b, t, f: (0, 0)),
            modspec(4),
            modspec(3),
            modspec(5),
            pl.BlockSpec((d, tf), lambda b, t, f: (0, f)),
            pl.BlockSpec((d, tf), lambda b, t, f: (0, f + nf)),
            pl.BlockSpec((tf, d), lambda b, t, f: (f, 0)),
        ],
        out_specs=pl.BlockSpec((1, tm, d), tok),
        out_shape=jax.ShapeDtypeStruct((bsz, s, d), F32),
        scratch_shapes=[pltpu.VMEM((tm, d), F32), pltpu.VMEM((tm, d), BF16), pltpu.VMEM((tm, d), F32)],
        compiler_params=_params(("parallel", "parallel", "arbitrary")),
        name="out_ffn",
    )(x, oa, ob, wo, mod4, gffn, mod4, mod4, mod4, wgu, wgu, wd)


def _t5_bucket_np(d):
    max_exact = N_BUCKETS // 2
    d = np.maximum(d, 0)
    df = np.maximum(d, 1).astype(np.float64)
    large = max_exact + (np.log(df / max_exact) / math.log(MAX_DIST / max_exact)
                         * (N_BUCKETS - max_exact)).astype(np.int64)
    large = np.minimum(large, N_BUCKETS - 1)
    return np.where(d < max_exact, d, large).astype(np.int32)


def _bias_expand_kernel(tab_ref, bucket_ref, o_ref):
    hd = pl.program_id(0)
    bucket = bucket_ref[...]
    acc = jnp.full(bucket.shape, NEG, F32)
    for b in range(N_BUCKETS):
        acc = jnp.where(bucket == b, tab_ref[hd, b], acc)
    o_ref[0] = acc


def _bias_expand(tab, bucket):
    nh = tab.shape[0]
    return pl.pallas_call(
        _bias_expand_kernel,
        grid=(nh,),
        in_specs=[pl.BlockSpec(memory_space=pltpu.SMEM),
                  pl.BlockSpec(bucket.shape, lambda h: (0, 0))],
        out_specs=pl.BlockSpec((1,) + bucket.shape, lambda h: (h, 0, 0)),
        out_shape=jax.ShapeDtypeStruct((nh,) + bucket.shape, F32),
        compiler_params=_params(("parallel",)),
        name="bias_expand",
    )(tab, jnp.asarray(bucket, jnp.int32))


def _bias_tables(rel_bias):
    tab = rel_bias.T.astype(F32) * LOG2_E
    i = np.arange(TQ)[None, :]
    d_near = i + TQ - np.arange(2 * TQ)[:, None]
    near = _bias_expand(tab, np.where(d_near >= 0, _t5_bucket_np(d_near), -1))
    d_win = i + 2 * TQ - np.arange(3 * TQ)[:, None]
    ok_win = (d_win >= 0) & (d_win < WINDOW)
    win = _bias_expand(tab[H_A:], np.where(ok_win, _t5_bucket_np(d_win), -1))
    far = jnp.broadcast_to(tab[:, N_BUCKETS - 1][:, None, None], (tab.shape[0], 8, LANES))
    return near, win, far


def _overlap_t(n_cmp_pad, n_cmp):
    cs = np.arange(n_cmp_pad)[None, :] * CMP_STRIDE
    ss = np.arange(LANES)[:, None] * SEL_BLK
    ov = (cs < ss + SEL_BLK) & (cs + CMP_LEN > ss) & (np.arange(n_cmp_pad)[None, :] < n_cmp)
    return jnp.asarray(ov.astype(np.float32), BF16)


def _block_diag(n):
    m = (np.arange(n)[:, None] // HEAD_DIM == np.arange(n)[None, :] // HEAD_DIM)
    return jnp.asarray(m.astype(np.float32) / HEAD_DIM, BF16)


def kernel(x, c, rel_bias, w_ada, b_ada, g_mix, w_in, q_norm_a, k_norm_a, q_norm_b, k_norm_cmp,
           k_norm_sel, k_norm_win, cmp_pe_k, cmp_w1_k, cmp_w2_k, cmp_pe_v, cmp_w1_v, cmp_w2_v,
           w_out, g_ffn, w_gu, w_down):
    bsz, s, d = x.shape
    depth = w_ada.shape[0]
    assert s % TM_IN == 0 and s % TM_FFN == 0 and s % (2 * TQ) == 0
    assert s // BLK_A <= HEAD_DIM and s // SEL_BLK <= LANES
    assert WINDOW == 2 * TQ and BLK_A == TQ and MAX_DIST <= TQ
    n_chunks = s // CMP_STRIDE
    n_cmp = (s - CMP_LEN) // CMP_STRIDE + 1
    scale = HEAD_DIM ** -0.5 * LOG2_E
    hd = HEAD_DIM

    near, win, far = _bias_tables(rel_bias)
    ovt = _overlap_t(n_chunks, n_cmp)
    bd = _block_diag(D_A)
    tile = lambda g, n: jnp.tile(g.astype(F32), n).reshape(1, -1)
    tile_t = lambda g, n: jnp.broadcast_to(jnp.tile(g.astype(F32), n)[:, None], (n * hd, TM_IN))

    for l in range(depth):
        mod = _ada(c, w_ada[l], b_ada[l])
        mod4 = mod.reshape(bsz, ADA_CHUNKS, 1, d)

        wl = w_in[l]
        cols = np.cumsum([0, H_A * hd, H_A * hd, H_A * hd, H_B * hd] + [G_B * hd] * 6)
        qa_c, ka_c, va_c, qb_c, kc_c, vc_c, ks_c, vs_c, kw_c, vw_c = [
            wl[:, int(a):int(b)] for a, b in zip(cols[:-1], cols[1:])]
        gl = wl[:, int(cols[-1]):].reshape(d, G_B, 3 * R_B)
        gl = jnp.pad(gl, ((0, 0), (0, 0), (0, GATE_ROWS - 3 * R_B))).reshape(d, G_B * GATE_ROWS)
        w_rows = jnp.concatenate([ka_c, kc_c, vc_c, ks_c, kw_c], axis=1).astype(BF16)
        w_t = jnp.concatenate([qa_c, va_c, qb_c, vs_c, vw_c, gl], axis=1).T.astype(BF16)

        (qa, kaug, vat, kmean, qb, kc, vc, ksaug, vst, kw, vwt, gates) = _inproj(
            x, mod4, mod4, g_mix[l].reshape(1, d), w_rows, w_t, bd,
            tile_t(q_norm_a[l], H_A) * scale, tile(k_norm_a[l], H_A), tile_t(q_norm_b[l], H_B) * scale,
            tile(k_norm_sel[l], G_B), tile(k_norm_win[l], G_B))

        nba = s // BLK_A
        km = kmean.reshape(bsz, nba, H_A, hd).transpose(0, 2, 1, 3)
        km = jnp.pad(km, ((0, 0), (0, 0), (0, hd - nba), (0, LANES - hd)))
        o_a = _moba(_moba_gate(qa, km), kaug, vat, near[:H_A], far[:H_A])

        w1 = jnp.stack([cmp_w1_k[l], cmp_w1_v[l]]).astype(BF16)
        w2t = jnp.stack([cmp_w2_k[l].T, cmp_w2_v[l].T]).astype(BF16)
        pe = jnp.stack([cmp_pe_k[l], cmp_pe_v[l]]).reshape(2, 1, CMP_LEN * hd)
        pe = jnp.broadcast_to(pe, (2, 8, CMP_LEN * hd)).astype(BF16)
        kcmp, vcmpt = _compress(kc, vc, w1, w2t, pe,
                                k_norm_cmp[l].astype(F32).reshape(1, hd))

        oc, selb = _nsa_cmp(qb, kcmp, vcmpt, ovt, n_cmp)
        o_b = _nsa_main(qb, selb, ksaug, vst, kw, vwt, oc, gates, near[H_A:], win, far[H_A:])

        x = _out_ffn(x, o_a, o_b, w_out[l].astype(BF16), mod4, g_ffn[l].reshape(1, d),
                     w_gu[l].astype(BF16), w_down[l].astype(BF16))
    return x
```

```python
import functools
import math

import jax
import jax.numpy as jnp
import numpy as np
from jax import lax
from jax.experimental import pallas as pl
from jax.experimental.pallas import tpu as pltpu

F32 = jnp.float32
BF16 = jnp.bfloat16

HEAD_DIM = 64
LANES = 128
BF16_ROWS = 16
H_A = 8
H_B = 8
G_B = 2
R_B = H_B // G_B
D_A = H_A * HEAD_DIM
D_B = H_B * HEAD_DIM
D_KV = G_B * HEAD_DIM
BLK_A = 256
TOPK_A = 3
CMP_LEN = 32
CMP_STRIDE = 16
CMP_HIDDEN = 256
SEL_BLK = 64
SEL_TOPK = 16
WINDOW = 512
N_BUCKETS = 32
MAX_DIST = 128
ADA_CHUNKS = 6
NEG = -1e30
N_FORCED = 3
EPS = 1e-6
LOG2_E = math.log2(math.e)

_LOG2_BLK_A = BLK_A.bit_length() - 1
_LOG2_SEL_BLK = SEL_BLK.bit_length() - 1

TQ = 256
TM_IN = 1024
TM_FFN = 512
ADA_TN = 512
GATE_TQ = 2048
GATE_HEADS = 4
CMP_TQ = 512
FAR_PAIRS_PER_TRIP = 8
MOBA_HEADS = 4
GATE_ROWS = 16
ACC_ROWS = HEAD_DIM + BF16_ROWS
V7X_VMEM_BYTES = 64 * 1024 * 1024
VMEM_LIMIT = V7X_VMEM_BYTES * 7 // 8


def _dot(a, b):
    return jnp.dot(a, b, preferred_element_type=F32)


def _dot_nt(a, b):
    return lax.dot_general(a, b, (((1,), (1,)), ((), ())), preferred_element_type=F32)


def _split(a):
    hi = a.astype(BF16)
    lo = (a - hi.astype(F32)).astype(BF16)
    return hi, lo


def _dot3(a, b):
    ah, al = _split(a)
    bh, bl = _split(b)
    return _dot(ah, bh) + (_dot(al, bh) + _dot(ah, bl))


def _params(sem):
    return pltpu.CompilerParams(dimension_semantics=sem, vmem_limit_bytes=VMEM_LIMIT)


def _ada_kernel(c_ref, w_ref, b_ref, o_ref):
    c = c_ref[...]
    o_ref[...] = _dot3(c * jax.nn.sigmoid(c), w_ref[...]) + b_ref[...]


def _ada(c, w, b):
    bsz, d = c.shape
    n = w.shape[1]
    tn = ADA_TN
    return pl.pallas_call(
        _ada_kernel,
        grid=(n // tn,),
        in_specs=[pl.BlockSpec((bsz, d), lambda j: (0, 0)),
                  pl.BlockSpec((d, tn), lambda j: (0, j)),
                  pl.BlockSpec((1, tn), lambda j: (0, j))],
        out_specs=pl.BlockSpec((bsz, tn), lambda j: (0, j)),
        out_shape=jax.ShapeDtypeStruct((bsz, n), F32),
        compiler_params=_params(("arbitrary",)),
        name="ada",
    )(c, w, b.reshape(1, n))


def _rms_rows(xf, g):
    ms = jnp.mean(xf * xf, axis=-1, keepdims=True)
    return xf * lax.rsqrt(ms + EPS) * g


def _head_norm(t, bd, gain):
    ms = _dot((t * t).astype(BF16), bd)
    return t * lax.rsqrt(ms + EPS) * gain


def _head_norm_t(t, gain):
    heads = []
    for hd in range(t.shape[0] // HEAD_DIM):
        th = t[hd * HEAD_DIM:(hd + 1) * HEAD_DIM]
        ms = jnp.mean(th * th, axis=0, keepdims=True)
        heads.append(th * lax.rsqrt(ms + EPS) * gain[hd * HEAD_DIM:(hd + 1) * HEAD_DIM])
    return heads


def _inproj_kernel(x_ref, sc_ref, sh_ref, gmix_ref, wr_ref, wt_ref, bd_ref, gqa_ref, gka_ref, gqb_ref,
                   gks_ref, gkw_ref,
                   qa_ref, kaug_ref, va_ref, kmean_ref, qb_ref, kc_ref, vc_ref, ksaug_ref,
                   vs_ref, kw_ref, vw_ref, gates_ref, chunk_ref):
    tm = x_ref.shape[1]
    ti = pl.program_id(1)
    xf = x_ref[0]
    h = _rms_rows(xf, gmix_ref[...]) * (1.0 + sc_ref[0, 0]) + sh_ref[0, 0]
    hb = h.astype(BF16)

    def proj(c0, c1):
        return _dot(hb, wr_ref[:, c0:c1])

    def proj_t(r0, r1):
        return _dot_nt(wt_ref[r0:r1, :], hb)

    bd = bd_ref[...]
    bd2 = bd_ref[0:LANES, 0:LANES]
    lane = lax.broadcasted_iota(jnp.int32, (tm, LANES), 1)
    row = lax.broadcasted_iota(jnp.int32, (tm, LANES), 0) + ti * tm
    low = lane < HEAD_DIM

    def k_in_low(pair, odd):
        return pltpu.roll(pair, HEAD_DIM, 1) if odd else pair

    t_qa, t_va, t_qb = 0, D_A, 2 * D_A
    t_vs, t_vw, t_gl = t_qb + D_B, t_qb + D_B + D_KV, t_qb + D_B + 2 * D_KV
    c_ka, c_kc, c_vc, c_ks, c_kw = 0, D_A, D_A + D_KV, D_A + 2 * D_KV, D_A + 3 * D_KV

    for hd, qh in enumerate(_head_norm_t(proj_t(t_qa, t_va), gqa_ref[...])):
        qa_ref[0, hd] = qh

    def put_tiles(ref, vt):
        for i in range(tm // TQ):
            ref[0, i] = vt[:, i * TQ:(i + 1) * TQ].astype(BF16)

    put_tiles(va_ref, proj_t(t_va, t_qb))
    for hd, qh in enumerate(_head_norm_t(proj_t(t_qb, t_vs), gqb_ref[...])):
        qb_ref[0, hd] = qh
    put_tiles(vs_ref, proj_t(t_vs, t_vw))
    put_tiles(vw_ref, proj_t(t_vw, t_gl))
    gl = jax.nn.sigmoid(proj_t(t_gl, t_gl + G_B * GATE_ROWS))
    for g in range(G_B):
        gates_ref[0, g] = gl[g * GATE_ROWS:(g + 1) * GATE_ROWS]

    ka = _head_norm(proj(c_ka, c_kc), bd, gka_ref[...])
    oh_a = jnp.where(lane - HEAD_DIM == (row >> _LOG2_BLK_A), 1.0, 0.0)
    for hd in range(H_A):
        pair = ka[:, (hd // 2) * LANES:(hd // 2 + 1) * LANES]
        kaug_ref[0, hd] = jnp.where(low, k_in_low(pair, hd % 2), oh_a).astype(BF16)
    for i in range(tm // BLK_A):
        kmean_ref[0, i] = jnp.mean(ka[i * BLK_A:(i + 1) * BLK_A], axis=0, keepdims=True)

    def put_chunks(ref, kv):
        chunk_ref[...] = kv
        rows = [chunk_ref[pl.ds(p, tm // CMP_STRIDE, stride=CMP_STRIDE), :] for p in range(CMP_STRIDE)]
        for g in range(G_B):
            ref[0, g] = jnp.concatenate(
                [r[:, g * HEAD_DIM:(g + 1) * HEAD_DIM] for r in rows], axis=1).astype(BF16)

    put_chunks(kc_ref, proj(c_kc, c_vc))
    put_chunks(vc_ref, proj(c_vc, c_ks))

    ks = _head_norm(proj(c_ks, c_kw), bd2, gks_ref[...])
    kw = _head_norm(proj(c_kw, c_kw + D_KV), bd2, gkw_ref[...])
    oh_s = jnp.where(lane == (row >> _LOG2_SEL_BLK), 1.0, 0.0).astype(BF16)
    for g in range(G_B):
        ksaug_ref[0, g] = jnp.concatenate(
            [jnp.where(low, k_in_low(ks, g), 0.0).astype(BF16), oh_s], axis=1)
        kw_ref[0, g] = jnp.where(low, k_in_low(kw, g), 0.0).astype(BF16)


def _inproj(x, sc, sh, gmix, wr, wt, bd, gqa, gka, gqb, gks, gkw):
    bsz, s, d = x.shape
    tm = TM_IN
    nt = s // tm
    nba = s // BLK_A
    const2 = lambda b, t: (0, 0)
    tok3 = lambda b, t: (b, t, 0)
    tok4 = lambda b, t: (b, 0, t, 0)
    tile4 = lambda b, t: (b, t, 0, 0)
    tr4 = lambda b, t: (b, 0, 0, t)
    in_specs = [
        pl.BlockSpec((1, tm, d), tok3),
        pl.BlockSpec((1, 1, 1, d), lambda b, t: (b, 1, 0, 0)),
        pl.BlockSpec((1, 1, 1, d), lambda b, t: (b, 0, 0, 0)),
        pl.BlockSpec((1, d), const2),
        pl.BlockSpec(wr.shape, const2),
        pl.BlockSpec(wt.shape, const2),
        pl.BlockSpec((D_A, D_A), const2),
        pl.BlockSpec((D_A, tm), const2),
        pl.BlockSpec((1, D_A), const2),
        pl.BlockSpec((D_B, tm), const2),
        pl.BlockSpec((1, LANES), const2),
        pl.BlockSpec((1, LANES), const2),
    ]
    out_shape = [
        jax.ShapeDtypeStruct((bsz, H_A, HEAD_DIM, s), F32),
        jax.ShapeDtypeStruct((bsz, H_A, s, LANES), BF16),
        jax.ShapeDtypeStruct((bsz, s // TQ, D_A, TQ), BF16),
        jax.ShapeDtypeStruct((bsz, nba, 1, D_A), F32),
        jax.ShapeDtypeStruct((bsz, H_B, HEAD_DIM, s), F32),
        jax.ShapeDtypeStruct((bsz, G_B, s // CMP_STRIDE, CMP_STRIDE * HEAD_DIM), BF16),
        jax.ShapeDtypeStruct((bsz, G_B, s // CMP_STRIDE, CMP_STRIDE * HEAD_DIM), BF16),
        jax.ShapeDtypeStruct((bsz, G_B, s, 2 * LANES), BF16),
        jax.ShapeDtypeStruct((bsz, s // TQ, LANES, TQ), BF16),
        jax.ShapeDtypeStruct((bsz, G_B, s, LANES), BF16),
        jax.ShapeDtypeStruct((bsz, s // TQ, LANES, TQ), BF16),
        jax.ShapeDtypeStruct((bsz, G_B, GATE_ROWS, s), F32),
    ]
    out_specs = [
        pl.BlockSpec((1, H_A, HEAD_DIM, tm), tr4),
        pl.BlockSpec((1, H_A, tm, LANES), tok4),
        pl.BlockSpec((1, tm // TQ, D_A, TQ), tile4),
        pl.BlockSpec((1, tm // BLK_A, 1, D_A), lambda b, t: (b, t, 0, 0)),
        pl.BlockSpec((1, H_B, HEAD_DIM, tm), tr4),
        pl.BlockSpec((1, G_B, tm // CMP_STRIDE, CMP_STRIDE * HEAD_DIM), tok4),
        pl.BlockSpec((1, G_B, tm // CMP_STRIDE, CMP_STRIDE * HEAD_DIM), tok4),
        pl.BlockSpec((1, G_B, tm, 2 * LANES), tok4),
        pl.BlockSpec((1, tm // TQ, LANES, TQ), tile4),
        pl.BlockSpec((1, G_B, tm, LANES), tok4),
        pl.BlockSpec((1, tm // TQ, LANES, TQ), tile4),
        pl.BlockSpec((1, G_B, GATE_ROWS, tm), tr4),
    ]
    return pl.pallas_call(
        _inproj_kernel,
        grid=(bsz, nt),
        in_specs=in_specs,
        out_specs=out_specs,
        out_shape=out_shape,
        scratch_shapes=[pltpu.VMEM((tm, LANES), F32)],
        compiler_params=_params(("parallel", "parallel")),
        name="inproj",
    )(x, sc, sh, gmix, wr, wt, bd, gqa, gka, gqb, gks, gkw)


def _compress_kernel(ck_ref, cv_ref, w1_ref, w2t_ref, pe_ref, gk_ref, ok_ref, ov_ref):
    half = CMP_STRIDE * HEAD_DIM
    for kv, c_ref in enumerate((ck_ref, cv_ref)):
        for g in range(G_B):
            c = c_ref[0, g]
            a = _dot(c, w1_ref[kv, 0:half, :])
            b = _dot(c, w1_ref[kv, half:2 * half, :])
            n = a.shape[0]
            b_next = pltpu.roll(b, n - 1, 0)
            pe_term = _dot(pe_ref[kv], w1_ref[kv])[0:1]
            hid = jax.nn.gelu(a + b_next + pe_term).astype(BF16)
            if kv == 0:
                y = _dot_nt(hid, w2t_ref[kv])
                ms = jnp.mean(y * y, axis=1, keepdims=True)
                y = y * lax.rsqrt(ms + EPS) * gk_ref[...]
                ok_ref[0, g] = jnp.concatenate([y, jnp.zeros_like(y)], axis=1).astype(BF16)
            else:
                ov_ref[0, g] = _dot_nt(w2t_ref[kv], hid).astype(BF16)


def _compress(ck, cv, w1, w2t, pe, gk):
    bsz, g, n, width = ck.shape
    blk = pl.BlockSpec((1, g, n, width), lambda b: (b, 0, 0, 0))
    full = lambda a: pl.BlockSpec(a.shape, lambda b: (0,) * a.ndim)
    return pl.pallas_call(
        _compress_kernel,
        grid=(bsz,),
        in_specs=[blk, blk, full(w1), full(w2t), full(pe), full(gk)],
        out_specs=[pl.BlockSpec((1, g, n, LANES), lambda b: (b, 0, 0, 0)),
                   pl.BlockSpec((1, g, HEAD_DIM, n), lambda b: (b, 0, 0, 0))],
        out_shape=[jax.ShapeDtypeStruct((bsz, g, n, LANES), BF16),
                   jax.ShapeDtypeStruct((bsz, g, HEAD_DIM, n), BF16)],
        compiler_params=_params(("parallel",)),
        name="compress",
    )(ck, cv, w1, w2t, pe, gk)


def _with_ones(vt):
    return jnp.concatenate([vt, jnp.ones((BF16_ROWS, vt.shape[1]), BF16)], axis=0)


def _col_max(s):
    while s.shape[0] > 8:
        half = s.shape[0] // 2
        s = jnp.maximum(s[0:half], s[half:2 * half])
    return jnp.max(s, axis=0, keepdims=True)


def _flash_update(carries, chains):
    m_news = []
    for carry, tiles in zip(carries, chains):
        tops = []
        for s, _, bias, top in tiles:
            top = _col_max(s) if top is None else top
            tops.append(top if bias is None else top + bias)
        m_news.append(functools.reduce(jnp.maximum, tops if carry is None else tops + [carry[0]]))
    pvs = [None] * len(chains)
    for t in range(max(len(tiles) for tiles in chains)):
        for c, tiles in enumerate(chains):
            if t < len(tiles):
                s, vt, bias, _ = tiles[t]
                p = jnp.exp2(s - (m_news[c] if bias is None else m_news[c] - bias)).astype(BF16)
                part = _dot(_with_ones(vt), p)
                pvs[c] = part if pvs[c] is None else pvs[c] + part
    outs = []
    for carry, m_new, pv in zip(carries, m_news, pvs):
        outs.append((m_new, pv if carry is None else jnp.exp2(carry[0] - m_new) * carry[1] + pv))
    return outs


def _flash_out(acc):
    return acc[0:HEAD_DIM] / acc[HEAD_DIM:HEAD_DIM + 1]


def _topk_rows(scores, index, k):
    scores = list(scores)
    picked = [jnp.zeros(sc.shape, F32) for sc in scores]
    for _ in range(k):
        mx = [jnp.max(sc, axis=0, keepdims=True) for sc in scores]
        cand = [jnp.where(sc == m, index, jnp.int32(1 << 20)) for sc, m in zip(scores, mx)]
        first = [jnp.min(c, axis=0, keepdims=True) for c in cand]
        hit = [index == f for f in first]
        picked = [jnp.where(h, 1.0, p) for h, p in zip(hit, picked)]
        scores = [jnp.where(h, -jnp.inf, sc) for h, sc in zip(hit, scores)]
    return picked


def _tile_rows(j):
    return pl.ds(pl.multiple_of(j * TQ, TQ), TQ)


def _moba_gate_kernel(q_ref, km_ref, o_ref, *, n_sel, n_rows):
    nh, tg = q_ref.shape[1], q_ref.shape[3]
    t0 = pl.program_id(2) * tg
    shape = (n_rows, tg)
    blk = lax.broadcasted_iota(jnp.int32, shape, 0)
    own = (lax.broadcasted_iota(jnp.int32, shape, 1) + t0) >> _LOG2_BLK_A
    valid = blk < own
    qs, gates = [], []
    for hh in range(nh):
        q = q_ref[0, hh]
        gate = _dot3(km_ref[0, hh, 0:n_rows, :], jnp.concatenate([q, jnp.zeros_like(q)], axis=0))
        qs.append(q)
        gates.append(jnp.where(valid, gate, -jnp.inf))
    unused = jnp.full((HEAD_DIM - n_rows, tg), NEG, F32)
    for hh, picked in enumerate(_topk_rows(gates, blk, n_sel)):
        keep = jnp.where(valid, picked, 0.0) + jnp.where(blk == own, 1.0, 0.0)
        selb = jnp.where(keep > 0.0, 0.0, NEG)
        o_ref[0, hh] = jnp.concatenate([qs[hh], selb, unused], axis=0).astype(BF16)


def _moba_gate(qa, km):
    bsz, nh, _, s = qa.shape
    tg = min(s, GATE_TQ)
    hpb = GATE_HEADS
    n_sel = max(1, min(TOPK_A, s // BLK_A - 1))
    return pl.pallas_call(
        functools.partial(_moba_gate_kernel, n_sel=n_sel, n_rows=-(-(s // BLK_A) // 8) * 8),
        grid=(bsz, nh // hpb, s // tg),
        in_specs=[pl.BlockSpec((1, hpb, HEAD_DIM, tg), lambda b, h, t: (b, h, 0, t)),
                  pl.BlockSpec((1, hpb, HEAD_DIM, LANES), lambda b, h, t: (b, h, 0, 0))],
        out_specs=pl.BlockSpec((1, hpb, LANES, tg), lambda b, h, t: (b, h, 0, t)),
        out_shape=jax.ShapeDtypeStruct((bsz, nh, LANES, s), BF16),
        compiler_params=_params(("parallel", "parallel", "parallel")),
        name="moba_gate",
    )(qa, km)


class _NearTiles:
    def __init__(self, sn_ref):
        self.sn_ref, self.count = sn_ref, 0

    def put(self, s):
        slot = self.count
        self.count += 1
        self.sn_ref[slot] = s
        return slot, _col_max(s)

    def tile(self, stashed, vt, bias):
        slot, top = stashed
        return self.sn_ref[slot], vt, bias, top


def _far_tiles(qi):
    n_far = jnp.maximum(qi - 1, 0)
    left = jnp.maximum(n_far - 1, 0)
    off_left = jnp.where((n_far & 1) == 1, 0.0, NEG)
    return n_far >> 1, left, off_left


class _FarLoop:
    def __init__(self, n_pairs, heads, qk_tile, values, cbs, sa_ref, sb_ref, m_ref, acc_ref):
        self.n_pairs, self.heads, self.qk_tile, self.values, self.cbs = n_pairs, heads, qk_tile, values, cbs
        self.sa_ref, self.sb_ref, self.m_ref, self.acc_ref = sa_ref, sb_ref, m_ref, acc_ref
        self.last = jnp.maximum(n_pairs - 1, 0)

    def fetch(self, buf_ref, h, i):
        s_lo = self.qk_tile(h, 2 * i)
        s_hi = self.qk_tile(h, 2 * i + 1)
        buf_ref[h, 0:TQ, :] = s_lo
        buf_ref[h, TQ:2 * TQ, :] = s_hi
        return _col_max(s_lo), _col_max(s_hi)

    def consume(self, buf_ref, h, top, i):
        (m, acc), = _flash_update([(self.m_ref[h], self.acc_ref[h])], [[
            (buf_ref[h, 0:TQ, :], self.values(h, 2 * i), self.cbs[h], top[0]),
            (buf_ref[h, TQ:2 * TQ, :], self.values(h, 2 * i + 1), self.cbs[h], top[1])]])
        self.m_ref[h] = m
        self.acc_ref[h] = acc

    def run(self, tops_first):
        def two_pairs(ia, tops_a):
            tops_b, tops_next = [], []
            for h in self.heads:
                tops_b.append(self.fetch(self.sb_ref, h, ia + 1))
                self.consume(self.sa_ref, h, tops_a[h], ia)
            for h in self.heads:
                tops_next.append(self.fetch(self.sa_ref, h, jnp.minimum(ia + 2, self.last)))
                self.consume(self.sb_ref, h, tops_b[h], ia + 1)
            return tuple(tops_next)

        def pairs(n, start, tops):
            for k in range(0, n, 2):
                tops = two_pairs(start + k, tops)
            return tops

        trip = FAR_PAIRS_PER_TRIP
        n_trips = self.n_pairs // trip
        tops = lax.fori_loop(0, n_trips, lambda q, t: pairs(trip, trip * q, t), tops_first)
        done = trip * n_trips
        n = trip // 2
        while n >= 2:
            has = ((self.n_pairs // n) & 1) == 1
            tops = lax.cond(has, functools.partial(pairs, n, done), lambda t: t, tops)
            done = done + jnp.where(has, n, 0)
            n //= 2
        tops_last = tops

        @pl.when((self.n_pairs & 1) == 1)
        def _():
            for h in self.heads:
                self.consume(self.sa_ref, h, tops_last[h], self.last)


def _moba_kernel(q_ref, k_ref, vt_ref, tab_ref, cb_ref, o_ref, sn_ref, sa_ref, sb_ref, m_ref, acc_ref):
    qi = pl.program_id(2)
    jp = jnp.maximum(qi - 1, 0)
    off_p = jnp.where(qi >= 1, 0.0, NEG)
    n_pairs, jl, off_l = _far_tiles(qi)
    heads = range(MOBA_HEADS)
    qaug = [q_ref[0, hh] for hh in heads]
    cbs = [cb_ref[hh][0:1, 0:1] for hh in heads]

    def scores(hh, j):
        return _dot(k_ref[0, hh, _tile_rows(j), :], qaug[hh])

    def values(hh, j):
        return vt_ref[0, j, hh * HEAD_DIM:(hh + 1) * HEAD_DIM, :]

    far = _FarLoop(n_pairs, heads, scores, values, cbs, sa_ref, sb_ref, m_ref, acc_ref)

    near = _NearTiles(sn_ref)
    stash = [(near.put(scores(hh, qi) + tab_ref[hh, TQ:2 * TQ, :]),
              near.put(scores(hh, jp) + tab_ref[hh, 0:TQ, :]),
              near.put(scores(hh, jl))) for hh in heads]
    tops_first = []
    for hh in heads:
        tops_first.append(far.fetch(sa_ref, hh, 0))
        (m, acc), = _flash_update([None], [[near.tile(stash[hh][0], values(hh, qi), None),
                                            near.tile(stash[hh][1], values(hh, jp), off_p),
                                            near.tile(stash[hh][2], values(hh, jl), cbs[hh] + off_l)]])
        m_ref[hh] = m
        acc_ref[hh] = acc
    far.run(tuple(tops_first))
    out_t = jnp.concatenate([_flash_out(acc_ref[hh]) for hh in heads], axis=0)
    o_ref[0] = out_t.T.astype(BF16)


def _flash_scratch(n_heads):
    return [pltpu.VMEM((n_heads, 2 * TQ, TQ), F32),
            pltpu.VMEM((n_heads, 2 * TQ, TQ), F32),
            pltpu.VMEM((n_heads, 1, TQ), F32),
            pltpu.VMEM((n_heads, ACC_ROWS, TQ), F32)]


def _moba(qaug, kaug, vat, tab, cb):
    bsz, _, _, s = qaug.shape
    nq = s // TQ
    nh = MOBA_HEADS
    return pl.pallas_call(
        _moba_kernel,
        grid=(bsz, H_A // nh, nq),
        in_specs=[
            pl.BlockSpec((1, nh, LANES, TQ), lambda b, hp, qi: (b, hp, 0, qi)),
            pl.BlockSpec((1, nh, s, LANES), lambda b, hp, qi: (b, hp, 0, 0)),
            pl.BlockSpec((1, nq, nh * HEAD_DIM, TQ), lambda b, hp, qi: (b, 0, hp, 0)),
            pl.BlockSpec((nh, 2 * TQ, TQ), lambda b, hp, qi: (hp, 0, 0)),
            pl.BlockSpec((nh, 8, LANES), lambda b, hp, qi: (hp, 0, 0)),
        ],
        out_specs=pl.BlockSpec((1, TQ, nh * HEAD_DIM), lambda b, hp, qi: (b, qi, hp)),
        out_shape=jax.ShapeDtypeStruct((bsz, s, H_A * HEAD_DIM), BF16),
        scratch_shapes=[pltpu.VMEM((3 * nh, TQ, TQ), F32)] + _flash_scratch(nh),
        compiler_params=_params(("parallel", "parallel", "arbitrary")),
        name="moba",
    )(qaug, kaug, vat, tab, cb)


def _nsa_cmp_kernel(q_ref, kc_ref, vct_ref, ovt_ref, oc_ref, selb_ref, *, n_sel, n_cmp, n_parts):
    qi = pl.program_id(1)
    ncp = kc_ref.shape[2]
    tq = q_ref.shape[3]
    t0 = qi * tq
    zeros = jnp.zeros((HEAD_DIM, tq), BF16)
    qs = [jnp.concatenate([q_ref[0, hd].astype(BF16), zeros], axis=0) for hd in range(H_B)]

    def body(nk, nb):
        n_idx = lax.broadcasted_iota(jnp.int32, (nk, tq), 0)
        t_idx = lax.broadcasted_iota(jnp.int32, (nk, tq), 1) + t0
        mask = (n_idx * CMP_STRIDE + (CMP_LEN - 1) <= t_idx) & (n_idx < n_cmp)
        any_key = t_idx[0:1] >= CMP_LEN - 1
        blk = lax.broadcasted_iota(jnp.int32, (nb, tq), 0)
        cur = (lax.broadcasted_iota(jnp.int32, (nb, tq), 1) + t0) >> _LOG2_SEL_BLK
        ok = blk <= cur
        forced = (blk == 0) | (blk == cur) | (blk == cur - 1)
        ovt = ovt_ref[0:nb, 0:nk]
        scores = []
        for g in range(G_B):
            kc = kc_ref[0, g, 0:nk, :]
            vct = _with_ones(vct_ref[0, g, :, 0:nk])
            psum = jnp.zeros((nk, tq), F32)
            for hd in range(g * R_B, (g + 1) * R_B):
                z = jnp.where(mask, _dot(kc, qs[hd]), NEG)
                e = jnp.exp2(z - _col_max(z))
                acc = _dot(vct, e.astype(BF16))
                rinv = jnp.where(any_key, 1.0 / acc[HEAD_DIM:HEAD_DIM + 1], 0.0)
                oc_ref[0, hd] = acc[0:HEAD_DIM] * rinv
                psum = psum + e * rinv
            ph, pl_ = _split(psum)
            imp_t = _dot(ovt, ph) + _dot(ovt, pl_)
            scores.append(jnp.where(ok & jnp.logical_not(forced), imp_t, -jnp.inf))
        for g, picked in enumerate(_topk_rows(scores, blk, max(n_sel - N_FORCED, 0))):
            selb_ref[0, g, 0:nb, :] = jnp.where(ok & (forced | (picked > 0.0)), 0.0, NEG).astype(BF16)
            if nb < LANES:
                selb_ref[0, g, nb:LANES, :] = jnp.full((LANES - nb, tq), NEG, BF16)

    part = ncp // n_parts
    need = jnp.minimum(((qi + 1) * (tq // CMP_STRIDE) + part - 1) // part, n_parts)
    for v in range(1, n_parts + 1):
        pl.when(need == v)(functools.partial(body, v * part, min(LANES, v * part * CMP_STRIDE // SEL_BLK)))


def _cmp_parts(ncp):
    return 4 if ncp % (4 * LANES) == 0 else 1


def _nsa_cmp(qb, kcmp, vcmpt, ovt, n_cmp):
    bsz, _, _, s = qb.shape
    tq = CMP_TQ if s % CMP_TQ == 0 else TQ
    nq = s // tq
    ncp = kcmp.shape[2]
    n_sel = min(SEL_TOPK, s // SEL_BLK)
    assert n_sel >= N_FORCED
    n_parts = _cmp_parts(ncp)
    return pl.pallas_call(
        functools.partial(_nsa_cmp_kernel, n_sel=n_sel, n_cmp=n_cmp, n_parts=n_parts),
        grid=(bsz, nq),
        in_specs=[
            pl.BlockSpec((1, H_B, HEAD_DIM, tq), lambda b, qi: (b, 0, 0, qi)),
            pl.BlockSpec((1, G_B, ncp, LANES), lambda b, qi: (b, 0, 0, 0)),
            pl.BlockSpec((1, G_B, HEAD_DIM, ncp), lambda b, qi: (b, 0, 0, 0)),
            pl.BlockSpec((LANES, ncp), lambda b, qi: (0, 0)),
        ],
        out_specs=[
            pl.BlockSpec((1, H_B, HEAD_DIM, tq), lambda b, qi: (b, 0, 0, qi)),
            pl.BlockSpec((1, G_B, LANES, tq), lambda b, qi: (b, 0, 0, qi)),
        ],
        out_shape=[jax.ShapeDtypeStruct((bsz, H_B, HEAD_DIM, s), F32),
                   jax.ShapeDtypeStruct((bsz, G_B, LANES, s), BF16)],
        compiler_params=_params(("parallel", "parallel")),
        name="nsa_cmp",
    )(qb, kcmp, vcmpt, ovt)


def _nsa_main_kernel(q_ref, selb_ref, ks_ref, vst_ref, kw_ref, vwt_ref, oc_ref, gates_ref,
                     tabs_ref, tabw_ref, cb_ref, o_ref, sn_ref, sa_ref, sb_ref, m_ref, acc_ref):
    qi = pl.program_id(2)
    selb = selb_ref[0, 0]
    gates = gates_ref[0, 0]
    zeros = jnp.zeros((HEAD_DIM, TQ), BF16)
    j1 = jnp.maximum(qi - 1, 0)
    j2 = jnp.maximum(qi - 2, 0)
    off1 = jnp.where(qi >= 1, 0.0, NEG)
    off2 = jnp.where(qi >= 2, 0.0, NEG)
    n_pairs, jl, off_l = _far_tiles(qi)

    heads = range(R_B)
    qw = [jnp.concatenate([q_ref[0, r].astype(BF16), zeros], axis=0) for r in heads]
    qs = [jnp.concatenate([qw[r], selb], axis=0) for r in heads]
    cbs = [cb_ref[r][0:1, 0:1] for r in heads]

    def s_sel(r, j):
        return _dot(ks_ref[0, 0, _tile_rows(j), :], qs[r])

    def s_win(r, j):
        return _dot(kw_ref[0, 0, _tile_rows(j), :], qw[r])

    far = _FarLoop(n_pairs, heads, s_sel, lambda r, j: vst_ref[0, j], cbs, sa_ref, sb_ref, m_ref, acc_ref)

    near = _NearTiles(sn_ref)
    wst = [(near.put(s_win(r, qi) + tabw_ref[r, 2 * TQ:3 * TQ, :]), near.put(s_win(r, j1) + tabw_ref[r, TQ:2 * TQ, :]),
            near.put(s_win(r, j2) + tabw_ref[r, 0:TQ, :])) for r in heads]
    sst = [(near.put(s_sel(r, qi) + tabs_ref[r, TQ:2 * TQ, :]), near.put(s_sel(r, j1) + tabs_ref[r, 0:TQ, :]),
            near.put(s_sel(r, jl))) for r in heads]
    tops_first, win = [], []
    for r in heads:
        tops_first.append(far.fetch(sa_ref, r, 0))
        done = _flash_update([None, None], [
            [near.tile(wst[r][0], vwt_ref[0, qi], None), near.tile(wst[r][1], vwt_ref[0, j1], off1),
             near.tile(wst[r][2], vwt_ref[0, j2], off2)],
            [near.tile(sst[r][0], vst_ref[0, qi], None), near.tile(sst[r][1], vst_ref[0, j1], off1),
             near.tile(sst[r][2], vst_ref[0, jl], cbs[r] + off_l)]])
        win.append(_flash_out(done[0][1]))
        m_ref[r], acc_ref[r] = done[1]
    far.run(tuple(tops_first))

    outs = [gates[3 * r:3 * r + 1] * oc_ref[0, r]
            + gates[3 * r + 1:3 * r + 2] * _flash_out(acc_ref[r])
            + gates[3 * r + 2:3 * r + 3] * win[r] for r in heads]
    o_ref[0] = jnp.concatenate(outs, axis=0).T.astype(BF16)


def _nsa_main(qb, selb, ksaug, vst, kw, vwt, oc, gates, tabs, tabw, cb):
    bsz, _, _, s = qb.shape
    nq = s // TQ
    return pl.pallas_call(
        _nsa_main_kernel,
        grid=(bsz, G_B, nq),
        in_specs=[
            pl.BlockSpec((1, R_B, HEAD_DIM, TQ), lambda b, g, qi: (b, g, 0, qi)),
            pl.BlockSpec((1, 1, LANES, TQ), lambda b, g, qi: (b, g, 0, qi)),
            pl.BlockSpec((1, 1, s, 2 * LANES), lambda b, g, qi: (b, g, 0, 0)),
            pl.BlockSpec((1, nq, HEAD_DIM, TQ), lambda b, g, qi: (b, 0, g, 0)),
            pl.BlockSpec((1, 1, s, LANES), lambda b, g, qi: (b, g, 0, 0)),
            pl.BlockSpec((1, nq, HEAD_DIM, TQ), lambda b, g, qi: (b, 0, g, 0)),
            pl.BlockSpec((1, R_B, HEAD_DIM, TQ), lambda b, g, qi: (b, g, 0, qi)),
            pl.BlockSpec((1, 1, GATE_ROWS, TQ), lambda b, g, qi: (b, g, 0, qi)),
            pl.BlockSpec((R_B, 2 * TQ, TQ), lambda b, g, qi: (g, 0, 0)),
            pl.BlockSpec((R_B, 3 * TQ, TQ), lambda b, g, qi: (g, 0, 0)),
            pl.BlockSpec((R_B, 8, LANES), lambda b, g, qi: (g, 0, 0)),
        ],
        out_specs=pl.BlockSpec((1, TQ, R_B * HEAD_DIM), lambda b, g, qi: (b, qi, g)),
        out_shape=jax.ShapeDtypeStruct((bsz, s, H_B * HEAD_DIM), BF16),
        scratch_shapes=[pltpu.VMEM((6 * R_B, TQ, TQ), F32)] + _flash_scratch(R_B),
        compiler_params=_params(("parallel", "parallel", "arbitrary")),
        name="nsa_main",
    )(qb, selb, ksaug, vst, kw, vwt, oc, gates, tabs, tabw, cb)


def _out_ffn_kernel(x_ref, oa_ref, ob_ref, wo_ref, gtm_ref, gffn_ref, scf_ref, shf_ref, gtf_ref,
                    wg_ref, wu_ref, wd_ref, o_ref, x1_ref, h_ref, acc_ref):
    f = pl.program_id(2)
    half = oa_ref.shape[2]

    @pl.when(f == 0)
    def _():
        mix = _dot(oa_ref[0], wo_ref[0:half, :]) + _dot(ob_ref[0], wo_ref[half:2 * half, :])
        x1 = x_ref[0] + gtm_ref[0, 0] * mix
        x1_ref[...] = x1
        h = _rms_rows(x1, gffn_ref[...]) * (1.0 + scf_ref[0, 0]) + shf_ref[0, 0]
        h_ref[...] = h.astype(BF16)

    h = h_ref[...]
    gate = _dot(h, wg_ref[...])
    up = _dot(h, wu_ref[...])
    act = (gate * jax.nn.sigmoid(gate) * up).astype(BF16)
    part = _dot(act, wd_ref[...])

    @pl.when(f == 0)
    def _():
        acc_ref[...] = part

    @pl.when(f > 0)
    def _():
        acc_ref[...] += part

    @pl.when(f == pl.num_programs(2) - 1)
    def _():
        o_ref[0] = x1_ref[...] + gtf_ref[0, 0] * acc_ref[...]


def _out_ffn(x, oa, ob, wo, mod4, gffn, wgu, wd):
    bsz, s, d = x.shape
    fh = wd.shape[0]
    tm = TM_FFN
    tf = fh // 2 if (fh // 2) % LANES == 0 else fh
    nf = fh // tf
    tok = lambda b, t, f: (b, t, 0)
    modspec = lambda k: pl.BlockSpec((1, 1, 1, d), lambda b, t, f: (b, k, 0, 0))
    return pl.pallas_call(
        _out_ffn_kernel,
        grid=(bsz, s // tm, nf),
        in_specs=[
            pl.BlockSpec((1, tm, d), tok),
            pl.BlockSpec((1, tm, oa.shape[2]), tok),
            pl.BlockSpec((1, tm, ob.shape[2]), tok),
            pl.BlockSpec((d, d), lambda b, t, f: (0, 0)),
            modspec(2),
            pl.BlockSpec((1, d), lambda b, t, f: (0, 0)),
            modspec(4),
            modspec(3),
            modspec(5),
            pl.BlockSpec((d, tf), lambda b, t, f: (0, f)),
            pl.BlockSpec((d, tf), lambda b, t, f: (0, f + nf)),
            pl.BlockSpec((tf, d), lambda b, t, f: (f, 0)),
        ],
        out_specs=pl.BlockSpec((1, tm, d), tok),
        out_shape=jax.ShapeDtypeStruct((bsz, s, d), F32),
        scratch_shapes=[pltpu.VMEM((tm, d), F32), pltpu.VMEM((tm, d), BF16), pltpu.VMEM((tm, d), F32)],
        compiler_params=_params(("parallel", "parallel", "arbitrary")),
        name="out_ffn",
    )(x, oa, ob, wo, mod4, gffn, mod4, mod4, mod4, wgu, wgu, wd)


def _t5_bucket_np(d):
    max_exact = N_BUCKETS // 2
    d = np.maximum(d, 0)
    df = np.maximum(d, 1).astype(np.float64)
    large = max_exact + (np.log(df / max_exact) / math.log(MAX_DIST / max_exact)
                         * (N_BUCKETS - max_exact)).astype(np.int64)
    large = np.minimum(large, N_BUCKETS - 1)
    return np.where(d < max_exact, d, large).astype(np.int32)


def _bias_expand_kernel(tab_ref, bucket_ref, o_ref):
    hd = pl.program_id(0)
    bucket = bucket_ref[...]
    acc = jnp.full(bucket.shape, NEG, F32)
    for b in range(N_BUCKETS):
        acc = jnp.where(bucket == b, tab_ref[hd, b], acc)
    o_ref[0] = acc


def _bias_expand(tab, bucket):
    nh = tab.shape[0]
    return pl.pallas_call(
        _bias_expand_kernel,
        grid=(nh,),
        in_specs=[pl.BlockSpec(memory_space=pltpu.SMEM),
                  pl.BlockSpec(bucket.shape, lambda h: (0, 0))],
        out_specs=pl.BlockSpec((1,) + bucket.shape, lambda h: (h, 0, 0)),
        out_shape=jax.ShapeDtypeStruct((nh,) + bucket.shape, F32),
        compiler_params=_params(("parallel",)),
        name="bias_expand",
    )(tab, jnp.asarray(bucket, jnp.int32))


def _bias_tables(rel_bias):
    tab = rel_bias.T.astype(F32) * LOG2_E
    i = np.arange(TQ)[None, :]
    d_near = i + TQ - np.arange(2 * TQ)[:, None]
    near = _bias_expand(tab, np.where(d_near >= 0, _t5_bucket_np(d_near), -1))
    d_win = i + 2 * TQ - np.arange(3 * TQ)[:, None]
    ok_win = (d_win >= 0) & (d_win < WINDOW)
    win = _bias_expand(tab[H_A:], np.where(ok_win, _t5_bucket_np(d_win), -1))
    far = jnp.broadcast_to(tab[:, N_BUCKETS - 1][:, None, None], (tab.shape[0], 8, LANES))
    return near, win, far


def _overlap_t(n_cmp_pad, n_cmp):
    cs = np.arange(n_cmp_pad)[None, :] * CMP_STRIDE
    ss = np.arange(LANES)[:, None] * SEL_BLK
    ov = (cs < ss + SEL_BLK) & (cs + CMP_LEN > ss) & (np.arange(n_cmp_pad)[None, :] < n_cmp)
    return jnp.asarray(ov.astype(np.float32), BF16)


def _block_diag(n):
    m = (np.arange(n)[:, None] // HEAD_DIM == np.arange(n)[None, :] // HEAD_DIM)
    return jnp.asarray(m.astype(np.float32) / HEAD_DIM, BF16)


def kernel(x, c, rel_bias, w_ada, b_ada, g_mix, w_in, q_norm_a, k_norm_a, q_norm_b, k_norm_cmp,
           k_norm_sel, k_norm_win, cmp_pe_k, cmp_w1_k, cmp_w2_k, cmp_pe_v, cmp_w1_v, cmp_w2_v,
           w_out, g_ffn, w_gu, w_down):
    bsz, s, d = x.shape
    depth = w_ada.shape[0]
    assert s % TM_IN == 0 and s % TM_FFN == 0 and s % (2 * TQ) == 0
    assert s // BLK_A <= HEAD_DIM and s // SEL_BLK <= LANES
    assert WINDOW == 2 * TQ and BLK_A == TQ and MAX_DIST <= TQ
    n_chunks = s // CMP_STRIDE
    n_cmp = (s - CMP_LEN) // CMP_STRIDE + 1
    scale = HEAD_DIM ** -0.5 * LOG2_E
    hd = HEAD_DIM

    near, win, far = _bias_tables(rel_bias)
    ovt = _overlap_t(n_chunks, n_cmp)
    bd = _block_diag(D_A)
    tile = lambda g, n: jnp.tile(g.astype(F32), n).reshape(1, -1)
    tile_t = lambda g, n: jnp.broadcast_to(jnp.tile(g.astype(F32), n)[:, None], (n * hd, TM_IN))

    for l in range(depth):
        mod = _ada(c, w_ada[l], b_ada[l])
        mod4 = mod.reshape(bsz, ADA_CHUNKS, 1, d)

        wl = w_in[l]
        cols = np.cumsum([0, H_A * hd, H_A * hd, H_A * hd, H_B * hd] + [G_B * hd] * 6)
        qa_c, ka_c, va_c, qb_c, kc_c, vc_c, ks_c, vs_c, kw_c, vw_c = [
            wl[:, int(a):int(b)] for a, b in zip(cols[:-1], cols[1:])]
        gl = wl[:, int(cols[-1]):].reshape(d, G_B, 3 * R_B)
        gl = jnp.pad(gl, ((0, 0), (0, 0), (0, GATE_ROWS - 3 * R_B))).reshape(d, G_B * GATE_ROWS)
        w_rows = jnp.concatenate([ka_c, kc_c, vc_c, ks_c, kw_c], axis=1).astype(BF16)
        w_t = jnp.concatenate([qa_c, va_c, qb_c, vs_c, vw_c, gl], axis=1).T.astype(BF16)

        (qa, kaug, vat, kmean, qb, kc, vc, ksaug, vst, kw, vwt, gates) = _inproj(
            x, mod4, mod4, g_mix[l].reshape(1, d), w_rows, w_t, bd,
            tile_t(q_norm_a[l], H_A) * scale, tile(k_norm_a[l], H_A), tile_t(q_norm_b[l], H_B) * scale,
            tile(k_norm_sel[l], G_B), tile(k_norm_win[l], G_B))

        nba = s // BLK_A
        km = kmean.reshape(bsz, nba, H_A, hd).transpose(0, 2, 1, 3)
        km = jnp.pad(km, ((0, 0), (0, 0), (0, hd - nba), (0, LANES - hd)))
        o_a = _moba(_moba_gate(qa, km), kaug, vat, near[:H_A], far[:H_A])

        w1 = jnp.stack([cmp_w1_k[l], cmp_w1_v[l]]).astype(BF16)
        w2t = jnp.stack([cmp_w2_k[l].T, cmp_w2_v[l].T]).astype(BF16)
        pe = jnp.stack([cmp_pe_k[l], cmp_pe_v[l]]).reshape(2, 1, CMP_LEN * hd)
        pe = jnp.broadcast_to(pe, (2, 8, CMP_LEN * hd)).astype(BF16)
        kcmp, vcmpt = _compress(kc, vc, w1, w2t, pe,
                                k_norm_cmp[l].astype(F32).reshape(1, hd))

        oc, selb = _nsa_cmp(qb, kcmp, vcmpt, ovt, n_cmp)
        o_b = _nsa_main(qb, selb, ksaug, vst, kw, vwt, oc, gates, near[H_A:], win, far[H_A:])

        x = _out_ffn(x, o_a, o_b, w_out[l].astype(BF16), mod4, g_ffn[l].reshape(1, d),
                     w_gu[l].astype(BF16), w_down[l].astype(BF16))
    return x
```

```python
import functools
import math

import jax
import jax.numpy as jnp
import numpy as np
from jax import lax
from jax.experimental import pallas as pl
from jax.experimental.pallas import tpu as pltpu

F32 = jnp.float32
BF16 = jnp.bfloat16

HEAD_DIM = 64
LANES = 128
BF16_ROWS = 16
H_A = 8
H_B = 8
G_B = 2
R_B = H_B // G_B
D_A = H_A * HEAD_DIM
D_B = H_B * HEAD_DIM
D_KV = G_B * HEAD_DIM
BLK_A = 256
TOPK_A = 3
CMP_LEN = 32
CMP_STRIDE = 16
CMP_HIDDEN = 256
SEL_BLK = 64
SEL_TOPK = 16
WINDOW = 512
N_BUCKETS = 32
MAX_DIST = 128
ADA_CHUNKS = 6
NEG = -1e30
N_FORCED = 3
EPS = 1e-6
LOG2_E = math.log2(math.e)

_LOG2_BLK_A = BLK_A.bit_length() - 1
_LOG2_SEL_BLK = SEL_BLK.bit_length() - 1

TQ = 256
TM_IN = 1024
TM_FFN = 512
ADA_TN = 512
GATE_TQ = 2048
GATE_HEADS = 4
CMP_TQ = 512
FAR_PAIRS_PER_TRIP = 8
MOBA_HEADS = 4
GATE_ROWS = 16
ACC_ROWS = HEAD_DIM + BF16_ROWS
V7X_VMEM_BYTES = 64 * 1024 * 1024
VMEM_LIMIT = V7X_VMEM_BYTES * 7 // 8


def _dot(a, b):
    return jnp.dot(a, b, preferred_element_type=F32)


def _dot_nt(a, b):
    return lax.dot_general(a, b, (((1,), (1,)), ((), ())), preferred_element_type=F32)


def _split(a):
    hi = a.astype(BF16)
    lo = (a - hi.astype(F32)).astype(BF16)
    return hi, lo


def _dot3(a, b):
    ah, al = _split(a)
    bh, bl = _split(b)
    return _dot(ah, bh) + (_dot(al, bh) + _dot(ah, bl))


def _params(sem):
    return pltpu.CompilerParams(dimension_semantics=sem, vmem_limit_bytes=VMEM_LIMIT)


def _ada_kernel(c_ref, w_ref, b_ref, o_ref):
    c = c_ref[...]
    o_ref[...] = _dot3(c * jax.nn.sigmoid(c), w_ref[...]) + b_ref[...]


def _ada(c, w, b):
    bsz, d = c.shape
    n = w.shape[1]
    tn = ADA_TN
    return pl.pallas_call(
        _ada_kernel,
        grid=(n // tn,),
        in_specs=[pl.BlockSpec((bsz, d), lambda j: (0, 0)),
                  pl.BlockSpec((d, tn), lambda j: (0, j)),
                  pl.BlockSpec((1, tn), lambda j: (0, j))],
        out_specs=pl.BlockSpec((bsz, tn), lambda j: (0, j)),
        out_shape=jax.ShapeDtypeStruct((bsz, n), F32),
        compiler_params=_params(("arbitrary",)),
        name="ada",
    )(c, w, b.reshape(1, n))


def _rms_rows(xf, g):
    ms = jnp.mean(xf * xf, axis=-1, keepdims=True)
    return xf * lax.rsqrt(ms + EPS) * g


def _head_norm(t, bd, gain):
    ms = _dot((t * t).astype(BF16), bd)
    return t * lax.rsqrt(ms + EPS) * gain


def _head_norm_t(t, gain):
    heads = []
    for hd in range(t.shape[0] // HEAD_DIM):
        th = t[hd * HEAD_DIM:(hd + 1) * HEAD_DIM]
        ms = jnp.mean(th * th, axis=0, keepdims=True)
        heads.append(th * lax.rsqrt(ms + EPS) * gain[hd * HEAD_DIM:(hd + 1) * HEAD_DIM])
    return heads


def _inproj_kernel(x_ref, sc_ref, sh_ref, gmix_ref, wr_ref, wt_ref, bd_ref, gqa_ref, gka_ref, gqb_ref,
                   gks_ref, gkw_ref,
                   qa_ref, kaug_ref, va_ref, kmean_ref, qb_ref, kc_ref, vc_ref, ksaug_ref,
                   vs_ref, kw_ref, vw_ref, gates_ref, chunk_ref):
    tm = x_ref.shape[1]
    ti = pl.program_id(1)
    xf = x_ref[0]
    h = _rms_rows(xf, gmix_ref[...]) * (1.0 + sc_ref[0, 0]) + sh_ref[0, 0]
    hb = h.astype(BF16)

    def proj(c0, c1):
        return _dot(hb, wr_ref[:, c0:c1])

    def proj_t(r0, r1):
        return _dot_nt(wt_ref[r0:r1, :], hb)

    bd = bd_ref[...]
    bd2 = bd_ref[0:LANES, 0:LANES]
    lane = lax.broadcasted_iota(jnp.int32, (tm, LANES), 1)
    row = lax.broadcasted_iota(jnp.int32, (tm, LANES), 0) + ti * tm
    low = lane < HEAD_DIM

    def k_in_low(pair, odd):
        return pltpu.roll(pair, HEAD_DIM, 1) if odd else pair

    t_qa, t_va, t_qb = 0, D_A, 2 * D_A
    t_vs, t_vw, t_gl = t_qb + D_B, t_qb + D_B + D_KV, t_qb + D_B + 2 * D_KV
    c_ka, c_kc, c_vc, c_ks, c_kw = 0, D_A, D_A + D_KV, D_A + 2 * D_KV, D_A + 3 * D_KV

    for hd, qh in enumerate(_head_norm_t(proj_t(t_qa, t_va), gqa_ref[...])):
        qa_ref[0, hd] = qh

    def put_tiles(ref, vt):
        for i in range(tm // TQ):
            ref[0, i] = vt[:, i * TQ:(i + 1) * TQ].astype(BF16)

    put_tiles(va_ref, proj_t(t_va, t_qb))
    for hd, qh in enumerate(_head_norm_t(proj_t(t_qb, t_vs), gqb_ref[...])):
        qb_ref[0, hd] = qh
    put_tiles(vs_ref, proj_t(t_vs, t_vw))
    put_tiles(vw_ref, proj_t(t_vw, t_gl))
    gl = jax.nn.sigmoid(proj_t(t_gl, t_gl + G_B * GATE_ROWS))
    for g in range(G_B):
        gates_ref[0, g] = gl[g * GATE_ROWS:(g + 1) * GATE_ROWS]

    ka = _head_norm(proj(c_ka, c_kc), bd, gka_ref[...])
    oh_a = jnp.where(lane - HEAD_DIM == (row >> _LOG2_BLK_A), 1.0, 0.0)
    for hd in range(H_A):
        pair = ka[:, (hd // 2) * LANES:(hd // 2 + 1) * LANES]
        kaug_ref[0, hd] = jnp.where(low, k_in_low(pair, hd % 2), oh_a).astype(BF16)
    for i in range(tm // BLK_A):
        kmean_ref[0, i] = jnp.mean(ka[i * BLK_A:(i + 1) * BLK_A], axis=0, keepdims=True)

    def put_chunks(ref, kv):
        chunk_ref[...] = kv
        rows = [chunk_ref[pl.ds(p, tm // CMP_STRIDE, stride=CMP_STRIDE), :] for p in range(CMP_STRIDE)]
        for g in range(G_B):
            ref[0, g] = jnp.concatenate(
                [r[:, g * HEAD_DIM:(g + 1) * HEAD_DIM] for r in rows], axis=1).astype(BF16)

    put_chunks(kc_ref, proj(c_kc, c_vc))
    put_chunks(vc_ref, proj(c_vc, c_ks))

    ks = _head_norm(proj(c_ks, c_kw), bd2, gks_ref[...])
    kw = _head_norm(proj(c_kw, c_kw + D_KV), bd2, gkw_ref[...])
    oh_s = jnp.where(lane == (row >> _LOG2_SEL_BLK), 1.0, 0.0).astype(BF16)
    for g in range(G_B):
        ksaug_ref[0, g] = jnp.concatenate(
            [jnp.where(low, k_in_low(ks, g), 0.0).astype(BF16), oh_s], axis=1)
        kw_ref[0, g] = jnp.where(low, k_in_low(kw, g), 0.0).astype(BF16)


def _inproj(x, sc, sh, gmix, wr, wt, bd, gqa, gka, gqb, gks, gkw):
    bsz, s, d = x.shape
    tm = TM_IN
    nt = s // tm
    nba = s // BLK_A
    const2 = lambda b, t: (0, 0)
    tok3 = lambda b, t: (b, t, 0)
    tok4 = lambda b, t: (b, 0, t, 0)
    tile4 = lambda b, t: (b, t, 0, 0)
    tr4 = lambda b, t: (b, 0, 0, t)
    in_specs = [
        pl.BlockSpec((1, tm, d), tok3),
        pl.BlockSpec((1, 1, 1, d), lambda b, t: (b, 1, 0, 0)),
        pl.BlockSpec((1, 1, 1, d), lambda b, t: (b, 0, 0, 0)),
        pl.BlockSpec((1, d), const2),
        pl.BlockSpec(wr.shape, const2),
        pl.BlockSpec(wt.shape, const2),
        pl.BlockSpec((D_A, D_A), const2),
        pl.BlockSpec((D_A, tm), const2),
        pl.BlockSpec((1, D_A), const2),
        pl.BlockSpec((D_B, tm), const2),
        pl.BlockSpec((1, LANES), const2),
        pl.BlockSpec((1, LANES), const2),
    ]
    out_shape = [
        jax.ShapeDtypeStruct((bsz, H_A, HEAD_DIM, s), F32),
        jax.ShapeDtypeStruct((bsz, H_A, s, LANES), BF16),
        jax.ShapeDtypeStruct((bsz, s // TQ, D_A, TQ), BF16),
        jax.ShapeDtypeStruct((bsz, nba, 1, D_A), F32),
        jax.ShapeDtypeStruct((bsz, H_B, HEAD_DIM, s), F32),
        jax.ShapeDtypeStruct((bsz, G_B, s // CMP_STRIDE, CMP_STRIDE * HEAD_DIM), BF16),
        jax.ShapeDtypeStruct((bsz, G_B, s // CMP_STRIDE, CMP_STRIDE * HEAD_DIM), BF16),
        jax.ShapeDtypeStruct((bsz, G_B, s, 2 * LANES), BF16),
        jax.ShapeDtypeStruct((bsz, s // TQ, LANES, TQ), BF16),
        jax.ShapeDtypeStruct((bsz, G_B, s, LANES), BF16),
        jax.ShapeDtypeStruct((bsz, s // TQ, LANES, TQ), BF16),
        jax.ShapeDtypeStruct((bsz, G_B, GATE_ROWS, s), F32),
    ]
    out_specs = [
        pl.BlockSpec((1, H_A, HEAD_DIM, tm), tr4),
        pl.BlockSpec((1, H_A, tm, LANES), tok4),
        pl.BlockSpec((1, tm // TQ, D_A, TQ), tile4),
        pl.BlockSpec((1, tm // BLK_A, 1, D_A), lambda b, t: (b, t, 0, 0)),
        pl.BlockSpec((1, H_B, HEAD_DIM, tm), tr4),
        pl.BlockSpec((1, G_B, tm // CMP_STRIDE, CMP_STRIDE * HEAD_DIM), tok4),
        pl.BlockSpec((1, G_B, tm // CMP_STRIDE, CMP_STRIDE * HEAD_DIM), tok4),
        pl.BlockSpec((1, G_B, tm, 2 * LANES), tok4),
        pl.BlockSpec((1, tm // TQ, LANES, TQ), tile4),
        pl.BlockSpec((1, G_B, tm, LANES), tok4),
        pl.BlockSpec((1, tm // TQ, LANES, TQ), tile4),
        pl.BlockSpec((1, G_B, GATE_ROWS, tm), tr4),
    ]
    return pl.pallas_call(
        _inproj_kernel,
        grid=(bsz, nt),
        in_specs=in_specs,
        out_specs=out_specs,
        out_shape=out_shape,
        scratch_shapes=[pltpu.VMEM((tm, LANES), F32)],
        compiler_params=_params(("parallel", "parallel")),
        name="inproj",
    )(x, sc, sh, gmix, wr, wt, bd, gqa, gka, gqb, gks, gkw)


def _compress_kernel(ck_ref, cv_ref, w1_ref, w2t_ref, pe_ref, gk_ref, ok_ref, ov_ref):
    half = CMP_STRIDE * HEAD_DIM
    for kv, c_ref in enumerate((ck_ref, cv_ref)):
        for g in range(G_B):
            c = c_ref[0, g]
            a = _dot(c, w1_ref[kv, 0:half, :])
            b = _dot(c, w1_ref[kv, half:2 * half, :])
            n = a.shape[0]
            b_next = pltpu.roll(b, n - 1, 0)
            pe_term = _dot(pe_ref[kv], w1_ref[kv])[0:1]
            hid = jax.nn.gelu(a + b_next + pe_term).astype(BF16)
            if kv == 0:
                y = _dot_nt(hid, w2t_ref[kv])
                ms = jnp.mean(y * y, axis=1, keepdims=True)
                y = y * lax.rsqrt(ms + EPS) * gk_ref[...]
                ok_ref[0, g] = jnp.concatenate([y, jnp.zeros_like(y)], axis=1).astype(BF16)
            else:
                ov_ref[0, g] = _dot_nt(w2t_ref[kv], hid).astype(BF16)


def _compress(ck, cv, w1, w2t, pe, gk):
    bsz, g, n, width = ck.shape
    blk = pl.BlockSpec((1, g, n, width), lambda b: (b, 0, 0, 0))
    full = lambda a: pl.BlockSpec(a.shape, lambda b: (0,) * a.ndim)
    return pl.pallas_call(
        _compress_kernel,
        grid=(bsz,),
        in_specs=[blk, blk, full(w1), full(w2t), full(pe), full(gk)],
        out_specs=[pl.BlockSpec((1, g, n, LANES), lambda b: (b, 0, 0, 0)),
                   pl.BlockSpec((1, g, HEAD_DIM, n), lambda b: (b, 0, 0, 0))],
        out_shape=[jax.ShapeDtypeStruct((bsz, g, n, LANES), BF16),
                   jax.ShapeDtypeStruct((bsz, g, HEAD_DIM, n), BF16)],
        compiler_params=_params(("parallel",)),
        name="compress",
    )(ck, cv, w1, w2t, pe, gk)


def _with_ones(vt):
    return jnp.concatenate([vt, jnp.ones((BF16_ROWS, vt.shape[1]), BF16)], axis=0)


def _col_max(s):
    while s.shape[0] > 8:
        half = s.shape[0] // 2
        s = jnp.maximum(s[0:half], s[half:2 * half])
    return jnp.max(s, axis=0, keepdims=True)


def _flash_update(carries, chains):
    m_news = []
    for carry, tiles in zip(carries, chains):
        tops = []
        for s, _, bias, top in tiles:
            top = _col_max(s) if top is None else top
            tops.append(top if bias is None else top + bias)
        m_news.append(functools.reduce(jnp.maximum, tops if carry is None else tops + [carry[0]]))
    pvs = [None] * len(chains)
    for t in range(max(len(tiles) for tiles in chains)):
        for c, tiles in enumerate(chains):
            if t < len(tiles):
                s, vt, bias, _ = tiles[t]
                p = jnp.exp2(s - (m_news[c] if bias is None else m_news[c] - bias)).astype(BF16)
                part = _dot(_with_ones(vt), p)
                pvs[c] = part if pvs[c] is None else pvs[c] + part
    outs = []
    for carry, m_new, pv in zip(carries, m_news, pvs):
        outs.append((m_new, pv if carry is None else jnp.exp2(carry[0] - m_new) * carry[1] + pv))
    return outs


def _flash_out(acc):
    return acc[0:HEAD_DIM] / acc[HEAD_DIM:HEAD_DIM + 1]


def _topk_rows(scores, index, k):
    scores = list(scores)
    picked = [jnp.zeros(sc.shape, F32) for sc in scores]
    for _ in range(k):
        mx = [jnp.max(sc, axis=0, keepdims=True) for sc in scores]
        cand = [jnp.where(sc == m, index, jnp.int32(1 << 20)) for sc, m in zip(scores, mx)]
        first = [jnp.min(c, axis=0, keepdims=True) for c in cand]
        hit = [index == f for f in first]
        picked = [jnp.where(h, 1.0, p) for h, p in zip(hit, picked)]
        scores = [jnp.where(h, -jnp.inf, sc) for h, sc in zip(hit, scores)]
    return picked


def _tile_rows(j):
    return pl.ds(pl.multiple_of(j * TQ, TQ), TQ)


def _moba_gate_kernel(q_ref, km_ref, o_ref, *, n_sel, n_rows):
    nh, tg = q_ref.shape[1], q_ref.shape[3]
    t0 = pl.program_id(2) * tg
    shape = (n_rows, tg)
    blk = lax.broadcasted_iota(jnp.int32, shape, 0)
    own = (lax.broadcasted_iota(jnp.int32, shape, 1) + t0) >> _LOG2_BLK_A
    valid = blk < own
    qs, gates = [], []
    for hh in range(nh):
        q = q_ref[0, hh]
        gate = _dot3(km_ref[0, hh, 0:n_rows, :], jnp.concatenate([q, jnp.zeros_like(q)], axis=0))
        qs.append(q)
        gates.append(jnp.where(valid, gate, -jnp.inf))
    unused = jnp.full((HEAD_DIM - n_rows, tg), NEG, F32)
    for hh, picked in enumerate(_topk_rows(gates, blk, n_sel)):
        keep = jnp.where(valid, picked, 0.0) + jnp.where(blk == own, 1.0, 0.0)
        selb = jnp.where(keep > 0.0, 0.0, NEG)
        o_ref[0, hh] = jnp.concatenate([qs[hh], selb, unused], axis=0).astype(BF16)


def _moba_gate(qa, km):
    bsz, nh, _, s = qa.shape
    tg = min(s, GATE_TQ)
    hpb = GATE_HEADS
    n_sel = max(1, min(TOPK_A, s // BLK_A - 1))
    return pl.pallas_call(
        functools.partial(_moba_gate_kernel, n_sel=n_sel, n_rows=-(-(s // BLK_A) // 8) * 8),
        grid=(bsz, nh // hpb, s // tg),
        in_specs=[pl.BlockSpec((1, hpb, HEAD_DIM, tg), lambda b, h, t: (b, h, 0, t)),
                  pl.BlockSpec((1, hpb, HEAD_DIM, LANES), lambda b, h, t: (b, h, 0, 0))],
        out_specs=pl.BlockSpec((1, hpb, LANES, tg), lambda b, h, t: (b, h, 0, t)),
        out_shape=jax.ShapeDtypeStruct((bsz, nh, LANES, s), BF16),
        compiler_params=_params(("parallel", "parallel", "parallel")),
        name="moba_gate",
    )(qa, km)


class _NearTiles:
    def __init__(self, sn_ref):
        self.sn_ref, self.count = sn_ref, 0

    def put(self, s):
        slot = self.count
        self.count += 1
        self.sn_ref[slot] = s
        return slot, _col_max(s)

    def tile(self, stashed, vt, bias):
        slot, top = stashed
        return self.sn_ref[slot], vt, bias, top


def _far_tiles(qi):
    n_far = jnp.maximum(qi - 1, 0)
    left = jnp.maximum(n_far - 1, 0)
    off_left = jnp.where((n_far & 1) == 1, 0.0, NEG)
    return n_far >> 1, left, off_left


class _FarLoop:
    def __init__(self, n_pairs, heads, qk_tile, values, cbs, sa_ref, sb_ref, m_ref, acc_ref):
        self.n_pairs, self.heads, self.qk_tile, self.values, self.cbs = n_pairs, heads, qk_tile, values, cbs
        self.sa_ref, self.sb_ref, self.m_ref, self.acc_ref = sa_ref, sb_ref, m_ref, acc_ref
        self.last = jnp.maximum(n_pairs - 1, 0)

    def fetch(self, buf_ref, h, i):
        s_lo = self.qk_tile(h, 2 * i)
        s_hi = self.qk_tile(h, 2 * i + 1)
        buf_ref[h, 0:TQ, :] = s_lo
        buf_ref[h, TQ:2 * TQ, :] = s_hi
        return _col_max(s_lo), _col_max(s_hi)

    def consume(self, buf_ref, h, top, i):
        (m, acc), = _flash_update([(self.m_ref[h], self.acc_ref[h])], [[
            (buf_ref[h, 0:TQ, :], self.values(h, 2 * i), self.cbs[h], top[0]),
            (buf_ref[h, TQ:2 * TQ, :], self.values(h, 2 * i + 1), self.cbs[h], top[1])]])
        self.m_ref[h] = m
        self.acc_ref[h] = acc

    def run(self, tops_first):
        def two_pairs(ia, tops_a):
            tops_b, tops_next = [], []
            for h in self.heads:
                tops_b.append(self.fetch(self.sb_ref, h, ia + 1))
                self.consume(self.sa_ref, h, tops_a[h], ia)
            for h in self.heads:
                tops_next.append(self.fetch(self.sa_ref, h, jnp.minimum(ia + 2, self.last)))
                self.consume(self.sb_ref, h, tops_b[h], ia + 1)
            return tuple(tops_next)

        def pairs(n, start, tops):
            for k in range(0, n, 2):
                tops = two_pairs(start + k, tops)
            return tops

        trip = FAR_PAIRS_PER_TRIP
        n_trips = self.n_pairs // trip
        tops = lax.fori_loop(0, n_trips, lambda q, t: pairs(trip, trip * q, t), tops_first)
        done = trip * n_trips
        n = trip // 2
        while n >= 2:
            has = ((self.n_pairs // n) & 1) == 1
            tops = lax.cond(has, functools.partial(pairs, n, done), lambda t: t, tops)
            done = done + jnp.where(has, n, 0)
            n //= 2
        tops_last = tops

        @pl.when((self.n_pairs & 1) == 1)
        def _():
            for h in self.heads:
                self.consume(self.sa_ref, h, tops_last[h], self.last)


def _moba_kernel(q_ref, k_ref, vt_ref, tab_ref, cb_ref, o_ref, sn_ref, sa_ref, sb_ref, m_ref, acc_ref):
    qi = pl.program_id(2)
    jp = jnp.maximum(qi - 1, 0)
    off_p = jnp.where(qi >= 1, 0.0, NEG)
    n_pairs, jl, off_l = _far_tiles(qi)
    heads = range(MOBA_HEADS)
    qaug = [q_ref[0, hh] for hh in heads]
    cbs = [cb_ref[hh][0:1, 0:1] for hh in heads]

    def scores(hh, j):
        return _dot(k_ref[0, hh, _tile_rows(j), :], qaug[hh])

    def values(hh, j):
        return vt_ref[0, j, hh * HEAD_DIM:(hh + 1) * HEAD_DIM, :]

    far = _FarLoop(n_pairs, heads, scores, values, cbs, sa_ref, sb_ref, m_ref, acc_ref)

    near = _NearTiles(sn_ref)
    stash = [(near.put(scores(hh, qi) + tab_ref[hh, TQ:2 * TQ, :]),
              near.put(scores(hh, jp) + tab_ref[hh, 0:TQ, :]),
              near.put(scores(hh, jl))) for hh in heads]
    tops_first = []
    for hh in heads:
        tops_first.append(far.fetch(sa_ref, hh, 0))
        (m, acc), = _flash_update([None], [[near.tile(stash[hh][0], values(hh, qi), None),
                                            near.tile(stash[hh][1], values(hh, jp), off_p),
                                            near.tile(stash[hh][2], values(hh, jl), cbs[hh] + off_l)]])
        m_ref[hh] = m
        acc_ref[hh] = acc
    far.run(tuple(tops_first))
    out_t = jnp.concatenate([_flash_out(acc_ref[hh]) for hh in heads], axis=0)
    o_ref[0] = out_t.T.astype(BF16)


def _flash_scratch(n_heads):
    return [pltpu.VMEM((n_heads, 2 * TQ, TQ), F32),
            pltpu.VMEM((n_heads, 2 * TQ, TQ), F32),
            pltpu.VMEM((n_heads, 1, TQ), F32),
            pltpu.VMEM((n_heads, ACC_ROWS, TQ), F32)]


def _moba(qaug, kaug, vat, tab, cb):
    bsz, _, _, s = qaug.shape
    nq = s // TQ
    nh = MOBA_HEADS
    return pl.pallas_call(
        _moba_kernel,
        grid=(bsz, H_A // nh, nq),
        in_specs=[
            pl.BlockSpec((1, nh, LANES, TQ), lambda b, hp, qi: (b, hp, 0, qi)),
            pl.BlockSpec((1, nh, s, LANES), lambda b, hp, qi: (b, hp, 0, 0)),
            pl.BlockSpec((1, nq, nh * HEAD_DIM, TQ), lambda b, hp, qi: (b, 0, hp, 0)),
            pl.BlockSpec((nh, 2 * TQ, TQ), lambda b, hp, qi: (hp, 0, 0)),
            pl.BlockSpec((nh, 8, LANES), lambda b, hp, qi: (hp, 0, 0)),
        ],
        out_specs=pl.BlockSpec((1, TQ, nh * HEAD_DIM), lambda b, hp, qi: (b, qi, hp)),
        out_shape=jax.ShapeDtypeStruct((bsz, s, H_A * HEAD_DIM), BF16),
        scratch_shapes=[pltpu.VMEM((3 * nh, TQ, TQ), F32)] + _flash_scratch(nh),
        compiler_params=_params(("parallel", "parallel", "arbitrary")),
        name="moba",
    )(qaug, kaug, vat, tab, cb)


def _nsa_cmp_kernel(q_ref, kc_ref, vct_ref, ovt_ref, oc_ref, selb_ref, *, n_sel, n_cmp, n_parts):
    qi = pl.program_id(1)
    ncp = kc_ref.shape[2]
    tq = q_ref.shape[3]
    t0 = qi * tq
    zeros = jnp.zeros((HEAD_DIM, tq), BF16)
    qs = [jnp.concatenate([q_ref[0, hd].astype(BF16), zeros], axis=0) for hd in range(H_B)]

    def body(nk, nb):
        n_idx = lax.broadcasted_iota(jnp.int32, (nk, tq), 0)
        t_idx = lax.broadcasted_iota(jnp.int32, (nk, tq), 1) + t0
        mask = (n_idx * CMP_STRIDE + (CMP_LEN - 1) <= t_idx) & (n_idx < n_cmp)
        any_key = t_idx[0:1] >= CMP_LEN - 1
        blk = lax.broadcasted_iota(jnp.int32, (nb, tq), 0)
        cur = (lax.broadcasted_iota(jnp.int32, (nb, tq), 1) + t0) >> _LOG2_SEL_BLK
        ok = blk <= cur
        forced = (blk == 0) | (blk == cur) | (blk == cur - 1)
        ovt = ovt_ref[0:nb, 0:nk]
        scores = []
        for g in range(G_B):
            kc = kc_ref[0, g, 0:nk, :]
            vct = _with_ones(vct_ref[0, g, :, 0:nk])
            psum = jnp.zeros((nk, tq), F32)
            for hd in range(g * R_B, (g + 1) * R_B):
                z = jnp.where(mask, _dot(kc, qs[hd]), NEG)
                e = jnp.exp2(z - _col_max(z))
                acc = _dot(vct, e.astype(BF16))
                rinv = jnp.where(any_key, 1.0 / acc[HEAD_DIM:HEAD_DIM + 1], 0.0)
                oc_ref[0, hd] = acc[0:HEAD_DIM] * rinv
                psum = psum + e * rinv
            ph, pl_ = _split(psum)
            imp_t = _dot(ovt, ph) + _dot(ovt, pl_)
            scores.append(jnp.where(ok & jnp.logical_not(forced), imp_t, -jnp.inf))
        for g, picked in enumerate(_topk_rows(scores, blk, max(n_sel - N_FORCED, 0))):
            selb_ref[0, g, 0:nb, :] = jnp.where(ok & (forced | (picked > 0.0)), 0.0, NEG).astype(BF16)
            if nb < LANES:
                selb_ref[0, g, nb:LANES, :] = jnp.full((LANES - nb, tq), NEG, BF16)

    part = ncp // n_parts
    need = jnp.minimum(((qi + 1) * (tq // CMP_STRIDE) + part - 1) // part, n_parts)
    for v in range(1, n_parts + 1):
        pl.when(need == v)(functools.partial(body, v * part, min(LANES, v * part * CMP_STRIDE // SEL_BLK)))


def _cmp_parts(ncp):
    return 4 if ncp % (4 * LANES) == 0 else 1


def _nsa_cmp(qb, kcmp, vcmpt, ovt, n_cmp):
    bsz, _, _, s = qb.shape
    tq = CMP_TQ if s % CMP_TQ == 0 else TQ
    nq = s // tq
    ncp = kcmp.shape[2]
    n_sel = min(SEL_TOPK, s // SEL_BLK)
    assert n_sel >= N_FORCED
    n_parts = _cmp_parts(ncp)
    return pl.pallas_call(
        functools.partial(_nsa_cmp_kernel, n_sel=n_sel, n_cmp=n_cmp, n_parts=n_parts),
        grid=(bsz, nq),
        in_specs=[
            pl.BlockSpec((1, H_B, HEAD_DIM, tq), lambda b, qi: (b, 0, 0, qi)),
            pl.BlockSpec((1, G_B, ncp, LANES), lambda b, qi: (b, 0, 0, 0)),
            pl.BlockSpec((1, G_B, HEAD_DIM, ncp), lambda b, qi: (b, 0, 0, 0)),
            pl.BlockSpec((LANES, ncp), lambda b, qi: (0, 0)),
        ],
        out_specs=[
            pl.BlockSpec((1, H_B, HEAD_DIM, tq), lambda b, qi: (b, 0, 0, qi)),
            pl.BlockSpec((1, G_B, LANES, tq), lambda b, qi: (b, 0, 0, qi)),
        ],
        out_shape=[jax.ShapeDtypeStruct((bsz, H_B, HEAD_DIM, s), F32),
                   jax.ShapeDtypeStruct((bsz, G_B, LANES, s), BF16)],
        compiler_params=_params(("parallel", "parallel")),
        name="nsa_cmp",
    )(qb, kcmp, vcmpt, ovt)


def _nsa_main_kernel(q_ref, selb_ref, ks_ref, vst_ref, kw_ref, vwt_ref, oc_ref, gates_ref,
                     tabs_ref, tabw_ref, cb_ref, o_ref, sa_ref, sb_ref, m_ref, acc_ref):
    qi = pl.program_id(2)
    selb = selb_ref[0, 0]
    gates = gates_ref[0, 0]
    zeros = jnp.zeros((HEAD_DIM, TQ), BF16)
    j1 = jnp.maximum(qi - 1, 0)
    j2 = jnp.maximum(qi - 2, 0)
    off1 = jnp.where(qi >= 1, 0.0, NEG)
    off2 = jnp.where(qi >= 2, 0.0, NEG)
    n_pairs, jl, off_l = _far_tiles(qi)

    heads = range(R_B)
    qw = [jnp.concatenate([q_ref[0, r].astype(BF16), zeros], axis=0) for r in heads]
    qs = [jnp.concatenate([qw[r], selb], axis=0) for r in heads]
    cbs = [cb_ref[r][0:1, 0:1] for r in heads]

    def s_sel(r, j):
        return _dot(ks_ref[0, 0, _tile_rows(j), :], qs[r])

    def s_win(r, j):
        return _dot(kw_ref[0, 0, _tile_rows(j), :], qw[r])

    far = _FarLoop(n_pairs, heads, s_sel, lambda r, j: vst_ref[0, j], cbs, sa_ref, sb_ref, m_ref, acc_ref)

    sw = [[s_win(r, j) for r in heads] for j in (qi, j1, j2)]
    ss = [[s_sel(r, j) for r in heads] for j in (qi, j1, jl)]
    tops_first = tuple(far.fetch(sa_ref, r, 0) for r in heads)

    chains = [[(sw[0][r] + tabw_ref[r, 2 * TQ:3 * TQ, :], vwt_ref[0, qi], None, None),
               (sw[1][r] + tabw_ref[r, TQ:2 * TQ, :], vwt_ref[0, j1], off1, None),
               (sw[2][r] + tabw_ref[r, 0:TQ, :], vwt_ref[0, j2], off2, None)] for r in heads]
    chains += [[(ss[0][r] + tabs_ref[r, TQ:2 * TQ, :], vst_ref[0, qi], None, None),
                (ss[1][r] + tabs_ref[r, 0:TQ, :], vst_ref[0, j1], off1, None),
                (ss[2][r], vst_ref[0, jl], cbs[r] + off_l, None)] for r in heads]
    done = _flash_update([None] * len(chains), chains)
    win = [_flash_out(acc) for _, acc in done[:R_B]]
    for r, (m, acc) in zip(heads, done[R_B:]):
        m_ref[r] = m
        acc_ref[r] = acc
    far.run(tops_first)

    outs = [gates[3 * r:3 * r + 1] * oc_ref[0, r]
            + gates[3 * r + 1:3 * r + 2] * _flash_out(acc_ref[r])
            + gates[3 * r + 2:3 * r + 3] * win[r] for r in heads]
    o_ref[0] = jnp.concatenate(outs, axis=0).T.astype(BF16)


def _nsa_main(qb, selb, ksaug, vst, kw, vwt, oc, gates, tabs, tabw, cb):
    bsz, _, _, s = qb.shape
    nq = s // TQ
    return pl.pallas_call(
        _nsa_main_kernel,
        grid=(bsz, G_B, nq),
        in_specs=[
            pl.BlockSpec((1, R_B, HEAD_DIM, TQ), lambda b, g, qi: (b, g, 0, qi)),
            pl.BlockSpec((1, 1, LANES, TQ), lambda b, g, qi: (b, g, 0, qi)),
            pl.BlockSpec((1, 1, s, 2 * LANES), lambda b, g, qi: (b, g, 0, 0)),
            pl.BlockSpec((1, nq, HEAD_DIM, TQ), lambda b, g, qi: (b, 0, g, 0)),
            pl.BlockSpec((1, 1, s, LANES), lambda b, g, qi: (b, g, 0, 0)),
            pl.BlockSpec((1, nq, HEAD_DIM, TQ), lambda b, g, qi: (b, 0, g, 0)),
            pl.BlockSpec((1, R_B, HEAD_DIM, TQ), lambda b, g, qi: (b, g, 0, qi)),
            pl.BlockSpec((1, 1, GATE_ROWS, TQ), lambda b, g, qi: (b, g, 0, qi)),
            pl.BlockSpec((R_B, 2 * TQ, TQ), lambda b, g, qi: (g, 0, 0)),
            pl.BlockSpec((R_B, 3 * TQ, TQ), lambda b, g, qi: (g, 0, 0)),
            pl.BlockSpec((R_B, 8, LANES), lambda b, g, qi: (g, 0, 0)),
        ],
        out_specs=pl.BlockSpec((1, TQ, R_B * HEAD_DIM), lambda b, g, qi: (b, qi, g)),
        out_shape=jax.ShapeDtypeStruct((bsz, s, H_B * HEAD_DIM), BF16),
        scratch_shapes=_flash_scratch(R_B),
        compiler_params=_params(("parallel", "parallel", "arbitrary")),
        name="nsa_main",
    )(qb, selb, ksaug, vst, kw, vwt, oc, gates, tabs, tabw, cb)


def _out_ffn_kernel(x_ref, oa_ref, ob_ref, wo_ref, gtm_ref, gffn_ref, scf_ref, shf_ref, gtf_ref,
                    wg_ref, wu_ref, wd_ref, o_ref, x1_ref, h_ref, acc_ref):
    f = pl.program_id(2)
    half = oa_ref.shape[2]

    @pl.when(f == 0)
    def _():
        mix = _dot(oa_ref[0], wo_ref[0:half, :]) + _dot(ob_ref[0], wo_ref[half:2 * half, :])
        x1 = x_ref[0] + gtm_ref[0, 0] * mix
        x1_ref[...] = x1
        h = _rms_rows(x1, gffn_ref[...]) * (1.0 + scf_ref[0, 0]) + shf_ref[0, 0]
        h_ref[...] = h.astype(BF16)

    h = h_ref[...]
    gate = _dot(h, wg_ref[...])
    up = _dot(h, wu_ref[...])
    act = (gate * jax.nn.sigmoid(gate) * up).astype(BF16)
    part = _dot(act, wd_ref[...])

    @pl.when(f == 0)
    def _():
        acc_ref[...] = part

    @pl.when(f > 0)
    def _():
        acc_ref[...] += part

    @pl.when(f == pl.num_programs(2) - 1)
    def _():
        o_ref[0] = x1_ref[...] + gtf_ref[0, 0] * acc_ref[...]


def _out_ffn(x, oa, ob, wo, mod4, gffn, wgu, wd):
    bsz, s, d = x.shape
    fh = wd.shape[0]
    tm = TM_FFN
    tf = fh // 2 if (fh // 2) % LANES == 0 else fh
    nf = fh // tf
    tok = lambda b, t, f: (b, t, 0)
    modspec = lambda k: pl.BlockSpec((1, 1, 1, d), lambda b, t, f: (b, k, 0, 0))
    return pl.pallas_call(
        _out_ffn_kernel,
        grid=(bsz, s // tm, nf),
        in_specs=[
            pl.BlockSpec((1, tm, d), tok),
            pl.BlockSpec((1, tm, oa.shape[2]), tok),
            pl.BlockSpec((1, tm, ob.shape[2]), tok),
            pl.BlockSpec((d, d), lambda b, t, f: (0, 0)),
            modspec(2),
            pl.BlockSpec((1, d), lambda b, t, f: (0, 0)),
            modspec(4),
            modspec(3),
            modspec(5),
            pl.BlockSpec((d, tf), lambda b, t, f: (0, f)),
            pl.BlockSpec((d, tf), lambda b, t, f: (0, f + nf)),
            pl.BlockSpec((tf, d), lambda b, t, f: (f, 0)),
        ],
        out_specs=pl.BlockSpec((1, tm, d), tok),
        out_shape=jax.ShapeDtypeStruct((bsz, s, d), F32),
        scratch_shapes=[pltpu.VMEM((tm, d), F32), pltpu.VMEM((tm, d), BF16), pltpu.VMEM((tm, d), F32)],
        compiler_params=_params(("parallel", "parallel", "arbitrary")),
        name="out_ffn",
    )(x, oa, ob, wo, mod4, gffn, mod4, mod4, mod4, wgu, wgu, wd)


def _t5_bucket_np(d):
    max_exact = N_BUCKETS // 2
    d = np.maximum(d, 0)
    df = np.maximum(d, 1).astype(np.float64)
    large = max_exact + (np.log(df / max_exact) / math.log(MAX_DIST / max_exact)
                         * (N_BUCKETS - max_exact)).astype(np.int64)
    large = np.minimum(large, N_BUCKETS - 1)
    return np.where(d < max_exact, d, large).astype(np.int32)


def _bias_expand_kernel(tab_ref, bucket_ref, o_ref):
    hd = pl.program_id(0)
    bucket = bucket_ref[...]
    acc = jnp.full(bucket.shape, NEG, F32)
    for b in range(N_BUCKETS):
        acc = jnp.where(bucket == b, tab_ref[hd, b], acc)
    o_ref[0] = acc


def _bias_expand(tab, bucket):
    nh = tab.shape[0]
    return pl.pallas_call(
        _bias_expand_kernel,
        grid=(nh,),
        in_specs=[pl.BlockSpec(memory_space=pltpu.SMEM),
                  pl.BlockSpec(bucket.shape, lambda h: (0, 0))],
        out_specs=pl.BlockSpec((1,) + bucket.shape, lambda h: (h, 0, 0)),
        out_shape=jax.ShapeDtypeStruct((nh,) + bucket.shape, F32),
        compiler_params=_params(("parallel",)),
        name="bias_expand",
    )(tab, jnp.asarray(bucket, jnp.int32))


def _bias_tables(rel_bias):
    tab = rel_bias.T.astype(F32) * LOG2_E
    i = np.arange(TQ)[None, :]
    d_near = i + TQ - np.arange(2 * TQ)[:, None]
    near = _bias_expand(tab, np.where(d_near >= 0, _t5_bucket_np(d_near), -1))
    d_win = i + 2 * TQ - np.arange(3 * TQ)[:, None]
    ok_win = (d_win >= 0) & (d_win < WINDOW)
    win = _bias_expand(tab[H_A:], np.where(ok_win, _t5_bucket_np(d_win), -1))
    far = jnp.broadcast_to(tab[:, N_BUCKETS - 1][:, None, None], (tab.shape[0], 8, LANES))
    return near, win, far


def _overlap_t(n_cmp_pad, n_cmp):
    cs = np.arange(n_cmp_pad)[None, :] * CMP_STRIDE
    ss = np.arange(LANES)[:, None] * SEL_BLK
    ov = (cs < ss + SEL_BLK) & (cs + CMP_LEN > ss) & (np.arange(n_cmp_pad)[None, :] < n_cmp)
    return jnp.asarray(ov.astype(np.float32), BF16)


def _block_diag(n):
    m = (np.arange(n)[:, None] // HEAD_DIM == np.arange(n)[None, :] // HEAD_DIM)
    return jnp.asarray(m.astype(np.float32) / HEAD_DIM, BF16)


def kernel(x, c, rel_bias, w_ada, b_ada, g_mix, w_in, q_norm_a, k_norm_a, q_norm_b, k_norm_cmp,
           k_norm_sel, k_norm_win, cmp_pe_k, cmp_w1_k, cmp_w2_k, cmp_pe_v, cmp_w1_v, cmp_w2_v,
           w_out, g_ffn, w_gu, w_down):
    bsz, s, d = x.shape
    depth = w_ada.shape[0]
    assert s % TM_IN == 0 and s % TM_FFN == 0 and s % (2 * TQ) == 0
    assert s // BLK_A <= HEAD_DIM and s // SEL_BLK <= LANES
    assert WINDOW == 2 * TQ and BLK_A == TQ and MAX_DIST <= TQ
    n_chunks = s // CMP_STRIDE
    n_cmp = (s - CMP_LEN) // CMP_STRIDE + 1
    scale = HEAD_DIM ** -0.5 * LOG2_E
    hd = HEAD_DIM

    near, win, far = _bias_tables(rel_bias)
    ovt = _overlap_t(n_chunks, n_cmp)
    bd = _block_diag(D_A)
    tile = lambda g, n: jnp.tile(g.astype(F32), n).reshape(1, -1)
    tile_t = lambda g, n: jnp.broadcast_to(jnp.tile(g.astype(F32), n)[:, None], (n * hd, TM_IN))

    for l in range(depth):
        mod = _ada(c, w_ada[l], b_ada[l])
        mod4 = mod.reshape(bsz, ADA_CHUNKS, 1, d)

        wl = w_in[l]
        cols = np.cumsum([0, H_A * hd, H_A * hd, H_A * hd, H_B * hd] + [G_B * hd] * 6)
        qa_c, ka_c, va_c, qb_c, kc_c, vc_c, ks_c, vs_c, kw_c, vw_c = [
            wl[:, int(a):int(b)] for a, b in zip(cols[:-1], cols[1:])]
        gl = wl[:, int(cols[-1]):].reshape(d, G_B, 3 * R_B)
        gl = jnp.pad(gl, ((0, 0), (0, 0), (0, GATE_ROWS - 3 * R_B))).reshape(d, G_B * GATE_ROWS)
        w_rows = jnp.concatenate([ka_c, kc_c, vc_c, ks_c, kw_c], axis=1).astype(BF16)
        w_t = jnp.concatenate([qa_c, va_c, qb_c, vs_c, vw_c, gl], axis=1).T.astype(BF16)

        (qa, kaug, vat, kmean, qb, kc, vc, ksaug, vst, kw, vwt, gates) = _inproj(
            x, mod4, mod4, g_mix[l].reshape(1, d), w_rows, w_t, bd,
            tile_t(q_norm_a[l], H_A) * scale, tile(k_norm_a[l], H_A), tile_t(q_norm_b[l], H_B) * scale,
            tile(k_norm_sel[l], G_B), tile(k_norm_win[l], G_B))

        nba = s // BLK_A
        km = kmean.reshape(bsz, nba, H_A, hd).transpose(0, 2, 1, 3)
        km = jnp.pad(km, ((0, 0), (0, 0), (0, hd - nba), (0, LANES - hd)))
        o_a = _moba(_moba_gate(qa, km), kaug, vat, near[:H_A], far[:H_A])

        w1 = jnp.stack([cmp_w1_k[l], cmp_w1_v[l]]).astype(BF16)
        w2t = jnp.stack([cmp_w2_k[l].T, cmp_w2_v[l].T]).astype(BF16)
        pe = jnp.stack([cmp_pe_k[l], cmp_pe_v[l]]).reshape(2, 1, CMP_LEN * hd)
        pe = jnp.broadcast_to(pe, (2, 8, CMP_LEN * hd)).astype(BF16)
        kcmp, vcmpt = _compress(kc, vc, w1, w2t, pe,
                                k_norm_cmp[l].astype(F32).reshape(1, hd))

        oc, selb = _nsa_cmp(qb, kcmp, vcmpt, ovt, n_cmp)
        o_b = _nsa_main(qb, selb, ksaug, vst, kw, vwt, oc, gates, near[H_A:], win, far[H_A:])

        x = _out_ffn(x, o_a, o_b, w_out[l].astype(BF16), mod4, g_ffn[l].reshape(1, d),
                     w_gu[l].astype(BF16), w_down[l].astype(BF16))
    return x
```

```python
import functools
import math

import jax
import jax.numpy as jnp
import numpy as np
from jax import lax
from jax.experimental import pallas as pl
from jax.experimental.pallas import tpu as pltpu

F32 = jnp.float32
BF16 = jnp.bfloat16

HEAD_DIM = 64
LANES = 128
BF16_ROWS = 16
H_A = 8
H_B = 8
G_B = 2
R_B = H_B // G_B
D_A = H_A * HEAD_DIM
D_B = H_B * HEAD_DIM
D_KV = G_B * HEAD_DIM
BLK_A = 256
TOPK_A = 3
CMP_LEN = 32
CMP_STRIDE = 16
CMP_HIDDEN = 256
SEL_BLK = 64
SEL_TOPK = 16
WINDOW = 512
N_BUCKETS = 32
MAX_DIST = 128
ADA_CHUNKS = 6
NEG = -1e30
N_FORCED = 3
EPS = 1e-6
LOG2_E = math.log2(math.e)

_LOG2_BLK_A = BLK_A.bit_length() - 1
_LOG2_SEL_BLK = SEL_BLK.bit_length() - 1

TQ = 256
TM_IN = 1024
FFN_CHUNK = 512
TM_FFN = 512
ADA_TN = 512
GATE_TQ = 2048
GATE_HEADS = 4
CMP_TQ = 512
FAR_PAIRS_PER_TRIP = 8
MOBA_HEADS = 4
GATE_ROWS = 16
ACC_ROWS = HEAD_DIM + BF16_ROWS
V7X_VMEM_BYTES = 64 * 1024 * 1024
VMEM_LIMIT = V7X_VMEM_BYTES * 7 // 8


def _dot(a, b):
    return jnp.dot(a, b, preferred_element_type=F32)


def _dot_nt(a, b):
    return lax.dot_general(a, b, (((1,), (1,)), ((), ())), preferred_element_type=F32)


def _split(a):
    hi = a.astype(BF16)
    lo = (a - hi.astype(F32)).astype(BF16)
    return hi, lo


def _dot3(a, b):
    ah, al = _split(a)
    bh, bl = _split(b)
    return _dot(ah, bh) + (_dot(al, bh) + _dot(ah, bl))


def _params(sem):
    return pltpu.CompilerParams(dimension_semantics=sem, vmem_limit_bytes=VMEM_LIMIT)


def _ada_kernel(c_ref, w_ref, b_ref, o_ref):
    c = c_ref[...]
    o_ref[...] = _dot3(c * jax.nn.sigmoid(c), w_ref[...]) + b_ref[...]


def _ada(c, w, b):
    bsz, d = c.shape
    n = w.shape[1]
    tn = ADA_TN
    return pl.pallas_call(
        _ada_kernel,
        grid=(n // tn,),
        in_specs=[pl.BlockSpec((bsz, d), lambda j: (0, 0)),
                  pl.BlockSpec((d, tn), lambda j: (0, j)),
                  pl.BlockSpec((1, tn), lambda j: (0, j))],
        out_specs=pl.BlockSpec((bsz, tn), lambda j: (0, j)),
        out_shape=jax.ShapeDtypeStruct((bsz, n), F32),
        compiler_params=_params(("arbitrary",)),
        name="ada",
    )(c, w, b.reshape(1, n))


def _rms_rows(xf, g):
    ms = jnp.mean(xf * xf, axis=-1, keepdims=True)
    return xf * lax.rsqrt(ms + EPS) * g


def _head_norm(t, bd, gain):
    ms = _dot((t * t).astype(BF16), bd)
    return t * lax.rsqrt(ms + EPS) * gain


def _head_norm_t(t, gain):
    heads = []
    for hd in range(t.shape[0] // HEAD_DIM):
        th = t[hd * HEAD_DIM:(hd + 1) * HEAD_DIM]
        ms = jnp.mean(th * th, axis=0, keepdims=True)
        heads.append(th * lax.rsqrt(ms + EPS) * gain[hd * HEAD_DIM:(hd + 1) * HEAD_DIM])
    return heads


def _inproj_kernel(x_ref, sc_ref, sh_ref, gmix_ref, wr_ref, wt_ref, bd_ref, gqa_ref, gka_ref, gqb_ref,
                   gks_ref, gkw_ref,
                   qa_ref, kaug_ref, va_ref, kmean_ref, qb_ref, kc_ref, vc_ref, ksaug_ref,
                   vs_ref, kw_ref, vw_ref, gates_ref, chunk_ref):
    tm = x_ref.shape[1]
    ti = pl.program_id(1)
    xf = x_ref[0]
    h = _rms_rows(xf, gmix_ref[...]) * (1.0 + sc_ref[0, 0]) + sh_ref[0, 0]
    hb = h.astype(BF16)

    def proj(c0, c1):
        return _dot(hb, wr_ref[:, c0:c1])

    def proj_t(r0, r1):
        return _dot_nt(wt_ref[r0:r1, :], hb)

    bd = bd_ref[...]
    bd2 = bd_ref[0:LANES, 0:LANES]
    lane = lax.broadcasted_iota(jnp.int32, (tm, LANES), 1)
    row = lax.broadcasted_iota(jnp.int32, (tm, LANES), 0) + ti * tm
    low = lane < HEAD_DIM

    def k_in_low(pair, odd):
        return pltpu.roll(pair, HEAD_DIM, 1) if odd else pair

    t_qa, t_va, t_qb = 0, D_A, 2 * D_A
    t_vs, t_vw, t_gl = t_qb + D_B, t_qb + D_B + D_KV, t_qb + D_B + 2 * D_KV
    c_ka, c_kc, c_vc, c_ks, c_kw = 0, D_A, D_A + D_KV, D_A + 2 * D_KV, D_A + 3 * D_KV

    for hd, qh in enumerate(_head_norm_t(proj_t(t_qa, t_va), gqa_ref[...])):
        qa_ref[0, hd] = qh

    def put_tiles(ref, vt):
        for i in range(tm // TQ):
            ref[0, i] = vt[:, i * TQ:(i + 1) * TQ].astype(BF16)

    put_tiles(va_ref, proj_t(t_va, t_qb))
    for hd, qh in enumerate(_head_norm_t(proj_t(t_qb, t_vs), gqb_ref[...])):
        qb_ref[0, hd] = qh
    put_tiles(vs_ref, proj_t(t_vs, t_vw))
    put_tiles(vw_ref, proj_t(t_vw, t_gl))
    gl = jax.nn.sigmoid(proj_t(t_gl, t_gl + G_B * GATE_ROWS))
    for g in range(G_B):
        gates_ref[0, g] = gl[g * GATE_ROWS:(g + 1) * GATE_ROWS]

    ka = _head_norm(proj(c_ka, c_kc), bd, gka_ref[...])
    oh_a = jnp.where(lane - HEAD_DIM == (row >> _LOG2_BLK_A), 1.0, 0.0)
    for hd in range(H_A):
        pair = ka[:, (hd // 2) * LANES:(hd // 2 + 1) * LANES]
        kaug_ref[0, hd] = jnp.where(low, k_in_low(pair, hd % 2), oh_a).astype(BF16)
    for i in range(tm // BLK_A):
        kmean_ref[0, i] = jnp.mean(ka[i * BLK_A:(i + 1) * BLK_A], axis=0, keepdims=True)

    def put_chunks(ref, kv):
        chunk_ref[...] = kv
        rows = [chunk_ref[pl.ds(p, tm // CMP_STRIDE, stride=CMP_STRIDE), :] for p in range(CMP_STRIDE)]
        for g in range(G_B):
            ref[0, g] = jnp.concatenate(
                [r[:, g * HEAD_DIM:(g + 1) * HEAD_DIM] for r in rows], axis=1).astype(BF16)

    put_chunks(kc_ref, proj(c_kc, c_vc))
    put_chunks(vc_ref, proj(c_vc, c_ks))

    ks = _head_norm(proj(c_ks, c_kw), bd2, gks_ref[...])
    kw = _head_norm(proj(c_kw, c_kw + D_KV), bd2, gkw_ref[...])
    oh_s = jnp.where(lane == (row >> _LOG2_SEL_BLK), 1.0, 0.0).astype(BF16)
    for g in range(G_B):
        ksaug_ref[0, g] = jnp.concatenate(
            [jnp.where(low, k_in_low(ks, g), 0.0).astype(BF16), oh_s], axis=1)
        kw_ref[0, g] = jnp.where(low, k_in_low(kw, g), 0.0).astype(BF16)


def _inproj(x, sc, sh, gmix, wr, wt, bd, gqa, gka, gqb, gks, gkw):
    bsz, s, d = x.shape
    tm = TM_IN
    nt = s // tm
    nba = s // BLK_A
    const2 = lambda b, t: (0, 0)
    tok3 = lambda b, t: (b, t, 0)
    tok4 = lambda b, t: (b, 0, t, 0)
    tile4 = lambda b, t: (b, t, 0, 0)
    tr4 = lambda b, t: (b, 0, 0, t)
    in_specs = [
        pl.BlockSpec((1, tm, d), tok3),
        pl.BlockSpec((1, 1, 1, d), lambda b, t: (b, 1, 0, 0)),
        pl.BlockSpec((1, 1, 1, d), lambda b, t: (b, 0, 0, 0)),
        pl.BlockSpec((1, d), const2),
        pl.BlockSpec(wr.shape, const2),
        pl.BlockSpec(wt.shape, const2),
        pl.BlockSpec((D_A, D_A), const2),
        pl.BlockSpec((D_A, tm), const2),
        pl.BlockSpec((1, D_A), const2),
        pl.BlockSpec((D_B, tm), const2),
        pl.BlockSpec((1, LANES), const2),
        pl.BlockSpec((1, LANES), const2),
    ]
    out_shape = [
        jax.ShapeDtypeStruct((bsz, H_A, HEAD_DIM, s), F32),
        jax.ShapeDtypeStruct((bsz, H_A, s, LANES), BF16),
        jax.ShapeDtypeStruct((bsz, s // TQ, D_A, TQ), BF16),
        jax.ShapeDtypeStruct((bsz, nba, 1, D_A), F32),
        jax.ShapeDtypeStruct((bsz, H_B, HEAD_DIM, s), F32),
        jax.ShapeDtypeStruct((bsz, G_B, s // CMP_STRIDE, CMP_STRIDE * HEAD_DIM), BF16),
        jax.ShapeDtypeStruct((bsz, G_B, s // CMP_STRIDE, CMP_STRIDE * HEAD_DIM), BF16),
        jax.ShapeDtypeStruct((bsz, G_B, s, 2 * LANES), BF16),
        jax.ShapeDtypeStruct((bsz, s // TQ, LANES, TQ), BF16),
        jax.ShapeDtypeStruct((bsz, G_B, s, LANES), BF16),
        jax.ShapeDtypeStruct((bsz, s // TQ, LANES, TQ), BF16),
        jax.ShapeDtypeStruct((bsz, G_B, GATE_ROWS, s), F32),
    ]
    out_specs = [
        pl.BlockSpec((1, H_A, HEAD_DIM, tm), tr4),
        pl.BlockSpec((1, H_A, tm, LANES), tok4),
        pl.BlockSpec((1, tm // TQ, D_A, TQ), tile4),
        pl.BlockSpec((1, tm // BLK_A, 1, D_A), lambda b, t: (b, t, 0, 0)),
        pl.BlockSpec((1, H_B, HEAD_DIM, tm), tr4),
        pl.BlockSpec((1, G_B, tm // CMP_STRIDE, CMP_STRIDE * HEAD_DIM), tok4),
        pl.BlockSpec((1, G_B, tm // CMP_STRIDE, CMP_STRIDE * HEAD_DIM), tok4),
        pl.BlockSpec((1, G_B, tm, 2 * LANES), tok4),
        pl.BlockSpec((1, tm // TQ, LANES, TQ), tile4),
        pl.BlockSpec((1, G_B, tm, LANES), tok4),
        pl.BlockSpec((1, tm // TQ, LANES, TQ), tile4),
        pl.BlockSpec((1, G_B, GATE_ROWS, tm), tr4),
    ]
    return pl.pallas_call(
        _inproj_kernel,
        grid=(bsz, nt),
        in_specs=in_specs,
        out_specs=out_specs,
        out_shape=out_shape,
        scratch_shapes=[pltpu.VMEM((tm, LANES), F32)],
        compiler_params=_params(("parallel", "parallel")),
        name="inproj",
    )(x, sc, sh, gmix, wr, wt, bd, gqa, gka, gqb, gks, gkw)


def _compress_kernel(ck_ref, cv_ref, w1_ref, w2t_ref, pe_ref, gk_ref, ok_ref, ov_ref):
    half = CMP_STRIDE * HEAD_DIM
    for kv, c_ref in enumerate((ck_ref, cv_ref)):
        for g in range(G_B):
            c = c_ref[0, g]
            a = _dot(c, w1_ref[kv, 0:half, :])
            b = _dot(c, w1_ref[kv, half:2 * half, :])
            n = a.shape[0]
            b_next = pltpu.roll(b, n - 1, 0)
            pe_term = _dot(pe_ref[kv], w1_ref[kv])[0:1]
            hid = jax.nn.gelu(a + b_next + pe_term).astype(BF16)
            if kv == 0:
                y = _dot_nt(hid, w2t_ref[kv])
                ms = jnp.mean(y * y, axis=1, keepdims=True)
                y = y * lax.rsqrt(ms + EPS) * gk_ref[...]
                ok_ref[0, g] = jnp.concatenate([y, jnp.zeros_like(y)], axis=1).astype(BF16)
            else:
                ov_ref[0, g] = _dot_nt(w2t_ref[kv], hid).astype(BF16)


def _compress(ck, cv, w1, w2t, pe, gk):
    bsz, g, n, width = ck.shape
    blk = pl.BlockSpec((1, g, n, width), lambda b: (b, 0, 0, 0))
    full = lambda a: pl.BlockSpec(a.shape, lambda b: (0,) * a.ndim)
    return pl.pallas_call(
        _compress_kernel,
        grid=(bsz,),
        in_specs=[blk, blk, full(w1), full(w2t), full(pe), full(gk)],
        out_specs=[pl.BlockSpec((1, g, n, LANES), lambda b: (b, 0, 0, 0)),
                   pl.BlockSpec((1, g, HEAD_DIM, n), lambda b: (b, 0, 0, 0))],
        out_shape=[jax.ShapeDtypeStruct((bsz, g, n, LANES), BF16),
                   jax.ShapeDtypeStruct((bsz, g, HEAD_DIM, n), BF16)],
        compiler_params=_params(("parallel",)),
        name="compress",
    )(ck, cv, w1, w2t, pe, gk)


def _with_ones(vt):
    return jnp.concatenate([vt, jnp.ones((BF16_ROWS, vt.shape[1]), BF16)], axis=0)


def _col_max(s):
    while s.shape[0] > 8:
        half = s.shape[0] // 2
        s = jnp.maximum(s[0:half], s[half:2 * half])
    return jnp.max(s, axis=0, keepdims=True)


def _flash_update(carries, chains):
    m_news = []
    for carry, tiles in zip(carries, chains):
        tops = []
        for s, _, bias, top in tiles:
            top = _col_max(s) if top is None else top
            tops.append(top if bias is None else top + bias)
        m_news.append(functools.reduce(jnp.maximum, tops if carry is None else tops + [carry[0]]))
    pvs = [None] * len(chains)
    for t in range(max(len(tiles) for tiles in chains)):
        for c, tiles in enumerate(chains):
            if t < len(tiles):
                s, vt, bias, _ = tiles[t]
                p = jnp.exp2(s - (m_news[c] if bias is None else m_news[c] - bias)).astype(BF16)
                part = _dot(_with_ones(vt), p)
                pvs[c] = part if pvs[c] is None else pvs[c] + part
    outs = []
    for carry, m_new, pv in zip(carries, m_news, pvs):
        outs.append((m_new, pv if carry is None else jnp.exp2(carry[0] - m_new) * carry[1] + pv))
    return outs


def _flash_out(acc):
    return acc[0:HEAD_DIM] / acc[HEAD_DIM:HEAD_DIM + 1]


def _topk_rows(scores, index, k):
    scores = list(scores)
    picked = [jnp.zeros(sc.shape, F32) for sc in scores]
    for _ in range(k):
        mx = [jnp.max(sc, axis=0, keepdims=True) for sc in scores]
        cand = [jnp.where(sc == m, index, jnp.int32(1 << 20)) for sc, m in zip(scores, mx)]
        first = [jnp.min(c, axis=0, keepdims=True) for c in cand]
        hit = [index == f for f in first]
        picked = [jnp.where(h, 1.0, p) for h, p in zip(hit, picked)]
        scores = [jnp.where(h, -jnp.inf, sc) for h, sc in zip(hit, scores)]
    return picked


def _tile_rows(j):
    return pl.ds(pl.multiple_of(j * TQ, TQ), TQ)


def _moba_gate_kernel(q_ref, km_ref, o_ref, *, n_sel, n_rows):
    nh, tg = q_ref.shape[1], q_ref.shape[3]
    t0 = pl.program_id(2) * tg
    shape = (n_rows, tg)
    blk = lax.broadcasted_iota(jnp.int32, shape, 0)
    own = (lax.broadcasted_iota(jnp.int32, shape, 1) + t0) >> _LOG2_BLK_A
    valid = blk < own
    qs, gates = [], []
    for hh in range(nh):
        q = q_ref[0, hh]
        gate = _dot3(km_ref[0, hh, 0:n_rows, :], jnp.concatenate([q, jnp.zeros_like(q)], axis=0))
        qs.append(q)
        gates.append(jnp.where(valid, gate, -jnp.inf))
    unused = jnp.full((HEAD_DIM - n_rows, tg), NEG, F32)
    for hh, picked in enumerate(_topk_rows(gates, blk, n_sel)):
        keep = jnp.where(valid, picked, 0.0) + jnp.where(blk == own, 1.0, 0.0)
        selb = jnp.where(keep > 0.0, 0.0, NEG)
        o_ref[0, hh] = jnp.concatenate([qs[hh], selb, unused], axis=0).astype(BF16)


def _moba_gate(qa, km):
    bsz, nh, _, s = qa.shape
    tg = min(s, GATE_TQ)
    hpb = GATE_HEADS
    n_sel = max(1, min(TOPK_A, s // BLK_A - 1))
    return pl.pallas_call(
        functools.partial(_moba_gate_kernel, n_sel=n_sel, n_rows=-(-(s // BLK_A) // 8) * 8),
        grid=(bsz, nh // hpb, s // tg),
        in_specs=[pl.BlockSpec((1, hpb, HEAD_DIM, tg), lambda b, h, t: (b, h, 0, t)),
                  pl.BlockSpec((1, hpb, HEAD_DIM, LANES), lambda b, h, t: (b, h, 0, 0))],
        out_specs=pl.BlockSpec((1, hpb, LANES, tg), lambda b, h, t: (b, h, 0, t)),
        out_shape=jax.ShapeDtypeStruct((bsz, nh, LANES, s), BF16),
        compiler_params=_params(("parallel", "parallel", "parallel")),
        name="moba_gate",
    )(qa, km)


class _NearTiles:
    def __init__(self, sn_ref):
        self.sn_ref, self.count = sn_ref, 0

    def put(self, s):
        slot = self.count
        self.count += 1
        self.sn_ref[slot] = s
        return slot, _col_max(s)

    def tile(self, stashed, vt, bias):
        slot, top = stashed
        return self.sn_ref[slot], vt, bias, top


def _far_tiles(qi):
    n_far = jnp.maximum(qi - 1, 0)
    left = jnp.maximum(n_far - 1, 0)
    off_left = jnp.where((n_far & 1) == 1, 0.0, NEG)
    return n_far >> 1, left, off_left


class _FarLoop:
    def __init__(self, n_pairs, heads, qk_tile, values, cbs, sa_ref, sb_ref, m_ref, acc_ref):
        self.n_pairs, self.heads, self.qk_tile, self.values, self.cbs = n_pairs, heads, qk_tile, values, cbs
        self.sa_ref, self.sb_ref, self.m_ref, self.acc_ref = sa_ref, sb_ref, m_ref, acc_ref
        self.last = jnp.maximum(n_pairs - 1, 0)

    def fetch(self, buf_ref, h, i):
        s_lo = self.qk_tile(h, 2 * i)
        s_hi = self.qk_tile(h, 2 * i + 1)
        buf_ref[h, 0:TQ, :] = s_lo
        buf_ref[h, TQ:2 * TQ, :] = s_hi
        return _col_max(s_lo), _col_max(s_hi)

    def consume(self, buf_ref, h, top, i):
        (m, acc), = _flash_update([(self.m_ref[h], self.acc_ref[h])], [[
            (buf_ref[h, 0:TQ, :], self.values(h, 2 * i), self.cbs[h], top[0]),
            (buf_ref[h, TQ:2 * TQ, :], self.values(h, 2 * i + 1), self.cbs[h], top[1])]])
        self.m_ref[h] = m
        self.acc_ref[h] = acc

    def run(self, tops_first):
        def two_pairs(ia, tops_a):
            tops_b, tops_next = [], []
            for h in self.heads:
                tops_b.append(self.fetch(self.sb_ref, h, ia + 1))
                self.consume(self.sa_ref, h, tops_a[h], ia)
            for h in self.heads:
                tops_next.append(self.fetch(self.sa_ref, h, jnp.minimum(ia + 2, self.last)))
                self.consume(self.sb_ref, h, tops_b[h], ia + 1)
            return tuple(tops_next)

        def pairs(n, start, tops):
            for k in range(0, n, 2):
                tops = two_pairs(start + k, tops)
            return tops

        trip = FAR_PAIRS_PER_TRIP
        n_trips = self.n_pairs // trip
        tops = lax.fori_loop(0, n_trips, lambda q, t: pairs(trip, trip * q, t), tops_first)
        done = trip * n_trips
        n = trip // 2
        while n >= 2:
            has = ((self.n_pairs // n) & 1) == 1
            tops = lax.cond(has, functools.partial(pairs, n, done), lambda t: t, tops)
            done = done + jnp.where(has, n, 0)
            n //= 2
        tops_last = tops

        @pl.when((self.n_pairs & 1) == 1)
        def _():
            for h in self.heads:
                self.consume(self.sa_ref, h, tops_last[h], self.last)


def _moba_kernel(q_ref, k_ref, vt_ref, tab_ref, cb_ref, o_ref, sn_ref, sa_ref, sb_ref, m_ref, acc_ref):
    qi = pl.program_id(2)
    jp = jnp.maximum(qi - 1, 0)
    off_p = jnp.where(qi >= 1, 0.0, NEG)
    n_pairs, jl, off_l = _far_tiles(qi)
    heads = range(MOBA_HEADS)
    qaug = [q_ref[0, hh] for hh in heads]
    cbs = [cb_ref[hh][0:1, 0:1] for hh in heads]

    def scores(hh, j):
        return _dot(k_ref[0, hh, _tile_rows(j), :], qaug[hh])

    def values(hh, j):
        return vt_ref[0, j, hh * HEAD_DIM:(hh + 1) * HEAD_DIM, :]

    far = _FarLoop(n_pairs, heads, scores, values, cbs, sa_ref, sb_ref, m_ref, acc_ref)

    near = _NearTiles(sn_ref)
    stash = [(near.put(scores(hh, qi) + tab_ref[hh, TQ:2 * TQ, :]),
              near.put(scores(hh, jp) + tab_ref[hh, 0:TQ, :]),
              near.put(scores(hh, jl))) for hh in heads]
    tops_first = []
    for hh in heads:
        tops_first.append(far.fetch(sa_ref, hh, 0))
        (m, acc), = _flash_update([None], [[near.tile(stash[hh][0], values(hh, qi), None),
                                            near.tile(stash[hh][1], values(hh, jp), off_p),
                                            near.tile(stash[hh][2], values(hh, jl), cbs[hh] + off_l)]])
        m_ref[hh] = m
        acc_ref[hh] = acc
    far.run(tuple(tops_first))
    out_t = jnp.concatenate([_flash_out(acc_ref[hh]) for hh in heads], axis=0)
    o_ref[0] = out_t.T.astype(BF16)


def _flash_scratch(n_heads):
    return [pltpu.VMEM((n_heads, 2 * TQ, TQ), F32),
            pltpu.VMEM((n_heads, 2 * TQ, TQ), F32),
            pltpu.VMEM((n_heads, 1, TQ), F32),
            pltpu.VMEM((n_heads, ACC_ROWS, TQ), F32)]


def _moba(qaug, kaug, vat, tab, cb):
    bsz, _, _, s = qaug.shape
    nq = s // TQ
    nh = MOBA_HEADS
    return pl.pallas_call(
        _moba_kernel,
        grid=(bsz, H_A // nh, nq),
        in_specs=[
            pl.BlockSpec((1, nh, LANES, TQ), lambda b, hp, qi: (b, hp, 0, qi)),
            pl.BlockSpec((1, nh, s, LANES), lambda b, hp, qi: (b, hp, 0, 0)),
            pl.BlockSpec((1, nq, nh * HEAD_DIM, TQ), lambda b, hp, qi: (b, 0, hp, 0)),
            pl.BlockSpec((nh, 2 * TQ, TQ), lambda b, hp, qi: (hp, 0, 0)),
            pl.BlockSpec((nh, 8, LANES), lambda b, hp, qi: (hp, 0, 0)),
        ],
        out_specs=pl.BlockSpec((1, TQ, nh * HEAD_DIM), lambda b, hp, qi: (b, qi, hp)),
        out_shape=jax.ShapeDtypeStruct((bsz, s, H_A * HEAD_DIM), BF16),
        scratch_shapes=[pltpu.VMEM((3 * nh, TQ, TQ), F32)] + _flash_scratch(nh),
        compiler_params=_params(("parallel", "parallel", "arbitrary")),
        name="moba",
    )(qaug, kaug, vat, tab, cb)


def _nsa_cmp_kernel(q_ref, kc_ref, vct_ref, ovt_ref, oc_ref, selb_ref, *, n_sel, n_cmp, n_parts):
    qi = pl.program_id(1)
    ncp = kc_ref.shape[2]
    tq = q_ref.shape[3]
    t0 = qi * tq
    zeros = jnp.zeros((HEAD_DIM, tq), BF16)
    qs = [jnp.concatenate([q_ref[0, hd].astype(BF16), zeros], axis=0) for hd in range(H_B)]

    def body(nk, nb):
        n_idx = lax.broadcasted_iota(jnp.int32, (nk, tq), 0)
        t_idx = lax.broadcasted_iota(jnp.int32, (nk, tq), 1) + t0
        mask = (n_idx * CMP_STRIDE + (CMP_LEN - 1) <= t_idx) & (n_idx < n_cmp)
        any_key = t_idx[0:1] >= CMP_LEN - 1
        blk = lax.broadcasted_iota(jnp.int32, (nb, tq), 0)
        cur = (lax.broadcasted_iota(jnp.int32, (nb, tq), 1) + t0) >> _LOG2_SEL_BLK
        ok = blk <= cur
        forced = (blk == 0) | (blk == cur) | (blk == cur - 1)
        ovt = ovt_ref[0:nb, 0:nk]
        scores = []
        for g in range(G_B):
            kc = kc_ref[0, g, 0:nk, :]
            vct = _with_ones(vct_ref[0, g, :, 0:nk])
            psum = jnp.zeros((nk, tq), F32)
            for hd in range(g * R_B, (g + 1) * R_B):
                z = jnp.where(mask, _dot(kc, qs[hd]), NEG)
                e = jnp.exp2(z - _col_max(z))
                acc = _dot(vct, e.astype(BF16))
                rinv = jnp.where(any_key, 1.0 / acc[HEAD_DIM:HEAD_DIM + 1], 0.0)
                oc_ref[0, hd] = acc[0:HEAD_DIM] * rinv
                psum = psum + e * rinv
            ph, pl_ = _split(psum)
            imp_t = _dot(ovt, ph) + _dot(ovt, pl_)
            scores.append(jnp.where(ok & jnp.logical_not(forced), imp_t, -jnp.inf))
        for g, picked in enumerate(_topk_rows(scores, blk, max(n_sel - N_FORCED, 0))):
            selb_ref[0, g, 0:nb, :] = jnp.where(ok & (forced | (picked > 0.0)), 0.0, NEG).astype(BF16)
            if nb < LANES:
                selb_ref[0, g, nb:LANES, :] = jnp.full((LANES - nb, tq), NEG, BF16)

    part = ncp // n_parts
    need = jnp.minimum(((qi + 1) * (tq // CMP_STRIDE) + part - 1) // part, n_parts)
    for v in range(1, n_parts + 1):
        pl.when(need == v)(functools.partial(body, v * part, min(LANES, v * part * CMP_STRIDE // SEL_BLK)))


def _cmp_parts(ncp):
    return 4 if ncp % (4 * LANES) == 0 else 1


def _nsa_cmp(qb, kcmp, vcmpt, ovt, n_cmp):
    bsz, _, _, s = qb.shape
    tq = CMP_TQ if s % CMP_TQ == 0 else TQ
    nq = s // tq
    ncp = kcmp.shape[2]
    n_sel = min(SEL_TOPK, s // SEL_BLK)
    assert n_sel >= N_FORCED
    n_parts = _cmp_parts(ncp)
    return pl.pallas_call(
        functools.partial(_nsa_cmp_kernel, n_sel=n_sel, n_cmp=n_cmp, n_parts=n_parts),
        grid=(bsz, nq),
        in_specs=[
            pl.BlockSpec((1, H_B, HEAD_DIM, tq), lambda b, qi: (b, 0, 0, qi)),
            pl.BlockSpec((1, G_B, ncp, LANES), lambda b, qi: (b, 0, 0, 0)),
            pl.BlockSpec((1, G_B, HEAD_DIM, ncp), lambda b, qi: (b, 0, 0, 0)),
            pl.BlockSpec((LANES, ncp), lambda b, qi: (0, 0)),
        ],
        out_specs=[
            pl.BlockSpec((1, H_B, HEAD_DIM, tq), lambda b, qi: (b, 0, 0, qi)),
            pl.BlockSpec((1, G_B, LANES, tq), lambda b, qi: (b, 0, 0, qi)),
        ],
        out_shape=[jax.ShapeDtypeStruct((bsz, H_B, HEAD_DIM, s), F32),
                   jax.ShapeDtypeStruct((bsz, G_B, LANES, s), BF16)],
        compiler_params=_params(("parallel", "parallel")),
        name="nsa_cmp",
    )(qb, kcmp, vcmpt, ovt)


def _nsa_main_kernel(q_ref, selb_ref, ks_ref, vst_ref, kw_ref, vwt_ref, oc_ref, gates_ref,
                     tabs_ref, tabw_ref, cb_ref, o_ref, sa_ref, sb_ref, m_ref, acc_ref):
    qi = pl.program_id(2)
    selb = selb_ref[0, 0]
    gates = gates_ref[0, 0]
    zeros = jnp.zeros((HEAD_DIM, TQ), BF16)
    j1 = jnp.maximum(qi - 1, 0)
    j2 = jnp.maximum(qi - 2, 0)
    off1 = jnp.where(qi >= 1, 0.0, NEG)
    off2 = jnp.where(qi >= 2, 0.0, NEG)
    n_pairs, jl, off_l = _far_tiles(qi)

    heads = range(R_B)
    qw = [jnp.concatenate([q_ref[0, r].astype(BF16), zeros], axis=0) for r in heads]
    qs = [jnp.concatenate([qw[r], selb], axis=0) for r in heads]
    cbs = [cb_ref[r][0:1, 0:1] for r in heads]

    def s_sel(r, j):
        return _dot(ks_ref[0, 0, _tile_rows(j), :], qs[r])

    def s_win(r, j):
        return _dot(kw_ref[0, 0, _tile_rows(j), :], qw[r])

    far = _FarLoop(n_pairs, heads, s_sel, lambda r, j: vst_ref[0, j], cbs, sa_ref, sb_ref, m_ref, acc_ref)

    sw = [[s_win(r, j) for r in heads] for j in (qi, j1, j2)]
    ss = [[s_sel(r, j) for r in heads] for j in (qi, j1, jl)]
    tops_first = tuple(far.fetch(sa_ref, r, 0) for r in heads)

    chains = [[(sw[0][r] + tabw_ref[r, 2 * TQ:3 * TQ, :], vwt_ref[0, qi], None, None),
               (sw[1][r] + tabw_ref[r, TQ:2 * TQ, :], vwt_ref[0, j1], off1, None),
               (sw[2][r] + tabw_ref[r, 0:TQ, :], vwt_ref[0, j2], off2, None)] for r in heads]
    chains += [[(ss[0][r] + tabs_ref[r, TQ:2 * TQ, :], vst_ref[0, qi], None, None),
                (ss[1][r] + tabs_ref[r, 0:TQ, :], vst_ref[0, j1], off1, None),
                (ss[2][r], vst_ref[0, jl], cbs[r] + off_l, None)] for r in heads]
    done = _flash_update([None] * len(chains), chains)
    win = [_flash_out(acc) for _, acc in done[:R_B]]
    for r, (m, acc) in zip(heads, done[R_B:]):
        m_ref[r] = m
        acc_ref[r] = acc
    far.run(tops_first)

    outs = [gates[3 * r:3 * r + 1] * oc_ref[0, r]
            + gates[3 * r + 1:3 * r + 2] * _flash_out(acc_ref[r])
            + gates[3 * r + 2:3 * r + 3] * win[r] for r in heads]
    o_ref[0] = jnp.concatenate(outs, axis=0).T.astype(BF16)


def _nsa_main(qb, selb, ksaug, vst, kw, vwt, oc, gates, tabs, tabw, cb):
    bsz, _, _, s = qb.shape
    nq = s // TQ
    return pl.pallas_call(
        _nsa_main_kernel,
        grid=(bsz, G_B, nq),
        in_specs=[
            pl.BlockSpec((1, R_B, HEAD_DIM, TQ), lambda b, g, qi: (b, g, 0, qi)),
            pl.BlockSpec((1, 1, LANES, TQ), lambda b, g, qi: (b, g, 0, qi)),
            pl.BlockSpec((1, 1, s, 2 * LANES), lambda b, g, qi: (b, g, 0, 0)),
            pl.BlockSpec((1, nq, HEAD_DIM, TQ), lambda b, g, qi: (b, 0, g, 0)),
            pl.BlockSpec((1, 1, s, LANES), lambda b, g, qi: (b, g, 0, 0)),
            pl.BlockSpec((1, nq, HEAD_DIM, TQ), lambda b, g, qi: (b, 0, g, 0)),
            pl.BlockSpec((1, R_B, HEAD_DIM, TQ), lambda b, g, qi: (b, g, 0, qi)),
            pl.BlockSpec((1, 1, GATE_ROWS, TQ), lambda b, g, qi: (b, g, 0, qi)),
            pl.BlockSpec((R_B, 2 * TQ, TQ), lambda b, g, qi: (g, 0, 0)),
            pl.BlockSpec((R_B, 3 * TQ, TQ), lambda b, g, qi: (g, 0, 0)),
            pl.BlockSpec((R_B, 8, LANES), lambda b, g, qi: (g, 0, 0)),
        ],
        out_specs=pl.BlockSpec((1, TQ, R_B * HEAD_DIM), lambda b, g, qi: (b, qi, g)),
        out_shape=jax.ShapeDtypeStruct((bsz, s, H_B * HEAD_DIM), BF16),
        scratch_shapes=_flash_scratch(R_B),
        compiler_params=_params(("parallel", "parallel", "arbitrary")),
        name="nsa_main",
    )(qb, selb, ksaug, vst, kw, vwt, oc, gates, tabs, tabw, cb)


def _out_ffn_kernel(x_ref, oa_ref, ob_ref, wo_ref, gtm_ref, gffn_ref, scf_ref, shf_ref, gtf_ref,
                    wgu_ref, wd_ref, o_ref):
    half = oa_ref.shape[2]
    mix = _dot(oa_ref[0], wo_ref[0:half, :]) + _dot(ob_ref[0], wo_ref[half:2 * half, :])
    x1 = x_ref[0] + gtm_ref[0, 0] * mix
    h = (_rms_rows(x1, gffn_ref[...]) * (1.0 + scf_ref[0, 0]) + shf_ref[0, 0]).astype(BF16)
    fh = wd_ref.shape[0]
    acc = None
    for c0 in range(0, fh, FFN_CHUNK):
        c1 = min(c0 + FFN_CHUNK, fh)
        gate = _dot(h, wgu_ref[:, c0:c1])
        up = _dot(h, wgu_ref[:, fh + c0:fh + c1])
        act = (gate * jax.nn.sigmoid(gate) * up).astype(BF16)
        part = _dot(act, wd_ref[c0:c1, :])
        acc = part if acc is None else acc + part
    o_ref[0] = x1 + gtf_ref[0, 0] * acc


def _out_ffn(x, oa, ob, wo, mod4, gffn, wgu, wd):
    bsz, s, d = x.shape
    tm = TM_FFN
    tok = lambda b, t: (b, t, 0)
    modspec = lambda k: pl.BlockSpec((1, 1, 1, d), lambda b, t: (b, k, 0, 0))
    resident = lambda a: pl.BlockSpec(a.shape, lambda b, t: (0, 0), pipeline_mode=pl.Buffered(1))
    return pl.pallas_call(
        _out_ffn_kernel,
        grid=(bsz, s // tm),
        in_specs=[
            pl.BlockSpec((1, tm, d), tok),
            pl.BlockSpec((1, tm, oa.shape[2]), tok),
            pl.BlockSpec((1, tm, ob.shape[2]), tok),
            resident(wo),
            modspec(2),
            pl.BlockSpec((1, d), lambda b, t: (0, 0)),
            modspec(4),
            modspec(3),
            modspec(5),
            resident(wgu), resident(wd),
        ],
        out_specs=pl.BlockSpec((1, tm, d), tok),
        out_shape=jax.ShapeDtypeStruct((bsz, s, d), F32),
        compiler_params=_params(("parallel", "parallel")),
        name="out_ffn",
    )(x, oa, ob, wo, mod4, gffn, mod4, mod4, mod4, wgu, wd)


def _t5_bucket_np(d):
    max_exact = N_BUCKETS // 2
    d = np.maximum(d, 0)
    df = np.maximum(d, 1).astype(np.float64)
    large = max_exact + (np.log(df / max_exact) / math.log(MAX_DIST / max_exact)
                         * (N_BUCKETS - max_exact)).astype(np.int64)
    large = np.minimum(large, N_BUCKETS - 1)
    return np.where(d < max_exact, d, large).astype(np.int32)


def _bias_expand_kernel(tab_ref, bucket_ref, o_ref):
    hd = pl.program_id(0)
    bucket = bucket_ref[...]
    acc = jnp.full(bucket.shape, NEG, F32)
    for b in range(N_BUCKETS):
        acc = jnp.where(bucket == b, tab_ref[hd, b], acc)
    o_ref[0] = acc


def _bias_expand(tab, bucket):
    nh = tab.shape[0]
    return pl.pallas_call(
        _bias_expand_kernel,
        grid=(nh,),
        in_specs=[pl.BlockSpec(memory_space=pltpu.SMEM),
                  pl.BlockSpec(bucket.shape, lambda h: (0, 0))],
        out_specs=pl.BlockSpec((1,) + bucket.shape, lambda h: (h, 0, 0)),
        out_shape=jax.ShapeDtypeStruct((nh,) + bucket.shape, F32),
        compiler_params=_params(("parallel",)),
        name="bias_expand",
    )(tab, jnp.asarray(bucket, jnp.int32))


def _bias_tables(rel_bias):
    tab = rel_bias.T.astype(F32) * LOG2_E
    i = np.arange(TQ)[None, :]
    d_near = i + TQ - np.arange(2 * TQ)[:, None]
    near = _bias_expand(tab, np.where(d_near >= 0, _t5_bucket_np(d_near), -1))
    d_win = i + 2 * TQ - np.arange(3 * TQ)[:, None]
    ok_win = (d_win >= 0) & (d_win < WINDOW)
    win = _bias_expand(tab[H_A:], np.where(ok_win, _t5_bucket_np(d_win), -1))
    far = jnp.broadcast_to(tab[:, N_BUCKETS - 1][:, None, None], (tab.shape[0], 8, LANES))
    return near, win, far


def _overlap_t(n_cmp_pad, n_cmp):
    cs = np.arange(n_cmp_pad)[None, :] * CMP_STRIDE
    ss = np.arange(LANES)[:, None] * SEL_BLK
    ov = (cs < ss + SEL_BLK) & (cs + CMP_LEN > ss) & (np.arange(n_cmp_pad)[None, :] < n_cmp)
    return jnp.asarray(ov.astype(np.float32), BF16)


def _block_diag(n):
    m = (np.arange(n)[:, None] // HEAD_DIM == np.arange(n)[None, :] // HEAD_DIM)
    return jnp.asarray(m.astype(np.float32) / HEAD_DIM, BF16)


def kernel(x, c, rel_bias, w_ada, b_ada, g_mix, w_in, q_norm_a, k_norm_a, q_norm_b, k_norm_cmp,
           k_norm_sel, k_norm_win, cmp_pe_k, cmp_w1_k, cmp_w2_k, cmp_pe_v, cmp_w1_v, cmp_w2_v,
           w_out, g_ffn, w_gu, w_down):
    bsz, s, d = x.shape
    depth = w_ada.shape[0]
    assert s % TM_IN == 0 and s % TM_FFN == 0 and s % (2 * TQ) == 0
    assert s // BLK_A <= HEAD_DIM and s // SEL_BLK <= LANES
    assert WINDOW == 2 * TQ and BLK_A == TQ and MAX_DIST <= TQ
    n_chunks = s // CMP_STRIDE
    n_cmp = (s - CMP_LEN) // CMP_STRIDE + 1
    scale = HEAD_DIM ** -0.5 * LOG2_E
    hd = HEAD_DIM

    near, win, far = _bias_tables(rel_bias)
    ovt = _overlap_t(n_chunks, n_cmp)
    bd = _block_diag(D_A)
    tile = lambda g, n: jnp.tile(g.astype(F32), n).reshape(1, -1)
    tile_t = lambda g, n: jnp.broadcast_to(jnp.tile(g.astype(F32), n)[:, None], (n * hd, TM_IN))

    for l in range(depth):
        mod = _ada(c, w_ada[l], b_ada[l])
        mod4 = mod.reshape(bsz, ADA_CHUNKS, 1, d)

        wl = w_in[l]
        cols = np.cumsum([0, H_A * hd, H_A * hd, H_A * hd, H_B * hd] + [G_B * hd] * 6)
        qa_c, ka_c, va_c, qb_c, kc_c, vc_c, ks_c, vs_c, kw_c, vw_c = [
            wl[:, int(a):int(b)] for a, b in zip(cols[:-1], cols[1:])]
        gl = wl[:, int(cols[-1]):].reshape(d, G_B, 3 * R_B)
        gl = jnp.pad(gl, ((0, 0), (0, 0), (0, GATE_ROWS - 3 * R_B))).reshape(d, G_B * GATE_ROWS)
        w_rows = jnp.concatenate([ka_c, kc_c, vc_c, ks_c, kw_c], axis=1).astype(BF16)
        w_t = jnp.concatenate([qa_c, va_c, qb_c, vs_c, vw_c, gl], axis=1).T.astype(BF16)

        (qa, kaug, vat, kmean, qb, kc, vc, ksaug, vst, kw, vwt, gates) = _inproj(
            x, mod4, mod4, g_mix[l].reshape(1, d), w_rows, w_t, bd,
            tile_t(q_norm_a[l], H_A) * scale, tile(k_norm_a[l], H_A), tile_t(q_norm_b[l], H_B) * scale,
            tile(k_norm_sel[l], G_B), tile(k_norm_win[l], G_B))

        nba = s // BLK_A
        km = kmean.reshape(bsz, nba, H_A, hd).transpose(0, 2, 1, 3)
        km = jnp.pad(km, ((0, 0), (0, 0), (0, hd - nba), (0, LANES - hd)))
        o_a = _moba(_moba_gate(qa, km), kaug, vat, near[:H_A], far[:H_A])

        w1 = jnp.stack([cmp_w1_k[l], cmp_w1_v[l]]).astype(BF16)
        w2t = jnp.stack([cmp_w2_k[l].T, cmp_w2_v[l].T]).astype(BF16)
        pe = jnp.stack([cmp_pe_k[l], cmp_pe_v[l]]).reshape(2, 1, CMP_LEN * hd)
        pe = jnp.broadcast_to(pe, (2, 8, CMP_LEN * hd)).astype(BF16)
        kcmp, vcmpt = _compress(kc, vc, w1, w2t, pe,
                                k_norm_cmp[l].astype(F32).reshape(1, hd))

        oc, selb = _nsa_cmp(qb, kcmp, vcmpt, ovt, n_cmp)
        o_b = _nsa_main(qb, selb, ksaug, vst, kw, vwt, oc, gates, near[H_A:], win, far[H_A:])

        x = _out_ffn(x, o_a, o_b, w_out[l].astype(BF16), mod4, g_ffn[l].reshape(1, d),
                     w_gu[l].astype(BF16), w_down[l].astype(BF16))
    return x
```
